```python
import math
import jax, jax.numpy as jnp
from jax import lax
import numpy as np


D_MODEL = 1024
BATCH = 32
SEQ = 2048
DEPTH = 4

CHUNK = 64
MIX_WIDTH = D_MODEL
POOL_WIDTH = MIX_WIDTH // 2
POOL_WINDOWS = (2, 4, 8, 16)
POOL_GROUPS = len(POOL_WINDOWS)
POOL_GROUP = POOL_WIDTH // POOL_GROUPS
DN_HEAD_DIM = 128
DN_HEADS = (MIX_WIDTH - POOL_WIDTH) // DN_HEAD_DIM
DN_WIDTH = DN_HEADS * DN_HEAD_DIM
DN_CONV = 4
SGU_WIDTH = MIX_WIDTH // 2
SGU_BLOCK = 128
SGU_HEADS = 4
SGU_HEAD_CH = SGU_WIDTH // SGU_HEADS
SC_WIDTH = MIX_WIDTH - SGU_WIDTH
SC_CONV = 3
FFN_DIM = ((8 * D_MODEL // 3 + 127) // 128) * 128
AB_IN = POOL_WIDTH + 4 * DN_WIDTH + 2 * DN_HEADS
CD_IN = 2 * SGU_WIDTH + 3 * SC_WIDTH
N_EVEN = (DEPTH + 1) // 2
N_ODD = DEPTH // 2
EPS = 1e-6

kernel_name = 'hybrid_chunk_causal_encoder'


def rmsnorm(x, g):
    xf = x.astype(jnp.float32)
    y = xf * lax.rsqrt(jnp.mean(xf * xf, axis=-1, keepdims=True) + EPS)
    return (y * g.astype(jnp.float32)).astype(x.dtype)


def layernorm(x, g, b):
    xf = x.astype(jnp.float32)
    mu = jnp.mean(xf, axis=-1, keepdims=True)
    xc = xf - mu
    y = xc * lax.rsqrt(jnp.mean(xc * xc, axis=-1, keepdims=True) + EPS)
    return (y * g.astype(jnp.float32) + b.astype(jnp.float32)).astype(x.dtype)


def l2norm(x):
    return x * lax.rsqrt(jnp.sum(x * x, axis=-1, keepdims=True) + EPS)


def causal_depthwise_conv(x, w):
    K, C = w.shape
    return lax.conv_general_dilated(
        x, w[:, None, :].astype(x.dtype), window_strides=(1,), padding=[(K - 1, 0)],
        dimension_numbers=('NWC', 'WIO', 'NWC'), feature_group_count=C)


def swiglu(x, w_gate, w_up, w_down):
    return (jax.nn.silu(x @ w_gate) * (x @ w_up)) @ w_down


def pool_mixer(a, w_pool, scale):
    B, S, _ = a.shape
    grp = a.astype(jnp.float32).reshape(B, S, POOL_GROUPS, POOL_GROUP)
    cs = jnp.cumsum(grp, axis=1)
    pos = jnp.arange(S)
    outs = []
    for gi, w in enumerate(POOL_WINDOWS):
        c = cs[:, :, gi]
        lagged = jnp.pad(c, ((0, 0), (w, 0), (0, 0)))[:, :S]
        cnt = jnp.minimum(pos + 1, w).astype(jnp.float32)[None, :, None]
        outs.append((c - lagged) / cnt - grp[:, :, gi])
    pooled = jnp.stack(outs, axis=2).astype(a.dtype)
    mixed = jnp.einsum('bsgc,gcd->bsgd', pooled, w_pool)
    return mixed.reshape(B, S, POOL_WIDTH) * scale


def gated_delta_rule(q, k, v, beta, g):
    B, S, H, dk = q.shape
    dv = v.shape[-1]
    N = S // CHUNK
    def chunks(t):
        t = jnp.swapaxes(t, 1, 2)
        return t.reshape((B, H, N, CHUNK) + t.shape[3:])
    q, k, v, beta, g = chunks(q), chunks(k), chunks(v), chunks(beta), chunks(g)
    gc = jnp.cumsum(g, axis=-1)
    tril = jnp.tril(jnp.ones((CHUNK, CHUNK), dtype=bool))
    strict = jnp.tril(jnp.ones((CHUNK, CHUNK), dtype=bool), k=-1)
    gamma = jnp.exp(jnp.where(tril, gc[..., :, None] - gc[..., None, :], -jnp.inf))
    kb = k * beta[..., None]
    lmat = jnp.where(strict, jnp.einsum('bhnid,bhnjd->bhnij', kb, k) * gamma, 0.0)
    eye = jnp.eye(CHUNK, dtype=q.dtype)
    rhs = jnp.concatenate([v * beta[..., None], kb * jnp.exp(gc)[..., None]], axis=-1)
    sol = lax.linalg.triangular_solve(eye + lmat, rhs, left_side=True, lower=True,
                                      unit_diagonal=True)
    u, w = sol[..., :dv], sol[..., dv:]
    aqk = jnp.einsum('bhnid,bhnjd->bhnij', q, k) * gamma
    q_dec = q * jnp.exp(gc)[..., None]
    k_dec = k * jnp.exp(gc[..., -1:] - gc)[..., None]
    last = jnp.exp(gc[..., -1])

    def step(state, xs):
        u_i, w_i, qd_i, a_i, kd_i, l_i = xs
        v_new = u_i - jnp.einsum('bhcd,bhde->bhce', w_i, state)
        o_i = (jnp.einsum('bhcd,bhde->bhce', qd_i, state)
               + jnp.einsum('bhij,bhje->bhie', a_i, v_new))
        state = state * l_i[..., None, None] + jnp.einsum('bhcd,bhce->bhde', kd_i, v_new)
        return state, o_i

    xs = tuple(jnp.moveaxis(t, 2, 0) for t in (u, w, q_dec, aqk, k_dec, last))
    s0 = jnp.zeros((B, H, dk, dv), q.dtype)
    _, o = lax.scan(step, s0, xs)
    return o.transpose(1, 0, 3, 2, 4).reshape(B, S, H, dv)


def mixer_ab(h, w_in, pool_w, pool_scale, conv_w, a_log, dt_bias, out_norm, w_out):
    B, S, _ = h.shape
    proj = h @ w_in
    o0 = POOL_WIDTH
    o1 = o0 + 3 * DN_WIDTH
    o2 = o1 + DN_WIDTH
    o3 = o2 + DN_HEADS
    a_in, qkv, z = proj[..., :o0], proj[..., o0:o1], proj[..., o1:o2]
    b_raw, g_raw = proj[..., o2:o3], proj[..., o3:]
    y_a = pool_mixer(a_in, pool_w, pool_scale)
    qkv = jax.nn.silu(causal_depthwise_conv(qkv, conv_w)).astype(jnp.float32)
    q, k, v = (t.reshape(B, S, DN_HEADS, DN_HEAD_DIM) for t in jnp.split(qkv, 3, axis=-1))
    q = l2norm(q) * (DN_HEAD_DIM ** -0.5)
    k = l2norm(k)
    beta = jax.nn.sigmoid(b_raw.astype(jnp.float32))
    g = -jnp.exp(a_log.astype(jnp.float32)) * jax.nn.softplus(
        g_raw.astype(jnp.float32) + dt_bias.astype(jnp.float32))
    o = gated_delta_rule(q, k, v, beta, g)
    o = rmsnorm(o, out_norm) * jax.nn.silu(z.reshape(B, S, DN_HEADS, DN_HEAD_DIM).astype(jnp.float32))
    y_b = o.reshape(B, S, DN_WIDTH).astype(h.dtype)
    return jnp.concatenate([y_a, y_b], axis=-1) @ w_out


def mixer_cd(h, w_in, sgu_norm_g, sgu_norm_b, sgu_w, sgu_bias, conv_w, w_out):
    B, S, _ = h.shape
    proj = h @ w_in
    uv = jax.nn.gelu(proj[..., :2 * SGU_WIDTH])
    u, v = uv[..., :SGU_WIDTH], uv[..., SGU_WIDTH:]
    v = layernorm(v, sgu_norm_g, sgu_norm_b)
    nb = S // SGU_BLOCK
    vb = v.reshape(B, nb, SGU_BLOCK, SGU_HEADS, SGU_HEAD_CH)
    mask = jnp.tril(jnp.ones((SGU_BLOCK, SGU_BLOCK), dtype=bool))
    ws = jnp.where(mask, sgu_w, 0.0).astype(v.dtype)
    mixed = jnp.einsum('hij,bnjhc->bnihc', ws, vb) + sgu_bias.T[None, None, :, :, None]
    y_c = u * mixed.reshape(B, S, SGU_WIDTH)
    sc = proj[..., 2 * SGU_WIDTH:]
    xd, bg, cg = sc[..., :SC_WIDTH], sc[..., SC_WIDTH:2 * SC_WIDTH], sc[..., 2 * SC_WIDTH:]
    y_d = bg * causal_depthwise_conv(cg * xd, conv_w)
    return jnp.concatenate([y_c, y_d], axis=-1) @ w_out


def _fwd_setup_inputs(seed: int = 0) -> dict:
    key = jax.random.key(seed)
    ks = jax.random.split(key, 32)
    f32 = jnp.float32
    def nrm(k, shape, scale):
        return jax.random.normal(k, shape, f32) * scale
    def gain(k, shape):
        return 1.0 + 0.02 * jax.random.normal(k, shape, f32)
    D, F = D_MODEL, FFN_DIM
    dt = jnp.exp(jax.random.uniform(ks[14], (N_EVEN, DN_HEADS), f32,
                                    minval=math.log(1e-3), maxval=math.log(1e-1)))
    return {
        'x': jax.random.normal(ks[0], (BATCH, SEQ, D), f32),
        'ffn1_norm': gain(ks[1], (DEPTH, D)),
        'ffn1_w_gate': nrm(ks[2], (DEPTH, D, F), D ** -0.5),
        'ffn1_w_up': nrm(ks[3], (DEPTH, D, F), D ** -0.5),
        'ffn1_w_down': nrm(ks[4], (DEPTH, F, D), F ** -0.5),
        'mix_norm': gain(ks[5], (DEPTH, D)),
        'ffn2_norm': gain(ks[6], (DEPTH, D)),
        'ffn2_w_gate': nrm(ks[7], (DEPTH, D, F), D ** -0.5),
        'ffn2_w_up': nrm(ks[8], (DEPTH, D, F), D ** -0.5),
        'ffn2_w_down': nrm(ks[9], (DEPTH, F, D), F ** -0.5),
        'ab_w_in': nrm(ks[10], (N_EVEN, D, AB_IN), D ** -0.5),
        'pool_w': nrm(ks[11], (N_EVEN, POOL_GROUPS, POOL_GROUP, POOL_GROUP), POOL_GROUP ** -0.5),
        'pool_scale': gain(ks[12], (N_EVEN, POOL_WIDTH)),
        'dn_conv_w': nrm(ks[13], (N_EVEN, DN_CONV, 3 * DN_WIDTH), DN_CONV ** -0.5),
        'dn_a_log': jnp.log(jax.random.uniform(ks[15], (N_EVEN, DN_HEADS), f32, minval=1.0, maxval=16.0)),
        'dn_dt_bias': dt + jnp.log(-jnp.expm1(-dt)),
        'dn_out_norm': gain(ks[16], (N_EVEN, DN_HEAD_DIM)),
        'ab_w_out': nrm(ks[17], (N_EVEN, MIX_WIDTH, D), MIX_WIDTH ** -0.5),
        'cd_w_in': nrm(ks[18], (N_ODD, D, CD_IN), D ** -0.5),
        'sgu_norm_g': gain(ks[19], (N_ODD, SGU_WIDTH)),
        'sgu_norm_b': nrm(ks[20], (N_ODD, SGU_WIDTH), 0.02),
        'sgu_w': nrm(ks[21], (N_ODD, SGU_HEADS, SGU_BLOCK, SGU_BLOCK), SGU_BLOCK ** -0.5),
        'sgu_bias': gain(ks[22], (N_ODD, SGU_HEADS, SGU_BLOCK)),
        'sc_conv_w': nrm(ks[23], (N_ODD, SC_CONV, SC_WIDTH), SC_CONV ** -0.5),
        'cd_w_out': nrm(ks[24], (N_ODD, MIX_WIDTH, D), MIX_WIDTH ** -0.5),
        'final_norm': gain(ks[25], (D,)),
    }


def _fwd_reference(x, ffn1_norm, ffn1_w_gate, ffn1_w_up, ffn1_w_down, mix_norm,
              ffn2_norm, ffn2_w_gate, ffn2_w_up, ffn2_w_down,
              ab_w_in, pool_w, pool_scale, dn_conv_w, dn_a_log, dn_dt_bias, dn_out_norm, ab_w_out,
              cd_w_in, sgu_norm_g, sgu_norm_b, sgu_w, sgu_bias, sc_conv_w, cd_w_out,
              final_norm):
    h = x
    for layer in range(DEPTH):
        h = h + 0.5 * swiglu(rmsnorm(h, ffn1_norm[layer]), ffn1_w_gate[layer],
                             ffn1_w_up[layer], ffn1_w_down[layer])
        hn = rmsnorm(h, mix_norm[layer])
        if layer % 2 == 0:
            e = layer // 2
            h = h + mixer_ab(hn, ab_w_in[e], pool_w[e], pool_scale[e], dn_conv_w[e],
                             dn_a_log[e], dn_dt_bias[e], dn_out_norm[e], ab_w_out[e])
        else:
            o = layer // 2
            h = h + mixer_cd(hn, cd_w_in[o], sgu_norm_g[o], sgu_norm_b[o], sgu_w[o],
                             sgu_bias[o], sc_conv_w[o], cd_w_out[o])
        h = h + 0.5 * swiglu(rmsnorm(h, ffn2_norm[layer]), ffn2_w_gate[layer],
                             ffn2_w_up[layer], ffn2_w_down[layer])
    return rmsnorm(h, final_norm)


import jax as _jax
import jax.numpy as _jnp

TWIN_FORMAT = 'train_step'
FWD_PARAMS = ['x', 'ffn1_norm', 'ffn1_w_gate', 'ffn1_w_up', 'ffn1_w_down', 'mix_norm', 'ffn2_norm', 'ffn2_w_gate', 'ffn2_w_up', 'ffn2_w_down', 'ab_w_in', 'pool_w', 'pool_scale', 'dn_conv_w', 'dn_a_log', 'dn_dt_bias', 'dn_out_norm', 'ab_w_out', 'cd_w_in', 'sgu_norm_g', 'sgu_norm_b', 'sgu_w', 'sgu_bias', 'sc_conv_w', 'cd_w_out', 'final_norm']
TWIN_WEIGHTS = ['ffn1_norm', 'ffn1_w_gate', 'ffn1_w_up', 'ffn1_w_down', 'mix_norm', 'ffn2_norm', 'ffn2_w_gate', 'ffn2_w_up', 'ffn2_w_down', 'ab_w_in', 'pool_w', 'pool_scale', 'dn_conv_w', 'dn_a_log', 'dn_dt_bias', 'dn_out_norm', 'ab_w_out', 'cd_w_in', 'sgu_norm_g', 'sgu_norm_b', 'sgu_w', 'sgu_bias', 'sc_conv_w', 'cd_w_out', 'final_norm']
TWIN_DIFF_INPUT = 'x'
TWIN_INPUTS = ['x', 'ffn1_norm', 'ffn1_w_gate', 'ffn1_w_up', 'ffn1_w_down', 'mix_norm', 'ffn2_norm', 'ffn2_w_gate', 'ffn2_w_up', 'ffn2_w_down', 'ab_w_in', 'pool_w', 'pool_scale', 'dn_conv_w', 'dn_a_log', 'dn_dt_bias', 'dn_out_norm', 'ab_w_out', 'cd_w_in', 'sgu_norm_g', 'sgu_norm_b', 'sgu_w', 'sgu_bias', 'sc_conv_w', 'cd_w_out', 'final_norm', 'loss_target', 'm_ffn1_norm', 'm_ffn1_w_gate', 'm_ffn1_w_up', 'm_ffn1_w_down', 'm_mix_norm', 'm_ffn2_norm', 'm_ffn2_w_gate', 'm_ffn2_w_up', 'm_ffn2_w_down', 'm_ab_w_in', 'm_pool_w', 'm_pool_scale', 'm_dn_conv_w', 'm_dn_a_log', 'm_dn_dt_bias', 'm_dn_out_norm', 'm_ab_w_out', 'm_cd_w_in', 'm_sgu_norm_g', 'm_sgu_norm_b', 'm_sgu_w', 'm_sgu_bias', 'm_sc_conv_w', 'm_cd_w_out', 'm_final_norm', 'v_ffn1_norm', 'v_ffn1_w_gate', 'v_ffn1_w_up', 'v_ffn1_w_down', 'v_mix_norm', 'v_ffn2_norm', 'v_ffn2_w_gate', 'v_ffn2_w_up', 'v_ffn2_w_down', 'v_ab_w_in', 'v_pool_w', 'v_pool_scale', 'v_dn_conv_w', 'v_dn_a_log', 'v_dn_dt_bias', 'v_dn_out_norm', 'v_ab_w_out', 'v_cd_w_in', 'v_sgu_norm_g', 'v_sgu_norm_b', 'v_sgu_w', 'v_sgu_bias', 'v_sc_conv_w', 'v_cd_w_out', 'v_final_norm']
TWIN_OUTPUTS = ['loss', 'grad_x', 'grad_ffn1_norm', 'grad_ffn1_w_gate', 'grad_ffn1_w_up', 'grad_ffn1_w_down', 'grad_mix_norm', 'grad_ffn2_norm', 'grad_ffn2_w_gate', 'grad_ffn2_w_up', 'grad_ffn2_w_down', 'grad_ab_w_in', 'grad_pool_w', 'grad_pool_scale', 'grad_dn_conv_w', 'grad_dn_a_log', 'grad_dn_dt_bias', 'grad_dn_out_norm', 'grad_ab_w_out', 'grad_cd_w_in', 'grad_sgu_norm_g', 'grad_sgu_norm_b', 'grad_sgu_w', 'grad_sgu_bias', 'grad_sc_conv_w', 'grad_cd_w_out', 'grad_final_norm', 'delta_ffn1_norm', 'delta_ffn1_w_gate', 'delta_ffn1_w_up', 'delta_ffn1_w_down', 'delta_mix_norm', 'delta_ffn2_norm', 'delta_ffn2_w_gate', 'delta_ffn2_w_up', 'delta_ffn2_w_down', 'delta_ab_w_in', 'delta_pool_w', 'delta_pool_scale', 'delta_dn_conv_w', 'delta_dn_a_log', 'delta_dn_dt_bias', 'delta_dn_out_norm', 'delta_ab_w_out', 'delta_cd_w_in', 'delta_sgu_norm_g', 'delta_sgu_norm_b', 'delta_sgu_w', 'delta_sgu_bias', 'delta_sc_conv_w', 'delta_cd_w_out', 'delta_final_norm', 'new_m_ffn1_norm', 'new_m_ffn1_w_gate', 'new_m_ffn1_w_up', 'new_m_ffn1_w_down', 'new_m_mix_norm', 'new_m_ffn2_norm', 'new_m_ffn2_w_gate', 'new_m_ffn2_w_up', 'new_m_ffn2_w_down', 'new_m_ab_w_in', 'new_m_pool_w', 'new_m_pool_scale', 'new_m_dn_conv_w', 'new_m_dn_a_log', 'new_m_dn_dt_bias', 'new_m_dn_out_norm', 'new_m_ab_w_out', 'new_m_cd_w_in', 'new_m_sgu_norm_g', 'new_m_sgu_norm_b', 'new_m_sgu_w', 'new_m_sgu_bias', 'new_m_sc_conv_w', 'new_m_cd_w_out', 'new_m_final_norm', 'new_v_ffn1_norm', 'new_v_ffn1_w_gate', 'new_v_ffn1_w_up', 'new_v_ffn1_w_down', 'new_v_mix_norm', 'new_v_ffn2_norm', 'new_v_ffn2_w_gate', 'new_v_ffn2_w_up', 'new_v_ffn2_w_down', 'new_v_ab_w_in', 'new_v_pool_w', 'new_v_pool_scale', 'new_v_dn_conv_w', 'new_v_dn_a_log', 'new_v_dn_dt_bias', 'new_v_dn_out_norm', 'new_v_ab_w_out', 'new_v_cd_w_in', 'new_v_sgu_norm_g', 'new_v_sgu_norm_b', 'new_v_sgu_w', 'new_v_sgu_bias', 'new_v_sc_conv_w', 'new_v_cd_w_out', 'new_v_final_norm']
TWIN_LEAF_KINDS = {'loss': 'loss', 'grad_x': 'grad_x', 'grad_ffn1_norm': 'grad_w', 'grad_ffn1_w_gate': 'grad_w', 'grad_ffn1_w_up': 'grad_w', 'grad_ffn1_w_down': 'grad_w', 'grad_mix_norm': 'grad_w', 'grad_ffn2_norm': 'grad_w', 'grad_ffn2_w_gate': 'grad_w', 'grad_ffn2_w_up': 'grad_w', 'grad_ffn2_w_down': 'grad_w', 'grad_ab_w_in': 'grad_w', 'grad_pool_w': 'grad_w', 'grad_pool_scale': 'grad_w', 'grad_dn_conv_w': 'grad_w', 'grad_dn_a_log': 'grad_w', 'grad_dn_dt_bias': 'grad_w', 'grad_dn_out_norm': 'grad_w', 'grad_ab_w_out': 'grad_w', 'grad_cd_w_in': 'grad_w', 'grad_sgu_norm_g': 'grad_w', 'grad_sgu_norm_b': 'grad_w', 'grad_sgu_w': 'grad_w', 'grad_sgu_bias': 'grad_w', 'grad_sc_conv_w': 'grad_w', 'grad_cd_w_out': 'grad_w', 'grad_final_norm': 'grad_w', 'delta_ffn1_norm': 'delta_w', 'delta_ffn1_w_gate': 'delta_w', 'delta_ffn1_w_up': 'delta_w', 'delta_ffn1_w_down': 'delta_w', 'delta_mix_norm': 'delta_w', 'delta_ffn2_norm': 'delta_w', 'delta_ffn2_w_gate': 'delta_w', 'delta_ffn2_w_up': 'delta_w', 'delta_ffn2_w_down': 'delta_w', 'delta_ab_w_in': 'delta_w', 'delta_pool_w': 'delta_w', 'delta_pool_scale': 'delta_w', 'delta_dn_conv_w': 'delta_w', 'delta_dn_a_log': 'delta_w', 'delta_dn_dt_bias': 'delta_w', 'delta_dn_out_norm': 'delta_w', 'delta_ab_w_out': 'delta_w', 'delta_cd_w_in': 'delta_w', 'delta_sgu_norm_g': 'delta_w', 'delta_sgu_norm_b': 'delta_w', 'delta_sgu_w': 'delta_w', 'delta_sgu_bias': 'delta_w', 'delta_sc_conv_w': 'delta_w', 'delta_cd_w_out': 'delta_w', 'delta_final_norm': 'delta_w', 'new_m_ffn1_norm': 'new_m', 'new_m_ffn1_w_gate': 'new_m', 'new_m_ffn1_w_up': 'new_m', 'new_m_ffn1_w_down': 'new_m', 'new_m_mix_norm': 'new_m', 'new_m_ffn2_norm': 'new_m', 'new_m_ffn2_w_gate': 'new_m', 'new_m_ffn2_w_up': 'new_m', 'new_m_ffn2_w_down': 'new_m', 'new_m_ab_w_in': 'new_m', 'new_m_pool_w': 'new_m', 'new_m_pool_scale': 'new_m', 'new_m_dn_conv_w': 'new_m', 'new_m_dn_a_log': 'new_m', 'new_m_dn_dt_bias': 'new_m', 'new_m_dn_out_norm': 'new_m', 'new_m_ab_w_out': 'new_m', 'new_m_cd_w_in': 'new_m', 'new_m_sgu_norm_g': 'new_m', 'new_m_sgu_norm_b': 'new_m', 'new_m_sgu_w': 'new_m', 'new_m_sgu_bias': 'new_m', 'new_m_sc_conv_w': 'new_m', 'new_m_cd_w_out': 'new_m', 'new_m_final_norm': 'new_m', 'new_v_ffn1_norm': 'new_v', 'new_v_ffn1_w_gate': 'new_v', 'new_v_ffn1_w_up': 'new_v', 'new_v_ffn1_w_down': 'new_v', 'new_v_mix_norm': 'new_v', 'new_v_ffn2_norm': 'new_v', 'new_v_ffn2_w_gate': 'new_v', 'new_v_ffn2_w_up': 'new_v', 'new_v_ffn2_w_down': 'new_v', 'new_v_ab_w_in': 'new_v', 'new_v_pool_w': 'new_v', 'new_v_pool_scale': 'new_v', 'new_v_dn_conv_w': 'new_v', 'new_v_dn_a_log': 'new_v', 'new_v_dn_dt_bias': 'new_v', 'new_v_dn_out_norm': 'new_v', 'new_v_ab_w_out': 'new_v', 'new_v_cd_w_in': 'new_v', 'new_v_sgu_norm_g': 'new_v', 'new_v_sgu_norm_b': 'new_v', 'new_v_sgu_w': 'new_v', 'new_v_sgu_bias': 'new_v', 'new_v_sc_conv_w': 'new_v', 'new_v_cd_w_out': 'new_v', 'new_v_final_norm': 'new_v'}


def _forward(args):
    return _fwd_reference(*[args[k] for k in FWD_PARAMS])


def _output_shape():
    out = _jax.eval_shape(lambda: _forward(_fwd_setup_inputs(0)))
    return out.shape, out.dtype

N_MICROBATCH = 1
ADAM_LR = 0.001
ADAM_B1 = 0.9
ADAM_B2 = 0.999
ADAM_EPS = 1e-08
ADAM_WD = 0.01
ADAM_STEP = 10
PER_EXAMPLE_BATCH_AXIS = {'x': 0, 'loss_target': 0}
SHARED_INPUTS = []
_WEIGHT_DTYPES = {'ffn1_norm': _jnp.float32, 'ffn1_w_gate': _jnp.float32, 'ffn1_w_up': _jnp.float32, 'ffn1_w_down': _jnp.float32, 'mix_norm': _jnp.float32, 'ffn2_norm': _jnp.float32, 'ffn2_w_gate': _jnp.float32, 'ffn2_w_up': _jnp.float32, 'ffn2_w_down': _jnp.float32, 'ab_w_in': _jnp.float32, 'pool_w': _jnp.float32, 'pool_scale': _jnp.float32, 'dn_conv_w': _jnp.float32, 'dn_a_log': _jnp.float32, 'dn_dt_bias': _jnp.float32, 'dn_out_norm': _jnp.float32, 'ab_w_out': _jnp.float32, 'cd_w_in': _jnp.float32, 'sgu_norm_g': _jnp.float32, 'sgu_norm_b': _jnp.float32, 'sgu_w': _jnp.float32, 'sgu_bias': _jnp.float32, 'sc_conv_w': _jnp.float32, 'cd_w_out': _jnp.float32, 'final_norm': _jnp.float32}
MOMENT_SCALE = {'ffn1_norm': 1.237171e-01, 'ffn1_w_gate': 5.184880e-02, 'ffn1_w_up': 5.008218e-02, 'ffn1_w_down': 8.316390e-02, 'mix_norm': 2.322801e-01, 'ffn2_norm': 9.277212e-02, 'ffn2_w_gate': 3.758697e-02, 'ffn2_w_up': 3.647018e-02, 'ffn2_w_down': 6.050876e-02, 'ab_w_in': 1.453123e-01, 'pool_w': 2.207235e-01, 'pool_scale': 2.170213e-01, 'dn_conv_w': 1.111129e-01, 'dn_a_log': 7.699603e-01, 'dn_dt_bias': 7.660338e-01, 'dn_out_norm': 3.179246e-01, 'ab_w_out': 1.871271e-01, 'cd_w_in': 1.433559e-01, 'sgu_norm_g': 7.132058e-02, 'sgu_norm_b': 7.780567e-02, 'sgu_w': 7.407639e-02, 'sgu_bias': 1.085554e-01, 'sc_conv_w': 1.681377e-01, 'cd_w_out': 1.469434e-01, 'final_norm': 6.396906e+01}


def _to_microbatches(a, axis):
    t = _jnp.moveaxis(a, axis, 0)
    t = t.reshape((N_MICROBATCH, t.shape[0] // N_MICROBATCH) + t.shape[1:])
    return _jnp.moveaxis(t, 1, axis + 1)


def setup_inputs(seed: int = 0) -> dict:
    inp = _fwd_setup_inputs(seed)
    key = _jax.random.fold_in(_jax.random.key(seed), 7919)
    shape, _ = _output_shape()
    out = dict(inp)
    out["loss_target"] = _jax.random.normal(_jax.random.fold_in(key, 0), shape, _jnp.float32)
    for i, name in enumerate(TWIN_WEIGHTS):
        w = inp[name].astype(_jnp.float32)
        if MOMENT_SCALE is None:
            s = _jnp.sqrt(_jnp.mean(_jnp.square(w)) + 1e-30)
        else:
            s = MOMENT_SCALE[name]
        km, kv = _jax.random.split(_jax.random.fold_in(key, i + 1))
        out[name] = w
        out["m_" + name] = s * _jax.random.normal(km, w.shape, _jnp.float32)
        out["v_" + name] = (s * s) * _jax.random.uniform(kv, w.shape, _jnp.float32, 0.5, 1.5)
    if N_MICROBATCH > 1:
        for name, axis in PER_EXAMPLE_BATCH_AXIS.items():
            out[name] = _to_microbatches(out[name], axis)
    return {'x': out['x'], 'ffn1_norm': out['ffn1_norm'], 'ffn1_w_gate': out['ffn1_w_gate'], 'ffn1_w_up': out['ffn1_w_up'], 'ffn1_w_down': out['ffn1_w_down'], 'mix_norm': out['mix_norm'], 'ffn2_norm': out['ffn2_norm'], 'ffn2_w_gate': out['ffn2_w_gate'], 'ffn2_w_up': out['ffn2_w_up'], 'ffn2_w_down': out['ffn2_w_down'], 'ab_w_in': out['ab_w_in'], 'pool_w': out['pool_w'], 'pool_scale': out['pool_scale'], 'dn_conv_w': out['dn_conv_w'], 'dn_a_log': out['dn_a_log'], 'dn_dt_bias': out['dn_dt_bias'], 'dn_out_norm': out['dn_out_norm'], 'ab_w_out': out['ab_w_out'], 'cd_w_in': out['cd_w_in'], 'sgu_norm_g': out['sgu_norm_g'], 'sgu_norm_b': out['sgu_norm_b'], 'sgu_w': out['sgu_w'], 'sgu_bias': out['sgu_bias'], 'sc_conv_w': out['sc_conv_w'], 'cd_w_out': out['cd_w_out'], 'final_norm': out['final_norm'], 'loss_target': out['loss_target'], 'm_ffn1_norm': out['m_ffn1_norm'], 'm_ffn1_w_gate': out['m_ffn1_w_gate'], 'm_ffn1_w_up': out['m_ffn1_w_up'], 'm_ffn1_w_down': out['m_ffn1_w_down'], 'm_mix_norm': out['m_mix_norm'], 'm_ffn2_norm': out['m_ffn2_norm'], 'm_ffn2_w_gate': out['m_ffn2_w_gate'], 'm_ffn2_w_up': out['m_ffn2_w_up'], 'm_ffn2_w_down': out['m_ffn2_w_down'], 'm_ab_w_in': out['m_ab_w_in'], 'm_pool_w': out['m_pool_w'], 'm_pool_scale': out['m_pool_scale'], 'm_dn_conv_w': out['m_dn_conv_w'], 'm_dn_a_log': out['m_dn_a_log'], 'm_dn_dt_bias': out['m_dn_dt_bias'], 'm_dn_out_norm': out['m_dn_out_norm'], 'm_ab_w_out': out['m_ab_w_out'], 'm_cd_w_in': out['m_cd_w_in'], 'm_sgu_norm_g': out['m_sgu_norm_g'], 'm_sgu_norm_b': out['m_sgu_norm_b'], 'm_sgu_w': out['m_sgu_w'], 'm_sgu_bias': out['m_sgu_bias'], 'm_sc_conv_w': out['m_sc_conv_w'], 'm_cd_w_out': out['m_cd_w_out'], 'm_final_norm': out['m_final_norm'], 'v_ffn1_norm': out['v_ffn1_norm'], 'v_ffn1_w_gate': out['v_ffn1_w_gate'], 'v_ffn1_w_up': out['v_ffn1_w_up'], 'v_ffn1_w_down': out['v_ffn1_w_down'], 'v_mix_norm': out['v_mix_norm'], 'v_ffn2_norm': out['v_ffn2_norm'], 'v_ffn2_w_gate': out['v_ffn2_w_gate'], 'v_ffn2_w_up': out['v_ffn2_w_up'], 'v_ffn2_w_down': out['v_ffn2_w_down'], 'v_ab_w_in': out['v_ab_w_in'], 'v_pool_w': out['v_pool_w'], 'v_pool_scale': out['v_pool_scale'], 'v_dn_conv_w': out['v_dn_conv_w'], 'v_dn_a_log': out['v_dn_a_log'], 'v_dn_dt_bias': out['v_dn_dt_bias'], 'v_dn_out_norm': out['v_dn_out_norm'], 'v_ab_w_out': out['v_ab_w_out'], 'v_cd_w_in': out['v_cd_w_in'], 'v_sgu_norm_g': out['v_sgu_norm_g'], 'v_sgu_norm_b': out['v_sgu_norm_b'], 'v_sgu_w': out['v_sgu_w'], 'v_sgu_bias': out['v_sgu_bias'], 'v_sc_conv_w': out['v_sc_conv_w'], 'v_cd_w_out': out['v_cd_w_out'], 'v_final_norm': out['v_final_norm']}


def _loss(weights, diff, rest, loss_target):
    with _jax.named_scope("forward"):
        args = {**rest, TWIN_DIFF_INPUT: diff, **{k: w.astype(_WEIGHT_DTYPES[k]) for k, w in weights.items()}}
        y = _forward(args)
    with _jax.named_scope("loss_head"):
        err = _jnp.square(y.astype(_jnp.float32) - loss_target)
        return 0.5 * _jnp.sum(_jnp.mean(err, axis=-1)) if err.ndim else 0.5 * err


def _adamw(w, g, m, v):
    m = ADAM_B1 * m + (1.0 - ADAM_B1) * g
    v = ADAM_B2 * v + (1.0 - ADAM_B2) * _jnp.square(g)
    m_hat = m / (1.0 - ADAM_B1 ** ADAM_STEP)
    v_hat = v / (1.0 - ADAM_B2 ** ADAM_STEP)
    delta = -ADAM_LR * (m_hat / (_jnp.sqrt(v_hat) + ADAM_EPS) + ADAM_WD * w)
    return delta, m, v


def reference(x, ffn1_norm, ffn1_w_gate, ffn1_w_up, ffn1_w_down, mix_norm, ffn2_norm, ffn2_w_gate, ffn2_w_up, ffn2_w_down, ab_w_in, pool_w, pool_scale, dn_conv_w, dn_a_log, dn_dt_bias, dn_out_norm, ab_w_out, cd_w_in, sgu_norm_g, sgu_norm_b, sgu_w, sgu_bias, sc_conv_w, cd_w_out, final_norm, loss_target, m_ffn1_norm, m_ffn1_w_gate, m_ffn1_w_up, m_ffn1_w_down, m_mix_norm, m_ffn2_norm, m_ffn2_w_gate, m_ffn2_w_up, m_ffn2_w_down, m_ab_w_in, m_pool_w, m_pool_scale, m_dn_conv_w, m_dn_a_log, m_dn_dt_bias, m_dn_out_norm, m_ab_w_out, m_cd_w_in, m_sgu_norm_g, m_sgu_norm_b, m_sgu_w, m_sgu_bias, m_sc_conv_w, m_cd_w_out, m_final_norm, v_ffn1_norm, v_ffn1_w_gate, v_ffn1_w_up, v_ffn1_w_down, v_mix_norm, v_ffn2_norm, v_ffn2_w_gate, v_ffn2_w_up, v_ffn2_w_down, v_ab_w_in, v_pool_w, v_pool_scale, v_dn_conv_w, v_dn_a_log, v_dn_dt_bias, v_dn_out_norm, v_ab_w_out, v_cd_w_in, v_sgu_norm_g, v_sgu_norm_b, v_sgu_w, v_sgu_bias, v_sc_conv_w, v_cd_w_out, v_final_norm):
    given = dict(x=x, ffn1_norm=ffn1_norm, ffn1_w_gate=ffn1_w_gate, ffn1_w_up=ffn1_w_up, ffn1_w_down=ffn1_w_down, mix_norm=mix_norm, ffn2_norm=ffn2_norm, ffn2_w_gate=ffn2_w_gate, ffn2_w_up=ffn2_w_up, ffn2_w_down=ffn2_w_down, ab_w_in=ab_w_in, pool_w=pool_w, pool_scale=pool_scale, dn_conv_w=dn_conv_w, dn_a_log=dn_a_log, dn_dt_bias=dn_dt_bias, dn_out_norm=dn_out_norm, ab_w_out=ab_w_out, cd_w_in=cd_w_in, sgu_norm_g=sgu_norm_g, sgu_norm_b=sgu_norm_b, sgu_w=sgu_w, sgu_bias=sgu_bias, sc_conv_w=sc_conv_w, cd_w_out=cd_w_out, final_norm=final_norm, loss_target=loss_target, m_ffn1_norm=m_ffn1_norm, m_ffn1_w_gate=m_ffn1_w_gate, m_ffn1_w_up=m_ffn1_w_up, m_ffn1_w_down=m_ffn1_w_down, m_mix_norm=m_mix_norm, m_ffn2_norm=m_ffn2_norm, m_ffn2_w_gate=m_ffn2_w_gate, m_ffn2_w_up=m_ffn2_w_up, m_ffn2_w_down=m_ffn2_w_down, m_ab_w_in=m_ab_w_in, m_pool_w=m_pool_w, m_pool_scale=m_pool_scale, m_dn_conv_w=m_dn_conv_w, m_dn_a_log=m_dn_a_log, m_dn_dt_bias=m_dn_dt_bias, m_dn_out_norm=m_dn_out_norm, m_ab_w_out=m_ab_w_out, m_cd_w_in=m_cd_w_in, m_sgu_norm_g=m_sgu_norm_g, m_sgu_norm_b=m_sgu_norm_b, m_sgu_w=m_sgu_w, m_sgu_bias=m_sgu_bias, m_sc_conv_w=m_sc_conv_w, m_cd_w_out=m_cd_w_out, m_final_norm=m_final_norm, v_ffn1_norm=v_ffn1_norm, v_ffn1_w_gate=v_ffn1_w_gate, v_ffn1_w_up=v_ffn1_w_up, v_ffn1_w_down=v_ffn1_w_down, v_mix_norm=v_mix_norm, v_ffn2_norm=v_ffn2_norm, v_ffn2_w_gate=v_ffn2_w_gate, v_ffn2_w_up=v_ffn2_w_up, v_ffn2_w_down=v_ffn2_w_down, v_ab_w_in=v_ab_w_in, v_pool_w=v_pool_w, v_pool_scale=v_pool_scale, v_dn_conv_w=v_dn_conv_w, v_dn_a_log=v_dn_a_log, v_dn_dt_bias=v_dn_dt_bias, v_dn_out_norm=v_dn_out_norm, v_ab_w_out=v_ab_w_out, v_cd_w_in=v_cd_w_in, v_sgu_norm_g=v_sgu_norm_g, v_sgu_norm_b=v_sgu_norm_b, v_sgu_w=v_sgu_w, v_sgu_bias=v_sgu_bias, v_sc_conv_w=v_sc_conv_w, v_cd_w_out=v_cd_w_out, v_final_norm=v_final_norm)
    weights = {n: given[n] for n in TWIN_WEIGHTS}
    shared = {n: given[n] for n in SHARED_INPUTS}
    per_example = {n: given[n] for n in ['x']}
    grad_fn = _jax.value_and_grad(_loss, argnums=(0, 1))

    def one_microbatch(ex, loss_target):
        ex = dict(ex)
        diff = ex.pop(TWIN_DIFF_INPUT)
        return grad_fn(weights, diff, {**shared, **ex}, loss_target)

    if N_MICROBATCH == 1:
        loss, (grad_w, grad_x) = one_microbatch(per_example, given["loss_target"])
    else:
        def body(carry, xs):
            loss_sum, grad_sum = carry
            l_k, (gw_k, gx_k) = one_microbatch(xs[0], xs[1])
            with _jax.named_scope("update"):
                return (loss_sum + l_k, _jax.tree.map(_jnp.add, grad_sum, gw_k)), gx_k

        init = (_jnp.zeros((), _jnp.float32), _jax.tree.map(_jnp.zeros_like, weights))
        (loss, grad_w), grad_x = _jax.lax.scan(body, init, (per_example, given["loss_target"]))
    with _jax.named_scope("update"):
        delta_w, new_m, new_v = {}, {}, {}
        for n in TWIN_WEIGHTS:
            delta_w[n], new_m[n], new_v[n] = _adamw(weights[n], grad_w[n], given["m_" + n], given["v_" + n])
    return (loss, grad_x, *[grad_w[n] for n in TWIN_WEIGHTS], *[delta_w[n] for n in TWIN_WEIGHTS],
            *[new_m[n] for n in TWIN_WEIGHTS], *[new_v[n] for n in TWIN_WEIGHTS])
```

```python
import functools
import math

import jax
import jax.numpy as jnp
from jax import lax
from jax.experimental import pallas as pl
from jax.experimental.pallas import tpu as pltpu

f32, bf16 = jnp.float32, jnp.bfloat16

D_MODEL = 1024
DEPTH = 4
CHUNK = 64
POOL_WINDOWS = (2, 4, 8, 16)
HEAD = 128
N_HEADS = 4
SGU_BLOCK = 128
SGU_TILE = 512
FFN_DIM = 2816
AB_IN = 2568
AB_IN_PAD = 2688
AB_SHARD = 321
AB_SHARD_PAD = 336
EPS = 1e-6
N_DEV = 8
MESH = pl.DeviceIdType.MESH

ADAM_LR, ADAM_B1, ADAM_B2, ADAM_EPS, ADAM_WD, ADAM_STEP = 0.001, 0.9, 0.999, 1e-08, 0.01, 10

VMEM_BIG = 56 * 1024 * 1024
VMEM_MID = 40 * 1024 * 1024


def _pcall(body, **kw):
    return pl.pallas_call(body, **kw)


def _params(sem=None, vmem=None):
    return pltpu.CompilerParams(dimension_semantics=sem, vmem_limit_bytes=vmem)


def _sds(shape, dtype):
    return jax.ShapeDtypeStruct(shape, dtype)


_NN2, _NT2, _TN2 = (((1,), (0,)), ((), ())), (((1,), (1,)), ((), ())), (((0,), (0,)), ((), ()))
_NN3, _NT3, _TN3 = (((2,), (1,)), ((0,), (0,))), (((2,), (2,)), ((0,), (0,))), (((1,), (1,)), ((0,), (0,)))


def _dg(a, b, dims, hi):
    if hi:
        return lax.dot_general(a.astype(f32), b.astype(f32), dims, preferred_element_type=f32,
                               precision=lax.Precision.HIGHEST)
    return lax.dot_general(a.astype(bf16), b.astype(bf16), dims, preferred_element_type=f32)


def _make_mm(nn, nt, tn, hi):
    @jax.custom_vjp
    def mm(a, b):
        return _dg(a, b, nn, hi)

    def mm_bwd(res, ct):
        a, b = res
        return _dg(ct, b, nt, hi).astype(a.dtype), _dg(a, ct, tn, hi).astype(b.dtype)

    mm.defvjp(lambda a, b: (_dg(a, b, nn, hi), (a, b)), mm_bwd)

    @jax.custom_vjp
    def mm_nt(a, b):
        return _dg(a, b, nt, hi)

    def mm_nt_bwd(res, ct):
        a, b = res
        return _dg(ct, b, nn, hi).astype(a.dtype), _dg(ct, a, tn, hi).astype(b.dtype)

    mm_nt.defvjp(lambda a, b: (_dg(a, b, nt, hi), (a, b)), mm_nt_bwd)

    @jax.custom_vjp
    def mm_tn(a, b):
        return _dg(a, b, tn, hi)

    def mm_tn_bwd(res, ct):
        a, b = res
        return _dg(b, ct, nt, hi).astype(a.dtype), _dg(a, ct, nn, hi).astype(b.dtype)

    mm_tn.defvjp(lambda a, b: (_dg(a, b, tn, hi), (a, b)), mm_tn_bwd)
    return mm, mm_nt, mm_tn


mm, mm_nt, mm_tn = _make_mm(_NN2, _NT2, _TN2, False)
bmm, bmm_nt, bmm_tn = _make_mm(_NN3, _NT3, _TN3, False)
bmm_hi, _, _ = _make_mm(_NN3, _NT3, _TN3, True)


def _shift_raw(x, k):
    n = x.shape[0]
    t = lax.broadcasted_iota(jnp.int32, x.shape, 0)
    if k > 0:
        return jnp.where(t >= k, pltpu.roll(x, k, axis=0), 0.0)
    k = -k
    return jnp.where(t < n - k, pltpu.roll(x, n - k, axis=0), 0.0)


@functools.partial(jax.custom_vjp, nondiff_argnums=(1,))
def shift(x, k):
    return _shift_raw(x, k)


shift.defvjp(lambda x, k: (_shift_raw(x, k), None), lambda k, _, ct: (_shift_raw(ct, -k),))


def _silu(x):
    return x * jax.nn.sigmoid(x)


def _softplus(x):
    return jnp.maximum(x, 0.0) + jnp.log(1.0 + jnp.exp(-jnp.abs(x)))


def _rms_fwd(h, gain):
    rstd = lax.rsqrt(jnp.mean(h * h, axis=-1, keepdims=True) + EPS)
    xhat = h * rstd
    return xhat * gain, xhat, rstd


def _rms_bwd(dxn, xhat, rstd, gain):
    dxhat = dxn * gain
    dh = rstd * (dxhat - xhat * jnp.mean(dxhat * xhat, axis=-1, keepdims=True))
    return dh, jnp.sum(dxn * xhat, axis=0, keepdims=True)


def _acc_rows(ref, val, first):
    @pl.when(first)
    def _():
        ref[...] = jnp.zeros_like(ref)
    ref[0:1, :] += val


def _tile(n, cap):
    t = min(n, cap)
    assert n % t == 0, (n, t)
    return t


def _resident(shape):
    nd = len(shape)
    return pl.BlockSpec(shape, lambda *_: (0,) * nd, pipeline_mode=pl.Buffered(1))


def ffn_fwd(h, gain, wgt, wut, wd):
    T, D = h.shape
    F = wgt.shape[0]
    tm = _tile(T, 256)

    def body(h_ref, gain_ref, wg_ref, wu_ref, wd_ref, ho_ref, g_ref, u_ref):
        hh = h_ref[...]
        xn, _, _ = _rms_fwd(hh, gain_ref[...])
        xb = xn.astype(bf16)
        g = _dg(xb, wg_ref[...], _NT2, False)
        u = _dg(xb, wu_ref[...], _NT2, False)
        y = _dg(_silu(g) * u, wd_ref[...], _NN2, False)
        ho_ref[...] = hh + 0.5 * y
        g_ref[...] = g.astype(bf16)
        u_ref[...] = u.astype(bf16)

    row = lambda w: pl.BlockSpec((tm, w), lambda i: (i, 0))
    return _pcall(
        body, name="ffn_fwd", grid=(T // tm,),
        in_specs=[row(D), _resident((1, D)), _resident((F, D)), _resident((F, D)), _resident((F, D))],
        out_specs=[row(D), row(F), row(F)],
        out_shape=[_sds((T, D), f32), _sds((T, F), bf16), _sds((T, F), bf16)],
        compiler_params=_params(("parallel",), VMEM_BIG),
    )(h, gain, wgt, wut, wd)


def ffn_bwd(h, gain, g, u, dout, wgt, wut, wd):
    T, D = h.shape
    F = wgt.shape[0]
    tm = _tile(T, 256)

    def body(h_ref, gain_ref, g_ref, u_ref, do_ref, wg_ref, wu_ref, wd_ref,
             dh_ref, dgain_ref, xn_ref, act_ref, dg_ref, du_ref, dy_ref):
        hh, dout_ = h_ref[...], do_ref[...]
        gain_ = gain_ref[...]
        xn, xhat, rstd = _rms_fwd(hh, gain_)
        gg, uu = g_ref[...].astype(f32), u_ref[...].astype(f32)
        dy = (0.5 * dout_).astype(bf16)
        dact = _dg(dy, wd_ref[...], _NT2, False)
        sg = jax.nn.sigmoid(gg)
        silu = gg * sg
        dgate = (dact * uu * (sg * (1.0 + gg * (1.0 - sg)))).astype(bf16)
        dup = (dact * silu).astype(bf16)
        dxn = _dg(dgate, wg_ref[...], _NN2, False) + _dg(dup, wu_ref[...], _NN2, False)
        dh, dgain = _rms_bwd(dxn, xhat, rstd, gain_)
        dh_ref[...] = dout_ + dh
        _acc_rows(dgain_ref, dgain, pl.program_id(0) == 0)
        xn_ref[...] = xn.astype(bf16)
        act_ref[...] = (silu * uu).astype(bf16)
        dg_ref[...] = dgate
        du_ref[...] = dup
        dy_ref[...] = dy

    row = lambda w: pl.BlockSpec((tm, w), lambda i: (i, 0))
    return _pcall(
        body, name="ffn_bwd", grid=(T // tm,),
        in_specs=[row(D), _resident((1, D)), row(F), row(F), row(D),
                  _resident((F, D)), _resident((F, D)), _resident((F, D))],
        out_specs=[row(D), pl.BlockSpec((8, D), lambda i: (0, 0)), row(D), row(F), row(F), row(F), row(D)],
        out_shape=[_sds((T, D), f32), _sds((8, D), f32), _sds((T, D), bf16), _sds((T, F), bf16),
                   _sds((T, F), bf16), _sds((T, F), bf16), _sds((T, D), bf16)],
        compiler_params=_params(("arbitrary",), VMEM_BIG),
    )(h, gain, g, u, dout, wgt, wut, wd)


def _col_tile(n, cap=1408):
    best = None
    for c in range(128, cap + 1, 128):
        if n % c == 0:
            best = c
    assert best is not None, n
    return best


def wgrad(a, b):
    T, N = a.shape
    K = b.shape[1]
    nc, tk = _col_tile(N), _tile(T, 512)
    nk = T // tk

    def body(a_ref, b_ref, o_ref, acc_ref):
        k = pl.program_id(1)

        @pl.when(k == 0)
        def _():
            acc_ref[...] = jnp.zeros_like(acc_ref)

        acc_ref[...] += _dg(a_ref[...], b_ref[...], _TN2, False)

        @pl.when(k == nk - 1)
        def _():
            o_ref[...] = acc_ref[...].astype(bf16)

    return _pcall(
        body, name="wgrad", grid=(N // nc, nk),
        in_specs=[pl.BlockSpec((tk, nc), lambda j, k: (k, j)), pl.BlockSpec((tk, K), lambda j, k: (k, 0))],
        out_specs=pl.BlockSpec((nc, K), lambda j, k: (j, 0)),
        out_shape=_sds((N, K), bf16),
        scratch_shapes=[pltpu.VMEM((nc, K), f32)],
        compiler_params=_params(("parallel", "arbitrary"), VMEM_MID),
    )(a, b)


def in_proj_fwd(h, gain, wt, widths):
    T, D = h.shape
    N = wt.shape[0]
    assert sum(widths) == N
    tm = _tile(T, 512)
    offs = [sum(widths[:i]) for i in range(len(widths))]

    def body(h_ref, gain_ref, w_ref, *outs):
        xn, _, _ = _rms_fwd(h_ref[...], gain_ref[...])
        p = _dg(xn, w_ref[...], _NT2, False)
        for o_ref, off, wd_ in zip(outs, offs, widths):
            o_ref[...] = p[:, off:off + wd_]

    row = lambda w: pl.BlockSpec((tm, w), lambda i: (i, 0))
    return _pcall(
        body, name="in_proj_fwd", grid=(T // tm,),
        in_specs=[row(D), _resident((1, D)), _resident((N, D))],
        out_specs=[row(w) for w in widths],
        out_shape=[_sds((T, w), f32) for w in widths],
        compiler_params=_params(("parallel",), VMEM_BIG),
    )(h, gain, wt)


def in_proj_bwd(h, gain, dpieces, dout, wt):
    T, D = h.shape
    N = wt.shape[0]
    widths = [p.shape[1] for p in dpieces]
    assert sum(widths) == N
    tm = _tile(T, 256)
    nt = T // tm
    npc = len(dpieces)

    def body(*refs):
        h_ref, gain_ref = refs[0], refs[1]
        p_refs = refs[2:2 + npc]
        do_ref, w_ref, dh_ref, dgain_ref, dw_ref, acc_ref = refs[2 + npc:]
        i = pl.program_id(0)
        gain_ = gain_ref[...]
        xn, xhat, rstd = _rms_fwd(h_ref[...], gain_)
        dp = jnp.concatenate([r[...].astype(bf16) for r in p_refs], axis=-1)
        dxn = _dg(dp, w_ref[...], _NN2, False)
        dh, dgain = _rms_bwd(dxn, xhat, rstd, gain_)
        dh_ref[...] = do_ref[...] + dh
        _acc_rows(dgain_ref, dgain, i == 0)

        @pl.when(i == 0)
        def _():
            acc_ref[...] = jnp.zeros_like(acc_ref)

        acc_ref[...] += _dg(dp, xn, _TN2, False)

        @pl.when(i == nt - 1)
        def _():
            dw_ref[...] = acc_ref[...].astype(bf16)

    row = lambda w: pl.BlockSpec((tm, w), lambda i: (i, 0))
    return _pcall(
        body, name="in_proj_bwd", grid=(nt,),
        in_specs=[row(D), _resident((1, D))] + [row(w) for w in widths] + [row(D), _resident((N, D))],
        out_specs=[row(D), pl.BlockSpec((8, D), lambda i: (0, 0)), pl.BlockSpec((N, D), lambda i: (0, 0))],
        out_shape=[_sds((T, D), f32), _sds((8, D), f32), _sds((N, D), bf16)],
        scratch_shapes=[pltpu.VMEM((N, D), f32)],
        compiler_params=_params(("arbitrary",), VMEM_BIG),
    )(h, gain, *dpieces, dout, wt)


def out_proj_fwd(h, ya, yb, w):
    T, D = h.shape
    half = ya.shape[1]
    tm = _tile(T, 512)

    def body(h_ref, ya_ref, yb_ref, w_ref, o_ref):
        y = jnp.concatenate([ya_ref[...], yb_ref[...]], axis=-1)
        o_ref[...] = h_ref[...] + _dg(y, w_ref[...], _NN2, False)

    row = lambda w_: pl.BlockSpec((tm, w_), lambda i: (i, 0))
    return _pcall(
        body, name="out_proj_fwd", grid=(T // tm,),
        in_specs=[row(D), row(half), row(half), _resident((2 * half, D))],
        out_specs=row(D), out_shape=_sds((T, D), f32),
        compiler_params=_params(("parallel",), VMEM_MID),
    )(h, ya, yb, w)


def out_proj_bwd(dout, ya, yb, w):
    T, D = dout.shape
    half = ya.shape[1]
    tm = _tile(T, 512)
    nt = T // tm

    def body(do_ref, ya_ref, yb_ref, w_ref, dya_ref, dyb_ref, dw_ref, acc_ref):
        i = pl.program_id(0)
        dob = do_ref[...].astype(bf16)
        dy = _dg(dob, w_ref[...], _NT2, False)
        dya_ref[...] = dy[:, :half]
        dyb_ref[...] = dy[:, half:]

        @pl.when(i == 0)
        def _():
            acc_ref[...] = jnp.zeros_like(acc_ref)

        y = jnp.concatenate([ya_ref[...], yb_ref[...]], axis=-1)
        acc_ref[...] += _dg(y, dob, _TN2, False)

        @pl.when(i == nt - 1)
        def _():
            dw_ref[...] = acc_ref[...].astype(bf16)

    row = lambda w_: pl.BlockSpec((tm, w_), lambda i: (i, 0))
    return _pcall(
        body, name="out_proj_bwd", grid=(nt,),
        in_specs=[row(D), row(half), row(half), _resident((2 * half, D))],
        out_specs=[row(half), row(half), pl.BlockSpec((2 * half, D), lambda i: (0, 0))],
        out_shape=[_sds((T, half), f32), _sds((T, half), f32), _sds((2 * half, D), bf16)],
        scratch_shapes=[pltpu.VMEM((2 * half, D), f32)],
        compiler_params=_params(("arbitrary",), VMEM_MID),
    )(dout, ya, yb, w)


def _conv_taps(x, w):
    K = w.shape[0]
    acc = x * w[K - 1:K, :]
    for i in range(K - 1):
        acc = acc + shift(x, K - 1 - i) * w[i:i + 1, :]
    return acc


def conv_silu_math(w, x):
    return _silu(_conv_taps(x, w))


def gated_conv_math(w, xd, bg, cg):
    return bg * _conv_taps(cg * xd, w)


def seq_chan_fwd(math, w, xs, seq, out_dtype, name):
    T, C = xs[0].shape
    K = w.shape[0]
    nb, nc = T // seq, C // HEAD

    def body(w_ref, *refs):
        o_ref = refs[-1]
        o_ref[...] = math(w_ref[...], *[r[...] for r in refs[:-1]]).astype(out_dtype)

    blk = pl.BlockSpec((seq, HEAD), lambda j, b: (b, j))
    return _pcall(
        body, name=name, grid=(nc, nb),
        in_specs=[pl.BlockSpec((K, HEAD), lambda j, b: (0, j))] + [blk] * len(xs),
        out_specs=blk, out_shape=_sds((T, C), out_dtype),
        compiler_params=_params(("parallel", "parallel"), VMEM_MID),
    )(w, *xs)


def seq_chan_bwd(math, w, xs, dy, seq, dx_dtype, name):
    T, C = xs[0].shape
    K = w.shape[0]
    nb, nc = T // seq, C // HEAD
    nx = len(xs)

    def body(w_ref, *refs):
        x_refs, dy_ref = refs[:nx], refs[nx]
        dx_refs, dw_ref = refs[nx + 1:2 * nx + 1], refs[2 * nx + 1]
        _, vjp = jax.vjp(math, w_ref[...], *[r[...] for r in x_refs])
        grads = vjp(dy_ref[...].astype(f32))

        @pl.when(pl.program_id(1) == 0)
        def _():
            dw_ref[...] = jnp.zeros_like(dw_ref)

        dw_ref[...] += grads[0]
        for r, gx in zip(dx_refs, grads[1:]):
            r[...] = gx.astype(dx_dtype)

    blk = pl.BlockSpec((seq, HEAD), lambda j, b: (b, j))
    wblk = pl.BlockSpec((K, HEAD), lambda j, b: (0, j))
    return _pcall(
        body, name=name, grid=(nc, nb),
        in_specs=[wblk] + [blk] * (nx + 1),
        out_specs=[blk] * nx + [wblk],
        out_shape=[_sds((T, C), dx_dtype)] * nx + [_sds((K, C), f32)],
        compiler_params=_params(("parallel", "arbitrary"), VMEM_MID),
    )(w, *xs, dy)


def pool_group_math(win, ag, pw, scale):
    t = lax.broadcasted_iota(jnp.int32, (ag.shape[0], 1), 0)
    s, k = ag, 1
    while k < win:
        s = s + shift(s, k)
        k *= 2
    pooled = s / jnp.minimum(t + 1, win).astype(f32) - ag
    return mm(pooled, pw) * scale


def pool_fwd(a, pw, scale, seq):
    T, C = a.shape

    def body(a_ref, pw_ref, sc_ref, o_ref):
        for gi, win in enumerate(POOL_WINDOWS):
            cols = slice(gi * HEAD, (gi + 1) * HEAD)
            o_ref[:, cols] = pool_group_math(win, a_ref[:, cols], pw_ref[gi], sc_ref[:, cols]).astype(bf16)

    blk = pl.BlockSpec((seq, C), lambda b: (b, 0))
    return _pcall(
        body, name="pool_fwd", grid=(T // seq,),
        in_specs=[blk, _resident(pw.shape), _resident((1, C))],
        out_specs=blk, out_shape=_sds((T, C), bf16),
        compiler_params=_params(("parallel",), VMEM_MID),
    )(a, pw, scale)


def pool_bwd(a, pw, scale, dy, seq):
    T, C = a.shape

    def body(a_ref, pw_ref, sc_ref, dy_ref, da_ref, dpw_ref, dsc_ref):
        first = pl.program_id(0) == 0

        @pl.when(first)
        def _():
            dpw_ref[...] = jnp.zeros_like(dpw_ref)
            dsc_ref[...] = jnp.zeros_like(dsc_ref)

        for gi, win in enumerate(POOL_WINDOWS):
            cols = slice(gi * HEAD, (gi + 1) * HEAD)
            _, vjp = jax.vjp(functools.partial(pool_group_math, win), a_ref[:, cols], pw_ref[gi], sc_ref[:, cols])
            da, dpw, dsc = vjp(dy_ref[:, cols])
            dpw_ref[gi] += dpw
            dsc_ref[0:1, cols] += dsc
            da_ref[:, cols] = da.astype(bf16)

    blk = pl.BlockSpec((seq, C), lambda b: (b, 0))
    return _pcall(
        body, name="pool_bwd", grid=(T // seq,),
        in_specs=[blk, _resident(pw.shape), _resident((1, C)), blk],
        out_specs=[blk, pl.BlockSpec(pw.shape, lambda b: (0, 0, 0)), pl.BlockSpec((8, C), lambda b: (0, 0))],
        out_shape=[_sds((T, C), bf16), _sds(pw.shape, f32), _sds((8, C), f32)],
        compiler_params=_params(("arbitrary",), VMEM_MID),
    )(a, pw, scale, dy)


def dn_prep_math(qkv, bg, alog, dtb):
    tt = qkv.shape[0]
    nt = tt // CHUNK
    nb = nt * N_HEADS
    W = N_HEADS * HEAD
    beta_all = jax.nn.sigmoid(bg)
    g_all = -jnp.exp(alog) * _softplus(bg + dtb)

    def heads(fn):
        return jnp.stack([fn(hd).reshape(nt, CHUNK, HEAD) for hd in range(N_HEADS)], axis=1).reshape(nb, CHUNK, HEAD)

    def l2n(x):
        return x * lax.rsqrt(jnp.sum(x * x, axis=-1, keepdims=True) + EPS)

    q = heads(lambda hd: l2n(qkv[:, hd * HEAD:(hd + 1) * HEAD]) * (HEAD ** -0.5))
    k = heads(lambda hd: l2n(qkv[:, W + hd * HEAD:W + (hd + 1) * HEAD]))
    v = heads(lambda hd: qkv[:, 2 * W + hd * HEAD:2 * W + (hd + 1) * HEAD])
    beta = heads(lambda hd: jnp.broadcast_to(beta_all[:, hd:hd + 1], (tt, HEAD)))
    g = heads(lambda hd: jnp.broadcast_to(g_all[:, N_HEADS + hd:N_HEADS + hd + 1], (tt, HEAD)))

    ii = lax.broadcasted_iota(jnp.int32, (CHUNK, CHUNK), 0)
    jj = lax.broadcasted_iota(jnp.int32, (CHUNK, CHUNK), 1)
    tril, strict = (ii >= jj)[None], (ii > jj)[None]
    ones_b = jnp.ones((nb, CHUNK, CHUNK), f32)
    tril_b = jnp.where(tril, ones_b, 0.0)
    eye_b = jnp.where((ii == jj)[None], ones_b, 0.0)

    gcb = bmm_hi(tril_b, g)
    gcol = gcb[:, :, :CHUNK]
    grow = bmm_hi(ones_b, eye_b * gcol)
    gamma = jnp.where(tril, jnp.exp(jnp.where(tril, gcol - grow, 0.0)), 0.0)
    kb = k * beta
    lmat = jnp.where(strict, bmm_nt(kb, k) * gamma, 0.0)
    inv = eye_b - lmat
    pw_ = bmm_hi(lmat, lmat)
    for i in range(5):
        inv = inv + bmm_hi(inv, pw_)
        if i < 4:
            pw_ = bmm_hi(pw_, pw_)
    egc = jnp.exp(gcb)
    u = bmm_hi(inv, v * beta)
    w = bmm_hi(inv, kb * egc)
    aqk = bmm_nt(q, k) * gamma
    qd = q * egc
    glast = gcb[:, CHUNK - 1:CHUNK, :]
    kd = k * jnp.exp(glast - gcb)
    last = jnp.exp(glast)
    r4 = lambda x: x.reshape((nt, N_HEADS) + x.shape[1:])
    return r4(u), r4(w), r4(qd), r4(kd), r4(aqk), r4(last)


_PREP_DTYPES = (f32, bf16, bf16, bf16, bf16, f32)


def _prep_specs(nt, T):
    nchunks = T // CHUNK
    shapes = [(HEAD,), (HEAD,), (HEAD,), (HEAD,), (CHUNK,), (HEAD,)]
    rows = [CHUNK, CHUNK, CHUNK, CHUNK, CHUNK, 1]
    specs = [pl.BlockSpec((nt, N_HEADS, r, s[0]), lambda i: (i, 0, 0, 0)) for r, s in zip(rows, shapes)]
    outs = [(nchunks, N_HEADS, r, s[0]) for r, s in zip(rows, shapes)]
    return specs, outs


def dn_prep_fwd(qkv, bg, alog, dtb):
    T = qkv.shape[0]
    tt = _tile(T, 256)
    nt = tt // CHUNK
    specs, shapes = _prep_specs(nt, T)

    def body(qkv_ref, bg_ref, alog_ref, dtb_ref, *outs):
        res = dn_prep_math(qkv_ref[...], bg_ref[...], alog_ref[...], dtb_ref[...])
        for o_ref, r, dt in zip(outs, res, _PREP_DTYPES):
            o_ref[...] = r.astype(dt)

    row = lambda w: pl.BlockSpec((tt, w), lambda i: (i, 0))
    return _pcall(
        body, name="dn_prep_fwd", grid=(T // tt,),
        in_specs=[row(qkv.shape[1]), row(HEAD), _resident((1, HEAD)), _resident((1, HEAD))],
        out_specs=specs, out_shape=[_sds(s, dt) for s, dt in zip(shapes, _PREP_DTYPES)],
        compiler_params=_params(("parallel",), VMEM_BIG),
    )(qkv, bg, alog, dtb)


def dn_prep_bwd(qkv, bg, alog, dtb, cts):
    T = qkv.shape[0]
    tt = _tile(T, 256)
    nt = tt // CHUNK
    specs, _ = _prep_specs(nt, T)

    def body(qkv_ref, bg_ref, alog_ref, dtb_ref, *refs):
        ct_refs, (dqkv_ref, dbg_ref, dalog_ref, ddtb_ref) = refs[:6], refs[6:]
        _, vjp = jax.vjp(dn_prep_math, qkv_ref[...], bg_ref[...], alog_ref[...], dtb_ref[...])
        dqkv, dbg, dalog, ddtb = vjp(tuple(r[...].astype(f32) for r in ct_refs))
        dqkv_ref[...] = dqkv
        dbg_ref[...] = dbg.astype(bf16)
        first = pl.program_id(0) == 0
        _acc_rows(dalog_ref, dalog, first)
        _acc_rows(ddtb_ref, ddtb, first)

    row = lambda w: pl.BlockSpec((tt, w), lambda i: (i, 0))
    small = pl.BlockSpec((8, HEAD), lambda i: (0, 0))
    return _pcall(
        body, name="dn_prep_bwd", grid=(T // tt,),
        in_specs=[row(qkv.shape[1]), row(HEAD), _resident((1, HEAD)), _resident((1, HEAD))] + specs,
        out_specs=[row(qkv.shape[1]), row(HEAD), small, small],
        out_shape=[_sds(qkv.shape, f32), _sds((T, HEAD), bf16), _sds((8, HEAD), f32), _sds((8, HEAD), f32)],
        compiler_params=_params(("arbitrary",), VMEM_BIG),
    )(qkv, bg, alog, dtb, *cts)


def dn_step(state, u, w, qd, kd, aqk, last):
    v_new = u - bmm(w, state)
    o = bmm(qd, state) + bmm(aqk, v_new)
    return state * last + bmm_tn(kd, v_new), o


def dn_gate(o, z, onorm):
    return o * lax.rsqrt(jnp.mean(o * o, axis=-1, keepdims=True) + EPS) * onorm * _silu(z)


def _recur_specs(n):
    rows = [CHUNK, CHUNK, CHUNK, CHUNK, CHUNK, 1]
    lanes = [HEAD, HEAD, HEAD, HEAD, CHUNK, HEAD]
    return [pl.BlockSpec((n, 1, r, l), lambda b, hd: (b, hd, 0, 0)) for r, l in zip(rows, lanes)]


def dn_recur_fwd(prep, z, onorm, seq):
    T, C = z.shape
    n = seq // CHUNK

    def body(u_ref, w_ref, qd_ref, kd_ref, aqk_ref, last_ref, z_ref, on_ref, y_ref, o_ref):
        def step(i, state):
            new, o = dn_step(state, u_ref[i], w_ref[i], qd_ref[i], kd_ref[i], aqk_ref[i], last_ref[i])
            o_ref[i] = o
            return new

        lax.fori_loop(0, n, step, jnp.zeros((1, HEAD, HEAD), f32))
        y_ref[...] = dn_gate(o_ref[...].reshape(seq, HEAD), z_ref[...], on_ref[...]).astype(bf16)

    blk = pl.BlockSpec((seq, HEAD), lambda b, hd: (b, hd))
    return _pcall(
        body, name="dn_recur_fwd", grid=(T // seq, N_HEADS),
        in_specs=_recur_specs(n) + [blk, pl.BlockSpec((1, HEAD), lambda b, hd: (0, 0))],
        out_specs=blk, out_shape=_sds((T, C), bf16),
        scratch_shapes=[pltpu.VMEM((n, 1, CHUNK, HEAD), f32)],
        compiler_params=_params(("parallel", "parallel"), VMEM_MID),
    )(*prep, z, onorm)


def dn_recur_bwd(prep, z, onorm, dy, seq):
    T, C = z.shape
    n = seq // CHUNK
    nchunks = T // CHUNK

    def body(u_ref, w_ref, qd_ref, kd_ref, aqk_ref, last_ref, z_ref, on_ref, dy_ref,
             du_ref, dw_ref, dqd_ref, dkd_ref, daqk_ref, dlast_ref, dz_ref, don_ref, st_ref, o_ref):
        args = lambda i: tuple(r[i].astype(f32) for r in (u_ref, w_ref, qd_ref, kd_ref, aqk_ref, last_ref))

        def fstep(i, state):
            st_ref[i] = state
            new, o = dn_step(state, *args(i))
            o_ref[i] = o
            return new

        lax.fori_loop(0, n, fstep, jnp.zeros((1, HEAD, HEAD), f32))
        _, gate_vjp = jax.vjp(dn_gate, o_ref[...].reshape(seq, HEAD), z_ref[...], on_ref[...])
        do, dz, don = gate_vjp(dy_ref[...])
        dz_ref[...] = dz.astype(bf16)
        o_ref[...] = do.reshape(n, 1, CHUNK, HEAD)

        def bstep(j, dstate):
            i = n - 1 - j
            _, vjp = jax.vjp(dn_step, st_ref[i], *args(i))
            ds, du, dw, dqd, dkd, daqk, dlast = vjp((dstate, o_ref[i]))
            du_ref[i], dw_ref[i], dqd_ref[i], dkd_ref[i], daqk_ref[i], dlast_ref[i] = du, dw, dqd, dkd, daqk, dlast
            return ds

        lax.fori_loop(0, n, bstep, jnp.zeros((1, HEAD, HEAD), f32))
        first = jnp.logical_and(pl.program_id(0) == 0, pl.program_id(1) == 0)
        _acc_rows(don_ref, don, first)

    blk = pl.BlockSpec((seq, HEAD), lambda b, hd: (b, hd))
    rows = [CHUNK, CHUNK, CHUNK, CHUNK, CHUNK, 1]
    lanes = [HEAD, HEAD, HEAD, HEAD, CHUNK, HEAD]
    return _pcall(
        body, name="dn_recur_bwd", grid=(T // seq, N_HEADS),
        in_specs=_recur_specs(n) + [blk, pl.BlockSpec((1, HEAD), lambda b, hd: (0, 0)), blk],
        out_specs=_recur_specs(n) + [blk, pl.BlockSpec((8, HEAD), lambda b, hd: (0, 0))],
        out_shape=[_sds((nchunks, N_HEADS, r, l), f32) for r, l in zip(rows, lanes)]
        + [_sds((T, C), bf16), _sds((8, HEAD), f32)],
        scratch_shapes=[pltpu.VMEM((n, 1, HEAD, HEAD), f32), pltpu.VMEM((n, 1, CHUNK, HEAD), f32)],
        compiler_params=_params(("arbitrary", "arbitrary"), VMEM_MID),
    )(*prep, z, onorm, dy)


def sgu_math(up, vp, ng, nb, sw, sbias):
    S = up.shape[0]
    nblk = S // SGU_BLOCK
    u = jax.nn.gelu(up, approximate=True)
    v = jax.nn.gelu(vp, approximate=True)
    xc = v - jnp.mean(v, axis=-1, keepdims=True)
    vn = xc * lax.rsqrt(jnp.mean(xc * xc, axis=-1, keepdims=True) + EPS) * ng + nb
    ii = lax.broadcasted_iota(jnp.int32, (SGU_BLOCK, SGU_BLOCK), 0)
    jj = lax.broadcasted_iota(jnp.int32, (SGU_BLOCK, SGU_BLOCK), 1)
    outs = []
    for hd in range(N_HEADS):
        vh = vn[:, hd * HEAD:(hd + 1) * HEAD].reshape(nblk, SGU_BLOCK, HEAD)
        ws = jnp.where(ii >= jj, sw[hd], 0.0)
        mixed = bmm(jnp.broadcast_to(ws[None], (nblk, SGU_BLOCK, SGU_BLOCK)), vh) + sbias[hd][None]
        outs.append(mixed.reshape(S, HEAD))
    return u * jnp.concatenate(outs, axis=-1)


def sgu_fwd(up, vp, ng, nb, sw, sbias, seq):
    T, C = up.shape

    def body(up_ref, vp_ref, ng_ref, nb_ref, sw_ref, sb_ref, o_ref):
        o_ref[...] = sgu_math(up_ref[...], vp_ref[...], ng_ref[...], nb_ref[...], sw_ref[...],
                              sb_ref[...]).astype(bf16)

    blk = pl.BlockSpec((seq, C), lambda b: (b, 0))
    return _pcall(
        body, name="sgu_fwd", grid=(T // seq,),
        in_specs=[blk, blk, _resident((1, C)), _resident((1, C)), _resident(sw.shape), _resident(sbias.shape)],
        out_specs=blk, out_shape=_sds((T, C), bf16),
        compiler_params=_params(("parallel",), VMEM_BIG),
    )(up, vp, ng, nb, sw, sbias)


def sgu_bwd(up, vp, ng, nb, sw, sbias, dy, seq):
    T, C = up.shape

    def body(up_ref, vp_ref, ng_ref, nb_ref, sw_ref, sb_ref, dy_ref,
             dup_ref, dvp_ref, dng_ref, dnb_ref, dsw_ref, dsb_ref):
        _, vjp = jax.vjp(sgu_math, up_ref[...], vp_ref[...], ng_ref[...], nb_ref[...], sw_ref[...], sb_ref[...])
        dup, dvp, dng, dnb, dsw, dsb = vjp(dy_ref[...])
        dup_ref[...] = dup.astype(bf16)
        dvp_ref[...] = dvp.astype(bf16)
        first = pl.program_id(0) == 0
        _acc_rows(dng_ref, dng, first)
        _acc_rows(dnb_ref, dnb, first)

        @pl.when(first)
        def _():
            dsw_ref[...] = jnp.zeros_like(dsw_ref)
            dsb_ref[...] = jnp.zeros_like(dsb_ref)

        dsw_ref[...] += dsw
        dsb_ref[...] += dsb

    blk = pl.BlockSpec((seq, C), lambda b: (b, 0))
    small = pl.BlockSpec((8, C), lambda b: (0, 0))
    return _pcall(
        body, name="sgu_bwd", grid=(T // seq,),
        in_specs=[blk, blk, _resident((1, C)), _resident((1, C)), _resident(sw.shape), _resident(sbias.shape), blk],
        out_specs=[blk, blk, small, small, pl.BlockSpec(sw.shape, lambda b: (0, 0, 0)),
                   pl.BlockSpec(sbias.shape, lambda b: (0, 0, 0))],
        out_shape=[_sds((T, C), bf16), _sds((T, C), bf16), _sds((8, C), f32), _sds((8, C), f32),
                   _sds(sw.shape, f32), _sds(sbias.shape, f32)],
        compiler_params=_params(("arbitrary",), VMEM_BIG),
    )(up, vp, ng, nb, sw, sbias, dy)


def loss_fwd_bwd(h, gain, target):
    T, D = h.shape
    tm = _tile(T, 512)

    def body(h_ref, gain_ref, t_ref, loss_ref, dh_ref, dgain_ref):
        gain_ = gain_ref[...]
        y, xhat, rstd = _rms_fwd(h_ref[...], gain_)
        err = y - t_ref[...]
        part = 0.5 * jnp.sum(jnp.mean(err * err, axis=-1, keepdims=True), axis=0, keepdims=True)
        dh, dgain = _rms_bwd(err * (1.0 / D), xhat, rstd, gain_)
        dh_ref[...] = dh
        first = pl.program_id(0) == 0
        _acc_rows(dgain_ref, dgain, first)

        @pl.when(first)
        def _():
            loss_ref[...] = jnp.zeros_like(loss_ref)

        loss_ref[...] += jnp.broadcast_to(part, loss_ref.shape)

    row = pl.BlockSpec((tm, D), lambda i: (i, 0))
    return _pcall(
        body, name="loss_fwd_bwd", grid=(T // tm,),
        in_specs=[row, _resident((1, D)), row],
        out_specs=[pl.BlockSpec((8, 128), lambda i: (0, 0)), row, pl.BlockSpec((8, D), lambda i: (0, 0))],
        out_shape=[_sds((8, 128), f32), _sds((T, D), f32), _sds((8, D), f32)],
        compiler_params=_params(("arbitrary",), VMEM_MID),
    )(h, gain, target)


def adamw(w, g, m, v):
    R, C = w.shape
    tr = max(c for c in range(8, 513, 8) if R % c == 0)
    c1, c2 = 1.0 - ADAM_B1 ** ADAM_STEP, 1.0 - ADAM_B2 ** ADAM_STEP

    def body(w_ref, g_ref, m_ref, v_ref, d_ref, nm_ref, nv_ref):
        gg = g_ref[...]
        nm = ADAM_B1 * m_ref[...] + (1.0 - ADAM_B1) * gg
        nv = ADAM_B2 * v_ref[...] + (1.0 - ADAM_B2) * (gg * gg)
        d_ref[...] = -ADAM_LR * ((nm / c1) / (jnp.sqrt(nv / c2) + ADAM_EPS) + ADAM_WD * w_ref[...])
        nm_ref[...] = nm
        nv_ref[...] = nv

    blk = pl.BlockSpec((tr, C), lambda i: (i, 0))
    return _pcall(
        body, name="adamw", grid=(R // tr,), in_specs=[blk] * 4, out_specs=[blk] * 3,
        out_shape=[_sds((R, C), f32)] * 3, compiler_params=_params(("parallel",), VMEM_MID),
    )(w, g, m, v)


def sum8(parts):
    _, R, C = parts.shape
    tr = R
    for cand in (512, 352, 336, 320, 256, 128):
        if R % cand == 0:
            tr = cand
            break

    def body(p_ref, o_ref):
        acc = p_ref[0].astype(f32)
        for i in range(1, N_DEV):
            acc = acc + p_ref[i].astype(f32)
        o_ref[...] = acc

    return _pcall(
        body, name="sum8", grid=(R // tr,),
        in_specs=[pl.BlockSpec((N_DEV, tr, C), lambda i: (0, i, 0))],
        out_specs=pl.BlockSpec((tr, C), lambda i: (i, 0)), out_shape=_sds((R, C), f32),
        compiler_params=_params(("parallel",), VMEM_MID),
    )(parts)


_FLIPS = [(fx, fy, fc) for fx in (0, 1) for fy in (0, 1) for fc in (0, 1)][1:]
_HBM = pl.BlockSpec(memory_space=pltpu.HBM)


def _me():
    return lax.axis_index("x"), lax.axis_index("y"), lax.axis_index("c")


def _peer(flip):
    x, y, c = _me()
    fx, fy, fc = flip
    return (1 - x if fx else x, 1 - y if fy else y, 1 - c if fc else c)


def _lin(dev):
    return 4 * dev[0] + 2 * dev[1] + dev[2]


def all_gather_rows(shards, name):
    n = len(shards)

    def body(*refs):
        ins, outs = refs[:n], refs[n:2 * n]
        send_sems, recv_sems, local_sems = refs[2 * n:]
        me = _lin(_me())
        locals_, sends = [], []
        for i in range(n):
            cp = pltpu.make_async_copy(ins[i], outs[i].at[me], local_sems.at[i])
            cp.start()
            locals_.append(cp)
            for k, flip in enumerate(_FLIPS):
                rc = pltpu.make_async_remote_copy(
                    src_ref=ins[i], dst_ref=outs[i].at[me], send_sem=send_sems.at[i * 7 + k],
                    recv_sem=recv_sems.at[i * 7 + k], device_id=_peer(flip), device_id_type=MESH)
                rc.start()
                sends.append(rc)
        for i in range(n):
            for k, flip in enumerate(_FLIPS):
                pltpu.make_async_remote_copy(
                    src_ref=ins[i], dst_ref=outs[i].at[_lin(_peer(flip))], send_sem=send_sems.at[i * 7 + k],
                    recv_sem=recv_sems.at[i * 7 + k], device_id=_peer(flip), device_id_type=MESH).wait_recv()
        for rc in sends:
            rc.wait_send()
        for cp in locals_:
            cp.wait()

    return _pcall(
        body, name=name, in_specs=[_HBM] * n, out_specs=[_HBM] * n,
        out_shape=[_sds((N_DEV,) + s.shape, s.dtype) for s in shards],
        scratch_shapes=[pltpu.SemaphoreType.DMA((7 * n,)), pltpu.SemaphoreType.DMA((7 * n,)),
                        pltpu.SemaphoreType.DMA((n,))],
    )(*shards)


def scatter_rows(fulls, name):
    n = len(fulls)
    rows = [f.shape[0] // N_DEV for f in fulls]

    def body(*refs):
        ins, outs = refs[:n], refs[n:2 * n]
        send_sems, recv_sems, local_sems = refs[2 * n:]
        me = _lin(_me())
        locals_, sends = [], []
        for i in range(n):
            blk = lambda p, i=i: ins[i].at[pl.ds(pl.multiple_of(p * rows[i], 16), rows[i])]
            cp = pltpu.make_async_copy(blk(me), outs[i].at[me], local_sems.at[i])
            cp.start()
            locals_.append(cp)
            for k, flip in enumerate(_FLIPS):
                peer = _peer(flip)
                rc = pltpu.make_async_remote_copy(
                    src_ref=blk(_lin(peer)), dst_ref=outs[i].at[me], send_sem=send_sems.at[i * 7 + k],
                    recv_sem=recv_sems.at[i * 7 + k], device_id=peer, device_id_type=MESH)
                rc.start()
                sends.append(rc)
        for i in range(n):
            for k, flip in enumerate(_FLIPS):
                peer = _peer(flip)
                pltpu.make_async_remote_copy(
                    src_ref=ins[i].at[pl.ds(0, rows[i])], dst_ref=outs[i].at[_lin(peer)],
                    send_sem=send_sems.at[i * 7 + k], recv_sem=recv_sems.at[i * 7 + k],
                    device_id=peer, device_id_type=MESH).wait_recv()
        for rc in sends:
            rc.wait_send()
        for cp in locals_:
            cp.wait()

    return _pcall(
        body, name=name, in_specs=[_HBM] * n, out_specs=[_HBM] * n,
        out_shape=[_sds((N_DEV, r, f.shape[1]), f.dtype) for r, f in zip(rows, fulls)],
        scratch_shapes=[pltpu.SemaphoreType.DMA((7 * n,)), pltpu.SemaphoreType.DMA((7 * n,)),
                        pltpu.SemaphoreType.DMA((n,))],
    )(*fulls)


def all_gather_small(x):
    R, C = x.shape

    def body(x_ref, o_ref, send_sems, recv_sems):
        me = _lin(_me())
        o_ref[me] = x_ref[...]
        sends = []
        for k, flip in enumerate(_FLIPS):
            rc = pltpu.make_async_remote_copy(
                src_ref=x_ref, dst_ref=o_ref.at[me], send_sem=send_sems.at[k], recv_sem=recv_sems.at[k],
                device_id=_peer(flip), device_id_type=MESH)
            rc.start()
            sends.append(rc)
        for k, flip in enumerate(_FLIPS):
            pltpu.make_async_remote_copy(
                src_ref=x_ref, dst_ref=o_ref.at[_lin(_peer(flip))], send_sem=send_sems.at[k],
                recv_sem=recv_sems.at[k], device_id=_peer(flip), device_id_type=MESH).wait_recv()
        for rc in sends:
            rc.wait_send()

    vm = pl.BlockSpec(memory_space=pltpu.VMEM)
    return _pcall(
        body, name="all_gather_small", in_specs=[vm], out_specs=vm, out_shape=_sds((N_DEV, R, C), x.dtype),
        scratch_shapes=[pltpu.SemaphoreType.DMA((7,)), pltpu.SemaphoreType.DMA((7,))],
        compiler_params=_params(None, VMEM_MID),
    )(x)


def all_reduce_small(x):
    R, C = x.shape

    def body(x_ref, o_ref, buf_ref, send_sems, recv_sems):
        me = _lin(_me())
        buf_ref[me] = x_ref[...]
        sends = []
        for k, flip in enumerate(_FLIPS):
            rc = pltpu.make_async_remote_copy(
                src_ref=x_ref, dst_ref=buf_ref.at[me], send_sem=send_sems.at[k], recv_sem=recv_sems.at[k],
                device_id=_peer(flip), device_id_type=MESH)
            rc.start()
            sends.append(rc)
        for k, flip in enumerate(_FLIPS):
            pltpu.make_async_remote_copy(
                src_ref=x_ref, dst_ref=buf_ref.at[_lin(_peer(flip))], send_sem=send_sems.at[k],
                recv_sem=recv_sems.at[k], device_id=_peer(flip), device_id_type=MESH).wait_recv()
        for rc in sends:
            rc.wait_send()
        acc = buf_ref[0]
        for p in range(1, N_DEV):
            acc = acc + buf_ref[p]
        o_ref[...] = acc

    vm = pl.BlockSpec(memory_space=pltpu.VMEM)
    return _pcall(
        body, name="all_reduce_small", in_specs=[vm], out_specs=vm, out_shape=_sds((R, C), f32),
        scratch_shapes=[pltpu.VMEM((N_DEV, R, C), f32), pltpu.SemaphoreType.DMA((7,)),
                        pltpu.SemaphoreType.DMA((7,))],
        compiler_params=_params(None, VMEM_MID),
    )(x)


def _pack(arrs):
    flat = jnp.concatenate([a.reshape(-1).astype(f32) for a in arrs])
    n = flat.shape[0]
    rows = -(-n // 128)
    rows = -(-rows // 8) * 8
    return jnp.pad(flat, (0, rows * 128 - n)).reshape(rows, 128)


def _unpack(buf, shapes):
    flat = buf.reshape(-1)
    out, off = [], 0
    for s in shapes:
        n = math.prod(s)
        out.append(flat[off:off + n].reshape(s))
        off += n
    return out


def _row(v):
    return v.reshape(1, -1)


def _lane_row(vals, offset):
    return jnp.pad(vals.reshape(1, -1), ((0, 0), (offset, HEAD - offset - vals.shape[-1])))


def kernel(x, ffn1_norm, ffn1_w_gate, ffn1_w_up, ffn1_w_down, mix_norm, ffn2_norm, ffn2_w_gate, ffn2_w_up, ffn2_w_down, ab_w_in, pool_w, pool_scale, dn_conv_w, dn_a_log, dn_dt_bias, dn_out_norm, ab_w_out, cd_w_in, sgu_norm_g, sgu_norm_b, sgu_w, sgu_bias, sc_conv_w, cd_w_out, final_norm, loss_target, m_ffn1_norm, m_ffn1_w_gate, m_ffn1_w_up, m_ffn1_w_down, m_mix_norm, m_ffn2_norm, m_ffn2_w_gate, m_ffn2_w_up, m_ffn2_w_down, m_ab_w_in, m_pool_w, m_pool_scale, m_dn_conv_w, m_dn_a_log, m_dn_dt_bias, m_dn_out_norm, m_ab_w_out, m_cd_w_in, m_sgu_norm_g, m_sgu_norm_b, m_sgu_w, m_sgu_bias, m_sc_conv_w, m_cd_w_out, m_final_norm, v_ffn1_norm, v_ffn1_w_gate, v_ffn1_w_up, v_ffn1_w_down, v_mix_norm, v_ffn2_norm, v_ffn2_w_gate, v_ffn2_w_up, v_ffn2_w_down, v_ab_w_in, v_pool_w, v_pool_scale, v_dn_conv_w, v_dn_a_log, v_dn_dt_bias, v_dn_out_norm, v_ab_w_out, v_cd_w_in, v_sgu_norm_g, v_sgu_norm_b, v_sgu_w, v_sgu_bias, v_sc_conv_w, v_cd_w_out, v_final_norm):
    names = ['ffn1_norm', 'ffn1_w_gate', 'ffn1_w_up', 'ffn1_w_down', 'mix_norm', 'ffn2_norm', 'ffn2_w_gate',
             'ffn2_w_up', 'ffn2_w_down', 'ab_w_in', 'pool_w', 'pool_scale', 'dn_conv_w', 'dn_a_log', 'dn_dt_bias',
             'dn_out_norm', 'ab_w_out', 'cd_w_in', 'sgu_norm_g', 'sgu_norm_b', 'sgu_w', 'sgu_bias', 'sc_conv_w',
             'cd_w_out', 'final_norm']
    loc = locals()
    W = {n: loc[n] for n in names}
    M = {n: loc['m_' + n] for n in names}
    V = {n: loc['v_' + n] for n in names}

    B, S, D = x.shape
    T = B * S
    me = _lin(_me())

    def rows_of(w):
        return w.astype(bf16).T

    gathered = []
    for layer in range(DEPTH):
        e = layer // 2
        shards = [rows_of(W['ffn1_w_gate'][layer]), rows_of(W['ffn1_w_up'][layer]), W['ffn1_w_down'][layer].astype(bf16)]
        if layer % 2 == 0:
            win = jnp.pad(rows_of(W['ab_w_in'][e]), ((0, AB_SHARD_PAD - AB_SHARD), (0, 0)))
            wout = W['ab_w_out'][e].astype(bf16)
        else:
            win = rows_of(W['cd_w_in'][e])
            wout = W['cd_w_out'][e].astype(bf16)
        shards += [win, wout]
        shards += [rows_of(W['ffn2_w_gate'][layer]), rows_of(W['ffn2_w_up'][layer]), W['ffn2_w_down'][layer].astype(bf16)]
        full = all_gather_rows(shards, name=f"gather_weights_{layer}")
        full = [f.reshape(-1, D) for f in full]
        if layer % 2 == 0:
            wi = full[3].reshape(N_DEV, AB_SHARD_PAD, D)[:, :AB_SHARD].reshape(AB_IN, D)
            full[3] = jnp.pad(wi, ((0, AB_IN_PAD - AB_IN), (0, 0)))
        gathered.append(full)

    small_shards = [W['dn_conv_w'], W['sgu_norm_g'], W['sgu_norm_b'], W['sc_conv_w']]
    gs = all_gather_small(_pack(small_shards))
    per_dev = [_unpack(gs[p], [a.shape for a in small_shards]) for p in range(N_DEV)]
    dn_conv_full, sgu_g_full, sgu_b_full, sc_conv_full = [
        jnp.concatenate([per_dev[p][i] for p in range(N_DEV)], axis=-1) for i in range(4)]

    h = x.reshape(T, D)
    saved = []
    for layer in range(DEPTH):
        e = layer // 2
        wg1, wu1, wd1, win, wout, wg2, wu2, wd2 = gathered[layer]
        sv = {'h0': h}
        h, sv['g1'], sv['u1'] = ffn_fwd(h, _row(W['ffn1_norm'][layer]), wg1, wu1, wd1)
        sv['h1'] = h
        if layer % 2 == 0:
            a_in, qkv_pre, z, bg = in_proj_fwd(h, _row(W['mix_norm'][layer]), win, [512, 1536, 512, 128])
            ya = pool_fwd(a_in, W['pool_w'][e], _row(W['pool_scale'][e]), S)
            qkv = seq_chan_fwd(conv_silu_math, dn_conv_full[e], [qkv_pre], S, f32, "dn_conv_fwd")
            alog, dtb = _lane_row(W['dn_a_log'][e], N_HEADS), _lane_row(W['dn_dt_bias'][e], N_HEADS)
            prep = dn_prep_fwd(qkv, bg, alog, dtb)
            yb = dn_recur_fwd(prep, z, _row(W['dn_out_norm'][e]), S)
            sv.update(a_in=a_in, qkv_pre=qkv_pre, z=z, bg=bg, qkv=qkv, alog=alog, dtb=dtb, prep=prep)
        else:
            up, vp, xd, bgate, cg = in_proj_fwd(h, _row(W['mix_norm'][layer]), win, [512] * 5)
            sbias = W['sgu_bias'][e].reshape(N_HEADS, SGU_BLOCK, 1)
            ya = sgu_fwd(up, vp, _row(sgu_g_full[e]), _row(sgu_b_full[e]), W['sgu_w'][e], sbias,
                         _tile(S, SGU_TILE))
            yb = seq_chan_fwd(gated_conv_math, sc_conv_full[e], [xd, bgate, cg], S, bf16, "sc_conv_fwd")
            sv.update(up=up, vp=vp, xd=xd, bgate=bgate, cg=cg, sbias=sbias)
        sv.update(ya=ya, yb=yb)
        h = out_proj_fwd(h, ya, yb, wout)
        sv['h2'] = h
        h, sv['g2'], sv['u2'] = ffn_fwd(h, _row(W['ffn2_norm'][layer]), wg2, wu2, wd2)
        saved.append(sv)

    loss_part, dh, dfinal = loss_fwd_bwd(h, _row(W['final_norm']), loss_target.reshape(T, D))
    loss = lax.psum(loss_part[0, 0], ("x", "y", "c"))

    G = {}
    G['final_norm'] = dfinal[0]
    for n in ('ffn1_norm', 'mix_norm', 'ffn2_norm'):
        G[n] = [None] * DEPTH
    for n in ('pool_w', 'pool_scale', 'dn_conv_w', 'dn_a_log', 'dn_dt_bias', 'dn_out_norm',
              'sgu_norm_g', 'sgu_norm_b', 'sgu_w', 'sgu_bias', 'sc_conv_w'):
        G[n] = [None] * 2
    big = [None] * DEPTH

    for layer in reversed(range(DEPTH)):
        e = layer // 2
        sv = saved[layer]
        wg1, wu1, wd1, win, wout, wg2, wu2, wd2 = gathered[layer]
        dh, dgain, xn, act, dg, du, dy = ffn_bwd(sv['h2'], _row(W['ffn2_norm'][layer]), sv['g2'], sv['u2'], dh,
                                                  wg2, wu2, wd2)
        G['ffn2_norm'][layer] = dgain[0]
        dwg2, dwu2, dwd2 = wgrad(dg, xn), wgrad(du, xn), wgrad(act, dy)
        dya, dyb, dwout = out_proj_bwd(dh, sv['ya'], sv['yb'], wout)
        if layer % 2 == 0:
            da, dpw, dsc = pool_bwd(sv['a_in'], W['pool_w'][e], _row(W['pool_scale'][e]), dya, S)
            G['pool_w'][e], G['pool_scale'][e] = dpw, dsc[0]
            *cts, dz, don = dn_recur_bwd(sv['prep'], sv['z'], _row(W['dn_out_norm'][e]), dyb, S)
            G['dn_out_norm'][e] = don[0]
            dqkv, dbg, dalog, ddtb = dn_prep_bwd(sv['qkv'], sv['bg'], sv['alog'], sv['dtb'], cts)
            G['dn_a_log'][e], G['dn_dt_bias'][e] = dalog[0, N_HEADS:2 * N_HEADS], ddtb[0, N_HEADS:2 * N_HEADS]
            dqkv_pre, dconv = seq_chan_bwd(conv_silu_math, dn_conv_full[e], [sv['qkv_pre']], dqkv, S, bf16,
                                           "dn_conv_bwd")
            G['dn_conv_w'][e] = dconv
            dpieces = [da, dqkv_pre, dz, dbg]
        else:
            dup, dvp, dng, dnb, dsw, dsb = sgu_bwd(sv['up'], sv['vp'], _row(sgu_g_full[e]), _row(sgu_b_full[e]),
                                                   W['sgu_w'][e], sv['sbias'], dya, _tile(S, SGU_TILE))
            G['sgu_norm_g'][e], G['sgu_norm_b'][e] = dng[0], dnb[0]
            G['sgu_w'][e], G['sgu_bias'][e] = dsw, dsb.reshape(N_HEADS, SGU_BLOCK)
            dxd, dbgate, dcg, dscw = seq_chan_bwd(gated_conv_math, sc_conv_full[e],
                                                  [sv['xd'], sv['bgate'], sv['cg']], dyb, S, bf16, "sc_conv_bwd")
            G['sc_conv_w'][e] = dscw
            dpieces = [dup, dvp, dxd, dbgate, dcg]
        dh, dgain, dwin = in_proj_bwd(sv['h1'], _row(W['mix_norm'][layer]), dpieces, dh, win)
        G['mix_norm'][layer] = dgain[0]
        if layer % 2 == 0:
            dwin = jnp.pad(dwin[:AB_IN].reshape(N_DEV, AB_SHARD, D), ((0, 0), (0, AB_SHARD_PAD - AB_SHARD), (0, 0)))
            dwin = dwin.reshape(N_DEV * AB_SHARD_PAD, D)
        dh, dgain, xn, act, dg, du, dy = ffn_bwd(sv['h0'], _row(W['ffn1_norm'][layer]), sv['g1'], sv['u1'], dh,
                                                  wg1, wu1, wd1)
        G['ffn1_norm'][layer] = dgain[0]
        dwg1, dwu1, dwd1 = wgrad(dg, xn), wgrad(du, xn), wgrad(act, dy)
        big[layer] = scatter_rows([dwg1, dwu1, dwd1, dwin, dwout, dwg2, dwu2, dwd2], name=f"scatter_grads_{layer}")

    grad_x = dh.reshape(B, S, D)

    small_names = ['ffn1_norm', 'mix_norm', 'ffn2_norm', 'pool_w', 'pool_scale', 'dn_conv_w', 'dn_a_log',
                   'dn_dt_bias', 'dn_out_norm', 'sgu_norm_g', 'sgu_norm_b', 'sgu_w', 'sgu_bias', 'sc_conv_w',
                   'final_norm']
    small_g = [G[n] if n == 'final_norm' else jnp.stack(G[n]) for n in small_names]
    reduced = _unpack(all_reduce_small(_pack(small_g)), [a.shape for a in small_g])
    grads = {}
    for n, g in zip(small_names, reduced):
        if n in ('dn_conv_w', 'sgu_norm_g', 'sgu_norm_b', 'sc_conv_w'):
            c = W[n].shape[-1]
            g = lax.dynamic_slice_in_dim(g, me * c, c, axis=g.ndim - 1)
        grads[n] = g

    def stack_layers(idx, transpose, sel):
        out = []
        for layer in sel:
            g = sum8(big[layer][idx])
            if idx == 3 and layer % 2 == 0:
                g = g[:AB_SHARD]
            out.append(g.T if transpose else g)
        return jnp.stack(out)

    all_layers, even, odd = range(DEPTH), range(0, DEPTH, 2), range(1, DEPTH, 2)
    grads['ffn1_w_gate'] = stack_layers(0, True, all_layers)
    grads['ffn1_w_up'] = stack_layers(1, True, all_layers)
    grads['ffn1_w_down'] = stack_layers(2, False, all_layers)
    grads['ab_w_in'] = stack_layers(3, True, even)
    grads['cd_w_in'] = stack_layers(3, True, odd)
    grads['ab_w_out'] = stack_layers(4, False, even)
    grads['cd_w_out'] = stack_layers(4, False, odd)
    grads['ffn2_w_gate'] = stack_layers(5, True, all_layers)
    grads['ffn2_w_up'] = stack_layers(6, True, all_layers)
    grads['ffn2_w_down'] = stack_layers(7, False, all_layers)

    big_names = ['ffn1_w_gate', 'ffn1_w_up', 'ffn1_w_down', 'ffn2_w_gate', 'ffn2_w_up', 'ffn2_w_down',
                 'ab_w_in', 'ab_w_out', 'cd_w_in', 'cd_w_out']
    delta, new_m, new_v = {}, {}, {}
    for n in big_names:
        shp = W[n].shape
        two = lambda a: a.reshape(-1, shp[-1])
        d_, m_, v_ = adamw(two(W[n]), two(grads[n]), two(M[n]), two(V[n]))
        delta[n], new_m[n], new_v[n] = d_.reshape(shp), m_.reshape(shp), v_.reshape(shp)
    shapes = [W[n].shape for n in small_names]
    d_, m_, v_ = adamw(_pack([W[n] for n in small_names]), _pack([grads[n] for n in small_names]),
                       _pack([M[n] for n in small_names]), _pack([V[n] for n in small_names]))
    for n, a, b_, c_ in zip(small_names, _unpack(d_, shapes), _unpack(m_, shapes), _unpack(v_, shapes)):
        delta[n], new_m[n], new_v[n] = a, b_, c_

    return (loss, grad_x, *[grads[n] for n in names], *[delta[n] for n in names],
            *[new_m[n] for n in names], *[new_v[n] for n in names])
```

```python
import functools
import math

import jax
import jax.numpy as jnp
from jax import lax
from jax.experimental import pallas as pl
from jax.experimental.pallas import tpu as pltpu

f32, bf16 = jnp.float32, jnp.bfloat16

D_MODEL = 1024
DEPTH = 4
CHUNK = 64
POOL_WINDOWS = (2, 4, 8, 16)
HEAD = 128
N_HEADS = 4
SGU_BLOCK = 128
SGU_TILE = 512
FFN_DIM = 2816
AB_IN = 2568
AB_IN_PAD = 2688
AB_SHARD = 321
AB_SHARD_PAD = 336
EPS = 1e-6
N_DEV = 8
MESH = pl.DeviceIdType.MESH

ADAM_LR, ADAM_B1, ADAM_B2, ADAM_EPS, ADAM_WD, ADAM_STEP = 0.001, 0.9, 0.999, 1e-08, 0.01, 10

VMEM_BIG = 56 * 1024 * 1024
VMEM_MID = 40 * 1024 * 1024


def _pcall(body, **kw):
    return pl.pallas_call(body, **kw)


def _params(sem=None, vmem=None):
    return pltpu.CompilerParams(dimension_semantics=sem, vmem_limit_bytes=vmem)


def _sds(shape, dtype):
    return jax.ShapeDtypeStruct(shape, dtype)


_NN2, _NT2, _TN2 = (((1,), (0,)), ((), ())), (((1,), (1,)), ((), ())), (((0,), (0,)), ((), ()))
_NN3, _NT3, _TN3 = (((2,), (1,)), ((0,), (0,))), (((2,), (2,)), ((0,), (0,))), (((1,), (1,)), ((0,), (0,)))


def _dg(a, b, dims, hi):
    if hi:
        return lax.dot_general(a.astype(f32), b.astype(f32), dims, preferred_element_type=f32,
                               precision=lax.Precision.HIGHEST)
    return lax.dot_general(a.astype(bf16), b.astype(bf16), dims, preferred_element_type=f32)


def _make_mm(nn, nt, tn, hi):
    @jax.custom_vjp
    def mm(a, b):
        return _dg(a, b, nn, hi)

    def mm_bwd(res, ct):
        a, b = res
        return _dg(ct, b, nt, hi).astype(a.dtype), _dg(a, ct, tn, hi).astype(b.dtype)

    mm.defvjp(lambda a, b: (_dg(a, b, nn, hi), (a, b)), mm_bwd)

    @jax.custom_vjp
    def mm_nt(a, b):
        return _dg(a, b, nt, hi)

    def mm_nt_bwd(res, ct):
        a, b = res
        return _dg(ct, b, nn, hi).astype(a.dtype), _dg(ct, a, tn, hi).astype(b.dtype)

    mm_nt.defvjp(lambda a, b: (_dg(a, b, nt, hi), (a, b)), mm_nt_bwd)

    @jax.custom_vjp
    def mm_tn(a, b):
        return _dg(a, b, tn, hi)

    def mm_tn_bwd(res, ct):
        a, b = res
        return _dg(b, ct, nt, hi).astype(a.dtype), _dg(a, ct, nn, hi).astype(b.dtype)

    mm_tn.defvjp(lambda a, b: (_dg(a, b, tn, hi), (a, b)), mm_tn_bwd)
    return mm, mm_nt, mm_tn


mm, mm_nt, mm_tn = _make_mm(_NN2, _NT2, _TN2, False)
bmm, bmm_nt, bmm_tn = _make_mm(_NN3, _NT3, _TN3, False)
bmm_hi, _, _ = _make_mm(_NN3, _NT3, _TN3, True)


def _shift_raw(x, k):
    n = x.shape[0]
    t = lax.broadcasted_iota(jnp.int32, x.shape, 0)
    if k > 0:
        return jnp.where(t >= k, pltpu.roll(x, k, axis=0), 0.0)
    k = -k
    return jnp.where(t < n - k, pltpu.roll(x, n - k, axis=0), 0.0)


@functools.partial(jax.custom_vjp, nondiff_argnums=(1,))
def shift(x, k):
    return _shift_raw(x, k)


shift.defvjp(lambda x, k: (_shift_raw(x, k), None), lambda k, _, ct: (_shift_raw(ct, -k),))


def _silu(x):
    return x * jax.nn.sigmoid(x)


def _softplus(x):
    return jnp.maximum(x, 0.0) + jnp.log(1.0 + jnp.exp(-jnp.abs(x)))


def _rms_fwd(h, gain):
    rstd = lax.rsqrt(jnp.mean(h * h, axis=-1, keepdims=True) + EPS)
    xhat = h * rstd
    return xhat * gain, xhat, rstd


def _rms_bwd(dxn, xhat, rstd, gain):
    dxhat = dxn * gain
    dh = rstd * (dxhat - xhat * jnp.mean(dxhat * xhat, axis=-1, keepdims=True))
    return dh, jnp.sum(dxn * xhat, axis=0, keepdims=True)


def _acc_rows(ref, val, first):
    @pl.when(first)
    def _():
        ref[...] = jnp.zeros_like(ref)
    ref[0:1, :] += val


def _tile(n, cap):
    t = min(n, cap)
    assert n % t == 0, (n, t)
    return t


def _resident(shape):
    nd = len(shape)
    return pl.BlockSpec(shape, lambda *_: (0,) * nd, pipeline_mode=pl.Buffered(1))


def ffn_fwd(h, gain, wgt, wut, wd):
    T, D = h.shape
    F = wgt.shape[0]
    tm = _tile(T, 256)

    def body(h_ref, gain_ref, wg_ref, wu_ref, wd_ref, ho_ref, g_ref, u_ref):
        hh = h_ref[...]
        xn, _, _ = _rms_fwd(hh, gain_ref[...])
        xb = xn.astype(bf16)
        g = _dg(xb, wg_ref[...], _NT2, False)
        u = _dg(xb, wu_ref[...], _NT2, False)
        y = _dg(_silu(g) * u, wd_ref[...], _NN2, False)
        ho_ref[...] = hh + 0.5 * y
        g_ref[...] = g.astype(bf16)
        u_ref[...] = u.astype(bf16)

    row = lambda w: pl.BlockSpec((tm, w), lambda i: (i, 0))
    return _pcall(
        body, name="ffn_fwd", grid=(T // tm,),
        in_specs=[row(D), _resident((1, D)), _resident((F, D)), _resident((F, D)), _resident((F, D))],
        out_specs=[row(D), row(F), row(F)],
        out_shape=[_sds((T, D), f32), _sds((T, F), bf16), _sds((T, F), bf16)],
        compiler_params=_params(("parallel",), VMEM_BIG),
    )(h, gain, wgt, wut, wd)


def ffn_bwd(h, gain, g, u, dout, wgt, wut, wd):
    T, D = h.shape
    F = wgt.shape[0]
    tm = _tile(T, 256)

    def body(h_ref, gain_ref, g_ref, u_ref, do_ref, wg_ref, wu_ref, wd_ref,
             dh_ref, dgain_ref, xn_ref, act_ref, dg_ref, du_ref, dy_ref):
        hh, dout_ = h_ref[...], do_ref[...]
        gain_ = gain_ref[...]
        xn, xhat, rstd = _rms_fwd(hh, gain_)
        gg, uu = g_ref[...].astype(f32), u_ref[...].astype(f32)
        dy = (0.5 * dout_).astype(bf16)
        dact = _dg(dy, wd_ref[...], _NT2, False)
        sg = jax.nn.sigmoid(gg)
        silu = gg * sg
        dgate = (dact * uu * (sg * (1.0 + gg * (1.0 - sg)))).astype(bf16)
        dup = (dact * silu).astype(bf16)
        dxn = _dg(dgate, wg_ref[...], _NN2, False) + _dg(dup, wu_ref[...], _NN2, False)
        dh, dgain = _rms_bwd(dxn, xhat, rstd, gain_)
        dh_ref[...] = dout_ + dh
        _acc_rows(dgain_ref, dgain, pl.program_id(0) == 0)
        xn_ref[...] = xn.astype(bf16)
        act_ref[...] = (silu * uu).astype(bf16)
        dg_ref[...] = dgate
        du_ref[...] = dup
        dy_ref[...] = dy

    row = lambda w: pl.BlockSpec((tm, w), lambda i: (i, 0))
    return _pcall(
        body, name="ffn_bwd", grid=(T // tm,),
        in_specs=[row(D), _resident((1, D)), row(F), row(F), row(D),
                  _resident((F, D)), _resident((F, D)), _resident((F, D))],
        out_specs=[row(D), pl.BlockSpec((8, D), lambda i: (0, 0)), row(D), row(F), row(F), row(F), row(D)],
        out_shape=[_sds((T, D), f32), _sds((8, D), f32), _sds((T, D), bf16), _sds((T, F), bf16),
                   _sds((T, F), bf16), _sds((T, F), bf16), _sds((T, D), bf16)],
        compiler_params=_params(("arbitrary",), VMEM_BIG),
    )(h, gain, g, u, dout, wgt, wut, wd)


def _col_tile(n, cap=1408):
    best = None
    for c in range(128, cap + 1, 128):
        if n % c == 0:
            best = c
    assert best is not None, n
    return best


def wgrad(a, b):
    T, N = a.shape
    K = b.shape[1]
    nc, tk = _col_tile(N), _tile(T, 512)
    nk = T // tk

    def body(a_ref, b_ref, o_ref, acc_ref):
        k = pl.program_id(1)

        @pl.when(k == 0)
        def _():
            acc_ref[...] = jnp.zeros_like(acc_ref)

        acc_ref[...] += _dg(a_ref[...], b_ref[...], _TN2, False)

        @pl.when(k == nk - 1)
        def _():
            o_ref[...] = acc_ref[...].astype(bf16)

    return _pcall(
        body, name="wgrad", grid=(N // nc, nk),
        in_specs=[pl.BlockSpec((tk, nc), lambda j, k: (k, j)), pl.BlockSpec((tk, K), lambda j, k: (k, 0))],
        out_specs=pl.BlockSpec((nc, K), lambda j, k: (j, 0)),
        out_shape=_sds((N, K), bf16),
        scratch_shapes=[pltpu.VMEM((nc, K), f32)],
        compiler_params=_params(("parallel", "arbitrary"), VMEM_MID),
    )(a, b)


def in_proj_fwd(h, gain, wt, widths):
    T, D = h.shape
    N = wt.shape[0]
    assert sum(widths) == N
    tm = _tile(T, 512)
    offs = [sum(widths[:i]) for i in range(len(widths))]

    def body(h_ref, gain_ref, w_ref, *outs):
        xn, _, _ = _rms_fwd(h_ref[...], gain_ref[...])
        p = _dg(xn, w_ref[...], _NT2, False)
        for o_ref, off, wd_ in zip(outs, offs, widths):
            o_ref[...] = p[:, off:off + wd_]

    row = lambda w: pl.BlockSpec((tm, w), lambda i: (i, 0))
    return _pcall(
        body, name="in_proj_fwd", grid=(T // tm,),
        in_specs=[row(D), _resident((1, D)), _resident((N, D))],
        out_specs=[row(w) for w in widths],
        out_shape=[_sds((T, w), f32) for w in widths],
        compiler_params=_params(("parallel",), VMEM_BIG),
    )(h, gain, wt)


def in_proj_bwd(h, gain, dpieces, dout, wt):
    T, D = h.shape
    N = wt.shape[0]
    widths = [p.shape[1] for p in dpieces]
    assert sum(widths) == N
    tm = _tile(T, 256)
    nt = T // tm
    npc = len(dpieces)

    def body(*refs):
        h_ref, gain_ref = refs[0], refs[1]
        p_refs = refs[2:2 + npc]
        do_ref, w_ref, dh_ref, dgain_ref, dw_ref, acc_ref = refs[2 + npc:]
        i = pl.program_id(0)
        gain_ = gain_ref[...]
        xn, xhat, rstd = _rms_fwd(h_ref[...], gain_)
        dp = jnp.concatenate([r[...].astype(bf16) for r in p_refs], axis=-1)
        dxn = _dg(dp, w_ref[...], _NN2, False)
        dh, dgain = _rms_bwd(dxn, xhat, rstd, gain_)
        dh_ref[...] = do_ref[...] + dh
        _acc_rows(dgain_ref, dgain, i == 0)

        @pl.when(i == 0)
        def _():
            acc_ref[...] = jnp.zeros_like(acc_ref)

        acc_ref[...] += _dg(dp, xn, _TN2, False)

        @pl.when(i == nt - 1)
        def _():
            dw_ref[...] = acc_ref[...].astype(bf16)

    row = lambda w: pl.BlockSpec((tm, w), lambda i: (i, 0))
    return _pcall(
        body, name="in_proj_bwd", grid=(nt,),
        in_specs=[row(D), _resident((1, D))] + [row(w) for w in widths] + [row(D), _resident((N, D))],
        out_specs=[row(D), pl.BlockSpec((8, D), lambda i: (0, 0)), pl.BlockSpec((N, D), lambda i: (0, 0))],
        out_shape=[_sds((T, D), f32), _sds((8, D), f32), _sds((N, D), bf16)],
        scratch_shapes=[pltpu.VMEM((N, D), f32)],
        compiler_params=_params(("arbitrary",), VMEM_BIG),
    )(h, gain, *dpieces, dout, wt)


def out_proj_fwd(h, ya, yb, w):
    T, D = h.shape
    half = ya.shape[1]
    tm = _tile(T, 512)

    def body(h_ref, ya_ref, yb_ref, w_ref, o_ref):
        y = jnp.concatenate([ya_ref[...], yb_ref[...]], axis=-1)
        o_ref[...] = h_ref[...] + _dg(y, w_ref[...], _NN2, False)

    row = lambda w_: pl.BlockSpec((tm, w_), lambda i: (i, 0))
    return _pcall(
        body, name="out_proj_fwd", grid=(T // tm,),
        in_specs=[row(D), row(half), row(half), _resident((2 * half, D))],
        out_specs=row(D), out_shape=_sds((T, D), f32),
        compiler_params=_params(("parallel",), VMEM_MID),
    )(h, ya, yb, w)


def out_proj_bwd(dout, ya, yb, w):
    T, D = dout.shape
    half = ya.shape[1]
    tm = _tile(T, 512)
    nt = T // tm

    def body(do_ref, ya_ref, yb_ref, w_ref, dya_ref, dyb_ref, dw_ref, acc_ref):
        i = pl.program_id(0)
        dob = do_ref[...].astype(bf16)
        dy = _dg(dob, w_ref[...], _NT2, False)
        dya_ref[...] = dy[:, :half]
        dyb_ref[...] = dy[:, half:]

        @pl.when(i == 0)
        def _():
            acc_ref[...] = jnp.zeros_like(acc_ref)

        y = jnp.concatenate([ya_ref[...], yb_ref[...]], axis=-1)
        acc_ref[...] += _dg(y, dob, _TN2, False)

        @pl.when(i == nt - 1)
        def _():
            dw_ref[...] = acc_ref[...].astype(bf16)

    row = lambda w_: pl.BlockSpec((tm, w_), lambda i: (i, 0))
    return _pcall(
        body, name="out_proj_bwd", grid=(nt,),
        in_specs=[row(D), row(half), row(half), _resident((2 * half, D))],
        out_specs=[row(half), row(half), pl.BlockSpec((2 * half, D), lambda i: (0, 0))],
        out_shape=[_sds((T, half), f32), _sds((T, half), f32), _sds((2 * half, D), bf16)],
        scratch_shapes=[pltpu.VMEM((2 * half, D), f32)],
        compiler_params=_params(("arbitrary",), VMEM_MID),
    )(dout, ya, yb, w)


def _conv_taps(x, w):
    K = w.shape[0]
    acc = x * w[K - 1:K, :]
    for i in range(K - 1):
        acc = acc + shift(x, K - 1 - i) * w[i:i + 1, :]
    return acc


def conv_silu_math(w, x):
    return _silu(_conv_taps(x, w))


def gated_conv_math(w, xd, bg, cg):
    return bg * _conv_taps(cg * xd, w)


def seq_chan_fwd(math, w, xs, seq, out_dtype, name):
    T, C = xs[0].shape
    K = w.shape[0]
    nb, nc = T // seq, C // HEAD

    def body(w_ref, *refs):
        o_ref = refs[-1]
        o_ref[...] = math(w_ref[...], *[r[...] for r in refs[:-1]]).astype(out_dtype)

    blk = pl.BlockSpec((seq, HEAD), lambda j, b: (b, j))
    return _pcall(
        body, name=name, grid=(nc, nb),
        in_specs=[pl.BlockSpec((K, HEAD), lambda j, b: (0, j))] + [blk] * len(xs),
        out_specs=blk, out_shape=_sds((T, C), out_dtype),
        compiler_params=_params(("parallel", "parallel"), VMEM_MID),
    )(w, *xs)


def seq_chan_bwd(math, w, xs, dy, seq, dx_dtype, name):
    T, C = xs[0].shape
    K = w.shape[0]
    nb, nc = T // seq, C // HEAD
    nx = len(xs)

    def body(w_ref, *refs):
        x_refs, dy_ref = refs[:nx], refs[nx]
        dx_refs, dw_ref = refs[nx + 1:2 * nx + 1], refs[2 * nx + 1]
        _, vjp = jax.vjp(math, w_ref[...], *[r[...] for r in x_refs])
        grads = vjp(dy_ref[...].astype(f32))

        @pl.when(pl.program_id(1) == 0)
        def _():
            dw_ref[...] = jnp.zeros_like(dw_ref)

        dw_ref[...] += grads[0]
        for r, gx in zip(dx_refs, grads[1:]):
            r[...] = gx.astype(dx_dtype)

    blk = pl.BlockSpec((seq, HEAD), lambda j, b: (b, j))
    wblk = pl.BlockSpec((K, HEAD), lambda j, b: (0, j))
    return _pcall(
        body, name=name, grid=(nc, nb),
        in_specs=[wblk] + [blk] * (nx + 1),
        out_specs=[blk] * nx + [wblk],
        out_shape=[_sds((T, C), dx_dtype)] * nx + [_sds((K, C), f32)],
        compiler_params=_params(("parallel", "arbitrary"), VMEM_MID),
    )(w, *xs, dy)


def pool_group_math(win, ag, pw, scale):
    t = lax.broadcasted_iota(jnp.int32, (ag.shape[0], 1), 0)
    s, k = ag, 1
    while k < win:
        s = s + shift(s, k)
        k *= 2
    pooled = s / jnp.minimum(t + 1, win).astype(f32) - ag
    return mm(pooled, pw) * scale


def pool_fwd(a, pw, scale, seq):
    T, C = a.shape

    def body(a_ref, pw_ref, sc_ref, o_ref):
        for gi, win in enumerate(POOL_WINDOWS):
            cols = slice(gi * HEAD, (gi + 1) * HEAD)
            o_ref[:, cols] = pool_group_math(win, a_ref[:, cols], pw_ref[gi], sc_ref[:, cols]).astype(bf16)

    blk = pl.BlockSpec((seq, C), lambda b: (b, 0))
    return _pcall(
        body, name="pool_fwd", grid=(T // seq,),
        in_specs=[blk, _resident(pw.shape), _resident((1, C))],
        out_specs=blk, out_shape=_sds((T, C), bf16),
        compiler_params=_params(("parallel",), VMEM_MID),
    )(a, pw, scale)


def pool_bwd(a, pw, scale, dy, seq):
    T, C = a.shape

    def body(a_ref, pw_ref, sc_ref, dy_ref, da_ref, dpw_ref, dsc_ref):
        first = pl.program_id(0) == 0

        @pl.when(first)
        def _():
            dpw_ref[...] = jnp.zeros_like(dpw_ref)
            dsc_ref[...] = jnp.zeros_like(dsc_ref)

        for gi, win in enumerate(POOL_WINDOWS):
            cols = slice(gi * HEAD, (gi + 1) * HEAD)
            _, vjp = jax.vjp(functools.partial(pool_group_math, win), a_ref[:, cols], pw_ref[gi], sc_ref[:, cols])
            da, dpw, dsc = vjp(dy_ref[:, cols])
            dpw_ref[gi] += dpw
            dsc_ref[0:1, cols] += dsc
            da_ref[:, cols] = da.astype(bf16)

    blk = pl.BlockSpec((seq, C), lambda b: (b, 0))
    return _pcall(
        body, name="pool_bwd", grid=(T // seq,),
        in_specs=[blk, _resident(pw.shape), _resident((1, C)), blk],
        out_specs=[blk, pl.BlockSpec(pw.shape, lambda b: (0, 0, 0)), pl.BlockSpec((8, C), lambda b: (0, 0))],
        out_shape=[_sds((T, C), bf16), _sds(pw.shape, f32), _sds((8, C), f32)],
        compiler_params=_params(("arbitrary",), VMEM_MID),
    )(a, pw, scale, dy)


def _neumann_inverse(lmat):
    n = lmat.shape[-1]
    ii = lax.broadcasted_iota(jnp.int32, (n, n), 0)
    jj = lax.broadcasted_iota(jnp.int32, (n, n), 1)
    inv = jnp.where((ii == jj)[None], 1.0, 0.0) - lmat
    pw_ = _dg(lmat, lmat, _NN3, True)
    steps = int(math.log2(n)) - 1
    for i in range(steps):
        inv = inv + _dg(inv, pw_, _NN3, True)
        if i < steps - 1:
            pw_ = _dg(pw_, pw_, _NN3, True)
    return inv


@jax.custom_vjp
def unit_lower_inverse(lmat):
    return _neumann_inverse(lmat)


def _unit_lower_inverse_fwd(lmat):
    inv = _neumann_inverse(lmat)
    return inv, inv


def _unit_lower_inverse_bwd(inv, ct):
    return (-_dg(_dg(inv, ct, _TN3, True), inv, _NT3, True),)


unit_lower_inverse.defvjp(_unit_lower_inverse_fwd, _unit_lower_inverse_bwd)


def dn_prep_math(qkv, bg, alog, dtb):
    tt = qkv.shape[0]
    nt = tt // CHUNK
    nb = nt * N_HEADS
    W = N_HEADS * HEAD
    beta_all = jax.nn.sigmoid(bg)
    g_all = -jnp.exp(alog) * _softplus(bg + dtb)

    def heads(fn):
        return jnp.stack([fn(hd).reshape(nt, CHUNK, HEAD) for hd in range(N_HEADS)], axis=1).reshape(nb, CHUNK, HEAD)

    def l2n(x):
        return x * lax.rsqrt(jnp.sum(x * x, axis=-1, keepdims=True) + EPS)

    q = heads(lambda hd: l2n(qkv[:, hd * HEAD:(hd + 1) * HEAD]) * (HEAD ** -0.5))
    k = heads(lambda hd: l2n(qkv[:, W + hd * HEAD:W + (hd + 1) * HEAD]))
    v = heads(lambda hd: qkv[:, 2 * W + hd * HEAD:2 * W + (hd + 1) * HEAD])
    beta = heads(lambda hd: jnp.broadcast_to(beta_all[:, hd:hd + 1], (tt, HEAD)))
    g = heads(lambda hd: jnp.broadcast_to(g_all[:, N_HEADS + hd:N_HEADS + hd + 1], (tt, HEAD)))

    ii = lax.broadcasted_iota(jnp.int32, (CHUNK, CHUNK), 0)
    jj = lax.broadcasted_iota(jnp.int32, (CHUNK, CHUNK), 1)
    tril, strict = (ii >= jj)[None], (ii > jj)[None]
    ones_b = jnp.ones((nb, CHUNK, CHUNK), f32)
    tril_b = jnp.where(tril, ones_b, 0.0)
    eye_b = jnp.where((ii == jj)[None], ones_b, 0.0)

    gcb = bmm_hi(tril_b, g)
    gcol = gcb[:, :, :CHUNK]
    grow = bmm_hi(ones_b, eye_b * gcol)
    gamma = jnp.where(tril, jnp.exp(jnp.where(tril, gcol - grow, 0.0)), 0.0)
    kb = k * beta
    lmat = jnp.where(strict, bmm_nt(kb, k) * gamma, 0.0)
    inv = unit_lower_inverse(lmat)
    egc = jnp.exp(gcb)
    u = bmm_hi(inv, v * beta)
    w = bmm_hi(inv, kb * egc)
    aqk = bmm_nt(q, k) * gamma
    qd = q * egc
    glast = gcb[:, CHUNK - 1:CHUNK, :]
    kd = k * jnp.exp(glast - gcb)
    last = jnp.exp(glast)
    r4 = lambda x: x.reshape((nt, N_HEADS) + x.shape[1:])
    return r4(u), r4(w), r4(qd), r4(kd), r4(aqk), r4(last)


_PREP_DTYPES = (f32, bf16, bf16, bf16, bf16, f32)


def _prep_specs(nt, T):
    nchunks = T // CHUNK
    shapes = [(HEAD,), (HEAD,), (HEAD,), (HEAD,), (CHUNK,), (HEAD,)]
    rows = [CHUNK, CHUNK, CHUNK, CHUNK, CHUNK, 1]
    specs = [pl.BlockSpec((nt, N_HEADS, r, s[0]), lambda i: (i, 0, 0, 0)) for r, s in zip(rows, shapes)]
    outs = [(nchunks, N_HEADS, r, s[0]) for r, s in zip(rows, shapes)]
    return specs, outs


def dn_prep_fwd(qkv, bg, alog, dtb):
    T = qkv.shape[0]
    tt = _tile(T, 256)
    nt = tt // CHUNK
    specs, shapes = _prep_specs(nt, T)

    def body(qkv_ref, bg_ref, alog_ref, dtb_ref, *outs):
        res = dn_prep_math(qkv_ref[...], bg_ref[...], alog_ref[...], dtb_ref[...])
        for o_ref, r, dt in zip(outs, res, _PREP_DTYPES):
            o_ref[...] = r.astype(dt)

    row = lambda w: pl.BlockSpec((tt, w), lambda i: (i, 0))
    return _pcall(
        body, name="dn_prep_fwd", grid=(T // tt,),
        in_specs=[row(qkv.shape[1]), row(HEAD), _resident((1, HEAD)), _resident((1, HEAD))],
        out_specs=specs, out_shape=[_sds(s, dt) for s, dt in zip(shapes, _PREP_DTYPES)],
        compiler_params=_params(("parallel",), VMEM_BIG),
    )(qkv, bg, alog, dtb)


def dn_prep_bwd(qkv, bg, alog, dtb, cts):
    T = qkv.shape[0]
    tt = _tile(T, 256)
    nt = tt // CHUNK
    specs, _ = _prep_specs(nt, T)

    def body(qkv_ref, bg_ref, alog_ref, dtb_ref, *refs):
        ct_refs, (dqkv_ref, dbg_ref, dalog_ref, ddtb_ref) = refs[:6], refs[6:]
        _, vjp = jax.vjp(dn_prep_math, qkv_ref[...], bg_ref[...], alog_ref[...], dtb_ref[...])
        dqkv, dbg, dalog, ddtb = vjp(tuple(r[...].astype(f32) for r in ct_refs))
        dqkv_ref[...] = dqkv
        dbg_ref[...] = dbg.astype(bf16)
        first = pl.program_id(0) == 0
        _acc_rows(dalog_ref, dalog, first)
        _acc_rows(ddtb_ref, ddtb, first)

    row = lambda w: pl.BlockSpec((tt, w), lambda i: (i, 0))
    small = pl.BlockSpec((8, HEAD), lambda i: (0, 0))
    return _pcall(
        body, name="dn_prep_bwd", grid=(T // tt,),
        in_specs=[row(qkv.shape[1]), row(HEAD), _resident((1, HEAD)), _resident((1, HEAD))] + specs,
        out_specs=[row(qkv.shape[1]), row(HEAD), small, small],
        out_shape=[_sds(qkv.shape, f32), _sds((T, HEAD), bf16), _sds((8, HEAD), f32), _sds((8, HEAD), f32)],
        compiler_params=_params(("arbitrary",), VMEM_BIG),
    )(qkv, bg, alog, dtb, *cts)


def dn_step(state, u, w, qd, kd, aqk, last):
    v_new = u - bmm(w, state)
    o = bmm(qd, state) + bmm(aqk, v_new)
    return state * last + bmm_tn(kd, v_new), o


def dn_gate(o, z, onorm):
    return o * lax.rsqrt(jnp.mean(o * o, axis=-1, keepdims=True) + EPS) * onorm * _silu(z)


def _recur_specs(n):
    rows = [CHUNK, CHUNK, CHUNK, CHUNK, CHUNK, 1]
    lanes = [HEAD, HEAD, HEAD, HEAD, CHUNK, HEAD]
    return [pl.BlockSpec((n, 1, r, l), lambda b, hd: (b, hd, 0, 0)) for r, l in zip(rows, lanes)]


def dn_recur_fwd(prep, z, onorm, seq):
    T, C = z.shape
    n = seq // CHUNK

    def body(u_ref, w_ref, qd_ref, kd_ref, aqk_ref, last_ref, z_ref, on_ref, y_ref, o_ref):
        def step(i, state):
            new, o = dn_step(state, u_ref[i], w_ref[i], qd_ref[i], kd_ref[i], aqk_ref[i], last_ref[i])
            o_ref[i] = o
            return new

        lax.fori_loop(0, n, step, jnp.zeros((1, HEAD, HEAD), f32))
        y_ref[...] = dn_gate(o_ref[...].reshape(seq, HEAD), z_ref[...], on_ref[...]).astype(bf16)

    blk = pl.BlockSpec((seq, HEAD), lambda b, hd: (b, hd))
    return _pcall(
        body, name="dn_recur_fwd", grid=(T // seq, N_HEADS),
        in_specs=_recur_specs(n) + [blk, pl.BlockSpec((1, HEAD), lambda b, hd: (0, 0))],
        out_specs=blk, out_shape=_sds((T, C), bf16),
        scratch_shapes=[pltpu.VMEM((n, 1, CHUNK, HEAD), f32)],
        compiler_params=_params(("parallel", "parallel"), VMEM_MID),
    )(*prep, z, onorm)


def dn_recur_bwd(prep, z, onorm, dy, seq):
    T, C = z.shape
    n = seq // CHUNK
    nchunks = T // CHUNK

    def body(u_ref, w_ref, qd_ref, kd_ref, aqk_ref, last_ref, z_ref, on_ref, dy_ref,
             du_ref, dw_ref, dqd_ref, dkd_ref, daqk_ref, dlast_ref, dz_ref, don_ref, st_ref, o_ref):
        args = lambda i: tuple(r[i].astype(f32) for r in (u_ref, w_ref, qd_ref, kd_ref, aqk_ref, last_ref))

        def fstep(i, state):
            st_ref[i] = state
            new, o = dn_step(state, *args(i))
            o_ref[i] = o
            return new

        lax.fori_loop(0, n, fstep, jnp.zeros((1, HEAD, HEAD), f32))
        _, gate_vjp = jax.vjp(dn_gate, o_ref[...].reshape(seq, HEAD), z_ref[...], on_ref[...])
        do, dz, don = gate_vjp(dy_ref[...])
        dz_ref[...] = dz.astype(bf16)
        o_ref[...] = do.reshape(n, 1, CHUNK, HEAD)

        def bstep(j, dstate):
            i = n - 1 - j
            _, vjp = jax.vjp(dn_step, st_ref[i], *args(i))
            ds, du, dw, dqd, dkd, daqk, dlast = vjp((dstate, o_ref[i]))
            du_ref[i], dw_ref[i], dqd_ref[i], dkd_ref[i], daqk_ref[i], dlast_ref[i] = du, dw, dqd, dkd, daqk, dlast
            return ds

        lax.fori_loop(0, n, bstep, jnp.zeros((1, HEAD, HEAD), f32))
        first = jnp.logical_and(pl.program_id(0) == 0, pl.program_id(1) == 0)
        _acc_rows(don_ref, don, first)

    blk = pl.BlockSpec((seq, HEAD), lambda b, hd: (b, hd))
    rows = [CHUNK, CHUNK, CHUNK, CHUNK, CHUNK, 1]
    lanes = [HEAD, HEAD, HEAD, HEAD, CHUNK, HEAD]
    return _pcall(
        body, name="dn_recur_bwd", grid=(T // seq, N_HEADS),
        in_specs=_recur_specs(n) + [blk, pl.BlockSpec((1, HEAD), lambda b, hd: (0, 0)), blk],
        out_specs=_recur_specs(n) + [blk, pl.BlockSpec((8, HEAD), lambda b, hd: (0, 0))],
        out_shape=[_sds((nchunks, N_HEADS, r, l), f32) for r, l in zip(rows, lanes)]
        + [_sds((T, C), bf16), _sds((8, HEAD), f32)],
        scratch_shapes=[pltpu.VMEM((n, 1, HEAD, HEAD), f32), pltpu.VMEM((n, 1, CHUNK, HEAD), f32)],
        compiler_params=_params(("arbitrary", "arbitrary"), VMEM_MID),
    )(*prep, z, onorm, dy)


def sgu_math(up, vp, ng, nb, sw, sbias):
    S = up.shape[0]
    nblk = S // SGU_BLOCK
    u = jax.nn.gelu(up, approximate=True)
    v = jax.nn.gelu(vp, approximate=True)
    xc = v - jnp.mean(v, axis=-1, keepdims=True)
    vn = xc * lax.rsqrt(jnp.mean(xc * xc, axis=-1, keepdims=True) + EPS) * ng + nb
    ii = lax.broadcasted_iota(jnp.int32, (SGU_BLOCK, SGU_BLOCK), 0)
    jj = lax.broadcasted_iota(jnp.int32, (SGU_BLOCK, SGU_BLOCK), 1)
    outs = []
    for hd in range(N_HEADS):
        vh = vn[:, hd * HEAD:(hd + 1) * HEAD].reshape(nblk, SGU_BLOCK, HEAD)
        ws = jnp.where(ii >= jj, sw[hd], 0.0)
        mixed = bmm(jnp.broadcast_to(ws[None], (nblk, SGU_BLOCK, SGU_BLOCK)), vh) + sbias[hd][None]
        outs.append(mixed.reshape(S, HEAD))
    return u * jnp.concatenate(outs, axis=-1)


def sgu_fwd(up, vp, ng, nb, sw, sbias, seq):
    T, C = up.shape

    def body(up_ref, vp_ref, ng_ref, nb_ref, sw_ref, sb_ref, o_ref):
        o_ref[...] = sgu_math(up_ref[...], vp_ref[...], ng_ref[...], nb_ref[...], sw_ref[...],
                              sb_ref[...]).astype(bf16)

    blk = pl.BlockSpec((seq, C), lambda b: (b, 0))
    return _pcall(
        body, name="sgu_fwd", grid=(T // seq,),
        in_specs=[blk, blk, _resident((1, C)), _resident((1, C)), _resident(sw.shape), _resident(sbias.shape)],
        out_specs=blk, out_shape=_sds((T, C), bf16),
        compiler_params=_params(("parallel",), VMEM_BIG),
    )(up, vp, ng, nb, sw, sbias)


def sgu_bwd(up, vp, ng, nb, sw, sbias, dy, seq):
    T, C = up.shape

    def body(up_ref, vp_ref, ng_ref, nb_ref, sw_ref, sb_ref, dy_ref,
             dup_ref, dvp_ref, dng_ref, dnb_ref, dsw_ref, dsb_ref):
        _, vjp = jax.vjp(sgu_math, up_ref[...], vp_ref[...], ng_ref[...], nb_ref[...], sw_ref[...], sb_ref[...])
        dup, dvp, dng, dnb, dsw, dsb = vjp(dy_ref[...])
        dup_ref[...] = dup.astype(bf16)
        dvp_ref[...] = dvp.astype(bf16)
        first = pl.program_id(0) == 0
        _acc_rows(dng_ref, dng, first)
        _acc_rows(dnb_ref, dnb, first)

        @pl.when(first)
        def _():
            dsw_ref[...] = jnp.zeros_like(dsw_ref)
            dsb_ref[...] = jnp.zeros_like(dsb_ref)

        dsw_ref[...] += dsw
        dsb_ref[...] += dsb

    blk = pl.BlockSpec((seq, C), lambda b: (b, 0))
    small = pl.BlockSpec((8, C), lambda b: (0, 0))
    return _pcall(
        body, name="sgu_bwd", grid=(T // seq,),
        in_specs=[blk, blk, _resident((1, C)), _resident((1, C)), _resident(sw.shape), _resident(sbias.shape), blk],
        out_specs=[blk, blk, small, small, pl.BlockSpec(sw.shape, lambda b: (0, 0, 0)),
                   pl.BlockSpec(sbias.shape, lambda b: (0, 0, 0))],
        out_shape=[_sds((T, C), bf16), _sds((T, C), bf16), _sds((8, C), f32), _sds((8, C), f32),
                   _sds(sw.shape, f32), _sds(sbias.shape, f32)],
        compiler_params=_params(("arbitrary",), VMEM_BIG),
    )(up, vp, ng, nb, sw, sbias, dy)


def loss_fwd_bwd(h, gain, target):
    T, D = h.shape
    tm = _tile(T, 512)

    def body(h_ref, gain_ref, t_ref, loss_ref, dh_ref, dgain_ref):
        gain_ = gain_ref[...]
        y, xhat, rstd = _rms_fwd(h_ref[...], gain_)
        err = y - t_ref[...]
        part = 0.5 * jnp.sum(jnp.mean(err * err, axis=-1, keepdims=True), axis=0, keepdims=True)
        dh, dgain = _rms_bwd(err * (1.0 / D), xhat, rstd, gain_)
        dh_ref[...] = dh
        first = pl.program_id(0) == 0
        _acc_rows(dgain_ref, dgain, first)

        @pl.when(first)
        def _():
            loss_ref[...] = jnp.zeros_like(loss_ref)

        loss_ref[...] += jnp.broadcast_to(part, loss_ref.shape)

    row = pl.BlockSpec((tm, D), lambda i: (i, 0))
    return _pcall(
        body, name="loss_fwd_bwd", grid=(T // tm,),
        in_specs=[row, _resident((1, D)), row],
        out_specs=[pl.BlockSpec((8, 128), lambda i: (0, 0)), row, pl.BlockSpec((8, D), lambda i: (0, 0))],
        out_shape=[_sds((8, 128), f32), _sds((T, D), f32), _sds((8, D), f32)],
        compiler_params=_params(("arbitrary",), VMEM_MID),
    )(h, gain, target)


def adamw(w, g, m, v):
    R, C = w.shape
    tr = max(c for c in range(8, 513, 8) if R % c == 0)
    c1, c2 = 1.0 - ADAM_B1 ** ADAM_STEP, 1.0 - ADAM_B2 ** ADAM_STEP

    def body(w_ref, g_ref, m_ref, v_ref, d_ref, nm_ref, nv_ref):
        gg = g_ref[...]
        nm = ADAM_B1 * m_ref[...] + (1.0 - ADAM_B1) * gg
        nv = ADAM_B2 * v_ref[...] + (1.0 - ADAM_B2) * (gg * gg)
        d_ref[...] = -ADAM_LR * ((nm / c1) / (jnp.sqrt(nv / c2) + ADAM_EPS) + ADAM_WD * w_ref[...])
        nm_ref[...] = nm
        nv_ref[...] = nv

    blk = pl.BlockSpec((tr, C), lambda i: (i, 0))
    return _pcall(
        body, name="adamw", grid=(R // tr,), in_specs=[blk] * 4, out_specs=[blk] * 3,
        out_shape=[_sds((R, C), f32)] * 3, compiler_params=_params(("parallel",), VMEM_MID),
    )(w, g, m, v)


def sum8(parts):
    _, R, C = parts.shape
    tr = R
    for cand in (512, 352, 336, 320, 256, 128):
        if R % cand == 0:
            tr = cand
            break

    def body(p_ref, o_ref):
        acc = p_ref[0].astype(f32)
        for i in range(1, N_DEV):
            acc = acc + p_ref[i].astype(f32)
        o_ref[...] = acc

    return _pcall(
        body, name="sum8", grid=(R // tr,),
        in_specs=[pl.BlockSpec((N_DEV, tr, C), lambda i: (0, i, 0))],
        out_specs=pl.BlockSpec((tr, C), lambda i: (i, 0)), out_shape=_sds((R, C), f32),
        compiler_params=_params(("parallel",), VMEM_MID),
    )(parts)


_FLIPS = [(fx, fy, fc) for fx in (0, 1) for fy in (0, 1) for fc in (0, 1)][1:]
_HBM = pl.BlockSpec(memory_space=pltpu.HBM)


def _me():
    return lax.axis_index("x"), lax.axis_index("y"), lax.axis_index("c")


def _peer(flip):
    x, y, c = _me()
    fx, fy, fc = flip
    return (1 - x if fx else x, 1 - y if fy else y, 1 - c if fc else c)


def _lin(dev):
    return 4 * dev[0] + 2 * dev[1] + dev[2]


def _src_block(ref, rows, dev, whole):
    return ref if whole else ref.at[pl.ds(pl.multiple_of(dev * rows, 16), rows)]


_SEM = pl.BlockSpec(memory_space=pltpu.SEMAPHORE)
_EFFECT = pltpu.SideEffectType.DATAFLOW_SIDE_EFFECTING


def push_start(srcs, whole, after, name):
    n = len(srcs)
    rows = [s.shape[0] if whole else s.shape[0] // N_DEV for s in srcs]
    land_shapes = [(N_DEV, r, s.shape[1]) for r, s in zip(rows, srcs)]

    def body(*refs):
        src_refs, land_refs = refs[:n], refs[n:2 * n]
        send_sems, recv_sems, token = refs[2 * n + 1], refs[2 * n + 2], refs[-1]
        me = _lin(_me())
        for i in range(n):
            for k, flip in enumerate(_FLIPS):
                peer = _peer(flip)
                pltpu.make_async_remote_copy(
                    src_ref=_src_block(src_refs[i], rows[i], _lin(peer), whole), dst_ref=land_refs[i].at[me],
                    send_sem=send_sems.at[i * 7 + k], recv_sem=recv_sems.at[i * 7 + k],
                    device_id=peer, device_id_type=MESH).start()
        token[...] = jnp.zeros_like(token)

    hbm = lambda a: pltpu.with_memory_space_constraint(a, pltpu.HBM)
    outs = _pcall(
        body, name=name,
        in_specs=[_HBM] * (2 * n) + [pl.BlockSpec(memory_space=pl.ANY)],
        out_specs=[_SEM, _SEM] + [_HBM] * (2 * n) + [pl.BlockSpec(memory_space=pltpu.VMEM)],
        out_shape=[pltpu.SemaphoreType.DMA((7 * n,)), pltpu.SemaphoreType.DMA((7 * n,))]
        + [pltpu.HBM(s.shape, s.dtype) for s in srcs]
        + [pltpu.HBM(shp, s.dtype) for shp, s in zip(land_shapes, srcs)] + [_sds((8, 128), f32)],
        input_output_aliases={i: 2 + i for i in range(2 * n)},
        compiler_params=pltpu.CompilerParams(has_side_effects=_EFFECT),
    )(*[hbm(s) for s in srcs], *[hbm(lax.empty(shp, s.dtype)) for shp, s in zip(land_shapes, srcs)], after)
    return outs[0], outs[1], list(outs[2:2 + n]), list(outs[2 + n:2 + 2 * n]), outs[-1]


def push_wait(handle, whole, after, name):
    send_sems, recv_sems, srcs, lands, _ = handle
    n = len(srcs)
    rows = [l.shape[1] for l in lands]

    def body(*refs):
        src_refs, land_refs = refs[:n], refs[n:2 * n]
        send_sems_, recv_sems_ = refs[2 * n], refs[2 * n + 1]
        for i in range(n):
            for k, flip in enumerate(_FLIPS):
                peer = _peer(flip)
                cp = pltpu.make_async_remote_copy(
                    src_ref=_src_block(src_refs[i], rows[i], _lin(peer), whole), dst_ref=land_refs[i].at[_lin(peer)],
                    send_sem=send_sems_.at[i * 7 + k], recv_sem=recv_sems_.at[i * 7 + k],
                    device_id=peer, device_id_type=MESH)
                cp.wait_send()
                cp.wait_recv()

    outs = _pcall(
        body, name=name,
        in_specs=[_HBM] * (2 * n) + [_SEM, _SEM, pl.BlockSpec(memory_space=pl.ANY)],
        out_specs=[_HBM] * (2 * n),
        out_shape=[pltpu.HBM(a.shape, a.dtype) for a in srcs + lands],
        input_output_aliases={i: i for i in range(2 * n)},
        compiler_params=pltpu.CompilerParams(has_side_effects=_EFFECT),
    )(*srcs, *lands, send_sems, recv_sems, after)
    return list(outs[:n]), list(outs[n:])


def place_own(srcs, lands, whole, name):
    n = len(srcs)
    rows = [l.shape[1] for l in lands]

    def body(*refs):
        src_refs, out_refs, sems = refs[:n], refs[2 * n:3 * n], refs[3 * n]
        me = _lin(_me())
        cps = [pltpu.make_async_copy(_src_block(src_refs[i], rows[i], me, whole), out_refs[i].at[me], sems.at[i])
               for i in range(n)]
        for cp in cps:
            cp.start()
        for cp in cps:
            cp.wait()

    return _pcall(
        body, name=name, in_specs=[_HBM] * (2 * n), out_specs=[_HBM] * n,
        out_shape=[_sds(l.shape, l.dtype) for l in lands],
        input_output_aliases={n + i: i for i in range(n)},
        scratch_shapes=[pltpu.SemaphoreType.DMA((n,))],
    )(*srcs, *lands)


def all_gather_small(x):
    R, C = x.shape

    def body(x_ref, o_ref, send_sems, recv_sems):
        me = _lin(_me())
        o_ref[me] = x_ref[...]
        sends = []
        for k, flip in enumerate(_FLIPS):
            rc = pltpu.make_async_remote_copy(
                src_ref=x_ref, dst_ref=o_ref.at[me], send_sem=send_sems.at[k], recv_sem=recv_sems.at[k],
                device_id=_peer(flip), device_id_type=MESH)
            rc.start()
            sends.append(rc)
        for k, flip in enumerate(_FLIPS):
            pltpu.make_async_remote_copy(
                src_ref=x_ref, dst_ref=o_ref.at[_lin(_peer(flip))], send_sem=send_sems.at[k],
                recv_sem=recv_sems.at[k], device_id=_peer(flip), device_id_type=MESH).wait_recv()
        for rc in sends:
            rc.wait_send()

    vm = pl.BlockSpec(memory_space=pltpu.VMEM)
    return _pcall(
        body, name="all_gather_small", in_specs=[vm], out_specs=vm, out_shape=_sds((N_DEV, R, C), x.dtype),
        scratch_shapes=[pltpu.SemaphoreType.DMA((7,)), pltpu.SemaphoreType.DMA((7,))],
        compiler_params=_params(None, VMEM_MID),
    )(x)


def all_reduce_small(x):
    R, C = x.shape

    def body(x_ref, o_ref, buf_ref, send_sems, recv_sems):
        me = _lin(_me())
        buf_ref[me] = x_ref[...]
        sends = []
        for k, flip in enumerate(_FLIPS):
            rc = pltpu.make_async_remote_copy(
                src_ref=x_ref, dst_ref=buf_ref.at[me], send_sem=send_sems.at[k], recv_sem=recv_sems.at[k],
                device_id=_peer(flip), device_id_type=MESH)
            rc.start()
            sends.append(rc)
        for k, flip in enumerate(_FLIPS):
            pltpu.make_async_remote_copy(
                src_ref=x_ref, dst_ref=buf_ref.at[_lin(_peer(flip))], send_sem=send_sems.at[k],
                recv_sem=recv_sems.at[k], device_id=_peer(flip), device_id_type=MESH).wait_recv()
        for rc in sends:
            rc.wait_send()
        acc = buf_ref[0]
        for p in range(1, N_DEV):
            acc = acc + buf_ref[p]
        o_ref[...] = acc

    vm = pl.BlockSpec(memory_space=pltpu.VMEM)
    return _pcall(
        body, name="all_reduce_small", in_specs=[vm], out_specs=vm, out_shape=_sds((R, C), f32),
        scratch_shapes=[pltpu.VMEM((N_DEV, R, C), f32), pltpu.SemaphoreType.DMA((7,)),
                        pltpu.SemaphoreType.DMA((7,))],
        compiler_params=_params(None, VMEM_MID),
    )(x)


def _pack(arrs):
    flat = jnp.concatenate([a.reshape(-1).astype(f32) for a in arrs])
    n = flat.shape[0]
    rows = -(-n // 128)
    rows = -(-rows // 8) * 8
    return jnp.pad(flat, (0, rows * 128 - n)).reshape(rows, 128)


def _unpack(buf, shapes):
    flat = buf.reshape(-1)
    out, off = [], 0
    for s in shapes:
        n = math.prod(s)
        out.append(flat[off:off + n].reshape(s))
        off += n
    return out


def _row(v):
    return v.reshape(1, -1)


def _lane_row(vals, offset):
    return jnp.pad(vals.reshape(1, -1), ((0, 0), (offset, HEAD - offset - vals.shape[-1])))


def kernel(x, ffn1_norm, ffn1_w_gate, ffn1_w_up, ffn1_w_down, mix_norm, ffn2_norm, ffn2_w_gate, ffn2_w_up, ffn2_w_down, ab_w_in, pool_w, pool_scale, dn_conv_w, dn_a_log, dn_dt_bias, dn_out_norm, ab_w_out, cd_w_in, sgu_norm_g, sgu_norm_b, sgu_w, sgu_bias, sc_conv_w, cd_w_out, final_norm, loss_target, m_ffn1_norm, m_ffn1_w_gate, m_ffn1_w_up, m_ffn1_w_down, m_mix_norm, m_ffn2_norm, m_ffn2_w_gate, m_ffn2_w_up, m_ffn2_w_down, m_ab_w_in, m_pool_w, m_pool_scale, m_dn_conv_w, m_dn_a_log, m_dn_dt_bias, m_dn_out_norm, m_ab_w_out, m_cd_w_in, m_sgu_norm_g, m_sgu_norm_b, m_sgu_w, m_sgu_bias, m_sc_conv_w, m_cd_w_out, m_final_norm, v_ffn1_norm, v_ffn1_w_gate, v_ffn1_w_up, v_ffn1_w_down, v_mix_norm, v_ffn2_norm, v_ffn2_w_gate, v_ffn2_w_up, v_ffn2_w_down, v_ab_w_in, v_pool_w, v_pool_scale, v_dn_conv_w, v_dn_a_log, v_dn_dt_bias, v_dn_out_norm, v_ab_w_out, v_cd_w_in, v_sgu_norm_g, v_sgu_norm_b, v_sgu_w, v_sgu_bias, v_sc_conv_w, v_cd_w_out, v_final_norm):
    names = ['ffn1_norm', 'ffn1_w_gate', 'ffn1_w_up', 'ffn1_w_down', 'mix_norm', 'ffn2_norm', 'ffn2_w_gate',
             'ffn2_w_up', 'ffn2_w_down', 'ab_w_in', 'pool_w', 'pool_scale', 'dn_conv_w', 'dn_a_log', 'dn_dt_bias',
             'dn_out_norm', 'ab_w_out', 'cd_w_in', 'sgu_norm_g', 'sgu_norm_b', 'sgu_w', 'sgu_bias', 'sc_conv_w',
             'cd_w_out', 'final_norm']
    loc = locals()
    W = {n: loc[n] for n in names}
    M = {n: loc['m_' + n] for n in names}
    V = {n: loc['v_' + n] for n in names}

    B, S, D = x.shape
    T = B * S
    me = _lin(_me())

    def rows_of(w):
        return w.astype(bf16).T

    def layer_shards(layer):
        e = layer // 2
        shards = [rows_of(W['ffn1_w_gate'][layer]), rows_of(W['ffn1_w_up'][layer]), W['ffn1_w_down'][layer].astype(bf16)]
        if layer % 2 == 0:
            win = jnp.pad(rows_of(W['ab_w_in'][e]), ((0, AB_SHARD_PAD - AB_SHARD), (0, 0)))
            wout = W['ab_w_out'][e].astype(bf16)
        else:
            win = rows_of(W['cd_w_in'][e])
            wout = W['cd_w_out'][e].astype(bf16)
        shards += [win, wout]
        shards += [rows_of(W['ffn2_w_gate'][layer]), rows_of(W['ffn2_w_up'][layer]), W['ffn2_w_down'][layer].astype(bf16)]
        return shards

    def finish_gather(handle, after, layer):
        srcs, lands = push_wait(handle, True, after, name=f"gather_wait_{layer}")
        full = [f.reshape(-1, D) for f in place_own(srcs, lands, True, name=f"gather_own_{layer}")]
        if layer % 2 == 0:
            wi = full[3].reshape(N_DEV, AB_SHARD_PAD, D)[:, :AB_SHARD].reshape(AB_IN, D)
            full[3] = jnp.pad(wi, ((0, AB_IN_PAD - AB_IN), (0, 0)))
        return full

    def tied(gain, token):
        return gain if token is None else gain + token[0:1, 0:1]

    gathered = [None] * DEPTH
    gather_handle = push_start(layer_shards(0), True, jnp.zeros((8, 128), f32), name="gather_start_0")

    small_shards = [W['dn_conv_w'], W['sgu_norm_g'], W['sgu_norm_b'], W['sc_conv_w']]
    gs = all_gather_small(_pack(small_shards))
    per_dev = [_unpack(gs[p], [a.shape for a in small_shards]) for p in range(N_DEV)]
    dn_conv_full, sgu_g_full, sgu_b_full, sc_conv_full = [
        jnp.concatenate([per_dev[p][i] for p in range(N_DEV)], axis=-1) for i in range(4)]

    h = x.reshape(T, D)
    saved = []
    for layer in range(DEPTH):
        e = layer // 2
        gathered[layer] = finish_gather(gather_handle, h, layer)
        token = None
        if layer + 1 < DEPTH:
            gather_handle = push_start(layer_shards(layer + 1), True, gathered[layer][0],
                                       name=f"gather_start_{layer + 1}")
            token = gather_handle[4]
        wg1, wu1, wd1, win, wout, wg2, wu2, wd2 = gathered[layer]
        sv = {'h0': h}
        h, sv['g1'], sv['u1'] = ffn_fwd(h, tied(_row(W['ffn1_norm'][layer]), token), wg1, wu1, wd1)
        sv['h1'] = h
        if layer % 2 == 0:
            a_in, qkv_pre, z, bg = in_proj_fwd(h, _row(W['mix_norm'][layer]), win, [512, 1536, 512, 128])
            ya = pool_fwd(a_in, W['pool_w'][e], _row(W['pool_scale'][e]), S)
            qkv = seq_chan_fwd(conv_silu_math, dn_conv_full[e], [qkv_pre], S, f32, "dn_conv_fwd")
            alog, dtb = _lane_row(W['dn_a_log'][e], N_HEADS), _lane_row(W['dn_dt_bias'][e], N_HEADS)
            prep = dn_prep_fwd(qkv, bg, alog, dtb)
            yb = dn_recur_fwd(prep, z, _row(W['dn_out_norm'][e]), S)
            sv.update(a_in=a_in, qkv_pre=qkv_pre, z=z, bg=bg, qkv=qkv, alog=alog, dtb=dtb, prep=prep)
        else:
            up, vp, xd, bgate, cg = in_proj_fwd(h, _row(W['mix_norm'][layer]), win, [512] * 5)
            sbias = W['sgu_bias'][e].reshape(N_HEADS, SGU_BLOCK, 1)
            ya = sgu_fwd(up, vp, _row(sgu_g_full[e]), _row(sgu_b_full[e]), W['sgu_w'][e], sbias,
                         _tile(S, SGU_TILE))
            yb = seq_chan_fwd(gated_conv_math, sc_conv_full[e], [xd, bgate, cg], S, bf16, "sc_conv_fwd")
            sv.update(up=up, vp=vp, xd=xd, bgate=bgate, cg=cg, sbias=sbias)
        sv.update(ya=ya, yb=yb)
        h = out_proj_fwd(h, ya, yb, wout)
        sv['h2'] = h
        h, sv['g2'], sv['u2'] = ffn_fwd(h, _row(W['ffn2_norm'][layer]), wg2, wu2, wd2)
        saved.append(sv)

    loss_part, dh, dfinal = loss_fwd_bwd(h, _row(W['final_norm']), loss_target.reshape(T, D))
    loss = lax.psum(loss_part[0, 0], ("x", "y", "c"))

    G = {}
    G['final_norm'] = dfinal[0]
    for n in ('ffn1_norm', 'mix_norm', 'ffn2_norm'):
        G[n] = [None] * DEPTH
    for n in ('pool_w', 'pool_scale', 'dn_conv_w', 'dn_a_log', 'dn_dt_bias', 'dn_out_norm',
              'sgu_norm_g', 'sgu_norm_b', 'sgu_w', 'sgu_bias', 'sc_conv_w'):
        G[n] = [None] * 2
    big = [None] * DEPTH
    scatter_handle = None

    def finish_scatter(handle, after, layer):
        srcs, lands = push_wait(handle, False, after, name=f"scatter_wait_{layer}")
        return place_own(srcs, lands, False, name=f"scatter_own_{layer}")

    for layer in reversed(range(DEPTH)):
        e = layer // 2
        sv = saved[layer]
        wg1, wu1, wd1, win, wout, wg2, wu2, wd2 = gathered[layer]
        token = None if scatter_handle is None else scatter_handle[4]
        dh, dgain, xn, act, dg, du, dy = ffn_bwd(sv['h2'], tied(_row(W['ffn2_norm'][layer]), token), sv['g2'],
                                                  sv['u2'], dh, wg2, wu2, wd2)
        G['ffn2_norm'][layer] = dgain[0]
        dwg2, dwu2, dwd2 = wgrad(dg, xn), wgrad(du, xn), wgrad(act, dy)
        dya, dyb, dwout = out_proj_bwd(dh, sv['ya'], sv['yb'], wout)
        if layer % 2 == 0:
            da, dpw, dsc = pool_bwd(sv['a_in'], W['pool_w'][e], _row(W['pool_scale'][e]), dya, S)
            G['pool_w'][e], G['pool_scale'][e] = dpw, dsc[0]
            *cts, dz, don = dn_recur_bwd(sv['prep'], sv['z'], _row(W['dn_out_norm'][e]), dyb, S)
            G['dn_out_norm'][e] = don[0]
            dqkv, dbg, dalog, ddtb = dn_prep_bwd(sv['qkv'], sv['bg'], sv['alog'], sv['dtb'], cts)
            G['dn_a_log'][e], G['dn_dt_bias'][e] = dalog[0, N_HEADS:2 * N_HEADS], ddtb[0, N_HEADS:2 * N_HEADS]
            dqkv_pre, dconv = seq_chan_bwd(conv_silu_math, dn_conv_full[e], [sv['qkv_pre']], dqkv, S, bf16,
                                           "dn_conv_bwd")
            G['dn_conv_w'][e] = dconv
            dpieces = [da, dqkv_pre, dz, dbg]
        else:
            dup, dvp, dng, dnb, dsw, dsb = sgu_bwd(sv['up'], sv['vp'], _row(sgu_g_full[e]), _row(sgu_b_full[e]),
                                                   W['sgu_w'][e], sv['sbias'], dya, _tile(S, SGU_TILE))
            G['sgu_norm_g'][e], G['sgu_norm_b'][e] = dng[0], dnb[0]
            G['sgu_w'][e], G['sgu_bias'][e] = dsw, dsb.reshape(N_HEADS, SGU_BLOCK)
            dxd, dbgate, dcg, dscw = seq_chan_bwd(gated_conv_math, sc_conv_full[e],
                                                  [sv['xd'], sv['bgate'], sv['cg']], dyb, S, bf16, "sc_conv_bwd")
            G['sc_conv_w'][e] = dscw
            dpieces = [dup, dvp, dxd, dbgate, dcg]
        dh, dgain, dwin = in_proj_bwd(sv['h1'], _row(W['mix_norm'][layer]), dpieces, dh, win)
        G['mix_norm'][layer] = dgain[0]
        if layer % 2 == 0:
            dwin = jnp.pad(dwin[:AB_IN].reshape(N_DEV, AB_SHARD, D), ((0, 0), (0, AB_SHARD_PAD - AB_SHARD), (0, 0)))
            dwin = dwin.reshape(N_DEV * AB_SHARD_PAD, D)
        dh, dgain, xn, act, dg, du, dy = ffn_bwd(sv['h0'], _row(W['ffn1_norm'][layer]), sv['g1'], sv['u1'], dh,
                                                  wg1, wu1, wd1)
        G['ffn1_norm'][layer] = dgain[0]
        dwg1, dwu1, dwd1 = wgrad(dg, xn), wgrad(du, xn), wgrad(act, dy)
        after = jnp.zeros((8, 128), f32)
        if scatter_handle is not None:
            big[layer + 1] = finish_scatter(scatter_handle, dh, layer + 1)
            after = big[layer + 1][0]
        scatter_handle = push_start([dwg1, dwu1, dwd1, dwin, dwout, dwg2, dwu2, dwd2], False, after,
                                    name=f"scatter_start_{layer}")

    big[0] = finish_scatter(scatter_handle, dh, 0)
    grad_x = dh.reshape(B, S, D)

    small_names = ['ffn1_norm', 'mix_norm', 'ffn2_norm', 'pool_w', 'pool_scale', 'dn_conv_w', 'dn_a_log',
                   'dn_dt_bias', 'dn_out_norm', 'sgu_norm_g', 'sgu_norm_b', 'sgu_w', 'sgu_bias', 'sc_conv_w',
                   'final_norm']
    small_g = [G[n] if n == 'final_norm' else jnp.stack(G[n]) for n in small_names]
    reduced = _unpack(all_reduce_small(_pack(small_g)), [a.shape for a in small_g])
    grads = {}
    for n, g in zip(small_names, reduced):
        if n in ('dn_conv_w', 'sgu_norm_g', 'sgu_norm_b', 'sc_conv_w'):
            c = W[n].shape[-1]
            g = lax.dynamic_slice_in_dim(g, me * c, c, axis=g.ndim - 1)
        grads[n] = g

    def stack_layers(idx, transpose, sel):
        out = []
        for layer in sel:
            g = sum8(big[layer][idx])
            if idx == 3 and layer % 2 == 0:
                g = g[:AB_SHARD]
            out.append(g.T if transpose else g)
        return jnp.stack(out)

    all_layers, even, odd = range(DEPTH), range(0, DEPTH, 2), range(1, DEPTH, 2)
    grads['ffn1_w_gate'] = stack_layers(0, True, all_layers)
    grads['ffn1_w_up'] = stack_layers(1, True, all_layers)
    grads['ffn1_w_down'] = stack_layers(2, False, all_layers)
    grads['ab_w_in'] = stack_layers(3, True, even)
    grads['cd_w_in'] = stack_layers(3, True, odd)
    grads['ab_w_out'] = stack_layers(4, False, even)
    grads['cd_w_out'] = stack_layers(4, False, odd)
    grads['ffn2_w_gate'] = stack_layers(5, True, all_layers)
    grads['ffn2_w_up'] = stack_layers(6, True, all_layers)
    grads['ffn2_w_down'] = stack_layers(7, False, all_layers)

    big_names = ['ffn1_w_gate', 'ffn1_w_up', 'ffn1_w_down', 'ffn2_w_gate', 'ffn2_w_up', 'ffn2_w_down',
                 'ab_w_in', 'ab_w_out', 'cd_w_in', 'cd_w_out']
    delta, new_m, new_v = {}, {}, {}
    for n in big_names:
        shp = W[n].shape
        two = lambda a: a.reshape(-1, shp[-1])
        d_, m_, v_ = adamw(two(W[n]), two(grads[n]), two(M[n]), two(V[n]))
        delta[n], new_m[n], new_v[n] = d_.reshape(shp), m_.reshape(shp), v_.reshape(shp)
    shapes = [W[n].shape for n in small_names]
    d_, m_, v_ = adamw(_pack([W[n] for n in small_names]), _pack([grads[n] for n in small_names]),
                       _pack([M[n] for n in small_names]), _pack([V[n] for n in small_names]))
    for n, a, b_, c_ in zip(small_names, _unpack(d_, shapes), _unpack(m_, shapes), _unpack(v_, shapes)):
        delta[n], new_m[n], new_v[n] = a, b_, c_

    return (loss, grad_x, *[grads[n] for n in names], *[delta[n] for n in names],
            *[new_m[n] for n in names], *[new_v[n] for n in names])
```

```python
import functools
import math

import jax
import jax.numpy as jnp
from jax import lax
from jax.experimental import pallas as pl
from jax.experimental.pallas import tpu as pltpu

f32, bf16 = jnp.float32, jnp.bfloat16

D_MODEL = 1024
DEPTH = 4
CHUNK = 64
POOL_WINDOWS = (2, 4, 8, 16)
HEAD = 128
N_HEADS = 4
SGU_BLOCK = 128
SGU_TILE = 512
FFN_DIM = 2816
AB_IN = 2568
AB_IN_PAD = 2688
AB_SHARD = 321
AB_SHARD_PAD = 336
EPS = 1e-6
N_DEV = 8
MESH = pl.DeviceIdType.MESH

ADAM_LR, ADAM_B1, ADAM_B2, ADAM_EPS, ADAM_WD, ADAM_STEP = 0.001, 0.9, 0.999, 1e-08, 0.01, 10

VMEM_BIG = 56 * 1024 * 1024
VMEM_MID = 40 * 1024 * 1024


def _pcall(body, **kw):
    return pl.pallas_call(body, **kw)


def _params(sem=None, vmem=None):
    return pltpu.CompilerParams(dimension_semantics=sem, vmem_limit_bytes=vmem)


def _sds(shape, dtype):
    return jax.ShapeDtypeStruct(shape, dtype)


_NN2, _NT2, _TN2 = (((1,), (0,)), ((), ())), (((1,), (1,)), ((), ())), (((0,), (0,)), ((), ()))
_NN3, _NT3, _TN3 = (((2,), (1,)), ((0,), (0,))), (((2,), (2,)), ((0,), (0,))), (((1,), (1,)), ((0,), (0,)))


def _dg(a, b, dims, hi):
    if hi:
        return lax.dot_general(a.astype(f32), b.astype(f32), dims, preferred_element_type=f32,
                               precision=lax.Precision.HIGHEST)
    return lax.dot_general(a.astype(bf16), b.astype(bf16), dims, preferred_element_type=f32)


def _make_mm(nn, nt, tn, hi):
    @jax.custom_vjp
    def mm(a, b):
        return _dg(a, b, nn, hi)

    def mm_bwd(res, ct):
        a, b = res
        return _dg(ct, b, nt, hi).astype(a.dtype), _dg(a, ct, tn, hi).astype(b.dtype)

    mm.defvjp(lambda a, b: (_dg(a, b, nn, hi), (a, b)), mm_bwd)

    @jax.custom_vjp
    def mm_nt(a, b):
        return _dg(a, b, nt, hi)

    def mm_nt_bwd(res, ct):
        a, b = res
        return _dg(ct, b, nn, hi).astype(a.dtype), _dg(ct, a, tn, hi).astype(b.dtype)

    mm_nt.defvjp(lambda a, b: (_dg(a, b, nt, hi), (a, b)), mm_nt_bwd)

    @jax.custom_vjp
    def mm_tn(a, b):
        return _dg(a, b, tn, hi)

    def mm_tn_bwd(res, ct):
        a, b = res
        return _dg(b, ct, nt, hi).astype(a.dtype), _dg(a, ct, nn, hi).astype(b.dtype)

    mm_tn.defvjp(lambda a, b: (_dg(a, b, tn, hi), (a, b)), mm_tn_bwd)
    return mm, mm_nt, mm_tn


mm, mm_nt, mm_tn = _make_mm(_NN2, _NT2, _TN2, False)
bmm, bmm_nt, bmm_tn = _make_mm(_NN3, _NT3, _TN3, False)
bmm_hi, _, _ = _make_mm(_NN3, _NT3, _TN3, True)


def _shift_raw(x, k):
    n = x.shape[0]
    t = lax.broadcasted_iota(jnp.int32, x.shape, 0)
    if k > 0:
        return jnp.where(t >= k, pltpu.roll(x, k, axis=0), 0.0)
    k = -k
    return jnp.where(t < n - k, pltpu.roll(x, n - k, axis=0), 0.0)


@functools.partial(jax.custom_vjp, nondiff_argnums=(1,))
def shift(x, k):
    return _shift_raw(x, k)


shift.defvjp(lambda x, k: (_shift_raw(x, k), None), lambda k, _, ct: (_shift_raw(ct, -k),))


def _silu(x):
    return x * jax.nn.sigmoid(x)


def _softplus(x):
    return jnp.maximum(x, 0.0) + jnp.log(1.0 + jnp.exp(-jnp.abs(x)))


def _rms_fwd(h, gain):
    rstd = lax.rsqrt(jnp.mean(h * h, axis=-1, keepdims=True) + EPS)
    xhat = h * rstd
    return xhat * gain, xhat, rstd


def _rms_bwd(dxn, xhat, rstd, gain):
    dxhat = dxn * gain
    dh = rstd * (dxhat - xhat * jnp.mean(dxhat * xhat, axis=-1, keepdims=True))
    return dh, jnp.sum(dxn * xhat, axis=0, keepdims=True)


def _acc_rows(ref, val, first):
    @pl.when(first)
    def _():
        ref[...] = jnp.zeros_like(ref)
    ref[0:1, :] += val


def _tile(n, cap):
    t = min(n, cap)
    assert n % t == 0, (n, t)
    return t


def _resident(shape):
    nd = len(shape)
    return pl.BlockSpec(shape, lambda *_: (0,) * nd, pipeline_mode=pl.Buffered(1))


def ffn_fwd(h, gain, wgt, wut, wd):
    T, D = h.shape
    F = wgt.shape[0]
    tm = _tile(T, 256)

    def body(h_ref, gain_ref, wg_ref, wu_ref, wd_ref, ho_ref, g_ref, u_ref):
        hh = h_ref[...]
        xn, _, _ = _rms_fwd(hh, gain_ref[...])
        xb = xn.astype(bf16)
        g = _dg(xb, wg_ref[...], _NT2, False)
        u = _dg(xb, wu_ref[...], _NT2, False)
        y = _dg(_silu(g) * u, wd_ref[...], _NN2, False)
        ho_ref[...] = hh + 0.5 * y
        g_ref[...] = g.astype(bf16)
        u_ref[...] = u.astype(bf16)

    row = lambda w: pl.BlockSpec((tm, w), lambda i: (i, 0))
    return _pcall(
        body, name="ffn_fwd", grid=(T // tm,),
        in_specs=[row(D), _resident((1, D)), _resident((F, D)), _resident((F, D)), _resident((F, D))],
        out_specs=[row(D), row(F), row(F)],
        out_shape=[_sds((T, D), f32), _sds((T, F), bf16), _sds((T, F), bf16)],
        compiler_params=_params(("parallel",), VMEM_BIG),
    )(h, gain, wgt, wut, wd)


def ffn_bwd(h, gain, g, u, dout, wgt, wut, wd):
    T, D = h.shape
    F = wgt.shape[0]
    tm = _tile(T, 256)

    def body(h_ref, gain_ref, g_ref, u_ref, do_ref, wg_ref, wu_ref, wd_ref,
             dh_ref, dgain_ref, xn_ref, act_ref, dg_ref, du_ref, dy_ref):
        hh, dout_ = h_ref[...], do_ref[...]
        gain_ = gain_ref[...]
        xn, xhat, rstd = _rms_fwd(hh, gain_)
        gg, uu = g_ref[...].astype(f32), u_ref[...].astype(f32)
        dy = (0.5 * dout_).astype(bf16)
        dact = _dg(dy, wd_ref[...], _NT2, False)
        sg = jax.nn.sigmoid(gg)
        silu = gg * sg
        dgate = (dact * uu * (sg * (1.0 + gg * (1.0 - sg)))).astype(bf16)
        dup = (dact * silu).astype(bf16)
        dxn = _dg(dgate, wg_ref[...], _NN2, False) + _dg(dup, wu_ref[...], _NN2, False)
        dh, dgain = _rms_bwd(dxn, xhat, rstd, gain_)
        dh_ref[...] = dout_ + dh
        _acc_rows(dgain_ref, dgain, pl.program_id(0) == 0)
        xn_ref[...] = xn.astype(bf16)
        act_ref[...] = (silu * uu).astype(bf16)
        dg_ref[...] = dgate
        du_ref[...] = dup
        dy_ref[...] = dy

    row = lambda w: pl.BlockSpec((tm, w), lambda i: (i, 0))
    return _pcall(
        body, name="ffn_bwd", grid=(T // tm,),
        in_specs=[row(D), _resident((1, D)), row(F), row(F), row(D),
                  _resident((F, D)), _resident((F, D)), _resident((F, D))],
        out_specs=[row(D), pl.BlockSpec((8, D), lambda i: (0, 0)), row(D), row(F), row(F), row(F), row(D)],
        out_shape=[_sds((T, D), f32), _sds((8, D), f32), _sds((T, D), bf16), _sds((T, F), bf16),
                   _sds((T, F), bf16), _sds((T, F), bf16), _sds((T, D), bf16)],
        compiler_params=_params(("arbitrary",), VMEM_BIG),
    )(h, gain, g, u, dout, wgt, wut, wd)


def _col_tile(n, cap=1408):
    best = None
    for c in range(128, cap + 1, 128):
        if n % c == 0:
            best = c
    assert best is not None, n
    return best


def wgrad(a, b):
    T, N = a.shape
    K = b.shape[1]
    nc, tk = _col_tile(N), _tile(T, 512)
    nk = T // tk

    def body(a_ref, b_ref, o_ref, acc_ref):
        k = pl.program_id(1)

        @pl.when(k == 0)
        def _():
            acc_ref[...] = jnp.zeros_like(acc_ref)

        acc_ref[...] += _dg(a_ref[...], b_ref[...], _TN2, False)

        @pl.when(k == nk - 1)
        def _():
            o_ref[...] = acc_ref[...].astype(bf16)

    return _pcall(
        body, name="wgrad", grid=(N // nc, nk),
        in_specs=[pl.BlockSpec((tk, nc), lambda j, k: (k, j)), pl.BlockSpec((tk, K), lambda j, k: (k, 0))],
        out_specs=pl.BlockSpec((nc, K), lambda j, k: (j, 0)),
        out_shape=_sds((N, K), bf16),
        scratch_shapes=[pltpu.VMEM((nc, K), f32)],
        compiler_params=_params(("parallel", "arbitrary"), VMEM_MID),
    )(a, b)


def in_proj_fwd(h, gain, wt, widths):
    T, D = h.shape
    N = wt.shape[0]
    assert sum(widths) == N
    tm = _tile(T, 512)
    offs = [sum(widths[:i]) for i in range(len(widths))]

    def body(h_ref, gain_ref, w_ref, *outs):
        xn, _, _ = _rms_fwd(h_ref[...], gain_ref[...])
        p = _dg(xn, w_ref[...], _NT2, False)
        for o_ref, off, wd_ in zip(outs, offs, widths):
            o_ref[...] = p[:, off:off + wd_]

    row = lambda w: pl.BlockSpec((tm, w), lambda i: (i, 0))
    return _pcall(
        body, name="in_proj_fwd", grid=(T // tm,),
        in_specs=[row(D), _resident((1, D)), _resident((N, D))],
        out_specs=[row(w) for w in widths],
        out_shape=[_sds((T, w), f32) for w in widths],
        compiler_params=_params(("parallel",), VMEM_BIG),
    )(h, gain, wt)


def in_proj_bwd(h, gain, dpieces, dout, wt):
    T, D = h.shape
    N = wt.shape[0]
    widths = [p.shape[1] for p in dpieces]
    assert sum(widths) == N
    tm = _tile(T, 256)
    nt = T // tm
    npc = len(dpieces)

    def body(*refs):
        h_ref, gain_ref = refs[0], refs[1]
        p_refs = refs[2:2 + npc]
        do_ref, w_ref, dh_ref, dgain_ref, dw_ref, acc_ref = refs[2 + npc:]
        i = pl.program_id(0)
        gain_ = gain_ref[...]
        xn, xhat, rstd = _rms_fwd(h_ref[...], gain_)
        dp = jnp.concatenate([r[...].astype(bf16) for r in p_refs], axis=-1)
        dxn = _dg(dp, w_ref[...], _NN2, False)
        dh, dgain = _rms_bwd(dxn, xhat, rstd, gain_)
        dh_ref[...] = do_ref[...] + dh
        _acc_rows(dgain_ref, dgain, i == 0)

        @pl.when(i == 0)
        def _():
            acc_ref[...] = jnp.zeros_like(acc_ref)

        acc_ref[...] += _dg(dp, xn, _TN2, False)

        @pl.when(i == nt - 1)
        def _():
            dw_ref[...] = acc_ref[...].astype(bf16)

    row = lambda w: pl.BlockSpec((tm, w), lambda i: (i, 0))
    return _pcall(
        body, name="in_proj_bwd", grid=(nt,),
        in_specs=[row(D), _resident((1, D))] + [row(w) for w in widths] + [row(D), _resident((N, D))],
        out_specs=[row(D), pl.BlockSpec((8, D), lambda i: (0, 0)), pl.BlockSpec((N, D), lambda i: (0, 0))],
        out_shape=[_sds((T, D), f32), _sds((8, D), f32), _sds((N, D), bf16)],
        scratch_shapes=[pltpu.VMEM((N, D), f32)],
        compiler_params=_params(("arbitrary",), VMEM_BIG),
    )(h, gain, *dpieces, dout, wt)


def out_proj_fwd(h, ya, yb, w):
    T, D = h.shape
    half = ya.shape[1]
    tm = _tile(T, 512)

    def body(h_ref, ya_ref, yb_ref, w_ref, o_ref):
        y = jnp.concatenate([ya_ref[...], yb_ref[...]], axis=-1)
        o_ref[...] = h_ref[...] + _dg(y, w_ref[...], _NN2, False)

    row = lambda w_: pl.BlockSpec((tm, w_), lambda i: (i, 0))
    return _pcall(
        body, name="out_proj_fwd", grid=(T // tm,),
        in_specs=[row(D), row(half), row(half), _resident((2 * half, D))],
        out_specs=row(D), out_shape=_sds((T, D), f32),
        compiler_params=_params(("parallel",), VMEM_MID),
    )(h, ya, yb, w)


def out_proj_bwd(dout, ya, yb, w):
    T, D = dout.shape
    half = ya.shape[1]
    tm = _tile(T, 512)
    nt = T // tm

    def body(do_ref, ya_ref, yb_ref, w_ref, dya_ref, dyb_ref, dw_ref, acc_ref):
        i = pl.program_id(0)
        dob = do_ref[...].astype(bf16)
        dy = _dg(dob, w_ref[...], _NT2, False)
        dya_ref[...] = dy[:, :half]
        dyb_ref[...] = dy[:, half:]

        @pl.when(i == 0)
        def _():
            acc_ref[...] = jnp.zeros_like(acc_ref)

        y = jnp.concatenate([ya_ref[...], yb_ref[...]], axis=-1)
        acc_ref[...] += _dg(y, dob, _TN2, False)

        @pl.when(i == nt - 1)
        def _():
            dw_ref[...] = acc_ref[...].astype(bf16)

    row = lambda w_: pl.BlockSpec((tm, w_), lambda i: (i, 0))
    return _pcall(
        body, name="out_proj_bwd", grid=(nt,),
        in_specs=[row(D), row(half), row(half), _resident((2 * half, D))],
        out_specs=[row(half), row(half), pl.BlockSpec((2 * half, D), lambda i: (0, 0))],
        out_shape=[_sds((T, half), f32), _sds((T, half), f32), _sds((2 * half, D), bf16)],
        scratch_shapes=[pltpu.VMEM((2 * half, D), f32)],
        compiler_params=_params(("arbitrary",), VMEM_MID),
    )(dout, ya, yb, w)


def _conv_taps(x, w):
    K = w.shape[0]
    acc = x * w[K - 1:K, :]
    for i in range(K - 1):
        acc = acc + shift(x, K - 1 - i) * w[i:i + 1, :]
    return acc


def conv_silu_math(w, x):
    return _silu(_conv_taps(x, w))


def gated_conv_math(w, xd, bg, cg):
    return bg * _conv_taps(cg * xd, w)


def seq_chan_fwd(math, w, xs, seq, out_dtype, name):
    T, C = xs[0].shape
    K = w.shape[0]
    nb, nc = T // seq, C // HEAD

    def body(w_ref, *refs):
        o_ref = refs[-1]
        o_ref[...] = math(w_ref[...], *[r[...] for r in refs[:-1]]).astype(out_dtype)

    blk = pl.BlockSpec((seq, HEAD), lambda j, b: (b, j))
    return _pcall(
        body, name=name, grid=(nc, nb),
        in_specs=[pl.BlockSpec((K, HEAD), lambda j, b: (0, j))] + [blk] * len(xs),
        out_specs=blk, out_shape=_sds((T, C), out_dtype),
        compiler_params=_params(("parallel", "parallel"), VMEM_MID),
    )(w, *xs)


def seq_chan_bwd(math, w, xs, dy, seq, dx_dtype, name):
    T, C = xs[0].shape
    K = w.shape[0]
    nb, nc = T // seq, C // HEAD
    nx = len(xs)

    def body(w_ref, *refs):
        x_refs, dy_ref = refs[:nx], refs[nx]
        dx_refs, dw_ref = refs[nx + 1:2 * nx + 1], refs[2 * nx + 1]
        _, vjp = jax.vjp(math, w_ref[...], *[r[...] for r in x_refs])
        grads = vjp(dy_ref[...].astype(f32))

        @pl.when(pl.program_id(1) == 0)
        def _():
            dw_ref[...] = jnp.zeros_like(dw_ref)

        dw_ref[...] += grads[0]
        for r, gx in zip(dx_refs, grads[1:]):
            r[...] = gx.astype(dx_dtype)

    blk = pl.BlockSpec((seq, HEAD), lambda j, b: (b, j))
    wblk = pl.BlockSpec((K, HEAD), lambda j, b: (0, j))
    return _pcall(
        body, name=name, grid=(nc, nb),
        in_specs=[wblk] + [blk] * (nx + 1),
        out_specs=[blk] * nx + [wblk],
        out_shape=[_sds((T, C), dx_dtype)] * nx + [_sds((K, C), f32)],
        compiler_params=_params(("parallel", "arbitrary"), VMEM_MID),
    )(w, *xs, dy)


def pool_group_math(win, ag, pw, scale):
    t = lax.broadcasted_iota(jnp.int32, (ag.shape[0], 1), 0)
    s, k = ag, 1
    while k < win:
        s = s + shift(s, k)
        k *= 2
    pooled = s / jnp.minimum(t + 1, win).astype(f32) - ag
    return mm(pooled, pw) * scale


def pool_fwd(a, pw, scale, seq):
    T, C = a.shape

    def body(a_ref, pw_ref, sc_ref, o_ref):
        for gi, win in enumerate(POOL_WINDOWS):
            cols = slice(gi * HEAD, (gi + 1) * HEAD)
            o_ref[:, cols] = pool_group_math(win, a_ref[:, cols], pw_ref[gi], sc_ref[:, cols]).astype(bf16)

    blk = pl.BlockSpec((seq, C), lambda b: (b, 0))
    return _pcall(
        body, name="pool_fwd", grid=(T // seq,),
        in_specs=[blk, _resident(pw.shape), _resident((1, C))],
        out_specs=blk, out_shape=_sds((T, C), bf16),
        compiler_params=_params(("parallel",), VMEM_MID),
    )(a, pw, scale)


def pool_bwd(a, pw, scale, dy, seq):
    T, C = a.shape

    def body(a_ref, pw_ref, sc_ref, dy_ref, da_ref, dpw_ref, dsc_ref):
        first = pl.program_id(0) == 0

        @pl.when(first)
        def _():
            dpw_ref[...] = jnp.zeros_like(dpw_ref)
            dsc_ref[...] = jnp.zeros_like(dsc_ref)

        for gi, win in enumerate(POOL_WINDOWS):
            cols = slice(gi * HEAD, (gi + 1) * HEAD)
            _, vjp = jax.vjp(functools.partial(pool_group_math, win), a_ref[:, cols], pw_ref[gi], sc_ref[:, cols])
            da, dpw, dsc = vjp(dy_ref[:, cols])
            dpw_ref[gi] += dpw
            dsc_ref[0:1, cols] += dsc
            da_ref[:, cols] = da.astype(bf16)

    blk = pl.BlockSpec((seq, C), lambda b: (b, 0))
    return _pcall(
        body, name="pool_bwd", grid=(T // seq,),
        in_specs=[blk, _resident(pw.shape), _resident((1, C)), blk],
        out_specs=[blk, pl.BlockSpec(pw.shape, lambda b: (0, 0, 0)), pl.BlockSpec((8, C), lambda b: (0, 0))],
        out_shape=[_sds((T, C), bf16), _sds(pw.shape, f32), _sds((8, C), f32)],
        compiler_params=_params(("arbitrary",), VMEM_MID),
    )(a, pw, scale, dy)


def _neumann_inverse(lmat):
    n = lmat.shape[-1]
    ii = lax.broadcasted_iota(jnp.int32, (n, n), 0)
    jj = lax.broadcasted_iota(jnp.int32, (n, n), 1)
    inv = jnp.where((ii == jj)[None], 1.0, 0.0) - lmat
    pw_ = _dg(lmat, lmat, _NN3, True)
    steps = int(math.log2(n)) - 1
    for i in range(steps):
        inv = inv + _dg(inv, pw_, _NN3, True)
        if i < steps - 1:
            pw_ = _dg(pw_, pw_, _NN3, True)
    return inv


@jax.custom_vjp
def unit_lower_inverse(lmat):
    return _neumann_inverse(lmat)


def _unit_lower_inverse_fwd(lmat):
    inv = _neumann_inverse(lmat)
    return inv, inv


def _unit_lower_inverse_bwd(inv, ct):
    return (-_dg(_dg(inv, ct, _TN3, True), inv, _NT3, True),)


unit_lower_inverse.defvjp(_unit_lower_inverse_fwd, _unit_lower_inverse_bwd)


def dn_prep_math(qkv, bg, alog, dtb):
    tt = qkv.shape[0]
    nt = tt // CHUNK
    nb = nt * N_HEADS
    W = N_HEADS * HEAD
    beta_all = jax.nn.sigmoid(bg)
    g_all = -jnp.exp(alog) * _softplus(bg + dtb)

    def heads(fn):
        return jnp.stack([fn(hd).reshape(nt, CHUNK, HEAD) for hd in range(N_HEADS)], axis=1).reshape(nb, CHUNK, HEAD)

    def l2n(x):
        return x * lax.rsqrt(jnp.sum(x * x, axis=-1, keepdims=True) + EPS)

    q = heads(lambda hd: l2n(qkv[:, hd * HEAD:(hd + 1) * HEAD]) * (HEAD ** -0.5))
    k = heads(lambda hd: l2n(qkv[:, W + hd * HEAD:W + (hd + 1) * HEAD]))
    v = heads(lambda hd: qkv[:, 2 * W + hd * HEAD:2 * W + (hd + 1) * HEAD])
    beta = heads(lambda hd: jnp.broadcast_to(beta_all[:, hd:hd + 1], (tt, HEAD)))
    g = heads(lambda hd: jnp.broadcast_to(g_all[:, N_HEADS + hd:N_HEADS + hd + 1], (tt, HEAD)))

    ii = lax.broadcasted_iota(jnp.int32, (CHUNK, CHUNK), 0)
    jj = lax.broadcasted_iota(jnp.int32, (CHUNK, CHUNK), 1)
    tril, strict = (ii >= jj)[None], (ii > jj)[None]
    ones_b = jnp.ones((nb, CHUNK, CHUNK), f32)
    tril_b = jnp.where(tril, ones_b, 0.0)
    eye_b = jnp.where((ii == jj)[None], ones_b, 0.0)

    gcb = bmm_hi(tril_b, g)
    gcol = gcb[:, :, :CHUNK]
    grow = bmm_hi(ones_b, eye_b * gcol)
    gamma = jnp.where(tril, jnp.exp(jnp.where(tril, gcol - grow, 0.0)), 0.0)
    kb = k * beta
    lmat = jnp.where(strict, bmm_nt(kb, k) * gamma, 0.0)
    inv = unit_lower_inverse(lmat)
    egc = jnp.exp(gcb)
    u = bmm_hi(inv, v * beta)
    w = bmm_hi(inv, kb * egc)
    aqk = bmm_nt(q, k) * gamma
    qd = q * egc
    glast = gcb[:, CHUNK - 1:CHUNK, :]
    kd = k * jnp.exp(glast - gcb)
    last = jnp.exp(glast)
    r4 = lambda x: x.reshape((nt, N_HEADS) + x.shape[1:])
    return r4(u), r4(w), r4(qd), r4(kd), r4(aqk), r4(last)


_PREP_DTYPES = (f32, bf16, bf16, bf16, bf16, f32)


def _prep_specs(nt, T):
    nchunks = T // CHUNK
    shapes = [(HEAD,), (HEAD,), (HEAD,), (HEAD,), (CHUNK,), (HEAD,)]
    rows = [CHUNK, CHUNK, CHUNK, CHUNK, CHUNK, 1]
    specs = [pl.BlockSpec((nt, N_HEADS, r, s[0]), lambda i: (i, 0, 0, 0)) for r, s in zip(rows, shapes)]
    outs = [(nchunks, N_HEADS, r, s[0]) for r, s in zip(rows, shapes)]
    return specs, outs


def dn_prep_fwd(qkv, bg, alog, dtb):
    T = qkv.shape[0]
    tt = _tile(T, 256)
    nt = tt // CHUNK
    specs, shapes = _prep_specs(nt, T)

    def body(qkv_ref, bg_ref, alog_ref, dtb_ref, *outs):
        res = dn_prep_math(qkv_ref[...], bg_ref[...], alog_ref[...], dtb_ref[...])
        for o_ref, r, dt in zip(outs, res, _PREP_DTYPES):
            o_ref[...] = r.astype(dt)

    row = lambda w: pl.BlockSpec((tt, w), lambda i: (i, 0))
    return _pcall(
        body, name="dn_prep_fwd", grid=(T // tt,),
        in_specs=[row(qkv.shape[1]), row(HEAD), _resident((1, HEAD)), _resident((1, HEAD))],
        out_specs=specs, out_shape=[_sds(s, dt) for s, dt in zip(shapes, _PREP_DTYPES)],
        compiler_params=_params(("parallel",), VMEM_BIG),
    )(qkv, bg, alog, dtb)


def dn_prep_bwd(qkv, bg, alog, dtb, cts):
    T = qkv.shape[0]
    tt = _tile(T, 256)
    nt = tt // CHUNK
    specs, _ = _prep_specs(nt, T)

    def body(qkv_ref, bg_ref, alog_ref, dtb_ref, *refs):
        ct_refs, (dqkv_ref, dbg_ref, dalog_ref, ddtb_ref) = refs[:6], refs[6:]
        _, vjp = jax.vjp(dn_prep_math, qkv_ref[...], bg_ref[...], alog_ref[...], dtb_ref[...])
        dqkv, dbg, dalog, ddtb = vjp(tuple(r[...].astype(f32) for r in ct_refs))
        dqkv_ref[...] = dqkv
        dbg_ref[...] = dbg.astype(bf16)
        first = pl.program_id(0) == 0
        _acc_rows(dalog_ref, dalog, first)
        _acc_rows(ddtb_ref, ddtb, first)

    row = lambda w: pl.BlockSpec((tt, w), lambda i: (i, 0))
    small = pl.BlockSpec((8, HEAD), lambda i: (0, 0))
    return _pcall(
        body, name="dn_prep_bwd", grid=(T // tt,),
        in_specs=[row(qkv.shape[1]), row(HEAD), _resident((1, HEAD)), _resident((1, HEAD))] + specs,
        out_specs=[row(qkv.shape[1]), row(HEAD), small, small],
        out_shape=[_sds(qkv.shape, f32), _sds((T, HEAD), bf16), _sds((8, HEAD), f32), _sds((8, HEAD), f32)],
        compiler_params=_params(("arbitrary",), VMEM_BIG),
    )(qkv, bg, alog, dtb, *cts)


def dn_step(state, u, w, qd, kd, aqk, last):
    v_new = u - bmm(w, state)
    o = bmm(qd, state) + bmm(aqk, v_new)
    return state * last + bmm_tn(kd, v_new), o


def dn_gate(o, z, onorm):
    return o * lax.rsqrt(jnp.mean(o * o, axis=-1, keepdims=True) + EPS) * onorm * _silu(z)


def _recur_specs(n):
    rows = [CHUNK, CHUNK, CHUNK, CHUNK, CHUNK, 1]
    lanes = [HEAD, HEAD, HEAD, HEAD, CHUNK, HEAD]
    return [pl.BlockSpec((n, 1, r, l), lambda b, hd: (b, hd, 0, 0)) for r, l in zip(rows, lanes)]


def dn_recur_fwd(prep, z, onorm, seq):
    T, C = z.shape
    n = seq // CHUNK

    def body(u_ref, w_ref, qd_ref, kd_ref, aqk_ref, last_ref, z_ref, on_ref, y_ref, o_ref):
        def step(i, state):
            new, o = dn_step(state, u_ref[i], w_ref[i], qd_ref[i], kd_ref[i], aqk_ref[i], last_ref[i])
            o_ref[i] = o
            return new

        lax.fori_loop(0, n, step, jnp.zeros((1, HEAD, HEAD), f32))
        y_ref[...] = dn_gate(o_ref[...].reshape(seq, HEAD), z_ref[...], on_ref[...]).astype(bf16)

    blk = pl.BlockSpec((seq, HEAD), lambda b, hd: (b, hd))
    return _pcall(
        body, name="dn_recur_fwd", grid=(T // seq, N_HEADS),
        in_specs=_recur_specs(n) + [blk, pl.BlockSpec((1, HEAD), lambda b, hd: (0, 0))],
        out_specs=blk, out_shape=_sds((T, C), bf16),
        scratch_shapes=[pltpu.VMEM((n, 1, CHUNK, HEAD), f32)],
        compiler_params=_params(("parallel", "parallel"), VMEM_MID),
    )(*prep, z, onorm)


def dn_recur_bwd(prep, z, onorm, dy, seq):
    T, C = z.shape
    n = seq // CHUNK
    nchunks = T // CHUNK

    def body(u_ref, w_ref, qd_ref, kd_ref, aqk_ref, last_ref, z_ref, on_ref, dy_ref,
             du_ref, dw_ref, dqd_ref, dkd_ref, daqk_ref, dlast_ref, dz_ref, don_ref, st_ref, o_ref):
        args = lambda i: tuple(r[i].astype(f32) for r in (u_ref, w_ref, qd_ref, kd_ref, aqk_ref, last_ref))

        def fstep(i, state):
            st_ref[i] = state
            new, o = dn_step(state, *args(i))
            o_ref[i] = o
            return new

        lax.fori_loop(0, n, fstep, jnp.zeros((1, HEAD, HEAD), f32))
        _, gate_vjp = jax.vjp(dn_gate, o_ref[...].reshape(seq, HEAD), z_ref[...], on_ref[...])
        do, dz, don = gate_vjp(dy_ref[...])
        dz_ref[...] = dz.astype(bf16)
        o_ref[...] = do.reshape(n, 1, CHUNK, HEAD)

        def bstep(j, dstate):
            i = n - 1 - j
            _, vjp = jax.vjp(dn_step, st_ref[i], *args(i))
            ds, du, dw, dqd, dkd, daqk, dlast = vjp((dstate, o_ref[i]))
            du_ref[i], dw_ref[i], dqd_ref[i], dkd_ref[i], daqk_ref[i], dlast_ref[i] = du, dw, dqd, dkd, daqk, dlast
            return ds

        lax.fori_loop(0, n, bstep, jnp.zeros((1, HEAD, HEAD), f32))
        first = jnp.logical_and(pl.program_id(0) == 0, pl.program_id(1) == 0)
        _acc_rows(don_ref, don, first)

    blk = pl.BlockSpec((seq, HEAD), lambda b, hd: (b, hd))
    rows = [CHUNK, CHUNK, CHUNK, CHUNK, CHUNK, 1]
    lanes = [HEAD, HEAD, HEAD, HEAD, CHUNK, HEAD]
    return _pcall(
        body, name="dn_recur_bwd", grid=(T // seq, N_HEADS),
        in_specs=_recur_specs(n) + [blk, pl.BlockSpec((1, HEAD), lambda b, hd: (0, 0)), blk],
        out_specs=_recur_specs(n) + [blk, pl.BlockSpec((8, HEAD), lambda b, hd: (0, 0))],
        out_shape=[_sds((nchunks, N_HEADS, r, l), f32) for r, l in zip(rows, lanes)]
        + [_sds((T, C), bf16), _sds((8, HEAD), f32)],
        scratch_shapes=[pltpu.VMEM((n, 1, HEAD, HEAD), f32), pltpu.VMEM((n, 1, CHUNK, HEAD), f32)],
        compiler_params=_params(("arbitrary", "arbitrary"), VMEM_MID),
    )(*prep, z, onorm, dy)


def sgu_math(up, vp, ng, nb, sw, sbias):
    S = up.shape[0]
    nblk = S // SGU_BLOCK
    u = jax.nn.gelu(up, approximate=True)
    v = jax.nn.gelu(vp, approximate=True)
    xc = v - jnp.mean(v, axis=-1, keepdims=True)
    vn = xc * lax.rsqrt(jnp.mean(xc * xc, axis=-1, keepdims=True) + EPS) * ng + nb
    ii = lax.broadcasted_iota(jnp.int32, (SGU_BLOCK, SGU_BLOCK), 0)
    jj = lax.broadcasted_iota(jnp.int32, (SGU_BLOCK, SGU_BLOCK), 1)
    outs = []
    for hd in range(N_HEADS):
        vh = vn[:, hd * HEAD:(hd + 1) * HEAD].reshape(nblk, SGU_BLOCK, HEAD)
        ws = jnp.where(ii >= jj, sw[hd], 0.0)
        mixed = bmm(jnp.broadcast_to(ws[None], (nblk, SGU_BLOCK, SGU_BLOCK)), vh) + sbias[hd][None]
        outs.append(mixed.reshape(S, HEAD))
    return u * jnp.concatenate(outs, axis=-1)


def sgu_fwd(up, vp, ng, nb, sw, sbias, seq):
    T, C = up.shape

    def body(up_ref, vp_ref, ng_ref, nb_ref, sw_ref, sb_ref, o_ref):
        o_ref[...] = sgu_math(up_ref[...], vp_ref[...], ng_ref[...], nb_ref[...], sw_ref[...],
                              sb_ref[...]).astype(bf16)

    blk = pl.BlockSpec((seq, C), lambda b: (b, 0))
    return _pcall(
        body, name="sgu_fwd", grid=(T // seq,),
        in_specs=[blk, blk, _resident((1, C)), _resident((1, C)), _resident(sw.shape), _resident(sbias.shape)],
        out_specs=blk, out_shape=_sds((T, C), bf16),
        compiler_params=_params(("parallel",), VMEM_BIG),
    )(up, vp, ng, nb, sw, sbias)


def sgu_bwd(up, vp, ng, nb, sw, sbias, dy, seq):
    T, C = up.shape

    def body(up_ref, vp_ref, ng_ref, nb_ref, sw_ref, sb_ref, dy_ref,
             dup_ref, dvp_ref, dng_ref, dnb_ref, dsw_ref, dsb_ref):
        _, vjp = jax.vjp(sgu_math, up_ref[...], vp_ref[...], ng_ref[...], nb_ref[...], sw_ref[...], sb_ref[...])
        dup, dvp, dng, dnb, dsw, dsb = vjp(dy_ref[...])
        dup_ref[...] = dup.astype(bf16)
        dvp_ref[...] = dvp.astype(bf16)
        first = pl.program_id(0) == 0
        _acc_rows(dng_ref, dng, first)
        _acc_rows(dnb_ref, dnb, first)

        @pl.when(first)
        def _():
            dsw_ref[...] = jnp.zeros_like(dsw_ref)
            dsb_ref[...] = jnp.zeros_like(dsb_ref)

        dsw_ref[...] += dsw
        dsb_ref[...] += dsb

    blk = pl.BlockSpec((seq, C), lambda b: (b, 0))
    small = pl.BlockSpec((8, C), lambda b: (0, 0))
    return _pcall(
        body, name="sgu_bwd", grid=(T // seq,),
        in_specs=[blk, blk, _resident((1, C)), _resident((1, C)), _resident(sw.shape), _resident(sbias.shape), blk],
        out_specs=[blk, blk, small, small, pl.BlockSpec(sw.shape, lambda b: (0, 0, 0)),
                   pl.BlockSpec(sbias.shape, lambda b: (0, 0, 0))],
        out_shape=[_sds((T, C), bf16), _sds((T, C), bf16), _sds((8, C), f32), _sds((8, C), f32),
                   _sds(sw.shape, f32), _sds(sbias.shape, f32)],
        compiler_params=_params(("arbitrary",), VMEM_BIG),
    )(up, vp, ng, nb, sw, sbias, dy)


def loss_fwd_bwd(h, gain, target):
    T, D = h.shape
    tm = _tile(T, 512)

    def body(h_ref, gain_ref, t_ref, loss_ref, dh_ref, dgain_ref):
        gain_ = gain_ref[...]
        y, xhat, rstd = _rms_fwd(h_ref[...], gain_)
        err = y - t_ref[...]
        part = 0.5 * jnp.sum(jnp.mean(err * err, axis=-1, keepdims=True), axis=0, keepdims=True)
        dh, dgain = _rms_bwd(err * (1.0 / D), xhat, rstd, gain_)
        dh_ref[...] = dh
        first = pl.program_id(0) == 0
        _acc_rows(dgain_ref, dgain, first)

        @pl.when(first)
        def _():
            loss_ref[...] = jnp.zeros_like(loss_ref)

        loss_ref[...] += jnp.broadcast_to(part, loss_ref.shape)

    row = pl.BlockSpec((tm, D), lambda i: (i, 0))
    return _pcall(
        body, name="loss_fwd_bwd", grid=(T // tm,),
        in_specs=[row, _resident((1, D)), row],
        out_specs=[pl.BlockSpec((8, 128), lambda i: (0, 0)), row, pl.BlockSpec((8, D), lambda i: (0, 0))],
        out_shape=[_sds((8, 128), f32), _sds((T, D), f32), _sds((8, D), f32)],
        compiler_params=_params(("arbitrary",), VMEM_MID),
    )(h, gain, target)


def adamw(w, g, m, v):
    R, C = w.shape
    tr = max(c for c in range(8, 513, 8) if R % c == 0)
    c1, c2 = 1.0 - ADAM_B1 ** ADAM_STEP, 1.0 - ADAM_B2 ** ADAM_STEP

    def body(w_ref, g_ref, m_ref, v_ref, d_ref, nm_ref, nv_ref):
        gg = g_ref[...]
        nm = ADAM_B1 * m_ref[...] + (1.0 - ADAM_B1) * gg
        nv = ADAM_B2 * v_ref[...] + (1.0 - ADAM_B2) * (gg * gg)
        d_ref[...] = -ADAM_LR * ((nm / c1) / (jnp.sqrt(nv / c2) + ADAM_EPS) + ADAM_WD * w_ref[...])
        nm_ref[...] = nm
        nv_ref[...] = nv

    blk = pl.BlockSpec((tr, C), lambda i: (i, 0))
    return _pcall(
        body, name="adamw", grid=(R // tr,), in_specs=[blk] * 4, out_specs=[blk] * 3,
        out_shape=[_sds((R, C), f32)] * 3, compiler_params=_params(("parallel",), VMEM_MID),
    )(w, g, m, v)


def sum8(parts):
    _, R, C = parts.shape
    tr = R
    for cand in (512, 352, 336, 320, 256, 128):
        if R % cand == 0:
            tr = cand
            break

    def body(p_ref, o_ref):
        acc = p_ref[0].astype(f32)
        for i in range(1, N_DEV):
            acc = acc + p_ref[i].astype(f32)
        o_ref[...] = acc

    return _pcall(
        body, name="sum8", grid=(R // tr,),
        in_specs=[pl.BlockSpec((N_DEV, tr, C), lambda i: (0, i, 0))],
        out_specs=pl.BlockSpec((tr, C), lambda i: (i, 0)), out_shape=_sds((R, C), f32),
        compiler_params=_params(("parallel",), VMEM_MID),
    )(parts)


_FLIPS = [(fx, fy, fc) for fx in (0, 1) for fy in (0, 1) for fc in (0, 1)][1:]
_HBM = pl.BlockSpec(memory_space=pltpu.HBM)


def _me():
    return lax.axis_index("x"), lax.axis_index("y"), lax.axis_index("c")


def _peer(flip):
    x, y, c = _me()
    fx, fy, fc = flip
    return (1 - x if fx else x, 1 - y if fy else y, 1 - c if fc else c)


def _lin(dev):
    return 4 * dev[0] + 2 * dev[1] + dev[2]


def _src_block(ref, rows, dev, whole):
    return ref if whole else ref.at[pl.ds(pl.multiple_of(dev * rows, 16), rows)]


_SEM = pl.BlockSpec(memory_space=pltpu.SEMAPHORE)
_EFFECT = pltpu.SideEffectType.DATAFLOW_SIDE_EFFECTING


def push_start(srcs, whole, after, name):
    n = len(srcs)
    rows = [s.shape[0] if whole else s.shape[0] // N_DEV for s in srcs]
    land_shapes = [(N_DEV, r, s.shape[1]) for r, s in zip(rows, srcs)]

    def body(*refs):
        src_refs, land_refs = refs[:n], refs[n:2 * n]
        send_sems, recv_sems, own_sems, token = refs[2 * n + 1], refs[2 * n + 2], refs[2 * n + 3], refs[-1]
        me = _lin(_me())
        for i in range(n):
            for k, flip in enumerate(_FLIPS):
                peer = _peer(flip)
                pltpu.make_async_remote_copy(
                    src_ref=_src_block(src_refs[i], rows[i], _lin(peer), whole), dst_ref=land_refs[i].at[me],
                    send_sem=send_sems.at[i * 7 + k], recv_sem=recv_sems.at[i * 7 + k],
                    device_id=peer, device_id_type=MESH).start()
        for i in range(n):
            pltpu.make_async_copy(_src_block(src_refs[i], rows[i], me, whole), land_refs[i].at[me],
                                  own_sems.at[i]).start()
        token[...] = jnp.zeros_like(token)

    hbm = lambda a: pltpu.with_memory_space_constraint(a, pltpu.HBM)
    outs = _pcall(
        body, name=name,
        in_specs=[_HBM] * (2 * n) + [pl.BlockSpec(memory_space=pl.ANY)],
        out_specs=[_SEM, _SEM, _SEM] + [_HBM] * (2 * n) + [pl.BlockSpec(memory_space=pltpu.VMEM)],
        out_shape=[pltpu.SemaphoreType.DMA((7 * n,)), pltpu.SemaphoreType.DMA((7 * n,)),
                   pltpu.SemaphoreType.DMA((n,))]
        + [pltpu.HBM(s.shape, s.dtype) for s in srcs]
        + [pltpu.HBM(shp, s.dtype) for shp, s in zip(land_shapes, srcs)] + [_sds((8, 128), f32)],
        input_output_aliases={i: 3 + i for i in range(2 * n)},
        compiler_params=pltpu.CompilerParams(has_side_effects=_EFFECT),
    )(*[hbm(s) for s in srcs], *[hbm(lax.empty(shp, s.dtype)) for shp, s in zip(land_shapes, srcs)], after)
    return outs[0], outs[1], outs[2], list(outs[3:3 + n]), list(outs[3 + n:3 + 2 * n]), outs[-1]


def push_wait(handle, whole, after, name):
    send_sems, recv_sems, own_sems, srcs, lands, _ = handle
    n = len(srcs)
    rows = [l.shape[1] for l in lands]

    def body(*refs):
        src_refs, land_refs = refs[:n], refs[n:2 * n]
        send_sems_, recv_sems_, own_sems_ = refs[2 * n], refs[2 * n + 1], refs[2 * n + 2]
        me = _lin(_me())
        for i in range(n):
            for k, flip in enumerate(_FLIPS):
                peer = _peer(flip)
                cp = pltpu.make_async_remote_copy(
                    src_ref=_src_block(src_refs[i], rows[i], _lin(peer), whole), dst_ref=land_refs[i].at[_lin(peer)],
                    send_sem=send_sems_.at[i * 7 + k], recv_sem=recv_sems_.at[i * 7 + k],
                    device_id=peer, device_id_type=MESH)
                cp.wait_send()
                cp.wait_recv()
            pltpu.make_async_copy(_src_block(src_refs[i], rows[i], me, whole), land_refs[i].at[me],
                                  own_sems_.at[i]).wait()

    outs = _pcall(
        body, name=name,
        in_specs=[_HBM] * (2 * n) + [_SEM, _SEM, _SEM, pl.BlockSpec(memory_space=pl.ANY)],
        out_specs=[_HBM] * (2 * n),
        out_shape=[pltpu.HBM(a.shape, a.dtype) for a in srcs + lands],
        input_output_aliases={i: i for i in range(2 * n)},
        compiler_params=pltpu.CompilerParams(has_side_effects=_EFFECT),
    )(*srcs, *lands, send_sems, recv_sems, own_sems, after)
    return list(outs[n:])


def all_gather_small(x):
    R, C = x.shape

    def body(x_ref, o_ref, send_sems, recv_sems):
        me = _lin(_me())
        o_ref[me] = x_ref[...]
        sends = []
        for k, flip in enumerate(_FLIPS):
            rc = pltpu.make_async_remote_copy(
                src_ref=x_ref, dst_ref=o_ref.at[me], send_sem=send_sems.at[k], recv_sem=recv_sems.at[k],
                device_id=_peer(flip), device_id_type=MESH)
            rc.start()
            sends.append(rc)
        for k, flip in enumerate(_FLIPS):
            pltpu.make_async_remote_copy(
                src_ref=x_ref, dst_ref=o_ref.at[_lin(_peer(flip))], send_sem=send_sems.at[k],
                recv_sem=recv_sems.at[k], device_id=_peer(flip), device_id_type=MESH).wait_recv()
        for rc in sends:
            rc.wait_send()

    vm = pl.BlockSpec(memory_space=pltpu.VMEM)
    return _pcall(
        body, name="all_gather_small", in_specs=[vm], out_specs=vm, out_shape=_sds((N_DEV, R, C), x.dtype),
        scratch_shapes=[pltpu.SemaphoreType.DMA((7,)), pltpu.SemaphoreType.DMA((7,))],
        compiler_params=_params(None, VMEM_MID),
    )(x)


def all_reduce_small(x):
    R, C = x.shape

    def body(x_ref, o_ref, buf_ref, send_sems, recv_sems):
        me = _lin(_me())
        buf_ref[me] = x_ref[...]
        sends = []
        for k, flip in enumerate(_FLIPS):
            rc = pltpu.make_async_remote_copy(
                src_ref=x_ref, dst_ref=buf_ref.at[me], send_sem=send_sems.at[k], recv_sem=recv_sems.at[k],
                device_id=_peer(flip), device_id_type=MESH)
            rc.start()
            sends.append(rc)
        for k, flip in enumerate(_FLIPS):
            pltpu.make_async_remote_copy(
                src_ref=x_ref, dst_ref=buf_ref.at[_lin(_peer(flip))], send_sem=send_sems.at[k],
                recv_sem=recv_sems.at[k], device_id=_peer(flip), device_id_type=MESH).wait_recv()
        for rc in sends:
            rc.wait_send()
        acc = buf_ref[0]
        for p in range(1, N_DEV):
            acc = acc + buf_ref[p]
        o_ref[...] = acc

    vm = pl.BlockSpec(memory_space=pltpu.VMEM)
    return _pcall(
        body, name="all_reduce_small", in_specs=[vm], out_specs=vm, out_shape=_sds((R, C), f32),
        scratch_shapes=[pltpu.VMEM((N_DEV, R, C), f32), pltpu.SemaphoreType.DMA((7,)),
                        pltpu.SemaphoreType.DMA((7,))],
        compiler_params=_params(None, VMEM_MID),
    )(x)


_PACK_ROWS = 8


def _packed_rows(shape):
    return -(-math.prod(shape) // (128 * _PACK_ROWS)) * _PACK_ROWS


def _pack(arrs):
    parts = []
    for a in arrs:
        flat = a.reshape(-1).astype(f32)
        rows = _packed_rows(a.shape)
        parts.append(jnp.pad(flat, (0, rows * 128 - flat.shape[0])).reshape(rows, 128))
    return jnp.concatenate(parts, axis=0)


def _unpack(buf, shapes):
    out, off = [], 0
    for s in shapes:
        rows = _packed_rows(s)
        out.append(buf[off:off + rows].reshape(-1)[:math.prod(s)].reshape(s))
        off += rows
    return out


def _row(v):
    return v.reshape(1, -1)


def _lane_row(vals, offset):
    return jnp.pad(vals.reshape(1, -1), ((0, 0), (offset, HEAD - offset - vals.shape[-1])))


def kernel(x, ffn1_norm, ffn1_w_gate, ffn1_w_up, ffn1_w_down, mix_norm, ffn2_norm, ffn2_w_gate, ffn2_w_up, ffn2_w_down, ab_w_in, pool_w, pool_scale, dn_conv_w, dn_a_log, dn_dt_bias, dn_out_norm, ab_w_out, cd_w_in, sgu_norm_g, sgu_norm_b, sgu_w, sgu_bias, sc_conv_w, cd_w_out, final_norm, loss_target, m_ffn1_norm, m_ffn1_w_gate, m_ffn1_w_up, m_ffn1_w_down, m_mix_norm, m_ffn2_norm, m_ffn2_w_gate, m_ffn2_w_up, m_ffn2_w_down, m_ab_w_in, m_pool_w, m_pool_scale, m_dn_conv_w, m_dn_a_log, m_dn_dt_bias, m_dn_out_norm, m_ab_w_out, m_cd_w_in, m_sgu_norm_g, m_sgu_norm_b, m_sgu_w, m_sgu_bias, m_sc_conv_w, m_cd_w_out, m_final_norm, v_ffn1_norm, v_ffn1_w_gate, v_ffn1_w_up, v_ffn1_w_down, v_mix_norm, v_ffn2_norm, v_ffn2_w_gate, v_ffn2_w_up, v_ffn2_w_down, v_ab_w_in, v_pool_w, v_pool_scale, v_dn_conv_w, v_dn_a_log, v_dn_dt_bias, v_dn_out_norm, v_ab_w_out, v_cd_w_in, v_sgu_norm_g, v_sgu_norm_b, v_sgu_w, v_sgu_bias, v_sc_conv_w, v_cd_w_out, v_final_norm):
    names = ['ffn1_norm', 'ffn1_w_gate', 'ffn1_w_up', 'ffn1_w_down', 'mix_norm', 'ffn2_norm', 'ffn2_w_gate',
             'ffn2_w_up', 'ffn2_w_down', 'ab_w_in', 'pool_w', 'pool_scale', 'dn_conv_w', 'dn_a_log', 'dn_dt_bias',
             'dn_out_norm', 'ab_w_out', 'cd_w_in', 'sgu_norm_g', 'sgu_norm_b', 'sgu_w', 'sgu_bias', 'sc_conv_w',
             'cd_w_out', 'final_norm']
    loc = locals()
    W = {n: loc[n] for n in names}
    M = {n: loc['m_' + n] for n in names}
    V = {n: loc['v_' + n] for n in names}

    B, S, D = x.shape
    T = B * S
    me = _lin(_me())

    def rows_of(w):
        return w.astype(bf16).T

    def layer_shards(layer):
        e = layer // 2
        shards = [rows_of(W['ffn1_w_gate'][layer]), rows_of(W['ffn1_w_up'][layer]), W['ffn1_w_down'][layer].astype(bf16)]
        if layer % 2 == 0:
            win = jnp.pad(rows_of(W['ab_w_in'][e]), ((0, AB_SHARD_PAD - AB_SHARD), (0, 0)))
            wout = W['ab_w_out'][e].astype(bf16)
        else:
            win = rows_of(W['cd_w_in'][e])
            wout = W['cd_w_out'][e].astype(bf16)
        shards += [win, wout]
        shards += [rows_of(W['ffn2_w_gate'][layer]), rows_of(W['ffn2_w_up'][layer]), W['ffn2_w_down'][layer].astype(bf16)]
        return shards

    def full_weight(idx, land, layer):
        w = land.reshape(-1, D)
        if idx == 3 and layer % 2 == 0:
            w = w.reshape(N_DEV, AB_SHARD_PAD, D)[:, :AB_SHARD].reshape(AB_IN, D)
            w = jnp.pad(w, ((0, AB_IN_PAD - AB_IN), (0, 0)))
        return w

    def tied(gain, token):
        return gain if token is None else gain + token[0:1, 0:1]

    def start_groups(arrays, groups, whole, after, name):
        out = []
        for gi, idx in enumerate(groups):
            suffix = "" if len(groups) == 1 else "abc"[gi]
            handle = push_start([arrays[i] for i in idx], whole, after, name=name + suffix)
            after = handle[5]
            out.append((idx, handle, suffix))
        return out

    def wait_group(inflight, gi, whole, after, name):
        idx, handle, suffix = inflight[gi]
        return dict(zip(idx, push_wait(handle, whole, after, name=name + suffix)))

    one_group = [tuple(range(8))]
    by_block = [(0, 1, 2), (3, 4), (5, 6, 7)]
    zeros_tile = jnp.zeros((8, 128), f32)
    gathered = [None] * DEPTH
    inflight = start_groups(layer_shards(0), by_block, True, zeros_tile, "gather_start_0")

    small_shards = [W['dn_conv_w'], W['sgu_norm_g'], W['sgu_norm_b'], W['sc_conv_w']]
    gs = all_gather_small(_pack(small_shards))
    per_dev = [_unpack(gs[p], [a.shape for a in small_shards]) for p in range(N_DEV)]
    dn_conv_full, sgu_g_full, sgu_b_full, sc_conv_full = [
        jnp.concatenate([per_dev[p][i] for p in range(N_DEV)], axis=-1) for i in range(4)]

    h = x.reshape(T, D)
    saved = []
    for layer in range(DEPTH):
        e = layer // 2
        mine, landed = inflight, {}
        landed.update(wait_group(mine, 0, True, h, f"gather_wait_{layer}"))
        token = None
        if layer + 1 < DEPTH:
            inflight = start_groups(layer_shards(layer + 1), one_group, True, landed[0], f"gather_start_{layer + 1}")
            token = inflight[-1][1][5]
        wg1, wu1, wd1 = [full_weight(i, landed[i], layer) for i in (0, 1, 2)]
        sv = {'h0': h}
        h, sv['g1'], sv['u1'] = ffn_fwd(h, tied(_row(W['ffn1_norm'][layer]), token), wg1, wu1, wd1)
        sv['h1'] = h
        if len(mine) > 1:
            landed.update(wait_group(mine, 1, True, h, f"gather_wait_{layer}"))
        win, wout = [full_weight(i, landed[i], layer) for i in (3, 4)]
        if layer % 2 == 0:
            a_in, qkv_pre, z, bg = in_proj_fwd(h, _row(W['mix_norm'][layer]), win, [512, 1536, 512, 128])
            ya = pool_fwd(a_in, W['pool_w'][e], _row(W['pool_scale'][e]), S)
            qkv = seq_chan_fwd(conv_silu_math, dn_conv_full[e], [qkv_pre], S, f32, "dn_conv_fwd")
            alog, dtb = _lane_row(W['dn_a_log'][e], N_HEADS), _lane_row(W['dn_dt_bias'][e], N_HEADS)
            prep = dn_prep_fwd(qkv, bg, alog, dtb)
            yb = dn_recur_fwd(prep, z, _row(W['dn_out_norm'][e]), S)
            sv.update(a_in=a_in, qkv_pre=qkv_pre, z=z, bg=bg, qkv=qkv, alog=alog, dtb=dtb, prep=prep)
        else:
            up, vp, xd, bgate, cg = in_proj_fwd(h, _row(W['mix_norm'][layer]), win, [512] * 5)
            sbias = W['sgu_bias'][e].reshape(N_HEADS, SGU_BLOCK, 1)
            ya = sgu_fwd(up, vp, _row(sgu_g_full[e]), _row(sgu_b_full[e]), W['sgu_w'][e], sbias,
                         _tile(S, SGU_TILE))
            yb = seq_chan_fwd(gated_conv_math, sc_conv_full[e], [xd, bgate, cg], S, bf16, "sc_conv_fwd")
            sv.update(up=up, vp=vp, xd=xd, bgate=bgate, cg=cg, sbias=sbias)
        sv.update(ya=ya, yb=yb)
        h = out_proj_fwd(h, ya, yb, wout)
        sv['h2'] = h
        if len(mine) > 1:
            landed.update(wait_group(mine, 2, True, h, f"gather_wait_{layer}"))
        wg2, wu2, wd2 = [full_weight(i, landed[i], layer) for i in (5, 6, 7)]
        h, sv['g2'], sv['u2'] = ffn_fwd(h, _row(W['ffn2_norm'][layer]), wg2, wu2, wd2)
        gathered[layer] = [wg1, wu1, wd1, win, wout, wg2, wu2, wd2]
        saved.append(sv)

    loss_part, dh, dfinal = loss_fwd_bwd(h, _row(W['final_norm']), loss_target.reshape(T, D))
    loss = lax.psum(loss_part[0, 0], ("x", "y", "c"))

    G = {}
    G['final_norm'] = dfinal[0]
    for n in ('ffn1_norm', 'mix_norm', 'ffn2_norm'):
        G[n] = [None] * DEPTH
    for n in ('pool_w', 'pool_scale', 'dn_conv_w', 'dn_a_log', 'dn_dt_bias', 'dn_out_norm',
              'sgu_norm_g', 'sgu_norm_b', 'sgu_w', 'sgu_bias', 'sc_conv_w'):
        G[n] = [None] * 2
    big = [None] * DEPTH
    inflight = None
    last = []

    for layer in reversed(range(DEPTH)):
        e = layer // 2
        sv = saved[layer]
        wg1, wu1, wd1, win, wout, wg2, wu2, wd2 = gathered[layer]
        token = None if inflight is None else inflight[-1][1][5]
        dh, dgain, xn, act, dg, du, dy = ffn_bwd(sv['h2'], tied(_row(W['ffn2_norm'][layer]), token), sv['g2'],
                                                  sv['u2'], dh, wg2, wu2, wd2)
        G['ffn2_norm'][layer] = dgain[0]
        dwg2, dwu2, dwd2 = wgrad(dg, xn), wgrad(du, xn), wgrad(act, dy)
        mix_token = ffn1_token = None
        if layer == 0:
            big[1] = wait_group(inflight, 0, False, dwd2, "scatter_wait_1")
            last += start_groups({5: dwg2, 6: dwu2, 7: dwd2}, [(5, 6, 7)], False, big[1][0], "scatter_start_0a")
            mix_token = last[-1][1][5]
        dya, dyb, dwout = out_proj_bwd(dh, sv['ya'], sv['yb'], wout)
        if layer % 2 == 0:
            da, dpw, dsc = pool_bwd(sv['a_in'], W['pool_w'][e], _row(W['pool_scale'][e]), dya, S)
            G['pool_w'][e], G['pool_scale'][e] = dpw, dsc[0]
            *cts, dz, don = dn_recur_bwd(sv['prep'], sv['z'], _row(W['dn_out_norm'][e]), dyb, S)
            G['dn_out_norm'][e] = don[0]
            dqkv, dbg, dalog, ddtb = dn_prep_bwd(sv['qkv'], sv['bg'], sv['alog'], sv['dtb'], cts)
            G['dn_a_log'][e], G['dn_dt_bias'][e] = dalog[0, N_HEADS:2 * N_HEADS], ddtb[0, N_HEADS:2 * N_HEADS]
            dqkv_pre, dconv = seq_chan_bwd(conv_silu_math, dn_conv_full[e], [sv['qkv_pre']], dqkv, S, bf16,
                                           "dn_conv_bwd")
            G['dn_conv_w'][e] = dconv
            dpieces = [da, dqkv_pre, dz, dbg]
        else:
            dup, dvp, dng, dnb, dsw, dsb = sgu_bwd(sv['up'], sv['vp'], _row(sgu_g_full[e]), _row(sgu_b_full[e]),
                                                   W['sgu_w'][e], sv['sbias'], dya, _tile(S, SGU_TILE))
            G['sgu_norm_g'][e], G['sgu_norm_b'][e] = dng[0], dnb[0]
            G['sgu_w'][e], G['sgu_bias'][e] = dsw, dsb.reshape(N_HEADS, SGU_BLOCK)
            dxd, dbgate, dcg, dscw = seq_chan_bwd(gated_conv_math, sc_conv_full[e],
                                                  [sv['xd'], sv['bgate'], sv['cg']], dyb, S, bf16, "sc_conv_bwd")
            G['sc_conv_w'][e] = dscw
            dpieces = [dup, dvp, dxd, dbgate, dcg]
        dh, dgain, dwin = in_proj_bwd(sv['h1'], tied(_row(W['mix_norm'][layer]), mix_token), dpieces, dh, win)
        G['mix_norm'][layer] = dgain[0]
        if layer % 2 == 0:
            dwin = jnp.pad(dwin[:AB_IN].reshape(N_DEV, AB_SHARD, D), ((0, 0), (0, AB_SHARD_PAD - AB_SHARD), (0, 0)))
            dwin = dwin.reshape(N_DEV * AB_SHARD_PAD, D)
        if layer == 0:
            last += start_groups({3: dwin, 4: dwout}, [(3, 4)], False, mix_token, "scatter_start_0b")
            ffn1_token = last[-1][1][5]
        dh, dgain, xn, act, dg, du, dy = ffn_bwd(sv['h0'], tied(_row(W['ffn1_norm'][layer]), ffn1_token), sv['g1'],
                                                  sv['u1'], dh, wg1, wu1, wd1)
        G['ffn1_norm'][layer] = dgain[0]
        dwg1, dwu1, dwd1 = wgrad(dg, xn), wgrad(du, xn), wgrad(act, dy)
        if layer == 0:
            last += start_groups({0: dwg1, 1: dwu1, 2: dwd1}, [(0, 1, 2)], False, ffn1_token, "scatter_start_0c")
        else:
            after = zeros_tile
            if inflight is not None:
                big[layer + 1] = wait_group(inflight, 0, False, dh, f"scatter_wait_{layer + 1}")
                after = big[layer + 1][0]
            inflight = start_groups([dwg1, dwu1, dwd1, dwin, dwout, dwg2, dwu2, dwd2], one_group, False, after,
                                    f"scatter_start_{layer}")

    grad_x = dh.reshape(B, S, D)

    small_names = ['ffn1_norm', 'mix_norm', 'ffn2_norm', 'pool_w', 'pool_scale', 'dn_conv_w', 'dn_a_log',
                   'dn_dt_bias', 'dn_out_norm', 'sgu_norm_g', 'sgu_norm_b', 'sgu_w', 'sgu_bias', 'sc_conv_w',
                   'final_norm']
    small_g = [G[n] if n == 'final_norm' else jnp.stack(G[n]) for n in small_names]
    reduced_buf = all_reduce_small(_pack(small_g))
    reduced = _unpack(reduced_buf, [a.shape for a in small_g])
    big[0] = wait_group(last, 0, False, reduced_buf, "scatter_wait_0a")
    big[0].update(wait_group(last, 1, False, big[0][5], "scatter_wait_0b"))
    grads = {}
    for n, g in zip(small_names, reduced):
        if n in ('dn_conv_w', 'sgu_norm_g', 'sgu_norm_b', 'sc_conv_w'):
            c = W[n].shape[-1]
            g = lax.dynamic_slice_in_dim(g, me * c, c, axis=g.ndim - 1)
        grads[n] = g

    def stack_layers(idx, transpose, sel):
        out = []
        for layer in sel:
            g = sum8(big[layer][idx])
            if idx == 3 and layer % 2 == 0:
                g = g[:AB_SHARD]
            out.append(g.T if transpose else g)
        return jnp.stack(out)

    all_layers, even, odd = range(DEPTH), range(0, DEPTH, 2), range(1, DEPTH, 2)
    delta, new_m, new_v = {}, {}, {}

    def update(n, idx, transpose, sel):
        grads[n] = stack_layers(idx, transpose, sel)
        shp = W[n].shape
        two = lambda a: a.reshape(-1, shp[-1])
        d_, m_, v_ = adamw(two(W[n]), two(grads[n]), two(M[n]), two(V[n]))
        delta[n], new_m[n], new_v[n] = d_.reshape(shp), m_.reshape(shp), v_.reshape(shp)

    shapes = [W[n].shape for n in small_names]
    d_, m_, v_ = adamw(_pack([W[n] for n in small_names]), _pack([grads[n] for n in small_names]),
                       _pack([M[n] for n in small_names]), _pack([V[n] for n in small_names]))
    for n, a, b_, c_ in zip(small_names, _unpack(d_, shapes), _unpack(m_, shapes), _unpack(v_, shapes)):
        delta[n], new_m[n], new_v[n] = a, b_, c_
    update('ab_w_in', 3, True, even)
    update('cd_w_in', 3, True, odd)
    update('ab_w_out', 4, False, even)
    update('cd_w_out', 4, False, odd)
    update('ffn2_w_gate', 5, True, all_layers)
    update('ffn2_w_up', 6, True, all_layers)
    update('ffn2_w_down', 7, False, all_layers)
    big[0].update(wait_group(last, 2, False, new_v['ffn2_w_down'], "scatter_wait_0c"))
    update('ffn1_w_gate', 0, True, all_layers)
    update('ffn1_w_up', 1, True, all_layers)
    update('ffn1_w_down', 2, False, all_layers)

    return (loss, grad_x, *[grads[n] for n in names], *[delta[n] for n in names],
            *[new_m[n] for n in names], *[new_v[n] for n in names])
```

```python
import functools
import math

import jax
import jax.numpy as jnp
from jax import lax
from jax.experimental import pallas as pl
from jax.experimental.pallas import tpu as pltpu

f32, bf16 = jnp.float32, jnp.bfloat16

D_MODEL = 1024
DEPTH = 4
CHUNK = 64
POOL_WINDOWS = (2, 4, 8, 16)
HEAD = 128
N_HEADS = 4
SGU_BLOCK = 128
SGU_TILE = 512
FFN_DIM = 2816
AB_IN = 2568
AB_IN_PAD = 2688
AB_SHARD = 321
AB_SHARD_PAD = 336
EPS = 1e-6
N_DEV = 8
MESH = pl.DeviceIdType.MESH

ADAM_LR, ADAM_B1, ADAM_B2, ADAM_EPS, ADAM_WD, ADAM_STEP = 0.001, 0.9, 0.999, 1e-08, 0.01, 10

VMEM_BIG = 56 * 1024 * 1024
VMEM_MID = 40 * 1024 * 1024


def _pcall(body, **kw):
    return pl.pallas_call(body, **kw)


def _params(sem=None, vmem=None):
    return pltpu.CompilerParams(dimension_semantics=sem, vmem_limit_bytes=vmem)


def _sds(shape, dtype):
    return jax.ShapeDtypeStruct(shape, dtype)


_NN2, _NT2, _TN2 = (((1,), (0,)), ((), ())), (((1,), (1,)), ((), ())), (((0,), (0,)), ((), ()))
_NN3, _NT3, _TN3 = (((2,), (1,)), ((0,), (0,))), (((2,), (2,)), ((0,), (0,))), (((1,), (1,)), ((0,), (0,)))


def _dg(a, b, dims, hi):
    if hi:
        return lax.dot_general(a.astype(f32), b.astype(f32), dims, preferred_element_type=f32,
                               precision=lax.Precision.HIGH)
    return lax.dot_general(a.astype(bf16), b.astype(bf16), dims, preferred_element_type=f32)


def _make_mm(nn, nt, tn, hi):
    @jax.custom_vjp
    def mm(a, b):
        return _dg(a, b, nn, hi)

    def mm_bwd(res, ct):
        a, b = res
        return _dg(ct, b, nt, hi).astype(a.dtype), _dg(a, ct, tn, hi).astype(b.dtype)

    mm.defvjp(lambda a, b: (_dg(a, b, nn, hi), (a, b)), mm_bwd)

    @jax.custom_vjp
    def mm_nt(a, b):
        return _dg(a, b, nt, hi)

    def mm_nt_bwd(res, ct):
        a, b = res
        return _dg(ct, b, nn, hi).astype(a.dtype), _dg(ct, a, tn, hi).astype(b.dtype)

    mm_nt.defvjp(lambda a, b: (_dg(a, b, nt, hi), (a, b)), mm_nt_bwd)

    @jax.custom_vjp
    def mm_tn(a, b):
        return _dg(a, b, tn, hi)

    def mm_tn_bwd(res, ct):
        a, b = res
        return _dg(b, ct, nt, hi).astype(a.dtype), _dg(a, ct, nn, hi).astype(b.dtype)

    mm_tn.defvjp(lambda a, b: (_dg(a, b, tn, hi), (a, b)), mm_tn_bwd)
    return mm, mm_nt, mm_tn


mm, mm_nt, mm_tn = _make_mm(_NN2, _NT2, _TN2, False)
bmm, bmm_nt, bmm_tn = _make_mm(_NN3, _NT3, _TN3, False)
bmm_hi, _, _ = _make_mm(_NN3, _NT3, _TN3, True)


def _shift_raw(x, k):
    n = x.shape[0]
    t = lax.broadcasted_iota(jnp.int32, x.shape, 0)
    if k > 0:
        return jnp.where(t >= k, pltpu.roll(x, k, axis=0), 0.0)
    k = -k
    return jnp.where(t < n - k, pltpu.roll(x, n - k, axis=0), 0.0)


@functools.partial(jax.custom_vjp, nondiff_argnums=(1,))
def shift(x, k):
    return _shift_raw(x, k)


shift.defvjp(lambda x, k: (_shift_raw(x, k), None), lambda k, _, ct: (_shift_raw(ct, -k),))


def _silu(x):
    return x * jax.nn.sigmoid(x)


def _softplus(x):
    return jnp.maximum(x, 0.0) + jnp.log(1.0 + jnp.exp(-jnp.abs(x)))


def _rms_fwd(h, gain):
    rstd = lax.rsqrt(jnp.mean(h * h, axis=-1, keepdims=True) + EPS)
    xhat = h * rstd
    return xhat * gain, xhat, rstd


def _rms_bwd(dxn, xhat, rstd, gain):
    dxhat = dxn * gain
    dh = rstd * (dxhat - xhat * jnp.mean(dxhat * xhat, axis=-1, keepdims=True))
    return dh, jnp.sum(dxn * xhat, axis=0, keepdims=True)


def _acc_rows(ref, val, first):
    @pl.when(first)
    def _():
        ref[...] = jnp.zeros_like(ref)
    ref[0:1, :] += val


def _tile(n, cap):
    t = min(n, cap)
    assert n % t == 0, (n, t)
    return t


def _resident(shape):
    nd = len(shape)
    return pl.BlockSpec(shape, lambda *_: (0,) * nd, pipeline_mode=pl.Buffered(1))


def ffn_fwd(h, gain, wgt, wut, wd):
    T, D = h.shape
    F = wgt.shape[0]
    tm = _tile(T, 256)

    def body(h_ref, gain_ref, wg_ref, wu_ref, wd_ref, ho_ref, g_ref, u_ref):
        hh = h_ref[...]
        xn, _, _ = _rms_fwd(hh, gain_ref[...])
        xb = xn.astype(bf16)
        g = _dg(xb, wg_ref[...], _NT2, False)
        u = _dg(xb, wu_ref[...], _NT2, False)
        y = _dg(_silu(g) * u, wd_ref[...], _NN2, False)
        ho_ref[...] = hh + 0.5 * y
        g_ref[...] = g.astype(bf16)
        u_ref[...] = u.astype(bf16)

    row = lambda w: pl.BlockSpec((tm, w), lambda i: (i, 0))
    return _pcall(
        body, name="ffn_fwd", grid=(T // tm,),
        in_specs=[row(D), _resident((1, D)), _resident((F, D)), _resident((F, D)), _resident((F, D))],
        out_specs=[row(D), row(F), row(F)],
        out_shape=[_sds((T, D), f32), _sds((T, F), bf16), _sds((T, F), bf16)],
        compiler_params=_params(("parallel",), VMEM_BIG),
    )(h, gain, wgt, wut, wd)


def ffn_bwd(h, gain, g, u, dout, wgt, wut, wd):
    T, D = h.shape
    F = wgt.shape[0]
    tm = _tile(T, 256)

    def body(h_ref, gain_ref, g_ref, u_ref, do_ref, wg_ref, wu_ref, wd_ref,
             dh_ref, dgain_ref, xn_ref, act_ref, dg_ref, du_ref, dy_ref):
        hh, dout_ = h_ref[...], do_ref[...]
        gain_ = gain_ref[...]
        xn, xhat, rstd = _rms_fwd(hh, gain_)
        gg, uu = g_ref[...].astype(f32), u_ref[...].astype(f32)
        dy = (0.5 * dout_).astype(bf16)
        dact = _dg(dy, wd_ref[...], _NT2, False)
        sg = jax.nn.sigmoid(gg)
        silu = gg * sg
        dgate = (dact * uu * (sg * (1.0 + gg * (1.0 - sg)))).astype(bf16)
        dup = (dact * silu).astype(bf16)
        dxn = _dg(dgate, wg_ref[...], _NN2, False) + _dg(dup, wu_ref[...], _NN2, False)
        dh, dgain = _rms_bwd(dxn, xhat, rstd, gain_)
        dh_ref[...] = dout_ + dh
        _acc_rows(dgain_ref, dgain, pl.program_id(0) == 0)
        xn_ref[...] = xn.astype(bf16)
        act_ref[...] = (silu * uu).astype(bf16)
        dg_ref[...] = dgate
        du_ref[...] = dup
        dy_ref[...] = dy

    row = lambda w: pl.BlockSpec((tm, w), lambda i: (i, 0))
    return _pcall(
        body, name="ffn_bwd", grid=(T // tm,),
        in_specs=[row(D), _resident((1, D)), row(F), row(F), row(D),
                  _resident((F, D)), _resident((F, D)), _resident((F, D))],
        out_specs=[row(D), pl.BlockSpec((8, D), lambda i: (0, 0)), row(D), row(F), row(F), row(F), row(D)],
        out_shape=[_sds((T, D), f32), _sds((8, D), f32), _sds((T, D), bf16), _sds((T, F), bf16),
                   _sds((T, F), bf16), _sds((T, F), bf16), _sds((T, D), bf16)],
        compiler_params=_params(("arbitrary",), VMEM_BIG),
    )(h, gain, g, u, dout, wgt, wut, wd)


def _col_tile(n, cap=1408):
    best = None
    for c in range(128, cap + 1, 128):
        if n % c == 0:
            best = c
    assert best is not None, n
    return best


def wgrad(a, b):
    T, N = a.shape
    K = b.shape[1]
    nc, tk = _col_tile(N), _tile(T, 512)
    nk = T // tk

    def body(a_ref, b_ref, o_ref, acc_ref):
        k = pl.program_id(1)

        @pl.when(k == 0)
        def _():
            acc_ref[...] = jnp.zeros_like(acc_ref)

        acc_ref[...] += _dg(a_ref[...], b_ref[...], _TN2, False)

        @pl.when(k == nk - 1)
        def _():
            o_ref[...] = acc_ref[...].astype(bf16)

    return _pcall(
        body, name="wgrad", grid=(N // nc, nk),
        in_specs=[pl.BlockSpec((tk, nc), lambda j, k: (k, j)), pl.BlockSpec((tk, K), lambda j, k: (k, 0))],
        out_specs=pl.BlockSpec((nc, K), lambda j, k: (j, 0)),
        out_shape=_sds((N, K), bf16),
        scratch_shapes=[pltpu.VMEM((nc, K), f32)],
        compiler_params=_params(("parallel", "arbitrary"), VMEM_MID),
    )(a, b)


def in_proj_fwd(h, gain, wt, widths):
    T, D = h.shape
    N = wt.shape[0]
    assert sum(widths) == N
    tm = _tile(T, 512)
    offs = [sum(widths[:i]) for i in range(len(widths))]

    def body(h_ref, gain_ref, w_ref, *outs):
        xn, _, _ = _rms_fwd(h_ref[...], gain_ref[...])
        p = _dg(xn, w_ref[...], _NT2, False)
        for o_ref, off, wd_ in zip(outs, offs, widths):
            o_ref[...] = p[:, off:off + wd_]

    row = lambda w: pl.BlockSpec((tm, w), lambda i: (i, 0))
    return _pcall(
        body, name="in_proj_fwd", grid=(T // tm,),
        in_specs=[row(D), _resident((1, D)), _resident((N, D))],
        out_specs=[row(w) for w in widths],
        out_shape=[_sds((T, w), f32) for w in widths],
        compiler_params=_params(("parallel",), VMEM_BIG),
    )(h, gain, wt)


def in_proj_bwd(h, gain, dpieces, dout, wt):
    T, D = h.shape
    N = wt.shape[0]
    widths = [p.shape[1] for p in dpieces]
    assert sum(widths) == N
    tm = _tile(T, 256)
    nt = T // tm
    npc = len(dpieces)

    def body(*refs):
        h_ref, gain_ref = refs[0], refs[1]
        p_refs = refs[2:2 + npc]
        do_ref, w_ref, dh_ref, dgain_ref, dw_ref, acc_ref = refs[2 + npc:]
        i = pl.program_id(0)
        gain_ = gain_ref[...]
        xn, xhat, rstd = _rms_fwd(h_ref[...], gain_)
        dp = jnp.concatenate([r[...].astype(bf16) for r in p_refs], axis=-1)
        dxn = _dg(dp, w_ref[...], _NN2, False)
        dh, dgain = _rms_bwd(dxn, xhat, rstd, gain_)
        dh_ref[...] = do_ref[...] + dh
        _acc_rows(dgain_ref, dgain, i == 0)

        @pl.when(i == 0)
        def _():
            acc_ref[...] = jnp.zeros_like(acc_ref)

        acc_ref[...] += _dg(dp, xn, _TN2, False)

        @pl.when(i == nt - 1)
        def _():
            dw_ref[...] = acc_ref[...].astype(bf16)

    row = lambda w: pl.BlockSpec((tm, w), lambda i: (i, 0))
    return _pcall(
        body, name="in_proj_bwd", grid=(nt,),
        in_specs=[row(D), _resident((1, D))] + [row(w) for w in widths] + [row(D), _resident((N, D))],
        out_specs=[row(D), pl.BlockSpec((8, D), lambda i: (0, 0)), pl.BlockSpec((N, D), lambda i: (0, 0))],
        out_shape=[_sds((T, D), f32), _sds((8, D), f32), _sds((N, D), bf16)],
        scratch_shapes=[pltpu.VMEM((N, D), f32)],
        compiler_params=_params(("arbitrary",), VMEM_BIG),
    )(h, gain, *dpieces, dout, wt)


def out_proj_fwd(h, ya, yb, w):
    T, D = h.shape
    half = ya.shape[1]
    tm = _tile(T, 512)

    def body(h_ref, ya_ref, yb_ref, w_ref, o_ref):
        y = jnp.concatenate([ya_ref[...], yb_ref[...]], axis=-1)
        o_ref[...] = h_ref[...] + _dg(y, w_ref[...], _NN2, False)

    row = lambda w_: pl.BlockSpec((tm, w_), lambda i: (i, 0))
    return _pcall(
        body, name="out_proj_fwd", grid=(T // tm,),
        in_specs=[row(D), row(half), row(half), _resident((2 * half, D))],
        out_specs=row(D), out_shape=_sds((T, D), f32),
        compiler_params=_params(("parallel",), VMEM_MID),
    )(h, ya, yb, w)


def out_proj_bwd(dout, ya, yb, w):
    T, D = dout.shape
    half = ya.shape[1]
    tm = _tile(T, 512)
    nt = T // tm

    def body(do_ref, ya_ref, yb_ref, w_ref, dya_ref, dyb_ref, dw_ref, acc_ref):
        i = pl.program_id(0)
        dob = do_ref[...].astype(bf16)
        dy = _dg(dob, w_ref[...], _NT2, False)
        dya_ref[...] = dy[:, :half]
        dyb_ref[...] = dy[:, half:]

        @pl.when(i == 0)
        def _():
            acc_ref[...] = jnp.zeros_like(acc_ref)

        y = jnp.concatenate([ya_ref[...], yb_ref[...]], axis=-1)
        acc_ref[...] += _dg(y, dob, _TN2, False)

        @pl.when(i == nt - 1)
        def _():
            dw_ref[...] = acc_ref[...].astype(bf16)

    row = lambda w_: pl.BlockSpec((tm, w_), lambda i: (i, 0))
    return _pcall(
        body, name="out_proj_bwd", grid=(nt,),
        in_specs=[row(D), row(half), row(half), _resident((2 * half, D))],
        out_specs=[row(half), row(half), pl.BlockSpec((2 * half, D), lambda i: (0, 0))],
        out_shape=[_sds((T, half), f32), _sds((T, half), f32), _sds((2 * half, D), bf16)],
        scratch_shapes=[pltpu.VMEM((2 * half, D), f32)],
        compiler_params=_params(("arbitrary",), VMEM_MID),
    )(dout, ya, yb, w)


def _conv_taps(x, w):
    K = w.shape[0]
    acc = x * w[K - 1:K, :]
    for i in range(K - 1):
        acc = acc + shift(x, K - 1 - i) * w[i:i + 1, :]
    return acc


def conv_silu_math(w, x):
    return _silu(_conv_taps(x, w))


def gated_conv_math(w, xd, bg, cg):
    return bg * _conv_taps(cg * xd, w)


def seq_chan_fwd(math, w, xs, seq, out_dtype, name):
    T, C = xs[0].shape
    K = w.shape[0]
    nb, nc = T // seq, C // HEAD

    def body(w_ref, *refs):
        o_ref = refs[-1]
        o_ref[...] = math(w_ref[...], *[r[...] for r in refs[:-1]]).astype(out_dtype)

    blk = pl.BlockSpec((seq, HEAD), lambda j, b: (b, j))
    return _pcall(
        body, name=name, grid=(nc, nb),
        in_specs=[pl.BlockSpec((K, HEAD), lambda j, b: (0, j))] + [blk] * len(xs),
        out_specs=blk, out_shape=_sds((T, C), out_dtype),
        compiler_params=_params(("parallel", "parallel"), VMEM_MID),
    )(w, *xs)


def seq_chan_bwd(math, w, xs, dy, seq, dx_dtype, name):
    T, C = xs[0].shape
    K = w.shape[0]
    nb, nc = T // seq, C // HEAD
    nx = len(xs)

    def body(w_ref, *refs):
        x_refs, dy_ref = refs[:nx], refs[nx]
        dx_refs, dw_ref = refs[nx + 1:2 * nx + 1], refs[2 * nx + 1]
        _, vjp = jax.vjp(math, w_ref[...], *[r[...] for r in x_refs])
        grads = vjp(dy_ref[...].astype(f32))

        @pl.when(pl.program_id(1) == 0)
        def _():
            dw_ref[...] = jnp.zeros_like(dw_ref)

        dw_ref[...] += grads[0]
        for r, gx in zip(dx_refs, grads[1:]):
            r[...] = gx.astype(dx_dtype)

    blk = pl.BlockSpec((seq, HEAD), lambda j, b: (b, j))
    wblk = pl.BlockSpec((K, HEAD), lambda j, b: (0, j))
    return _pcall(
        body, name=name, grid=(nc, nb),
        in_specs=[wblk] + [blk] * (nx + 1),
        out_specs=[blk] * nx + [wblk],
        out_shape=[_sds((T, C), dx_dtype)] * nx + [_sds((K, C), f32)],
        compiler_params=_params(("parallel", "arbitrary"), VMEM_MID),
    )(w, *xs, dy)


def pool_group_math(win, ag, pw, scale):
    t = lax.broadcasted_iota(jnp.int32, (ag.shape[0], 1), 0)
    s, k = ag, 1
    while k < win:
        s = s + shift(s, k)
        k *= 2
    pooled = s / jnp.minimum(t + 1, win).astype(f32) - ag
    return mm(pooled, pw) * scale


def pool_fwd(a, pw, scale, seq):
    T, C = a.shape

    def body(a_ref, pw_ref, sc_ref, o_ref):
        for gi, win in enumerate(POOL_WINDOWS):
            cols = slice(gi * HEAD, (gi + 1) * HEAD)
            o_ref[:, cols] = pool_group_math(win, a_ref[:, cols], pw_ref[gi], sc_ref[:, cols]).astype(bf16)

    blk = pl.BlockSpec((seq, C), lambda b: (b, 0))
    return _pcall(
        body, name="pool_fwd", grid=(T // seq,),
        in_specs=[blk, _resident(pw.shape), _resident((1, C))],
        out_specs=blk, out_shape=_sds((T, C), bf16),
        compiler_params=_params(("parallel",), VMEM_MID),
    )(a, pw, scale)


def pool_bwd(a, pw, scale, dy, seq):
    T, C = a.shape

    def body(a_ref, pw_ref, sc_ref, dy_ref, da_ref, dpw_ref, dsc_ref):
        first = pl.program_id(0) == 0

        @pl.when(first)
        def _():
            dpw_ref[...] = jnp.zeros_like(dpw_ref)
            dsc_ref[...] = jnp.zeros_like(dsc_ref)

        for gi, win in enumerate(POOL_WINDOWS):
            cols = slice(gi * HEAD, (gi + 1) * HEAD)
            _, vjp = jax.vjp(functools.partial(pool_group_math, win), a_ref[:, cols], pw_ref[gi], sc_ref[:, cols])
            da, dpw, dsc = vjp(dy_ref[:, cols])
            dpw_ref[gi] += dpw
            dsc_ref[0:1, cols] += dsc
            da_ref[:, cols] = da.astype(bf16)

    blk = pl.BlockSpec((seq, C), lambda b: (b, 0))
    return _pcall(
        body, name="pool_bwd", grid=(T // seq,),
        in_specs=[blk, _resident(pw.shape), _resident((1, C)), blk],
        out_specs=[blk, pl.BlockSpec(pw.shape, lambda b: (0, 0, 0)), pl.BlockSpec((8, C), lambda b: (0, 0))],
        out_shape=[_sds((T, C), bf16), _sds(pw.shape, f32), _sds((8, C), f32)],
        compiler_params=_params(("arbitrary",), VMEM_MID),
    )(a, pw, scale, dy)


def _neumann_inverse(lmat):
    n = lmat.shape[-1]
    ii = lax.broadcasted_iota(jnp.int32, (n, n), 0)
    jj = lax.broadcasted_iota(jnp.int32, (n, n), 1)
    inv = jnp.where((ii == jj)[None], 1.0, 0.0) - lmat
    pw_ = _dg(lmat, lmat, _NN3, True)
    steps = int(math.log2(n)) - 1
    for i in range(steps):
        inv = inv + _dg(inv, pw_, _NN3, True)
        if i < steps - 1:
            pw_ = _dg(pw_, pw_, _NN3, True)
    return inv


@jax.custom_vjp
def unit_lower_inverse(lmat):
    return _neumann_inverse(lmat)


def _unit_lower_inverse_fwd(lmat):
    inv = _neumann_inverse(lmat)
    return inv, inv


def _unit_lower_inverse_bwd(inv, ct):
    return (-_dg(_dg(inv, ct, _TN3, True), inv, _NT3, True),)


unit_lower_inverse.defvjp(_unit_lower_inverse_fwd, _unit_lower_inverse_bwd)


def dn_prep_math(qkv, bg, alog, dtb):
    tt = qkv.shape[0]
    nt = tt // CHUNK
    nb = nt * N_HEADS
    W = N_HEADS * HEAD
    beta_all = jax.nn.sigmoid(bg)
    g_all = -jnp.exp(alog) * _softplus(bg + dtb)

    def heads(fn):
        return jnp.stack([fn(hd).reshape(nt, CHUNK, HEAD) for hd in range(N_HEADS)], axis=1).reshape(nb, CHUNK, HEAD)

    def l2n(x):
        return x * lax.rsqrt(jnp.sum(x * x, axis=-1, keepdims=True) + EPS)

    q = heads(lambda hd: l2n(qkv[:, hd * HEAD:(hd + 1) * HEAD]) * (HEAD ** -0.5))
    k = heads(lambda hd: l2n(qkv[:, W + hd * HEAD:W + (hd + 1) * HEAD]))
    v = heads(lambda hd: qkv[:, 2 * W + hd * HEAD:2 * W + (hd + 1) * HEAD])
    beta = heads(lambda hd: jnp.broadcast_to(beta_all[:, hd:hd + 1], (tt, HEAD)))
    g = heads(lambda hd: jnp.broadcast_to(g_all[:, N_HEADS + hd:N_HEADS + hd + 1], (tt, HEAD)))

    ii = lax.broadcasted_iota(jnp.int32, (CHUNK, CHUNK), 0)
    jj = lax.broadcasted_iota(jnp.int32, (CHUNK, CHUNK), 1)
    tril, strict = (ii >= jj)[None], (ii > jj)[None]
    ones_b = jnp.ones((nb, CHUNK, CHUNK), f32)
    tril_b = jnp.where(tril, ones_b, 0.0)
    eye_b = jnp.where((ii == jj)[None], ones_b, 0.0)

    gcb = bmm_hi(tril_b, g)
    gcol = gcb[:, :, :CHUNK]
    grow = bmm_hi(ones_b, eye_b * gcol)
    gamma = jnp.where(tril, jnp.exp(jnp.where(tril, gcol - grow, 0.0)), 0.0)
    kb = k * beta
    lmat = jnp.where(strict, bmm_nt(kb, k) * gamma, 0.0)
    inv = unit_lower_inverse(lmat)
    egc = jnp.exp(gcb)
    u = bmm_hi(inv, v * beta)
    w = bmm_hi(inv, kb * egc)
    aqk = bmm_nt(q, k) * gamma
    qd = q * egc
    glast = gcb[:, CHUNK - 1:CHUNK, :]
    kd = k * jnp.exp(glast - gcb)
    last = jnp.exp(glast)
    r4 = lambda x: x.reshape((nt, N_HEADS) + x.shape[1:])
    return r4(u), r4(w), r4(qd), r4(kd), r4(aqk), r4(last)


_PREP_DTYPES = (f32, bf16, bf16, bf16, bf16, f32)


def _prep_specs(nt, T):
    nchunks = T // CHUNK
    shapes = [(HEAD,), (HEAD,), (HEAD,), (HEAD,), (CHUNK,), (HEAD,)]
    rows = [CHUNK, CHUNK, CHUNK, CHUNK, CHUNK, 1]
    specs = [pl.BlockSpec((nt, N_HEADS, r, s[0]), lambda i: (i, 0, 0, 0)) for r, s in zip(rows, shapes)]
    outs = [(nchunks, N_HEADS, r, s[0]) for r, s in zip(rows, shapes)]
    return specs, outs


def dn_prep_fwd(qkv, bg, alog, dtb):
    T = qkv.shape[0]
    tt = _tile(T, 256)
    nt = tt // CHUNK
    specs, shapes = _prep_specs(nt, T)

    def body(qkv_ref, bg_ref, alog_ref, dtb_ref, *outs):
        res = dn_prep_math(qkv_ref[...], bg_ref[...], alog_ref[...], dtb_ref[...])
        for o_ref, r, dt in zip(outs, res, _PREP_DTYPES):
            o_ref[...] = r.astype(dt)

    row = lambda w: pl.BlockSpec((tt, w), lambda i: (i, 0))
    return _pcall(
        body, name="dn_prep_fwd", grid=(T // tt,),
        in_specs=[row(qkv.shape[1]), row(HEAD), _resident((1, HEAD)), _resident((1, HEAD))],
        out_specs=specs, out_shape=[_sds(s, dt) for s, dt in zip(shapes, _PREP_DTYPES)],
        compiler_params=_params(("parallel",), VMEM_BIG),
    )(qkv, bg, alog, dtb)


def dn_prep_bwd(qkv, bg, alog, dtb, cts):
    T = qkv.shape[0]
    tt = _tile(T, 256)
    nt = tt // CHUNK
    specs, _ = _prep_specs(nt, T)

    def body(qkv_ref, bg_ref, alog_ref, dtb_ref, *refs):
        ct_refs, (dqkv_ref, dbg_ref, dalog_ref, ddtb_ref) = refs[:6], refs[6:]
        _, vjp = jax.vjp(dn_prep_math, qkv_ref[...], bg_ref[...], alog_ref[...], dtb_ref[...])
        dqkv, dbg, dalog, ddtb = vjp(tuple(r[...].astype(f32) for r in ct_refs))
        dqkv_ref[...] = dqkv
        dbg_ref[...] = dbg.astype(bf16)
        first = pl.program_id(0) == 0
        _acc_rows(dalog_ref, dalog, first)
        _acc_rows(ddtb_ref, ddtb, first)

    row = lambda w: pl.BlockSpec((tt, w), lambda i: (i, 0))
    small = pl.BlockSpec((8, HEAD), lambda i: (0, 0))
    return _pcall(
        body, name="dn_prep_bwd", grid=(T // tt,),
        in_specs=[row(qkv.shape[1]), row(HEAD), _resident((1, HEAD)), _resident((1, HEAD))] + specs,
        out_specs=[row(qkv.shape[1]), row(HEAD), small, small],
        out_shape=[_sds(qkv.shape, f32), _sds((T, HEAD), bf16), _sds((8, HEAD), f32), _sds((8, HEAD), f32)],
        compiler_params=_params(("arbitrary",), VMEM_BIG),
    )(qkv, bg, alog, dtb, *cts)


def dn_step(state, u, w, qd, kd, aqk, last):
    v_new = u - bmm(w, state)
    o = bmm(qd, state) + bmm(aqk, v_new)
    return state * last + bmm_tn(kd, v_new), o


def dn_gate(o, z, onorm):
    return o * lax.rsqrt(jnp.mean(o * o, axis=-1, keepdims=True) + EPS) * onorm * _silu(z)


DN_HEADS_PER_STEP = 2


def _recur_specs(n):
    rows = [CHUNK, CHUNK, CHUNK, CHUNK, CHUNK, 1]
    lanes = [HEAD, HEAD, HEAD, HEAD, CHUNK, HEAD]
    return [pl.BlockSpec((n, DN_HEADS_PER_STEP, r, l), lambda b, hp: (b, hp, 0, 0)) for r, l in zip(rows, lanes)]


def dn_recur_fwd(prep, z, onorm, seq):
    T, C = z.shape
    n = seq // CHUNK
    hp = DN_HEADS_PER_STEP

    def body(u_ref, w_ref, qd_ref, kd_ref, aqk_ref, last_ref, z_ref, on_ref, y_ref, o_ref):
        def step(i, state):
            new, o = dn_step(state, u_ref[i], w_ref[i], qd_ref[i], kd_ref[i], aqk_ref[i], last_ref[i])
            o_ref[i] = o
            return new

        lax.fori_loop(0, n, step, jnp.zeros((hp, HEAD, HEAD), f32))
        for j in range(hp):
            cols = slice(j * HEAD, (j + 1) * HEAD)
            y_ref[:, cols] = dn_gate(o_ref[:, j].reshape(seq, HEAD), z_ref[:, cols], on_ref[...]).astype(bf16)

    blk = pl.BlockSpec((seq, hp * HEAD), lambda b, p: (b, p))
    return _pcall(
        body, name="dn_recur_fwd", grid=(T // seq, N_HEADS // hp),
        in_specs=_recur_specs(n) + [blk, pl.BlockSpec((1, HEAD), lambda b, p: (0, 0))],
        out_specs=blk, out_shape=_sds((T, C), bf16),
        scratch_shapes=[pltpu.VMEM((n, hp, CHUNK, HEAD), f32)],
        compiler_params=_params(("parallel", "parallel"), VMEM_MID),
    )(*prep, z, onorm)


def dn_recur_bwd(prep, z, onorm, dy, seq):
    T, C = z.shape
    n = seq // CHUNK
    nchunks = T // CHUNK
    hp = DN_HEADS_PER_STEP

    def body(u_ref, w_ref, qd_ref, kd_ref, aqk_ref, last_ref, z_ref, on_ref, dy_ref,
             du_ref, dw_ref, dqd_ref, dkd_ref, daqk_ref, dlast_ref, dz_ref, don_ref, st_ref, o_ref):
        args = lambda i: tuple(r[i].astype(f32) for r in (u_ref, w_ref, qd_ref, kd_ref, aqk_ref, last_ref))

        def fstep(i, state):
            st_ref[i] = state
            new, o = dn_step(state, *args(i))
            o_ref[i] = o
            return new

        lax.fori_loop(0, n, fstep, jnp.zeros((hp, HEAD, HEAD), f32))
        first = jnp.logical_and(pl.program_id(0) == 0, pl.program_id(1) == 0)
        for j in range(hp):
            cols = slice(j * HEAD, (j + 1) * HEAD)
            _, gate_vjp = jax.vjp(dn_gate, o_ref[:, j].reshape(seq, HEAD), z_ref[:, cols], on_ref[...])
            do, dz, don = gate_vjp(dy_ref[:, cols])
            dz_ref[:, cols] = dz.astype(bf16)
            o_ref[:, j] = do.reshape(n, CHUNK, HEAD)
            _acc_rows(don_ref, don, jnp.logical_and(first, j == 0))

        def bstep(j, dstate):
            i = n - 1 - j
            _, vjp = jax.vjp(dn_step, st_ref[i], *args(i))
            ds, du, dw, dqd, dkd, daqk, dlast = vjp((dstate, o_ref[i]))
            du_ref[i], dw_ref[i], dqd_ref[i], dkd_ref[i], daqk_ref[i], dlast_ref[i] = du, dw, dqd, dkd, daqk, dlast
            return ds

        lax.fori_loop(0, n, bstep, jnp.zeros((hp, HEAD, HEAD), f32))

    blk = pl.BlockSpec((seq, hp * HEAD), lambda b, p: (b, p))
    rows = [CHUNK, CHUNK, CHUNK, CHUNK, CHUNK, 1]
    lanes = [HEAD, HEAD, HEAD, HEAD, CHUNK, HEAD]
    return _pcall(
        body, name="dn_recur_bwd", grid=(T // seq, N_HEADS // hp),
        in_specs=_recur_specs(n) + [blk, pl.BlockSpec((1, HEAD), lambda b, p: (0, 0)), blk],
        out_specs=_recur_specs(n) + [blk, pl.BlockSpec((8, HEAD), lambda b, p: (0, 0))],
        out_shape=[_sds((nchunks, N_HEADS, r, l), f32) for r, l in zip(rows, lanes)]
        + [_sds((T, C), bf16), _sds((8, HEAD), f32)],
        scratch_shapes=[pltpu.VMEM((n, hp, HEAD, HEAD), f32), pltpu.VMEM((n, hp, CHUNK, HEAD), f32)],
        compiler_params=_params(("arbitrary", "arbitrary"), VMEM_BIG),
    )(*prep, z, onorm, dy)


def sgu_math(up, vp, ng, nb, sw, sbias):
    S = up.shape[0]
    nblk = S // SGU_BLOCK
    u = jax.nn.gelu(up, approximate=True)
    v = jax.nn.gelu(vp, approximate=True)
    xc = v - jnp.mean(v, axis=-1, keepdims=True)
    vn = xc * lax.rsqrt(jnp.mean(xc * xc, axis=-1, keepdims=True) + EPS) * ng + nb
    ii = lax.broadcasted_iota(jnp.int32, (SGU_BLOCK, SGU_BLOCK), 0)
    jj = lax.broadcasted_iota(jnp.int32, (SGU_BLOCK, SGU_BLOCK), 1)
    outs = []
    for hd in range(N_HEADS):
        vh = vn[:, hd * HEAD:(hd + 1) * HEAD].reshape(nblk, SGU_BLOCK, HEAD)
        ws = jnp.where(ii >= jj, sw[hd], 0.0)
        mixed = bmm(jnp.broadcast_to(ws[None], (nblk, SGU_BLOCK, SGU_BLOCK)), vh) + sbias[hd][None]
        outs.append(mixed.reshape(S, HEAD))
    return u * jnp.concatenate(outs, axis=-1)


def sgu_fwd(up, vp, ng, nb, sw, sbias, seq):
    T, C = up.shape

    def body(up_ref, vp_ref, ng_ref, nb_ref, sw_ref, sb_ref, o_ref):
        o_ref[...] = sgu_math(up_ref[...], vp_ref[...], ng_ref[...], nb_ref[...], sw_ref[...],
                              sb_ref[...]).astype(bf16)

    blk = pl.BlockSpec((seq, C), lambda b: (b, 0))
    return _pcall(
        body, name="sgu_fwd", grid=(T // seq,),
        in_specs=[blk, blk, _resident((1, C)), _resident((1, C)), _resident(sw.shape), _resident(sbias.shape)],
        out_specs=blk, out_shape=_sds((T, C), bf16),
        compiler_params=_params(("parallel",), VMEM_BIG),
    )(up, vp, ng, nb, sw, sbias)


def sgu_bwd(up, vp, ng, nb, sw, sbias, dy, seq):
    T, C = up.shape

    def body(up_ref, vp_ref, ng_ref, nb_ref, sw_ref, sb_ref, dy_ref,
             dup_ref, dvp_ref, dng_ref, dnb_ref, dsw_ref, dsb_ref):
        _, vjp = jax.vjp(sgu_math, up_ref[...], vp_ref[...], ng_ref[...], nb_ref[...], sw_ref[...], sb_ref[...])
        dup, dvp, dng, dnb, dsw, dsb = vjp(dy_ref[...])
        dup_ref[...] = dup.astype(bf16)
        dvp_ref[...] = dvp.astype(bf16)
        first = pl.program_id(0) == 0
        _acc_rows(dng_ref, dng, first)
        _acc_rows(dnb_ref, dnb, first)

        @pl.when(first)
        def _():
            dsw_ref[...] = jnp.zeros_like(dsw_ref)
            dsb_ref[...] = jnp.zeros_like(dsb_ref)

        dsw_ref[...] += dsw
        dsb_ref[...] += dsb

    blk = pl.BlockSpec((seq, C), lambda b: (b, 0))
    small = pl.BlockSpec((8, C), lambda b: (0, 0))
    return _pcall(
        body, name="sgu_bwd", grid=(T // seq,),
        in_specs=[blk, blk, _resident((1, C)), _resident((1, C)), _resident(sw.shape), _resident(sbias.shape), blk],
        out_specs=[blk, blk, small, small, pl.BlockSpec(sw.shape, lambda b: (0, 0, 0)),
                   pl.BlockSpec(sbias.shape, lambda b: (0, 0, 0))],
        out_shape=[_sds((T, C), bf16), _sds((T, C), bf16), _sds((8, C), f32), _sds((8, C), f32),
                   _sds(sw.shape, f32), _sds(sbias.shape, f32)],
        compiler_params=_params(("arbitrary",), VMEM_BIG),
    )(up, vp, ng, nb, sw, sbias, dy)


def loss_fwd_bwd(h, gain, target):
    T, D = h.shape
    tm = _tile(T, 512)

    def body(h_ref, gain_ref, t_ref, loss_ref, dh_ref, dgain_ref):
        gain_ = gain_ref[...]
        y, xhat, rstd = _rms_fwd(h_ref[...], gain_)
        err = y - t_ref[...]
        part = 0.5 * jnp.sum(jnp.mean(err * err, axis=-1, keepdims=True), axis=0, keepdims=True)
        dh, dgain = _rms_bwd(err * (1.0 / D), xhat, rstd, gain_)
        dh_ref[...] = dh
        first = pl.program_id(0) == 0
        _acc_rows(dgain_ref, dgain, first)

        @pl.when(first)
        def _():
            loss_ref[...] = jnp.zeros_like(loss_ref)

        loss_ref[...] += jnp.broadcast_to(part, loss_ref.shape)

    row = pl.BlockSpec((tm, D), lambda i: (i, 0))
    return _pcall(
        body, name="loss_fwd_bwd", grid=(T // tm,),
        in_specs=[row, _resident((1, D)), row],
        out_specs=[pl.BlockSpec((8, 128), lambda i: (0, 0)), row, pl.BlockSpec((8, D), lambda i: (0, 0))],
        out_shape=[_sds((8, 128), f32), _sds((T, D), f32), _sds((8, D), f32)],
        compiler_params=_params(("arbitrary",), VMEM_MID),
    )(h, gain, target)


def adamw(w, g, m, v):
    R, C = w.shape
    tr = max(c for c in range(8, 513, 8) if R % c == 0)
    c1, c2 = 1.0 - ADAM_B1 ** ADAM_STEP, 1.0 - ADAM_B2 ** ADAM_STEP

    def body(w_ref, g_ref, m_ref, v_ref, d_ref, nm_ref, nv_ref):
        gg = g_ref[...]
        nm = ADAM_B1 * m_ref[...] + (1.0 - ADAM_B1) * gg
        nv = ADAM_B2 * v_ref[...] + (1.0 - ADAM_B2) * (gg * gg)
        d_ref[...] = -ADAM_LR * ((nm / c1) / (jnp.sqrt(nv / c2) + ADAM_EPS) + ADAM_WD * w_ref[...])
        nm_ref[...] = nm
        nv_ref[...] = nv

    blk = pl.BlockSpec((tr, C), lambda i: (i, 0))
    return _pcall(
        body, name="adamw", grid=(R // tr,), in_specs=[blk] * 4, out_specs=[blk] * 3,
        out_shape=[_sds((R, C), f32)] * 3, compiler_params=_params(("parallel",), VMEM_MID),
    )(w, g, m, v)


def sum8(parts):
    _, R, C = parts.shape
    tr = R
    for cand in (512, 352, 336, 320, 256, 128):
        if R % cand == 0:
            tr = cand
            break

    def body(p_ref, o_ref):
        acc = p_ref[0].astype(f32)
        for i in range(1, N_DEV):
            acc = acc + p_ref[i].astype(f32)
        o_ref[...] = acc

    return _pcall(
        body, name="sum8", grid=(R // tr,),
        in_specs=[pl.BlockSpec((N_DEV, tr, C), lambda i: (0, i, 0))],
        out_specs=pl.BlockSpec((tr, C), lambda i: (i, 0)), out_shape=_sds((R, C), f32),
        compiler_params=_params(("parallel",), VMEM_MID),
    )(parts)


_FLIPS = [(fx, fy, fc) for fx in (0, 1) for fy in (0, 1) for fc in (0, 1)][1:]
_HBM = pl.BlockSpec(memory_space=pltpu.HBM)


def _me():
    return lax.axis_index("x"), lax.axis_index("y"), lax.axis_index("c")


def _peer(flip):
    x, y, c = _me()
    fx, fy, fc = flip
    return (1 - x if fx else x, 1 - y if fy else y, 1 - c if fc else c)


def _lin(dev):
    return 4 * dev[0] + 2 * dev[1] + dev[2]


def _src_block(ref, rows, dev, whole):
    return ref if whole else ref.at[pl.ds(pl.multiple_of(dev * rows, 16), rows)]


_SEM = pl.BlockSpec(memory_space=pltpu.SEMAPHORE)
_EFFECT = pltpu.SideEffectType.DATAFLOW_SIDE_EFFECTING


def push_start(srcs, whole, after, name):
    n = len(srcs)
    rows = [s.shape[0] if whole else s.shape[0] // N_DEV for s in srcs]
    land_shapes = [(N_DEV, r, s.shape[1]) for r, s in zip(rows, srcs)]

    def body(*refs):
        src_refs, land_refs = refs[:n], refs[n:2 * n]
        send_sems, recv_sems, own_sems, token = refs[2 * n + 1], refs[2 * n + 2], refs[2 * n + 3], refs[-1]
        me = _lin(_me())
        for i in range(n):
            for k, flip in enumerate(_FLIPS):
                peer = _peer(flip)
                pltpu.make_async_remote_copy(
                    src_ref=_src_block(src_refs[i], rows[i], _lin(peer), whole), dst_ref=land_refs[i].at[me],
                    send_sem=send_sems.at[i * 7 + k], recv_sem=recv_sems.at[i * 7 + k],
                    device_id=peer, device_id_type=MESH).start()
        for i in range(n):
            pltpu.make_async_copy(_src_block(src_refs[i], rows[i], me, whole), land_refs[i].at[me],
                                  own_sems.at[i]).start()
        token[...] = jnp.zeros_like(token)

    hbm = lambda a: pltpu.with_memory_space_constraint(a, pltpu.HBM)
    outs = _pcall(
        body, name=name,
        in_specs=[_HBM] * (2 * n) + [pl.BlockSpec(memory_space=pl.ANY)],
        out_specs=[_SEM, _SEM, _SEM] + [_HBM] * (2 * n) + [pl.BlockSpec(memory_space=pltpu.VMEM)],
        out_shape=[pltpu.SemaphoreType.DMA((7 * n,)), pltpu.SemaphoreType.DMA((7 * n,)),
                   pltpu.SemaphoreType.DMA((n,))]
        + [pltpu.HBM(s.shape, s.dtype) for s in srcs]
        + [pltpu.HBM(shp, s.dtype) for shp, s in zip(land_shapes, srcs)] + [_sds((8, 128), f32)],
        input_output_aliases={i: 3 + i for i in range(2 * n)},
        compiler_params=pltpu.CompilerParams(has_side_effects=_EFFECT),
    )(*[hbm(s) for s in srcs], *[hbm(lax.empty(shp, s.dtype)) for shp, s in zip(land_shapes, srcs)], after)
    return outs[0], outs[1], outs[2], list(outs[3:3 + n]), list(outs[3 + n:3 + 2 * n]), outs[-1]


def push_wait(handle, whole, after, name):
    send_sems, recv_sems, own_sems, srcs, lands, _ = handle
    n = len(srcs)
    rows = [l.shape[1] for l in lands]

    def body(*refs):
        src_refs, land_refs = refs[:n], refs[n:2 * n]
        send_sems_, recv_sems_, own_sems_ = refs[2 * n], refs[2 * n + 1], refs[2 * n + 2]
        me = _lin(_me())
        for i in range(n):
            for k, flip in enumerate(_FLIPS):
                peer = _peer(flip)
                cp = pltpu.make_async_remote_copy(
                    src_ref=_src_block(src_refs[i], rows[i], _lin(peer), whole), dst_ref=land_refs[i].at[_lin(peer)],
                    send_sem=send_sems_.at[i * 7 + k], recv_sem=recv_sems_.at[i * 7 + k],
                    device_id=peer, device_id_type=MESH)
                cp.wait_send()
                cp.wait_recv()
            pltpu.make_async_copy(_src_block(src_refs[i], rows[i], me, whole), land_refs[i].at[me],
                                  own_sems_.at[i]).wait()

    outs = _pcall(
        body, name=name,
        in_specs=[_HBM] * (2 * n) + [_SEM, _SEM, _SEM, pl.BlockSpec(memory_space=pl.ANY)],
        out_specs=[_HBM] * (2 * n),
        out_shape=[pltpu.HBM(a.shape, a.dtype) for a in srcs + lands],
        input_output_aliases={i: i for i in range(2 * n)},
        compiler_params=pltpu.CompilerParams(has_side_effects=_EFFECT),
    )(*srcs, *lands, send_sems, recv_sems, own_sems, after)
    return list(outs[n:])


def all_gather_small(x):
    R, C = x.shape

    def body(x_ref, o_ref, send_sems, recv_sems):
        me = _lin(_me())
        o_ref[me] = x_ref[...]
        sends = []
        for k, flip in enumerate(_FLIPS):
            rc = pltpu.make_async_remote_copy(
                src_ref=x_ref, dst_ref=o_ref.at[me], send_sem=send_sems.at[k], recv_sem=recv_sems.at[k],
                device_id=_peer(flip), device_id_type=MESH)
            rc.start()
            sends.append(rc)
        for k, flip in enumerate(_FLIPS):
            pltpu.make_async_remote_copy(
                src_ref=x_ref, dst_ref=o_ref.at[_lin(_peer(flip))], send_sem=send_sems.at[k],
                recv_sem=recv_sems.at[k], device_id=_peer(flip), device_id_type=MESH).wait_recv()
        for rc in sends:
            rc.wait_send()

    vm = pl.BlockSpec(memory_space=pltpu.VMEM)
    return _pcall(
        body, name="all_gather_small", in_specs=[vm], out_specs=vm, out_shape=_sds((N_DEV, R, C), x.dtype),
        scratch_shapes=[pltpu.SemaphoreType.DMA((7,)), pltpu.SemaphoreType.DMA((7,))],
        compiler_params=_params(None, VMEM_MID),
    )(x)


def all_reduce_small(x):
    R, C = x.shape

    def body(x_ref, o_ref, buf_ref, send_sems, recv_sems):
        me = _lin(_me())
        buf_ref[me] = x_ref[...]
        sends = []
        for k, flip in enumerate(_FLIPS):
            rc = pltpu.make_async_remote_copy(
                src_ref=x_ref, dst_ref=buf_ref.at[me], send_sem=send_sems.at[k], recv_sem=recv_sems.at[k],
                device_id=_peer(flip), device_id_type=MESH)
            rc.start()
            sends.append(rc)
        for k, flip in enumerate(_FLIPS):
            pltpu.make_async_remote_copy(
                src_ref=x_ref, dst_ref=buf_ref.at[_lin(_peer(flip))], send_sem=send_sems.at[k],
                recv_sem=recv_sems.at[k], device_id=_peer(flip), device_id_type=MESH).wait_recv()
        for rc in sends:
            rc.wait_send()
        acc = buf_ref[0]
        for p in range(1, N_DEV):
            acc = acc + buf_ref[p]
        o_ref[...] = acc

    vm = pl.BlockSpec(memory_space=pltpu.VMEM)
    return _pcall(
        body, name="all_reduce_small", in_specs=[vm], out_specs=vm, out_shape=_sds((R, C), f32),
        scratch_shapes=[pltpu.VMEM((N_DEV, R, C), f32), pltpu.SemaphoreType.DMA((7,)),
                        pltpu.SemaphoreType.DMA((7,))],
        compiler_params=_params(None, VMEM_MID),
    )(x)


_PACK_ROWS = 8


def _packed_rows(shape):
    return -(-math.prod(shape) // (128 * _PACK_ROWS)) * _PACK_ROWS


def _pack(arrs):
    parts = []
    for a in arrs:
        flat = a.reshape(-1).astype(f32)
        rows = _packed_rows(a.shape)
        parts.append(jnp.pad(flat, (0, rows * 128 - flat.shape[0])).reshape(rows, 128))
    return jnp.concatenate(parts, axis=0)


def _unpack(buf, shapes):
    out, off = [], 0
    for s in shapes:
        rows = _packed_rows(s)
        out.append(buf[off:off + rows].reshape(-1)[:math.prod(s)].reshape(s))
        off += rows
    return out


def _row(v):
    return v.reshape(1, -1)


def _lane_row(vals, offset):
    return jnp.pad(vals.reshape(1, -1), ((0, 0), (offset, HEAD - offset - vals.shape[-1])))


def kernel(x, ffn1_norm, ffn1_w_gate, ffn1_w_up, ffn1_w_down, mix_norm, ffn2_norm, ffn2_w_gate, ffn2_w_up, ffn2_w_down, ab_w_in, pool_w, pool_scale, dn_conv_w, dn_a_log, dn_dt_bias, dn_out_norm, ab_w_out, cd_w_in, sgu_norm_g, sgu_norm_b, sgu_w, sgu_bias, sc_conv_w, cd_w_out, final_norm, loss_target, m_ffn1_norm, m_ffn1_w_gate, m_ffn1_w_up, m_ffn1_w_down, m_mix_norm, m_ffn2_norm, m_ffn2_w_gate, m_ffn2_w_up, m_ffn2_w_down, m_ab_w_in, m_pool_w, m_pool_scale, m_dn_conv_w, m_dn_a_log, m_dn_dt_bias, m_dn_out_norm, m_ab_w_out, m_cd_w_in, m_sgu_norm_g, m_sgu_norm_b, m_sgu_w, m_sgu_bias, m_sc_conv_w, m_cd_w_out, m_final_norm, v_ffn1_norm, v_ffn1_w_gate, v_ffn1_w_up, v_ffn1_w_down, v_mix_norm, v_ffn2_norm, v_ffn2_w_gate, v_ffn2_w_up, v_ffn2_w_down, v_ab_w_in, v_pool_w, v_pool_scale, v_dn_conv_w, v_dn_a_log, v_dn_dt_bias, v_dn_out_norm, v_ab_w_out, v_cd_w_in, v_sgu_norm_g, v_sgu_norm_b, v_sgu_w, v_sgu_bias, v_sc_conv_w, v_cd_w_out, v_final_norm):
    names = ['ffn1_norm', 'ffn1_w_gate', 'ffn1_w_up', 'ffn1_w_down', 'mix_norm', 'ffn2_norm', 'ffn2_w_gate',
             'ffn2_w_up', 'ffn2_w_down', 'ab_w_in', 'pool_w', 'pool_scale', 'dn_conv_w', 'dn_a_log', 'dn_dt_bias',
             'dn_out_norm', 'ab_w_out', 'cd_w_in', 'sgu_norm_g', 'sgu_norm_b', 'sgu_w', 'sgu_bias', 'sc_conv_w',
             'cd_w_out', 'final_norm']
    loc = locals()
    W = {n: loc[n] for n in names}
    M = {n: loc['m_' + n] for n in names}
    V = {n: loc['v_' + n] for n in names}

    B, S, D = x.shape
    T = B * S
    me = _lin(_me())

    def rows_of(w):
        return w.astype(bf16).T

    def layer_shards(layer):
        e = layer // 2
        shards = [rows_of(W['ffn1_w_gate'][layer]), rows_of(W['ffn1_w_up'][layer]), W['ffn1_w_down'][layer].astype(bf16)]
        if layer % 2 == 0:
            win = jnp.pad(rows_of(W['ab_w_in'][e]), ((0, AB_SHARD_PAD - AB_SHARD), (0, 0)))
            wout = W['ab_w_out'][e].astype(bf16)
        else:
            win = rows_of(W['cd_w_in'][e])
            wout = W['cd_w_out'][e].astype(bf16)
        shards += [win, wout]
        shards += [rows_of(W['ffn2_w_gate'][layer]), rows_of(W['ffn2_w_up'][layer]), W['ffn2_w_down'][layer].astype(bf16)]
        return shards

    def full_weight(idx, land, layer):
        w = land.reshape(-1, D)
        if idx == 3 and layer % 2 == 0:
            w = w.reshape(N_DEV, AB_SHARD_PAD, D)[:, :AB_SHARD].reshape(AB_IN, D)
            w = jnp.pad(w, ((0, AB_IN_PAD - AB_IN), (0, 0)))
        return w

    def tied(gain, token):
        return gain if token is None else gain + token[0:1, 0:1]

    def start_groups(arrays, groups, whole, after, name):
        out = []
        for gi, idx in enumerate(groups):
            suffix = "" if len(groups) == 1 else "abc"[gi]
            handle = push_start([arrays[i] for i in idx], whole, after, name=name + suffix)
            after = handle[5]
            out.append((idx, handle, suffix))
        return out

    def wait_group(inflight, gi, whole, after, name):
        idx, handle, suffix = inflight[gi]
        return dict(zip(idx, push_wait(handle, whole, after, name=name + suffix)))

    one_group = [tuple(range(8))]
    by_block = [(0, 1, 2), (3, 4), (5, 6, 7)]
    zeros_tile = jnp.zeros((8, 128), f32)
    gathered = [None] * DEPTH
    inflight = start_groups(layer_shards(0), by_block, True, zeros_tile, "gather_start_0")

    small_shards = [W['dn_conv_w'], W['sgu_norm_g'], W['sgu_norm_b'], W['sc_conv_w']]
    gs = all_gather_small(_pack(small_shards))
    per_dev = [_unpack(gs[p], [a.shape for a in small_shards]) for p in range(N_DEV)]
    dn_conv_full, sgu_g_full, sgu_b_full, sc_conv_full = [
        jnp.concatenate([per_dev[p][i] for p in range(N_DEV)], axis=-1) for i in range(4)]

    h = x.reshape(T, D)
    saved = []
    for layer in range(DEPTH):
        e = layer // 2
        mine, landed = inflight, {}
        landed.update(wait_group(mine, 0, True, h if layer else mine[-1][1][5], f"gather_wait_{layer}"))
        token = None
        if layer + 1 < DEPTH:
            inflight = start_groups(layer_shards(layer + 1), one_group, True, landed[0], f"gather_start_{layer + 1}")
            token = inflight[-1][1][5]
        wg1, wu1, wd1 = [full_weight(i, landed[i], layer) for i in (0, 1, 2)]
        sv = {'h0': h}
        h, sv['g1'], sv['u1'] = ffn_fwd(h, tied(_row(W['ffn1_norm'][layer]), token), wg1, wu1, wd1)
        sv['h1'] = h
        if len(mine) > 1:
            landed.update(wait_group(mine, 1, True, h, f"gather_wait_{layer}"))
        win, wout = [full_weight(i, landed[i], layer) for i in (3, 4)]
        if layer % 2 == 0:
            a_in, qkv_pre, z, bg = in_proj_fwd(h, _row(W['mix_norm'][layer]), win, [512, 1536, 512, 128])
            ya = pool_fwd(a_in, W['pool_w'][e], _row(W['pool_scale'][e]), S)
            qkv = seq_chan_fwd(conv_silu_math, dn_conv_full[e], [qkv_pre], S, f32, "dn_conv_fwd")
            alog, dtb = _lane_row(W['dn_a_log'][e], N_HEADS), _lane_row(W['dn_dt_bias'][e], N_HEADS)
            prep = dn_prep_fwd(qkv, bg, alog, dtb)
            yb = dn_recur_fwd(prep, z, _row(W['dn_out_norm'][e]), S)
            sv.update(a_in=a_in, qkv_pre=qkv_pre, z=z, bg=bg, qkv=qkv, alog=alog, dtb=dtb, prep=prep)
        else:
            up, vp, xd, bgate, cg = in_proj_fwd(h, _row(W['mix_norm'][layer]), win, [512] * 5)
            sbias = W['sgu_bias'][e].reshape(N_HEADS, SGU_BLOCK, 1)
            ya = sgu_fwd(up, vp, _row(sgu_g_full[e]), _row(sgu_b_full[e]), W['sgu_w'][e], sbias,
                         _tile(S, SGU_TILE))
            yb = seq_chan_fwd(gated_conv_math, sc_conv_full[e], [xd, bgate, cg], S, bf16, "sc_conv_fwd")
            sv.update(up=up, vp=vp, xd=xd, bgate=bgate, cg=cg, sbias=sbias)
        sv.update(ya=ya, yb=yb)
        h = out_proj_fwd(h, ya, yb, wout)
        sv['h2'] = h
        if len(mine) > 1:
            landed.update(wait_group(mine, 2, True, h, f"gather_wait_{layer}"))
        wg2, wu2, wd2 = [full_weight(i, landed[i], layer) for i in (5, 6, 7)]
        h, sv['g2'], sv['u2'] = ffn_fwd(h, _row(W['ffn2_norm'][layer]), wg2, wu2, wd2)
        gathered[layer] = [wg1, wu1, wd1, win, wout, wg2, wu2, wd2]
        saved.append(sv)

    loss_part, dh, dfinal = loss_fwd_bwd(h, _row(W['final_norm']), loss_target.reshape(T, D))
    loss = lax.psum(loss_part[0, 0], ("x", "y", "c"))

    G = {}
    G['final_norm'] = dfinal[0]
    for n in ('ffn1_norm', 'mix_norm', 'ffn2_norm'):
        G[n] = [None] * DEPTH
    for n in ('pool_w', 'pool_scale', 'dn_conv_w', 'dn_a_log', 'dn_dt_bias', 'dn_out_norm',
              'sgu_norm_g', 'sgu_norm_b', 'sgu_w', 'sgu_bias', 'sc_conv_w'):
        G[n] = [None] * 2
    big = [None] * DEPTH
    inflight = None
    last = []

    for layer in reversed(range(DEPTH)):
        e = layer // 2
        sv = saved[layer]
        wg1, wu1, wd1, win, wout, wg2, wu2, wd2 = gathered[layer]
        token = None if inflight is None else inflight[-1][1][5]
        dh, dgain, xn, act, dg, du, dy = ffn_bwd(sv['h2'], tied(_row(W['ffn2_norm'][layer]), token), sv['g2'],
                                                  sv['u2'], dh, wg2, wu2, wd2)
        G['ffn2_norm'][layer] = dgain[0]
        dwg2, dwu2, dwd2 = wgrad(dg, xn), wgrad(du, xn), wgrad(act, dy)
        mix_token = ffn1_token = None
        if layer == 0:
            big[1] = wait_group(inflight, 0, False, dwd2, "scatter_wait_1")
            last += start_groups({5: dwg2, 6: dwu2, 7: dwd2}, [(5, 6, 7)], False, big[1][0], "scatter_start_0a")
            mix_token = last[-1][1][5]
        dya, dyb, dwout = out_proj_bwd(dh, sv['ya'], sv['yb'], wout)
        if layer % 2 == 0:
            da, dpw, dsc = pool_bwd(sv['a_in'], W['pool_w'][e], _row(W['pool_scale'][e]), dya, S)
            G['pool_w'][e], G['pool_scale'][e] = dpw, dsc[0]
            *cts, dz, don = dn_recur_bwd(sv['prep'], sv['z'], _row(W['dn_out_norm'][e]), dyb, S)
            G['dn_out_norm'][e] = don[0]
            dqkv, dbg, dalog, ddtb = dn_prep_bwd(sv['qkv'], sv['bg'], sv['alog'], sv['dtb'], cts)
            G['dn_a_log'][e], G['dn_dt_bias'][e] = dalog[0, N_HEADS:2 * N_HEADS], ddtb[0, N_HEADS:2 * N_HEADS]
            dqkv_pre, dconv = seq_chan_bwd(conv_silu_math, dn_conv_full[e], [sv['qkv_pre']], dqkv, S, bf16,
                                           "dn_conv_bwd")
            G['dn_conv_w'][e] = dconv
            dpieces = [da, dqkv_pre, dz, dbg]
        else:
            dup, dvp, dng, dnb, dsw, dsb = sgu_bwd(sv['up'], sv['vp'], _row(sgu_g_full[e]), _row(sgu_b_full[e]),
                                                   W['sgu_w'][e], sv['sbias'], dya, _tile(S, SGU_TILE))
            G['sgu_norm_g'][e], G['sgu_norm_b'][e] = dng[0], dnb[0]
            G['sgu_w'][e], G['sgu_bias'][e] = dsw, dsb.reshape(N_HEADS, SGU_BLOCK)
            dxd, dbgate, dcg, dscw = seq_chan_bwd(gated_conv_math, sc_conv_full[e],
                                                  [sv['xd'], sv['bgate'], sv['cg']], dyb, S, bf16, "sc_conv_bwd")
            G['sc_conv_w'][e] = dscw
            dpieces = [dup, dvp, dxd, dbgate, dcg]
        dh, dgain, dwin = in_proj_bwd(sv['h1'], tied(_row(W['mix_norm'][layer]), mix_token), dpieces, dh, win)
        G['mix_norm'][layer] = dgain[0]
        if layer % 2 == 0:
            dwin = jnp.pad(dwin[:AB_IN].reshape(N_DEV, AB_SHARD, D), ((0, 0), (0, AB_SHARD_PAD - AB_SHARD), (0, 0)))
            dwin = dwin.reshape(N_DEV * AB_SHARD_PAD, D)
        if layer == 0:
            last += start_groups({3: dwin, 4: dwout}, [(3, 4)], False, mix_token, "scatter_start_0b")
            ffn1_token = last[-1][1][5]
        dh, dgain, xn, act, dg, du, dy = ffn_bwd(sv['h0'], tied(_row(W['ffn1_norm'][layer]), ffn1_token), sv['g1'],
                                                  sv['u1'], dh, wg1, wu1, wd1)
        G['ffn1_norm'][layer] = dgain[0]
        dwg1, dwu1, dwd1 = wgrad(dg, xn), wgrad(du, xn), wgrad(act, dy)
        if layer == 0:
            last += start_groups({0: dwg1, 1: dwu1, 2: dwd1}, [(0, 1, 2)], False, ffn1_token, "scatter_start_0c")
        else:
            after = zeros_tile
            if inflight is not None:
                big[layer + 1] = wait_group(inflight, 0, False, dh, f"scatter_wait_{layer + 1}")
                after = big[layer + 1][0]
            inflight = start_groups([dwg1, dwu1, dwd1, dwin, dwout, dwg2, dwu2, dwd2], one_group, False, after,
                                    f"scatter_start_{layer}")

    grad_x = dh.reshape(B, S, D)

    small_names = ['ffn1_norm', 'mix_norm', 'ffn2_norm', 'pool_w', 'pool_scale', 'dn_conv_w', 'dn_a_log',
                   'dn_dt_bias', 'dn_out_norm', 'sgu_norm_g', 'sgu_norm_b', 'sgu_w', 'sgu_bias', 'sc_conv_w',
                   'final_norm']
    small_g = [G[n] if n == 'final_norm' else jnp.stack(G[n]) for n in small_names]
    reduced_buf = all_reduce_small(tied(_pack(small_g), last[-1][1][5]))
    reduced = _unpack(reduced_buf, [a.shape for a in small_g])
    big[0] = wait_group(last, 0, False, reduced_buf, "scatter_wait_0a")
    big[0].update(wait_group(last, 1, False, big[0][5], "scatter_wait_0b"))
    grads = {}
    for n, g in zip(small_names, reduced):
        if n in ('dn_conv_w', 'sgu_norm_g', 'sgu_norm_b', 'sc_conv_w'):
            c = W[n].shape[-1]
            g = lax.dynamic_slice_in_dim(g, me * c, c, axis=g.ndim - 1)
        grads[n] = g

    def stack_layers(idx, transpose, sel):
        out = []
        for layer in sel:
            g = sum8(big[layer][idx])
            if idx == 3 and layer % 2 == 0:
                g = g[:AB_SHARD]
            out.append(g.T if transpose else g)
        return jnp.stack(out)

    all_layers, even, odd = range(DEPTH), range(0, DEPTH, 2), range(1, DEPTH, 2)
    delta, new_m, new_v = {}, {}, {}

    def update(n, idx, transpose, sel):
        grads[n] = stack_layers(idx, transpose, sel)
        shp = W[n].shape
        two = lambda a: a.reshape(-1, shp[-1])
        d_, m_, v_ = adamw(two(W[n]), two(grads[n]), two(M[n]), two(V[n]))
        delta[n], new_m[n], new_v[n] = d_.reshape(shp), m_.reshape(shp), v_.reshape(shp)

    shapes = [W[n].shape for n in small_names]
    d_, m_, v_ = adamw(_pack([W[n] for n in small_names]), _pack([grads[n] for n in small_names]),
                       _pack([M[n] for n in small_names]), _pack([V[n] for n in small_names]))
    for n, a, b_, c_ in zip(small_names, _unpack(d_, shapes), _unpack(m_, shapes), _unpack(v_, shapes)):
        delta[n], new_m[n], new_v[n] = a, b_, c_
    update('ab_w_in', 3, True, even)
    update('cd_w_in', 3, True, odd)
    update('ab_w_out', 4, False, even)
    update('cd_w_out', 4, False, odd)
    update('ffn2_w_gate', 5, True, all_layers)
    update('ffn2_w_up', 6, True, all_layers)
    update('ffn2_w_down', 7, False, all_layers)
    updated = sum(lax.slice(a, (0,) * a.ndim, (1,) * a.ndim).reshape(1) for a in new_v.values())
    big[0].update(wait_group(last, 2, False, updated, "scatter_wait_0c"))
    update('ffn1_w_gate', 0, True, all_layers)
    update('ffn1_w_up', 1, True, all_layers)
    update('ffn1_w_down', 2, False, all_layers)

    return (loss, grad_x, *[grads[n] for n in names], *[delta[n] for n in names],
            *[new_m[n] for n in names], *[new_v[n] for n in names])
```

```python
import functools
import math

import jax
import jax.numpy as jnp
from jax import lax
from jax.experimental import pallas as pl
from jax.experimental.pallas import tpu as pltpu

f32, bf16 = jnp.float32, jnp.bfloat16

D_MODEL = 1024
DEPTH = 4
CHUNK = 64
POOL_WINDOWS = (2, 4, 8, 16)
HEAD = 128
N_HEADS = 4
SGU_BLOCK = 128
SGU_TILE = 512
FFN_DIM = 2816
AB_IN = 2568
AB_IN_PAD = 2688
AB_SHARD = 321
AB_SHARD_PAD = 336
EPS = 1e-6
N_DEV = 8
MESH = pl.DeviceIdType.MESH

ADAM_LR, ADAM_B1, ADAM_B2, ADAM_EPS, ADAM_WD, ADAM_STEP = 0.001, 0.9, 0.999, 1e-08, 0.01, 10

VMEM_BIG = 56 * 1024 * 1024
VMEM_MID = 40 * 1024 * 1024


def _pcall(body, **kw):
    return pl.pallas_call(body, **kw)


def _params(sem=None, vmem=None):
    return pltpu.CompilerParams(dimension_semantics=sem, vmem_limit_bytes=vmem)


def _sds(shape, dtype):
    return jax.ShapeDtypeStruct(shape, dtype)


_NN2, _NT2, _TN2 = (((1,), (0,)), ((), ())), (((1,), (1,)), ((), ())), (((0,), (0,)), ((), ()))
_NN3, _NT3, _TN3 = (((2,), (1,)), ((0,), (0,))), (((2,), (2,)), ((0,), (0,))), (((1,), (1,)), ((0,), (0,)))


def _dg(a, b, dims, hi):
    if hi:
        return lax.dot_general(a.astype(f32), b.astype(f32), dims, preferred_element_type=f32,
                               precision=lax.Precision.HIGH)
    return lax.dot_general(a.astype(bf16), b.astype(bf16), dims, preferred_element_type=f32)


def _make_mm(nn, nt, tn, hi):
    @jax.custom_vjp
    def mm(a, b):
        return _dg(a, b, nn, hi)

    def mm_bwd(res, ct):
        a, b = res
        return _dg(ct, b, nt, hi).astype(a.dtype), _dg(a, ct, tn, hi).astype(b.dtype)

    mm.defvjp(lambda a, b: (_dg(a, b, nn, hi), (a, b)), mm_bwd)

    @jax.custom_vjp
    def mm_nt(a, b):
        return _dg(a, b, nt, hi)

    def mm_nt_bwd(res, ct):
        a, b = res
        return _dg(ct, b, nn, hi).astype(a.dtype), _dg(ct, a, tn, hi).astype(b.dtype)

    mm_nt.defvjp(lambda a, b: (_dg(a, b, nt, hi), (a, b)), mm_nt_bwd)

    @jax.custom_vjp
    def mm_tn(a, b):
        return _dg(a, b, tn, hi)

    def mm_tn_bwd(res, ct):
        a, b = res
        return _dg(b, ct, nt, hi).astype(a.dtype), _dg(a, ct, nn, hi).astype(b.dtype)

    mm_tn.defvjp(lambda a, b: (_dg(a, b, tn, hi), (a, b)), mm_tn_bwd)
    return mm, mm_nt, mm_tn


mm, mm_nt, mm_tn = _make_mm(_NN2, _NT2, _TN2, False)
bmm, bmm_nt, bmm_tn = _make_mm(_NN3, _NT3, _TN3, False)
bmm_hi, _, _ = _make_mm(_NN3, _NT3, _TN3, True)


def _shift_raw(x, k):
    n = x.shape[0]
    t = lax.broadcasted_iota(jnp.int32, x.shape, 0)
    if k > 0:
        return jnp.where(t >= k, pltpu.roll(x, k, axis=0), 0.0)
    k = -k
    return jnp.where(t < n - k, pltpu.roll(x, n - k, axis=0), 0.0)


@functools.partial(jax.custom_vjp, nondiff_argnums=(1,))
def shift(x, k):
    return _shift_raw(x, k)


shift.defvjp(lambda x, k: (_shift_raw(x, k), None), lambda k, _, ct: (_shift_raw(ct, -k),))


def _silu(x):
    return x * jax.nn.sigmoid(x)


def _softplus(x):
    return jnp.maximum(x, 0.0) + jnp.log(1.0 + jnp.exp(-jnp.abs(x)))


def _rms_fwd(h, gain):
    rstd = lax.rsqrt(jnp.mean(h * h, axis=-1, keepdims=True) + EPS)
    xhat = h * rstd
    return xhat * gain, xhat, rstd


def _rms_bwd(dxn, xhat, rstd, gain):
    dxhat = dxn * gain
    dh = rstd * (dxhat - xhat * jnp.mean(dxhat * xhat, axis=-1, keepdims=True))
    return dh, jnp.sum(dxn * xhat, axis=0, keepdims=True)


def _acc_rows(ref, val, first):
    @pl.when(first)
    def _():
        ref[...] = jnp.zeros_like(ref)
    ref[0:1, :] += val


def _tile(n, cap):
    t = min(n, cap)
    assert n % t == 0, (n, t)
    return t


def _resident(shape):
    nd = len(shape)
    return pl.BlockSpec(shape, lambda *_: (0,) * nd, pipeline_mode=pl.Buffered(1))


def _rows(w):
    return math.prod(w.shape[:-1])


def _w2d(ref):
    w = ref[...]
    return w.reshape(-1, w.shape[-1]) if w.ndim == 3 else w


def ffn_fwd(h, gain, wgt, wut, wd):
    T, D = h.shape
    F = _rows(wgt)
    tm = _tile(T, 256)

    def body(h_ref, gain_ref, wg_ref, wu_ref, wd_ref, ho_ref, g_ref, u_ref):
        hh = h_ref[...]
        xn, _, _ = _rms_fwd(hh, gain_ref[...])
        xb = xn.astype(bf16)
        g = _dg(xb, _w2d(wg_ref), _NT2, False)
        u = _dg(xb, _w2d(wu_ref), _NT2, False)
        y = _dg(_silu(g) * u, _w2d(wd_ref), _NN2, False)
        ho_ref[...] = hh + 0.5 * y
        g_ref[...] = g.astype(bf16)
        u_ref[...] = u.astype(bf16)

    row = lambda w: pl.BlockSpec((tm, w), lambda i: (i, 0))
    return _pcall(
        body, name="ffn_fwd", grid=(T // tm,),
        in_specs=[row(D), _resident((1, D)), _resident(wgt.shape), _resident(wut.shape), _resident(wd.shape)],
        out_specs=[row(D), row(F), row(F)],
        out_shape=[_sds((T, D), f32), _sds((T, F), bf16), _sds((T, F), bf16)],
        compiler_params=_params(("parallel",), VMEM_BIG),
    )(h, gain, wgt, wut, wd)


def ffn_bwd(h, gain, g, u, dout, wgt, wut, wd):
    T, D = h.shape
    F = _rows(wgt)
    tm = _tile(T, 256)

    def body(h_ref, gain_ref, g_ref, u_ref, do_ref, wg_ref, wu_ref, wd_ref,
             dh_ref, dgain_ref, xn_ref, act_ref, dg_ref, du_ref, dy_ref):
        hh, dout_ = h_ref[...], do_ref[...]
        gain_ = gain_ref[...]
        xn, xhat, rstd = _rms_fwd(hh, gain_)
        gg, uu = g_ref[...].astype(f32), u_ref[...].astype(f32)
        dy = (0.5 * dout_).astype(bf16)
        dact = _dg(dy, _w2d(wd_ref), _NT2, False)
        sg = jax.nn.sigmoid(gg)
        silu = gg * sg
        dgate = (dact * uu * (sg * (1.0 + gg * (1.0 - sg)))).astype(bf16)
        dup = (dact * silu).astype(bf16)
        dxn = _dg(dgate, _w2d(wg_ref), _NN2, False) + _dg(dup, _w2d(wu_ref), _NN2, False)
        dh, dgain = _rms_bwd(dxn, xhat, rstd, gain_)
        dh_ref[...] = dout_ + dh
        _acc_rows(dgain_ref, dgain, pl.program_id(0) == 0)
        xn_ref[...] = xn.astype(bf16)
        act_ref[...] = (silu * uu).astype(bf16)
        dg_ref[...] = dgate
        du_ref[...] = dup
        dy_ref[...] = dy

    row = lambda w: pl.BlockSpec((tm, w), lambda i: (i, 0))
    return _pcall(
        body, name="ffn_bwd", grid=(T // tm,),
        in_specs=[row(D), _resident((1, D)), row(F), row(F), row(D),
                  _resident(wgt.shape), _resident(wut.shape), _resident(wd.shape)],
        out_specs=[row(D), pl.BlockSpec((8, D), lambda i: (0, 0)), row(D), row(F), row(F), row(F), row(D)],
        out_shape=[_sds((T, D), f32), _sds((8, D), f32), _sds((T, D), bf16), _sds((T, F), bf16),
                   _sds((T, F), bf16), _sds((T, F), bf16), _sds((T, D), bf16)],
        compiler_params=_params(("arbitrary",), VMEM_BIG),
    )(h, gain, g, u, dout, wgt, wut, wd)


def _col_tile(n, cap=1408):
    best = None
    for c in range(128, cap + 1, 128):
        if n % c == 0:
            best = c
    assert best is not None, n
    return best


def wgrad(a, b, after):
    T, N = a.shape
    K = b.shape[1]
    nc, tk = _col_tile(N), _tile(T, 1024)
    nk = T // tk

    def body(a_ref, b_ref, after_ref, o_ref, acc_ref):
        k = pl.program_id(1)

        @pl.when(k == 0)
        def _():
            acc_ref[...] = jnp.zeros_like(acc_ref)

        acc_ref[...] += _dg(a_ref[...], b_ref[...], _TN2, False)

        @pl.when(k == nk - 1)
        def _():
            o_ref[...] = acc_ref[...].astype(bf16)

    return _pcall(
        body, name="wgrad", grid=(N // nc, nk),
        in_specs=[pl.BlockSpec((tk, nc), lambda j, k: (k, j)), pl.BlockSpec((tk, K), lambda j, k: (k, 0)),
                  pl.BlockSpec(memory_space=pl.ANY)],
        out_specs=pl.BlockSpec((nc, K), lambda j, k: (j, 0)),
        out_shape=_sds((N, K), bf16),
        scratch_shapes=[pltpu.VMEM((nc, K), f32)],
        compiler_params=_params(("parallel", "arbitrary"), VMEM_BIG),
    )(a, b, after)


def in_proj_fwd(h, gain, wt, widths):
    T, D = h.shape
    N = _rows(wt)
    assert sum(widths) == N
    tm = _tile(T, 512)
    offs = [sum(widths[:i]) for i in range(len(widths))]

    def body(h_ref, gain_ref, w_ref, *outs):
        xn, _, _ = _rms_fwd(h_ref[...], gain_ref[...])
        p = _dg(xn, _w2d(w_ref), _NT2, False)
        for o_ref, off, wd_ in zip(outs, offs, widths):
            o_ref[...] = p[:, off:off + wd_]

    row = lambda w: pl.BlockSpec((tm, w), lambda i: (i, 0))
    return _pcall(
        body, name="in_proj_fwd", grid=(T // tm,),
        in_specs=[row(D), _resident((1, D)), _resident(wt.shape)],
        out_specs=[row(w) for w in widths],
        out_shape=[_sds((T, w), f32) for w in widths],
        compiler_params=_params(("parallel",), VMEM_BIG),
    )(h, gain, wt)


def in_proj_bwd(h, gain, dpieces, dout, wt):
    T, D = h.shape
    N = _rows(wt)
    widths = [p.shape[1] for p in dpieces]
    assert sum(widths) == N
    tm = _tile(T, 512)
    nt = T // tm
    npc = len(dpieces)

    def body(*refs):
        h_ref, gain_ref = refs[0], refs[1]
        p_refs = refs[2:2 + npc]
        do_ref, w_ref, dh_ref, dgain_ref, dw_ref, acc_ref = refs[2 + npc:]
        i = pl.program_id(0)
        gain_ = gain_ref[...]
        xn, xhat, rstd = _rms_fwd(h_ref[...], gain_)
        dp = jnp.concatenate([r[...].astype(bf16) for r in p_refs], axis=-1)
        dxn = _dg(dp, _w2d(w_ref), _NN2, False)
        dh, dgain = _rms_bwd(dxn, xhat, rstd, gain_)
        dh_ref[...] = do_ref[...] + dh
        _acc_rows(dgain_ref, dgain, i == 0)

        @pl.when(i == 0)
        def _():
            acc_ref[...] = jnp.zeros_like(acc_ref)

        acc_ref[...] += _dg(dp, xn, _TN2, False)

        @pl.when(i == nt - 1)
        def _():
            dw_ref[...] = acc_ref[...].astype(bf16)

    row = lambda w: pl.BlockSpec((tm, w), lambda i: (i, 0))
    return _pcall(
        body, name="in_proj_bwd", grid=(nt,),
        in_specs=[row(D), _resident((1, D))] + [row(w) for w in widths] + [row(D), _resident(wt.shape)],
        out_specs=[row(D), pl.BlockSpec((8, D), lambda i: (0, 0)), pl.BlockSpec((N, D), lambda i: (0, 0))],
        out_shape=[_sds((T, D), f32), _sds((8, D), f32), _sds((N, D), bf16)],
        scratch_shapes=[pltpu.VMEM((N, D), f32)],
        compiler_params=_params(("arbitrary",), VMEM_BIG),
    )(h, gain, *dpieces, dout, wt)


def out_proj_fwd(h, ya, yb, w):
    T, D = h.shape
    half = ya.shape[1]
    tm = _tile(T, 512)

    def body(h_ref, ya_ref, yb_ref, w_ref, o_ref):
        y = jnp.concatenate([ya_ref[...], yb_ref[...]], axis=-1)
        o_ref[...] = h_ref[...] + _dg(y, _w2d(w_ref), _NN2, False)

    row = lambda w_: pl.BlockSpec((tm, w_), lambda i: (i, 0))
    return _pcall(
        body, name="out_proj_fwd", grid=(T // tm,),
        in_specs=[row(D), row(half), row(half), _resident(w.shape)],
        out_specs=row(D), out_shape=_sds((T, D), f32),
        compiler_params=_params(("parallel",), VMEM_MID),
    )(h, ya, yb, w)


def out_proj_bwd(dout, ya, yb, w):
    T, D = dout.shape
    half = ya.shape[1]
    tm = _tile(T, 512)
    nt = T // tm

    def body(do_ref, ya_ref, yb_ref, w_ref, dya_ref, dyb_ref, dw_ref, acc_ref):
        i = pl.program_id(0)
        dob = do_ref[...].astype(bf16)
        dy = _dg(dob, _w2d(w_ref), _NT2, False)
        dya_ref[...] = dy[:, :half]
        dyb_ref[...] = dy[:, half:]

        @pl.when(i == 0)
        def _():
            acc_ref[...] = jnp.zeros_like(acc_ref)

        y = jnp.concatenate([ya_ref[...], yb_ref[...]], axis=-1)
        acc_ref[...] += _dg(y, dob, _TN2, False)

        @pl.when(i == nt - 1)
        def _():
            dw_ref[...] = acc_ref[...].astype(bf16)

    row = lambda w_: pl.BlockSpec((tm, w_), lambda i: (i, 0))
    return _pcall(
        body, name="out_proj_bwd", grid=(nt,),
        in_specs=[row(D), row(half), row(half), _resident(w.shape)],
        out_specs=[row(half), row(half), pl.BlockSpec((2 * half, D), lambda i: (0, 0))],
        out_shape=[_sds((T, half), f32), _sds((T, half), f32), _sds((2 * half, D), bf16)],
        scratch_shapes=[pltpu.VMEM((2 * half, D), f32)],
        compiler_params=_params(("arbitrary",), VMEM_MID),
    )(dout, ya, yb, w)


def _conv_taps(x, w):
    K = w.shape[0]
    acc = x * w[K - 1:K, :]
    for i in range(K - 1):
        acc = acc + shift(x, K - 1 - i) * w[i:i + 1, :]
    return acc


def conv_silu_math(w, x):
    return _silu(_conv_taps(x, w))


def gated_conv_math(w, xd, bg, cg):
    return bg * _conv_taps(cg * xd, w)


def seq_chan_fwd(math, w, xs, seq, out_dtype, name):
    T, C = xs[0].shape
    K = w.shape[0]
    nb, nc = T // seq, C // HEAD

    def body(w_ref, *refs):
        o_ref = refs[-1]
        o_ref[...] = math(w_ref[...], *[r[...] for r in refs[:-1]]).astype(out_dtype)

    blk = pl.BlockSpec((seq, HEAD), lambda j, b: (b, j))
    return _pcall(
        body, name=name, grid=(nc, nb),
        in_specs=[pl.BlockSpec((K, HEAD), lambda j, b: (0, j))] + [blk] * len(xs),
        out_specs=blk, out_shape=_sds((T, C), out_dtype),
        compiler_params=_params(("parallel", "parallel"), VMEM_MID),
    )(w, *xs)


def seq_chan_bwd(math, w, xs, dy, seq, dx_dtype, name):
    T, C = xs[0].shape
    K = w.shape[0]
    nb, nc = T // seq, C // HEAD
    nx = len(xs)

    def body(w_ref, *refs):
        x_refs, dy_ref = refs[:nx], refs[nx]
        dx_refs, dw_ref = refs[nx + 1:2 * nx + 1], refs[2 * nx + 1]
        _, vjp = jax.vjp(math, w_ref[...], *[r[...] for r in x_refs])
        grads = vjp(dy_ref[...].astype(f32))

        @pl.when(pl.program_id(1) == 0)
        def _():
            dw_ref[...] = jnp.zeros_like(dw_ref)

        dw_ref[...] += grads[0]
        for r, gx in zip(dx_refs, grads[1:]):
            r[...] = gx.astype(dx_dtype)

    blk = pl.BlockSpec((seq, HEAD), lambda j, b: (b, j))
    wblk = pl.BlockSpec((K, HEAD), lambda j, b: (0, j))
    return _pcall(
        body, name=name, grid=(nc, nb),
        in_specs=[wblk] + [blk] * (nx + 1),
        out_specs=[blk] * nx + [wblk],
        out_shape=[_sds((T, C), dx_dtype)] * nx + [_sds((K, C), f32)],
        compiler_params=_params(("parallel", "arbitrary"), VMEM_MID),
    )(w, *xs, dy)


def pool_group_math(win, ag, pw, scale):
    t = lax.broadcasted_iota(jnp.int32, (ag.shape[0], 1), 0)
    s, k = ag, 1
    while k < win:
        s = s + shift(s, k)
        k *= 2
    pooled = s / jnp.minimum(t + 1, win).astype(f32) - ag
    return mm(pooled, pw) * scale


def pool_fwd(a, pw, scale, seq):
    T, C = a.shape

    def body(a_ref, pw_ref, sc_ref, o_ref):
        for gi, win in enumerate(POOL_WINDOWS):
            cols = slice(gi * HEAD, (gi + 1) * HEAD)
            o_ref[:, cols] = pool_group_math(win, a_ref[:, cols], pw_ref[gi], sc_ref[:, cols]).astype(bf16)

    blk = pl.BlockSpec((seq, C), lambda b: (b, 0))
    return _pcall(
        body, name="pool_fwd", grid=(T // seq,),
        in_specs=[blk, _resident(pw.shape), _resident((1, C))],
        out_specs=blk, out_shape=_sds((T, C), bf16),
        compiler_params=_params(("parallel",), VMEM_MID),
    )(a, pw, scale)


def pool_bwd(a, pw, scale, dy, seq):
    T, C = a.shape

    def body(a_ref, pw_ref, sc_ref, dy_ref, da_ref, dpw_ref, dsc_ref):
        first = pl.program_id(0) == 0

        @pl.when(first)
        def _():
            dpw_ref[...] = jnp.zeros_like(dpw_ref)
            dsc_ref[...] = jnp.zeros_like(dsc_ref)

        for gi, win in enumerate(POOL_WINDOWS):
            cols = slice(gi * HEAD, (gi + 1) * HEAD)
            _, vjp = jax.vjp(functools.partial(pool_group_math, win), a_ref[:, cols], pw_ref[gi], sc_ref[:, cols])
            da, dpw, dsc = vjp(dy_ref[:, cols])
            dpw_ref[gi] += dpw
            dsc_ref[0:1, cols] += dsc
            da_ref[:, cols] = da.astype(bf16)

    blk = pl.BlockSpec((seq, C), lambda b: (b, 0))
    return _pcall(
        body, name="pool_bwd", grid=(T // seq,),
        in_specs=[blk, _resident(pw.shape), _resident((1, C)), blk],
        out_specs=[blk, pl.BlockSpec(pw.shape, lambda b: (0, 0, 0)), pl.BlockSpec((8, C), lambda b: (0, 0))],
        out_shape=[_sds((T, C), bf16), _sds(pw.shape, f32), _sds((8, C), f32)],
        compiler_params=_params(("arbitrary",), VMEM_MID),
    )(a, pw, scale, dy)


def _neumann_inverse(lmat):
    n = lmat.shape[-1]
    ii = lax.broadcasted_iota(jnp.int32, (n, n), 0)
    jj = lax.broadcasted_iota(jnp.int32, (n, n), 1)
    inv = jnp.where((ii == jj)[None], 1.0, 0.0) - lmat
    pw_ = _dg(lmat, lmat, _NN3, True)
    steps = int(math.log2(n)) - 1
    for i in range(steps):
        inv = inv + _dg(inv, pw_, _NN3, True)
        if i < steps - 1:
            pw_ = _dg(pw_, pw_, _NN3, True)
    return inv


@jax.custom_vjp
def unit_lower_inverse(lmat):
    return _neumann_inverse(lmat)


def _unit_lower_inverse_fwd(lmat):
    inv = _neumann_inverse(lmat)
    return inv, inv


def _unit_lower_inverse_bwd(inv, ct):
    return (-_dg(_dg(inv, ct, _TN3, True), inv, _NT3, True),)


unit_lower_inverse.defvjp(_unit_lower_inverse_fwd, _unit_lower_inverse_bwd)


def dn_prep_math(qkv, bg, alog, dtb):
    tt = qkv.shape[0]
    nt = tt // CHUNK
    nb = nt * N_HEADS
    W = N_HEADS * HEAD
    beta_all = jax.nn.sigmoid(bg)
    g_all = -jnp.exp(alog) * _softplus(bg + dtb)

    def heads(fn):
        return jnp.stack([fn(hd).reshape(nt, CHUNK, HEAD) for hd in range(N_HEADS)], axis=1).reshape(nb, CHUNK, HEAD)

    def l2n(x):
        return x * lax.rsqrt(jnp.sum(x * x, axis=-1, keepdims=True) + EPS)

    q = heads(lambda hd: l2n(qkv[:, hd * HEAD:(hd + 1) * HEAD]) * (HEAD ** -0.5))
    k = heads(lambda hd: l2n(qkv[:, W + hd * HEAD:W + (hd + 1) * HEAD]))
    v = heads(lambda hd: qkv[:, 2 * W + hd * HEAD:2 * W + (hd + 1) * HEAD])
    beta = heads(lambda hd: jnp.broadcast_to(beta_all[:, hd:hd + 1], (tt, HEAD)))
    g = heads(lambda hd: jnp.broadcast_to(g_all[:, N_HEADS + hd:N_HEADS + hd + 1], (tt, HEAD)))

    ii = lax.broadcasted_iota(jnp.int32, (CHUNK, CHUNK), 0)
    jj = lax.broadcasted_iota(jnp.int32, (CHUNK, CHUNK), 1)
    tril, strict = (ii >= jj)[None], (ii > jj)[None]
    ones_b = jnp.ones((nb, CHUNK, CHUNK), f32)
    tril_b = jnp.where(tril, ones_b, 0.0)
    eye_b = jnp.where((ii == jj)[None], ones_b, 0.0)

    gcb = bmm_hi(tril_b, g)
    gcol = gcb[:, :, :CHUNK]
    grow = bmm_hi(ones_b, eye_b * gcol)
    gamma = jnp.where(tril, jnp.exp(jnp.where(tril, gcol - grow, 0.0)), 0.0)
    kb = k * beta
    lmat = jnp.where(strict, bmm_nt(kb, k) * gamma, 0.0)
    inv = unit_lower_inverse(lmat)
    egc = jnp.exp(gcb)
    u = bmm_hi(inv, v * beta)
    w = bmm_hi(inv, kb * egc)
    aqk = bmm_nt(q, k) * gamma
    qd = q * egc
    glast = gcb[:, CHUNK - 1:CHUNK, :]
    kd = k * jnp.exp(glast - gcb)
    last = jnp.exp(glast)
    r4 = lambda x: x.reshape((nt, N_HEADS) + x.shape[1:])
    return r4(u), r4(w), r4(qd), r4(kd), r4(aqk), r4(last)


_PREP_DTYPES = (f32, bf16, bf16, bf16, bf16, f32)


def _prep_specs(nt, T):
    nchunks = T // CHUNK
    shapes = [(HEAD,), (HEAD,), (HEAD,), (HEAD,), (CHUNK,), (HEAD,)]
    rows = [CHUNK, CHUNK, CHUNK, CHUNK, CHUNK, 1]
    specs = [pl.BlockSpec((nt, N_HEADS, r, s[0]), lambda i: (i, 0, 0, 0)) for r, s in zip(rows, shapes)]
    outs = [(nchunks, N_HEADS, r, s[0]) for r, s in zip(rows, shapes)]
    return specs, outs


def dn_prep_fwd(qkv, bg, alog, dtb):
    T = qkv.shape[0]
    tt = _tile(T, 256)
    nt = tt // CHUNK
    specs, shapes = _prep_specs(nt, T)

    def body(qkv_ref, bg_ref, alog_ref, dtb_ref, *outs):
        res = dn_prep_math(qkv_ref[...], bg_ref[...], alog_ref[...], dtb_ref[...])
        for o_ref, r, dt in zip(outs, res, _PREP_DTYPES):
            o_ref[...] = r.astype(dt)

    row = lambda w: pl.BlockSpec((tt, w), lambda i: (i, 0))
    return _pcall(
        body, name="dn_prep_fwd", grid=(T // tt,),
        in_specs=[row(qkv.shape[1]), row(HEAD), _resident((1, HEAD)), _resident((1, HEAD))],
        out_specs=specs, out_shape=[_sds(s, dt) for s, dt in zip(shapes, _PREP_DTYPES)],
        compiler_params=_params(("parallel",), VMEM_BIG),
    )(qkv, bg, alog, dtb)


def dn_prep_bwd(qkv, bg, alog, dtb, cts):
    T = qkv.shape[0]
    tt = _tile(T, 256)
    nt = tt // CHUNK
    specs, _ = _prep_specs(nt, T)

    def body(qkv_ref, bg_ref, alog_ref, dtb_ref, *refs):
        ct_refs, (dqkv_ref, dbg_ref, dalog_ref, ddtb_ref) = refs[:6], refs[6:]
        _, vjp = jax.vjp(dn_prep_math, qkv_ref[...], bg_ref[...], alog_ref[...], dtb_ref[...])
        dqkv, dbg, dalog, ddtb = vjp(tuple(r[...].astype(f32) for r in ct_refs))
        dqkv_ref[...] = dqkv
        dbg_ref[...] = dbg.astype(bf16)
        first = pl.program_id(0) == 0
        _acc_rows(dalog_ref, dalog, first)
        _acc_rows(ddtb_ref, ddtb, first)

    row = lambda w: pl.BlockSpec((tt, w), lambda i: (i, 0))
    small = pl.BlockSpec((8, HEAD), lambda i: (0, 0))
    return _pcall(
        body, name="dn_prep_bwd", grid=(T // tt,),
        in_specs=[row(qkv.shape[1]), row(HEAD), _resident((1, HEAD)), _resident((1, HEAD))] + specs,
        out_specs=[row(qkv.shape[1]), row(HEAD), small, small],
        out_shape=[_sds(qkv.shape, f32), _sds((T, HEAD), bf16), _sds((8, HEAD), f32), _sds((8, HEAD), f32)],
        compiler_params=_params(("arbitrary",), VMEM_BIG),
    )(qkv, bg, alog, dtb, *cts)


def dn_step(state, u, w, qd, kd, aqk, last):
    v_new = u - bmm(w, state)
    o = bmm(qd, state) + bmm(aqk, v_new)
    return state * last + bmm_tn(kd, v_new), o


def dn_gate(o, z, onorm):
    return o * lax.rsqrt(jnp.mean(o * o, axis=-1, keepdims=True) + EPS) * onorm * _silu(z)


DN_HEADS_PER_STEP_FWD, DN_HEADS_PER_STEP_BWD = 4, 2


def _recur_specs(n, hp):
    rows = [CHUNK, CHUNK, CHUNK, CHUNK, CHUNK, 1]
    lanes = [HEAD, HEAD, HEAD, HEAD, CHUNK, HEAD]
    return [pl.BlockSpec((n, hp, r, l), lambda b, p: (b, p, 0, 0)) for r, l in zip(rows, lanes)]


def dn_recur_fwd(prep, z, onorm, seq):
    T, C = z.shape
    n = seq // CHUNK
    hp = DN_HEADS_PER_STEP_FWD

    def body(u_ref, w_ref, qd_ref, kd_ref, aqk_ref, last_ref, z_ref, on_ref, y_ref, o_ref):
        def step(i, state):
            new, o = dn_step(state, u_ref[i], w_ref[i], qd_ref[i], kd_ref[i], aqk_ref[i], last_ref[i])
            o_ref[i] = o
            return new

        lax.fori_loop(0, n, step, jnp.zeros((hp, HEAD, HEAD), f32))
        for j in range(hp):
            cols = slice(j * HEAD, (j + 1) * HEAD)
            y_ref[:, cols] = dn_gate(o_ref[:, j].reshape(seq, HEAD), z_ref[:, cols], on_ref[...]).astype(bf16)

    blk = pl.BlockSpec((seq, hp * HEAD), lambda b, p: (b, p))
    return _pcall(
        body, name="dn_recur_fwd", grid=(T // seq, N_HEADS // hp),
        in_specs=_recur_specs(n, hp) + [blk, pl.BlockSpec((1, HEAD), lambda b, p: (0, 0))],
        out_specs=blk, out_shape=_sds((T, C), bf16),
        scratch_shapes=[pltpu.VMEM((n, hp, CHUNK, HEAD), f32)],
        compiler_params=_params(("parallel", "parallel"), VMEM_BIG),
    )(*prep, z, onorm)


def dn_recur_bwd(prep, z, onorm, dy, seq):
    T, C = z.shape
    n = seq // CHUNK
    nchunks = T // CHUNK
    hp = DN_HEADS_PER_STEP_BWD

    def body(u_ref, w_ref, qd_ref, kd_ref, aqk_ref, last_ref, z_ref, on_ref, dy_ref,
             du_ref, dw_ref, dqd_ref, dkd_ref, daqk_ref, dlast_ref, dz_ref, don_ref, st_ref, o_ref):
        args = lambda i: tuple(r[i].astype(f32) for r in (u_ref, w_ref, qd_ref, kd_ref, aqk_ref, last_ref))

        def fstep(i, state):
            st_ref[i] = state
            new, o = dn_step(state, *args(i))
            o_ref[i] = o
            return new

        lax.fori_loop(0, n, fstep, jnp.zeros((hp, HEAD, HEAD), f32))
        first = jnp.logical_and(pl.program_id(0) == 0, pl.program_id(1) == 0)
        for j in range(hp):
            cols = slice(j * HEAD, (j + 1) * HEAD)
            _, gate_vjp = jax.vjp(dn_gate, o_ref[:, j].reshape(seq, HEAD), z_ref[:, cols], on_ref[...])
            do, dz, don = gate_vjp(dy_ref[:, cols])
            dz_ref[:, cols] = dz.astype(bf16)
            o_ref[:, j] = do.reshape(n, CHUNK, HEAD)
            _acc_rows(don_ref, don, jnp.logical_and(first, j == 0))

        def bstep(j, dstate):
            i = n - 1 - j
            _, vjp = jax.vjp(dn_step, st_ref[i], *args(i))
            ds, du, dw, dqd, dkd, daqk, dlast = vjp((dstate, o_ref[i]))
            du_ref[i], dw_ref[i], dqd_ref[i], dkd_ref[i], daqk_ref[i], dlast_ref[i] = du, dw, dqd, dkd, daqk, dlast
            return ds

        lax.fori_loop(0, n, bstep, jnp.zeros((hp, HEAD, HEAD), f32))

    blk = pl.BlockSpec((seq, hp * HEAD), lambda b, p: (b, p))
    rows = [CHUNK, CHUNK, CHUNK, CHUNK, CHUNK, 1]
    lanes = [HEAD, HEAD, HEAD, HEAD, CHUNK, HEAD]
    return _pcall(
        body, name="dn_recur_bwd", grid=(T // seq, N_HEADS // hp),
        in_specs=_recur_specs(n, hp) + [blk, pl.BlockSpec((1, HEAD), lambda b, p: (0, 0)), blk],
        out_specs=_recur_specs(n, hp) + [blk, pl.BlockSpec((8, HEAD), lambda b, p: (0, 0))],
        out_shape=[_sds((nchunks, N_HEADS, r, l), f32) for r, l in zip(rows, lanes)]
        + [_sds((T, C), bf16), _sds((8, HEAD), f32)],
        scratch_shapes=[pltpu.VMEM((n, hp, HEAD, HEAD), f32), pltpu.VMEM((n, hp, CHUNK, HEAD), f32)],
        compiler_params=_params(("arbitrary", "arbitrary"), VMEM_BIG),
    )(*prep, z, onorm, dy)


def sgu_math(up, vp, ng, nb, sw, sbias):
    S = up.shape[0]
    nblk = S // SGU_BLOCK
    u = jax.nn.gelu(up, approximate=True)
    v = jax.nn.gelu(vp, approximate=True)
    xc = v - jnp.mean(v, axis=-1, keepdims=True)
    vn = xc * lax.rsqrt(jnp.mean(xc * xc, axis=-1, keepdims=True) + EPS) * ng + nb
    ii = lax.broadcasted_iota(jnp.int32, (SGU_BLOCK, SGU_BLOCK), 0)
    jj = lax.broadcasted_iota(jnp.int32, (SGU_BLOCK, SGU_BLOCK), 1)
    outs = []
    for hd in range(N_HEADS):
        vh = vn[:, hd * HEAD:(hd + 1) * HEAD].reshape(nblk, SGU_BLOCK, HEAD)
        ws = jnp.where(ii >= jj, sw[hd], 0.0)
        mixed = bmm(jnp.broadcast_to(ws[None], (nblk, SGU_BLOCK, SGU_BLOCK)), vh) + sbias[hd][None]
        outs.append(mixed.reshape(S, HEAD))
    return u * jnp.concatenate(outs, axis=-1)


def sgu_fwd(up, vp, ng, nb, sw, sbias, seq):
    T, C = up.shape

    def body(up_ref, vp_ref, ng_ref, nb_ref, sw_ref, sb_ref, o_ref):
        o_ref[...] = sgu_math(up_ref[...], vp_ref[...], ng_ref[...], nb_ref[...], sw_ref[...],
                              sb_ref[...]).astype(bf16)

    blk = pl.BlockSpec((seq, C), lambda b: (b, 0))
    return _pcall(
        body, name="sgu_fwd", grid=(T // seq,),
        in_specs=[blk, blk, _resident((1, C)), _resident((1, C)), _resident(sw.shape), _resident(sbias.shape)],
        out_specs=blk, out_shape=_sds((T, C), bf16),
        compiler_params=_params(("parallel",), VMEM_BIG),
    )(up, vp, ng, nb, sw, sbias)


def sgu_bwd(up, vp, ng, nb, sw, sbias, dy, seq):
    T, C = up.shape

    def body(up_ref, vp_ref, ng_ref, nb_ref, sw_ref, sb_ref, dy_ref,
             dup_ref, dvp_ref, dng_ref, dnb_ref, dsw_ref, dsb_ref):
        _, vjp = jax.vjp(sgu_math, up_ref[...], vp_ref[...], ng_ref[...], nb_ref[...], sw_ref[...], sb_ref[...])
        dup, dvp, dng, dnb, dsw, dsb = vjp(dy_ref[...])
        dup_ref[...] = dup.astype(bf16)
        dvp_ref[...] = dvp.astype(bf16)
        first = pl.program_id(0) == 0
        _acc_rows(dng_ref, dng, first)
        _acc_rows(dnb_ref, dnb, first)

        @pl.when(first)
        def _():
            dsw_ref[...] = jnp.zeros_like(dsw_ref)
            dsb_ref[...] = jnp.zeros_like(dsb_ref)

        dsw_ref[...] += dsw
        dsb_ref[...] += dsb

    blk = pl.BlockSpec((seq, C), lambda b: (b, 0))
    small = pl.BlockSpec((8, C), lambda b: (0, 0))
    return _pcall(
        body, name="sgu_bwd", grid=(T // seq,),
        in_specs=[blk, blk, _resident((1, C)), _resident((1, C)), _resident(sw.shape), _resident(sbias.shape), blk],
        out_specs=[blk, blk, small, small, pl.BlockSpec(sw.shape, lambda b: (0, 0, 0)),
                   pl.BlockSpec(sbias.shape, lambda b: (0, 0, 0))],
        out_shape=[_sds((T, C), bf16), _sds((T, C), bf16), _sds((8, C), f32), _sds((8, C), f32),
                   _sds(sw.shape, f32), _sds(sbias.shape, f32)],
        compiler_params=_params(("arbitrary",), VMEM_BIG),
    )(up, vp, ng, nb, sw, sbias, dy)


def loss_fwd_bwd(h, gain, target):
    T, D = h.shape
    tm = _tile(T, 512)

    def body(h_ref, gain_ref, t_ref, loss_ref, dh_ref, dgain_ref):
        gain_ = gain_ref[...]
        y, xhat, rstd = _rms_fwd(h_ref[...], gain_)
        err = y - t_ref[...]
        part = 0.5 * jnp.sum(jnp.mean(err * err, axis=-1, keepdims=True), axis=0, keepdims=True)
        dh, dgain = _rms_bwd(err * (1.0 / D), xhat, rstd, gain_)
        dh_ref[...] = dh
        first = pl.program_id(0) == 0
        _acc_rows(dgain_ref, dgain, first)

        @pl.when(first)
        def _():
            loss_ref[...] = jnp.zeros_like(loss_ref)

        loss_ref[...] += jnp.broadcast_to(part, loss_ref.shape)

    row = pl.BlockSpec((tm, D), lambda i: (i, 0))
    return _pcall(
        body, name="loss_fwd_bwd", grid=(T // tm,),
        in_specs=[row, _resident((1, D)), row],
        out_specs=[pl.BlockSpec((8, 128), lambda i: (0, 0)), row, pl.BlockSpec((8, D), lambda i: (0, 0))],
        out_shape=[_sds((8, 128), f32), _sds((T, D), f32), _sds((8, D), f32)],
        compiler_params=_params(("arbitrary",), VMEM_MID),
    )(h, gain, target)


def adamw(w, g, m, v):
    R, C = w.shape
    tr = max(c for c in range(8, 513, 8) if R % c == 0)
    c1, c2 = 1.0 - ADAM_B1 ** ADAM_STEP, 1.0 - ADAM_B2 ** ADAM_STEP

    def body(w_ref, g_ref, m_ref, v_ref, d_ref, nm_ref, nv_ref):
        gg = g_ref[...]
        nm = ADAM_B1 * m_ref[...] + (1.0 - ADAM_B1) * gg
        nv = ADAM_B2 * v_ref[...] + (1.0 - ADAM_B2) * (gg * gg)
        d_ref[...] = -ADAM_LR * ((nm / c1) / (jnp.sqrt(nv / c2) + ADAM_EPS) + ADAM_WD * w_ref[...])
        nm_ref[...] = nm
        nv_ref[...] = nv

    blk = pl.BlockSpec((tr, C), lambda i: (i, 0))
    return _pcall(
        body, name="adamw", grid=(R // tr,), in_specs=[blk] * 4, out_specs=[blk] * 3,
        out_shape=[_sds((R, C), f32)] * 3, compiler_params=_params(("parallel",), VMEM_MID),
    )(w, g, m, v)


def sum8(parts):
    _, R, C = parts.shape
    tr = R
    for cand in (512, 352, 336, 320, 256, 128):
        if R % cand == 0:
            tr = cand
            break

    def body(p_ref, o_ref):
        acc = p_ref[0].astype(f32)
        for i in range(1, N_DEV):
            acc = acc + p_ref[i].astype(f32)
        o_ref[...] = acc

    return _pcall(
        body, name="sum8", grid=(R // tr,),
        in_specs=[pl.BlockSpec((N_DEV, tr, C), lambda i: (0, i, 0))],
        out_specs=pl.BlockSpec((tr, C), lambda i: (i, 0)), out_shape=_sds((R, C), f32),
        compiler_params=_params(("parallel",), VMEM_MID),
    )(parts)


_FLIPS = [(fx, fy, fc) for fx in (0, 1) for fy in (0, 1) for fc in (0, 1)][1:]
_HBM = pl.BlockSpec(memory_space=pltpu.HBM)


def _me():
    return lax.axis_index("x"), lax.axis_index("y"), lax.axis_index("c")


def _peer(flip):
    x, y, c = _me()
    fx, fy, fc = flip
    return (1 - x if fx else x, 1 - y if fy else y, 1 - c if fc else c)


def _lin(dev):
    return 4 * dev[0] + 2 * dev[1] + dev[2]


def _src_block(ref, rows, dev, whole):
    return ref if whole else ref.at[pl.ds(pl.multiple_of(dev * rows, 16), rows)]


_SEM = pl.BlockSpec(memory_space=pltpu.SEMAPHORE)
_EFFECT = pltpu.SideEffectType.DATAFLOW_SIDE_EFFECTING


def push_start(srcs, whole, after, name):
    n = len(srcs)
    rows = [s.shape[0] if whole else s.shape[0] // N_DEV for s in srcs]
    land_shapes = [(N_DEV, r, s.shape[1]) for r, s in zip(rows, srcs)]

    def body(*refs):
        src_refs, land_refs = refs[:n], refs[n:2 * n]
        send_sems, recv_sems, own_sems, token = refs[2 * n + 1], refs[2 * n + 2], refs[2 * n + 3], refs[-1]
        me = _lin(_me())
        for i in range(n):
            for k, flip in enumerate(_FLIPS):
                peer = _peer(flip)
                pltpu.make_async_remote_copy(
                    src_ref=_src_block(src_refs[i], rows[i], _lin(peer), whole), dst_ref=land_refs[i].at[me],
                    send_sem=send_sems.at[i * 7 + k], recv_sem=recv_sems.at[i * 7 + k],
                    device_id=peer, device_id_type=MESH).start()
        for i in range(n):
            pltpu.make_async_copy(_src_block(src_refs[i], rows[i], me, whole), land_refs[i].at[me],
                                  own_sems.at[i]).start()
        token[...] = jnp.zeros_like(token)

    hbm = lambda a: pltpu.with_memory_space_constraint(a, pltpu.HBM)
    outs = _pcall(
        body, name=name,
        in_specs=[_HBM] * (2 * n) + [pl.BlockSpec(memory_space=pl.ANY)],
        out_specs=[_SEM, _SEM, _SEM] + [_HBM] * (2 * n) + [pl.BlockSpec(memory_space=pltpu.VMEM)],
        out_shape=[pltpu.SemaphoreType.DMA((7 * n,)), pltpu.SemaphoreType.DMA((7 * n,)),
                   pltpu.SemaphoreType.DMA((n,))]
        + [pltpu.HBM(s.shape, s.dtype) for s in srcs]
        + [pltpu.HBM(shp, s.dtype) for shp, s in zip(land_shapes, srcs)] + [_sds((8, 128), f32)],
        input_output_aliases={i: 3 + i for i in range(2 * n)},
        compiler_params=pltpu.CompilerParams(has_side_effects=_EFFECT),
    )(*[hbm(s) for s in srcs], *[hbm(lax.empty(shp, s.dtype)) for shp, s in zip(land_shapes, srcs)], after)
    return outs[0], outs[1], outs[2], list(outs[3:3 + n]), list(outs[3 + n:3 + 2 * n]), outs[-1]


def push_wait(handle, whole, after, name):
    send_sems, recv_sems, own_sems, srcs, lands, _ = handle
    n = len(srcs)
    rows = [l.shape[1] for l in lands]

    def body(*refs):
        src_refs, land_refs = refs[:n], refs[n:2 * n]
        send_sems_, recv_sems_, own_sems_ = refs[2 * n], refs[2 * n + 1], refs[2 * n + 2]
        me = _lin(_me())
        for i in range(n):
            for k, flip in enumerate(_FLIPS):
                peer = _peer(flip)
                cp = pltpu.make_async_remote_copy(
                    src_ref=_src_block(src_refs[i], rows[i], _lin(peer), whole), dst_ref=land_refs[i].at[_lin(peer)],
                    send_sem=send_sems_.at[i * 7 + k], recv_sem=recv_sems_.at[i * 7 + k],
                    device_id=peer, device_id_type=MESH)
                cp.wait_send()
                cp.wait_recv()
            pltpu.make_async_copy(_src_block(src_refs[i], rows[i], me, whole), land_refs[i].at[me],
                                  own_sems_.at[i]).wait()

    outs = _pcall(
        body, name=name,
        in_specs=[_HBM] * (2 * n) + [_SEM, _SEM, _SEM, pl.BlockSpec(memory_space=pl.ANY)],
        out_specs=[_HBM] * (2 * n),
        out_shape=[pltpu.HBM(a.shape, a.dtype) for a in srcs + lands],
        input_output_aliases={i: i for i in range(2 * n)},
        compiler_params=pltpu.CompilerParams(has_side_effects=_EFFECT),
    )(*srcs, *lands, send_sems, recv_sems, own_sems, after)
    return list(outs[n:])


def all_gather_small(x):
    R, C = x.shape

    def body(x_ref, o_ref, send_sems, recv_sems):
        me = _lin(_me())
        o_ref[me] = x_ref[...]
        sends = []
        for k, flip in enumerate(_FLIPS):
            rc = pltpu.make_async_remote_copy(
                src_ref=x_ref, dst_ref=o_ref.at[me], send_sem=send_sems.at[k], recv_sem=recv_sems.at[k],
                device_id=_peer(flip), device_id_type=MESH)
            rc.start()
            sends.append(rc)
        for k, flip in enumerate(_FLIPS):
            pltpu.make_async_remote_copy(
                src_ref=x_ref, dst_ref=o_ref.at[_lin(_peer(flip))], send_sem=send_sems.at[k],
                recv_sem=recv_sems.at[k], device_id=_peer(flip), device_id_type=MESH).wait_recv()
        for rc in sends:
            rc.wait_send()

    vm = pl.BlockSpec(memory_space=pltpu.VMEM)
    return _pcall(
        body, name="all_gather_small", in_specs=[vm], out_specs=vm, out_shape=_sds((N_DEV, R, C), x.dtype),
        scratch_shapes=[pltpu.SemaphoreType.DMA((7,)), pltpu.SemaphoreType.DMA((7,))],
        compiler_params=_params(None, VMEM_MID),
    )(x)


def sum_slots(parts):
    _, R, C = parts.shape

    def body(p_ref, o_ref):
        acc = p_ref[0]
        for p in range(1, N_DEV):
            acc = acc + p_ref[p]
        o_ref[...] = acc

    vm = pl.BlockSpec(memory_space=pltpu.VMEM)
    return _pcall(body, name="sum_slots", in_specs=[vm], out_specs=vm, out_shape=_sds((R, C), f32),
                  compiler_params=_params(None, VMEM_MID))(parts)


_PACK_ROWS = 8


def _packed_rows(shape):
    return -(-math.prod(shape) // (128 * _PACK_ROWS)) * _PACK_ROWS


def _pack(arrs):
    parts = []
    for a in arrs:
        flat = a.reshape(-1).astype(f32)
        rows = _packed_rows(a.shape)
        parts.append(jnp.pad(flat, (0, rows * 128 - flat.shape[0])).reshape(rows, 128))
    return jnp.concatenate(parts, axis=0)


def _unpack(buf, shapes):
    out, off = [], 0
    for s in shapes:
        rows = _packed_rows(s)
        out.append(buf[off:off + rows].reshape(-1)[:math.prod(s)].reshape(s))
        off += rows
    return out


def _row(v):
    return v.reshape(1, -1)


def _lane_row(vals, offset):
    return jnp.pad(vals.reshape(1, -1), ((0, 0), (offset, HEAD - offset - vals.shape[-1])))


def kernel(x, ffn1_norm, ffn1_w_gate, ffn1_w_up, ffn1_w_down, mix_norm, ffn2_norm, ffn2_w_gate, ffn2_w_up, ffn2_w_down, ab_w_in, pool_w, pool_scale, dn_conv_w, dn_a_log, dn_dt_bias, dn_out_norm, ab_w_out, cd_w_in, sgu_norm_g, sgu_norm_b, sgu_w, sgu_bias, sc_conv_w, cd_w_out, final_norm, loss_target, m_ffn1_norm, m_ffn1_w_gate, m_ffn1_w_up, m_ffn1_w_down, m_mix_norm, m_ffn2_norm, m_ffn2_w_gate, m_ffn2_w_up, m_ffn2_w_down, m_ab_w_in, m_pool_w, m_pool_scale, m_dn_conv_w, m_dn_a_log, m_dn_dt_bias, m_dn_out_norm, m_ab_w_out, m_cd_w_in, m_sgu_norm_g, m_sgu_norm_b, m_sgu_w, m_sgu_bias, m_sc_conv_w, m_cd_w_out, m_final_norm, v_ffn1_norm, v_ffn1_w_gate, v_ffn1_w_up, v_ffn1_w_down, v_mix_norm, v_ffn2_norm, v_ffn2_w_gate, v_ffn2_w_up, v_ffn2_w_down, v_ab_w_in, v_pool_w, v_pool_scale, v_dn_conv_w, v_dn_a_log, v_dn_dt_bias, v_dn_out_norm, v_ab_w_out, v_cd_w_in, v_sgu_norm_g, v_sgu_norm_b, v_sgu_w, v_sgu_bias, v_sc_conv_w, v_cd_w_out, v_final_norm):
    names = ['ffn1_norm', 'ffn1_w_gate', 'ffn1_w_up', 'ffn1_w_down', 'mix_norm', 'ffn2_norm', 'ffn2_w_gate',
             'ffn2_w_up', 'ffn2_w_down', 'ab_w_in', 'pool_w', 'pool_scale', 'dn_conv_w', 'dn_a_log', 'dn_dt_bias',
             'dn_out_norm', 'ab_w_out', 'cd_w_in', 'sgu_norm_g', 'sgu_norm_b', 'sgu_w', 'sgu_bias', 'sc_conv_w',
             'cd_w_out', 'final_norm']
    loc = locals()
    W = {n: loc[n] for n in names}
    M = {n: loc['m_' + n] for n in names}
    V = {n: loc['v_' + n] for n in names}

    B, S, D = x.shape
    T = B * S
    me = _lin(_me())

    def rows_of(w):
        return w.astype(bf16).T

    def layer_shards(layer):
        e = layer // 2
        shards = [rows_of(W['ffn1_w_gate'][layer]), rows_of(W['ffn1_w_up'][layer]), W['ffn1_w_down'][layer].astype(bf16)]
        if layer % 2 == 0:
            win = jnp.pad(rows_of(W['ab_w_in'][e]), ((0, AB_SHARD_PAD - AB_SHARD), (0, 0)))
            wout = W['ab_w_out'][e].astype(bf16)
        else:
            win = rows_of(W['cd_w_in'][e])
            wout = W['cd_w_out'][e].astype(bf16)
        shards += [win, wout]
        shards += [rows_of(W['ffn2_w_gate'][layer]), rows_of(W['ffn2_w_up'][layer]), W['ffn2_w_down'][layer].astype(bf16)]
        return shards

    def full_weight(idx, land, layer):
        if idx == 3 and layer % 2 == 0:
            return jnp.pad(land[:, :AB_SHARD].reshape(AB_IN, D), ((0, AB_IN_PAD - AB_IN), (0, 0)))
        return land

    def tied(gain, token):
        return gain if token is None else gain + token[0:1, 0:1]

    def start_groups(arrays, groups, whole, after, name):
        out = []
        for gi, idx in enumerate(groups):
            suffix = "" if len(groups) == 1 else "abc"[gi]
            handle = push_start([arrays[i] for i in idx], whole, after, name=name + suffix)
            after = handle[5]
            out.append((idx, handle, suffix))
        return out

    def wait_group(inflight, gi, whole, after, name):
        idx, handle, suffix = inflight[gi]
        return dict(zip(idx, push_wait(handle, whole, after, name=name + suffix)))

    one_group = [tuple(range(8))]
    by_block = [(0, 1, 2), (3, 4), (5, 6, 7)]
    zeros_tile = jnp.zeros((8, 128), f32)
    gathered = [None] * DEPTH
    inflight = start_groups(layer_shards(0), by_block, True, zeros_tile, "gather_start_0")

    small_shards = [W['dn_conv_w'], W['sgu_norm_g'], W['sgu_norm_b'], W['sc_conv_w']]
    gs = all_gather_small(_pack(small_shards))
    per_dev = [_unpack(gs[p], [a.shape for a in small_shards]) for p in range(N_DEV)]
    dn_conv_full, sgu_g_full, sgu_b_full, sc_conv_full = [
        jnp.concatenate([per_dev[p][i] for p in range(N_DEV)], axis=-1) for i in range(4)]

    h = x.reshape(T, D)
    saved = []
    for layer in range(DEPTH):
        e = layer // 2
        mine, landed = inflight, {}
        landed.update(wait_group(mine, 0, True, h if layer else mine[-1][1][5], f"gather_wait_{layer}"))
        token = None
        if layer + 1 < DEPTH:
            inflight = start_groups(layer_shards(layer + 1), one_group, True, landed[0], f"gather_start_{layer + 1}")
            token = inflight[-1][1][5]
        wg1, wu1, wd1 = [full_weight(i, landed[i], layer) for i in (0, 1, 2)]
        sv = {'h0': h}
        h, sv['g1'], sv['u1'] = ffn_fwd(h, tied(_row(W['ffn1_norm'][layer]), token), wg1, wu1, wd1)
        sv['h1'] = h
        if len(mine) > 1:
            landed.update(wait_group(mine, 1, True, h, f"gather_wait_{layer}"))
        win, wout = [full_weight(i, landed[i], layer) for i in (3, 4)]
        if layer % 2 == 0:
            a_in, qkv_pre, z, bg = in_proj_fwd(h, _row(W['mix_norm'][layer]), win, [512, 1536, 512, 128])
            ya = pool_fwd(a_in, W['pool_w'][e], _row(W['pool_scale'][e]), S)
            qkv = seq_chan_fwd(conv_silu_math, dn_conv_full[e], [qkv_pre], S, f32, "dn_conv_fwd")
            alog, dtb = _lane_row(W['dn_a_log'][e], N_HEADS), _lane_row(W['dn_dt_bias'][e], N_HEADS)
            prep = dn_prep_fwd(qkv, bg, alog, dtb)
            yb = dn_recur_fwd(prep, z, _row(W['dn_out_norm'][e]), S)
            sv.update(a_in=a_in, qkv_pre=qkv_pre, z=z, bg=bg, qkv=qkv, alog=alog, dtb=dtb, prep=prep)
        else:
            up, vp, xd, bgate, cg = in_proj_fwd(h, _row(W['mix_norm'][layer]), win, [512] * 5)
            sbias = W['sgu_bias'][e].reshape(N_HEADS, SGU_BLOCK, 1)
            ya = sgu_fwd(up, vp, _row(sgu_g_full[e]), _row(sgu_b_full[e]), W['sgu_w'][e], sbias,
                         _tile(S, SGU_TILE))
            yb = seq_chan_fwd(gated_conv_math, sc_conv_full[e], [xd, bgate, cg], S, bf16, "sc_conv_fwd")
            sv.update(up=up, vp=vp, xd=xd, bgate=bgate, cg=cg, sbias=sbias)
        sv.update(ya=ya, yb=yb)
        h = out_proj_fwd(h, ya, yb, wout)
        sv['h2'] = h
        if len(mine) > 1:
            landed.update(wait_group(mine, 2, True, h, f"gather_wait_{layer}"))
        wg2, wu2, wd2 = [full_weight(i, landed[i], layer) for i in (5, 6, 7)]
        h, sv['g2'], sv['u2'] = ffn_fwd(h, _row(W['ffn2_norm'][layer]), wg2, wu2, wd2)
        gathered[layer] = [wg1, wu1, wd1, win, wout, wg2, wu2, wd2]
        saved.append(sv)

    loss_part, dh, dfinal = loss_fwd_bwd(h, _row(W['final_norm']), loss_target.reshape(T, D))
    loss = lax.psum(loss_part[0, 0], ("x", "y", "c"))

    G = {}
    G['final_norm'] = dfinal[0]
    for n in ('ffn1_norm', 'mix_norm', 'ffn2_norm'):
        G[n] = [None] * DEPTH
    for n in ('pool_w', 'pool_scale', 'dn_conv_w', 'dn_a_log', 'dn_dt_bias', 'dn_out_norm',
              'sgu_norm_g', 'sgu_norm_b', 'sgu_w', 'sgu_bias', 'sc_conv_w'):
        G[n] = [None] * 2
    small_names = ['ffn1_norm', 'mix_norm', 'ffn2_norm', 'pool_w', 'pool_scale', 'dn_conv_w', 'dn_a_log',
                   'dn_dt_bias', 'dn_out_norm', 'sgu_norm_g', 'sgu_norm_b', 'sgu_w', 'sgu_bias', 'sc_conv_w',
                   'final_norm']
    big = [None] * DEPTH
    inflight = None
    last = []

    for layer in reversed(range(DEPTH)):
        e = layer // 2
        sv = saved[layer]
        wg1, wu1, wd1, win, wout, wg2, wu2, wd2 = gathered[layer]
        token = None if inflight is None else inflight[-1][1][5]
        dh, dgain, xn, act, dg, du, dy = ffn_bwd(sv['h2'], tied(_row(W['ffn2_norm'][layer]), token), sv['g2'],
                                                  sv['u2'], dh, wg2, wu2, wd2)
        G['ffn2_norm'][layer] = dgain[0]
        dwg2, dwu2, dwd2 = wgrad(dg, xn, zeros_tile), wgrad(du, xn, zeros_tile), wgrad(act, dy, zeros_tile)
        mix_token = ffn1_token = None
        if layer == 0:
            big[1] = wait_group(inflight, 0, False, dwd2, "scatter_wait_1")
            last += start_groups({5: dwg2, 6: dwu2, 7: dwd2}, [(5, 6, 7)], False, big[1][0], "scatter_start_0a")
            mix_token = last[-1][1][5]
        dya, dyb, dwout = out_proj_bwd(dh, sv['ya'], sv['yb'], wout)
        if layer % 2 == 0:
            da, dpw, dsc = pool_bwd(sv['a_in'], W['pool_w'][e], _row(W['pool_scale'][e]), dya, S)
            G['pool_w'][e], G['pool_scale'][e] = dpw, dsc[0]
            *cts, dz, don = dn_recur_bwd(sv['prep'], sv['z'], _row(W['dn_out_norm'][e]), dyb, S)
            G['dn_out_norm'][e] = don[0]
            dqkv, dbg, dalog, ddtb = dn_prep_bwd(sv['qkv'], sv['bg'], sv['alog'], sv['dtb'], cts)
            G['dn_a_log'][e], G['dn_dt_bias'][e] = dalog[0, N_HEADS:2 * N_HEADS], ddtb[0, N_HEADS:2 * N_HEADS]
            dqkv_pre, dconv = seq_chan_bwd(conv_silu_math, dn_conv_full[e], [sv['qkv_pre']], dqkv, S, bf16,
                                           "dn_conv_bwd")
            G['dn_conv_w'][e] = dconv
            dpieces = [da, dqkv_pre, dz, dbg]
        else:
            dup, dvp, dng, dnb, dsw, dsb = sgu_bwd(sv['up'], sv['vp'], _row(sgu_g_full[e]), _row(sgu_b_full[e]),
                                                   W['sgu_w'][e], sv['sbias'], dya, _tile(S, SGU_TILE))
            G['sgu_norm_g'][e], G['sgu_norm_b'][e] = dng[0], dnb[0]
            G['sgu_w'][e], G['sgu_bias'][e] = dsw, dsb.reshape(N_HEADS, SGU_BLOCK)
            dxd, dbgate, dcg, dscw = seq_chan_bwd(gated_conv_math, sc_conv_full[e],
                                                  [sv['xd'], sv['bgate'], sv['cg']], dyb, S, bf16, "sc_conv_bwd")
            G['sc_conv_w'][e] = dscw
            dpieces = [dup, dvp, dxd, dbgate, dcg]
        dh, dgain, dwin = in_proj_bwd(sv['h1'], tied(_row(W['mix_norm'][layer]), mix_token), dpieces, dh, win)
        G['mix_norm'][layer] = dgain[0]
        if layer % 2 == 0:
            dwin = jnp.pad(dwin[:AB_IN].reshape(N_DEV, AB_SHARD, D), ((0, 0), (0, AB_SHARD_PAD - AB_SHARD), (0, 0)))
            dwin = dwin.reshape(N_DEV * AB_SHARD_PAD, D)
        if layer == 0:
            last += start_groups({3: dwin, 4: dwout}, [(3, 4)], False, mix_token, "scatter_start_0b")
            ffn1_token = last[-1][1][5]
        dh, dgain, xn, act, dg, du, dy = ffn_bwd(sv['h0'], tied(_row(W['ffn1_norm'][layer]), ffn1_token), sv['g1'],
                                                  sv['u1'], dh, wg1, wu1, wd1)
        G['ffn1_norm'][layer] = dgain[0]
        order = zeros_tile
        if layer == 0:
            small_g = [G[n] if n == 'final_norm' else jnp.stack(G[n]) for n in small_names]
            small_x = start_groups([_pack(small_g)], [(0,)], True, ffn1_token, "small_grads_start")
            order = small_x[-1][1][5]
        dwg1, dwu1, dwd1 = wgrad(dg, xn, order), wgrad(du, xn, order), wgrad(act, dy, order)
        if layer == 0:
            last += start_groups({0: dwg1, 1: dwu1, 2: dwd1}, [(0, 1, 2)], False, order, "scatter_start_0c")
        else:
            after = zeros_tile
            if inflight is not None:
                big[layer + 1] = wait_group(inflight, 0, False, dh, f"scatter_wait_{layer + 1}")
                after = big[layer + 1][0]
            inflight = start_groups([dwg1, dwu1, dwd1, dwin, dwout, dwg2, dwu2, dwd2], one_group, False, after,
                                    f"scatter_start_{layer}")

    grad_x = dh.reshape(B, S, D)

    reduced_buf = sum_slots(wait_group(small_x, 0, True, last[-1][1][5], "small_grads_wait")[0])
    reduced = _unpack(reduced_buf, [a.shape for a in small_g])
    big[0] = wait_group(last, 0, False, reduced_buf, "scatter_wait_0a")
    big[0].update(wait_group(last, 1, False, big[0][5], "scatter_wait_0b"))
    grads = {}
    for n, g in zip(small_names, reduced):
        if n in ('dn_conv_w', 'sgu_norm_g', 'sgu_norm_b', 'sc_conv_w'):
            c = W[n].shape[-1]
            g = lax.dynamic_slice_in_dim(g, me * c, c, axis=g.ndim - 1)
        grads[n] = g

    def stack_layers(idx, transpose, sel):
        out = []
        for layer in sel:
            g = sum8(big[layer][idx])
            if idx == 3 and layer % 2 == 0:
                g = g[:AB_SHARD]
            out.append(g.T if transpose else g)
        return jnp.stack(out)

    all_layers, even, odd = range(DEPTH), range(0, DEPTH, 2), range(1, DEPTH, 2)
    delta, new_m, new_v = {}, {}, {}

    def update(n, idx, transpose, sel):
        grads[n] = stack_layers(idx, transpose, sel)
        shp = W[n].shape
        two = lambda a: a.reshape(-1, shp[-1])
        d_, m_, v_ = adamw(two(W[n]), two(grads[n]), two(M[n]), two(V[n]))
        delta[n], new_m[n], new_v[n] = d_.reshape(shp), m_.reshape(shp), v_.reshape(shp)

    shapes = [W[n].shape for n in small_names]
    d_, m_, v_ = adamw(_pack([W[n] for n in small_names]), _pack([grads[n] for n in small_names]),
                       _pack([M[n] for n in small_names]), _pack([V[n] for n in small_names]))
    for n, a, b_, c_ in zip(small_names, _unpack(d_, shapes), _unpack(m_, shapes), _unpack(v_, shapes)):
        delta[n], new_m[n], new_v[n] = a, b_, c_
    update('ab_w_in', 3, True, even)
    update('cd_w_in', 3, True, odd)
    update('ab_w_out', 4, False, even)
    update('cd_w_out', 4, False, odd)
    update('ffn2_w_gate', 5, True, all_layers)
    update('ffn2_w_up', 6, True, all_layers)
    update('ffn2_w_down', 7, False, all_layers)
    updated = sum(lax.slice(a, (0,) * a.ndim, (1,) * a.ndim).reshape(1) for a in new_v.values())
    big[0].update(wait_group(last, 2, False, updated, "scatter_wait_0c"))
    update('ffn1_w_gate', 0, True, all_layers)
    update('ffn1_w_up', 1, True, all_layers)
    update('ffn1_w_down', 2, False, all_layers)

    return (loss, grad_x, *[grads[n] for n in names], *[delta[n] for n in names],
            *[new_m[n] for n in names], *[new_v[n] for n in names])
```

```python
import functools
import math

import jax
import jax.numpy as jnp
from jax import lax
from jax.experimental import pallas as pl
from jax.experimental.pallas import tpu as pltpu

f32, bf16 = jnp.float32, jnp.bfloat16

D_MODEL = 1024
DEPTH = 4
CHUNK = 64
POOL_WINDOWS = (2, 4, 8, 16)
HEAD = 128
N_HEADS = 4
SGU_BLOCK = 128
SGU_TILE = 512
FFN_DIM = 2816
AB_IN = 2568
AB_IN_PAD = 2688
AB_SHARD = 321
AB_SHARD_PAD = 336
EPS = 1e-6
N_DEV = 8
MESH = pl.DeviceIdType.MESH

ADAM_LR, ADAM_B1, ADAM_B2, ADAM_EPS, ADAM_WD, ADAM_STEP = 0.001, 0.9, 0.999, 1e-08, 0.01, 10

VMEM_BIG = 56 * 1024 * 1024
VMEM_MID = 40 * 1024 * 1024


def _pcall(body, **kw):
    return pl.pallas_call(body, **kw)


def _params(sem=None, vmem=None):
    return pltpu.CompilerParams(dimension_semantics=sem, vmem_limit_bytes=vmem)


def _sds(shape, dtype):
    return jax.ShapeDtypeStruct(shape, dtype)


_NN2, _NT2, _TN2 = (((1,), (0,)), ((), ())), (((1,), (1,)), ((), ())), (((0,), (0,)), ((), ()))
_NN3, _NT3, _TN3 = (((2,), (1,)), ((0,), (0,))), (((2,), (2,)), ((0,), (0,))), (((1,), (1,)), ((0,), (0,)))


def _dg(a, b, dims, hi):
    if hi:
        return lax.dot_general(a.astype(f32), b.astype(f32), dims, preferred_element_type=f32,
                               precision=lax.Precision.HIGH)
    return lax.dot_general(a.astype(bf16), b.astype(bf16), dims, preferred_element_type=f32)


def _make_mm(nn, nt, tn, hi):
    @jax.custom_vjp
    def mm(a, b):
        return _dg(a, b, nn, hi)

    def mm_bwd(res, ct):
        a, b = res
        return _dg(ct, b, nt, hi).astype(a.dtype), _dg(a, ct, tn, hi).astype(b.dtype)

    mm.defvjp(lambda a, b: (_dg(a, b, nn, hi), (a, b)), mm_bwd)

    @jax.custom_vjp
    def mm_nt(a, b):
        return _dg(a, b, nt, hi)

    def mm_nt_bwd(res, ct):
        a, b = res
        return _dg(ct, b, nn, hi).astype(a.dtype), _dg(ct, a, tn, hi).astype(b.dtype)

    mm_nt.defvjp(lambda a, b: (_dg(a, b, nt, hi), (a, b)), mm_nt_bwd)

    @jax.custom_vjp
    def mm_tn(a, b):
        return _dg(a, b, tn, hi)

    def mm_tn_bwd(res, ct):
        a, b = res
        return _dg(b, ct, nt, hi).astype(a.dtype), _dg(a, ct, nn, hi).astype(b.dtype)

    mm_tn.defvjp(lambda a, b: (_dg(a, b, tn, hi), (a, b)), mm_tn_bwd)
    return mm, mm_nt, mm_tn


mm, mm_nt, mm_tn = _make_mm(_NN2, _NT2, _TN2, False)
bmm, bmm_nt, bmm_tn = _make_mm(_NN3, _NT3, _TN3, False)
bmm_hi, _, _ = _make_mm(_NN3, _NT3, _TN3, True)


def _shift_raw(x, k):
    n = x.shape[0]
    t = lax.broadcasted_iota(jnp.int32, x.shape, 0)
    if k > 0:
        return jnp.where(t >= k, pltpu.roll(x, k, axis=0), 0.0)
    k = -k
    return jnp.where(t < n - k, pltpu.roll(x, n - k, axis=0), 0.0)


@functools.partial(jax.custom_vjp, nondiff_argnums=(1,))
def shift(x, k):
    return _shift_raw(x, k)


shift.defvjp(lambda x, k: (_shift_raw(x, k), None), lambda k, _, ct: (_shift_raw(ct, -k),))


def _silu(x):
    return x * jax.nn.sigmoid(x)


def _softplus(x):
    return jnp.maximum(x, 0.0) + jnp.log(1.0 + jnp.exp(-jnp.abs(x)))


def _rms_fwd(h, gain):
    rstd = lax.rsqrt(jnp.mean(h * h, axis=-1, keepdims=True) + EPS)
    xhat = h * rstd
    return xhat * gain, xhat, rstd


def _rms_bwd(dxn, xhat, rstd, gain):
    dxhat = dxn * gain
    dh = rstd * (dxhat - xhat * jnp.mean(dxhat * xhat, axis=-1, keepdims=True))
    return dh, jnp.sum(dxn * xhat, axis=0, keepdims=True)


def _acc_rows(ref, val, first):
    @pl.when(first)
    def _():
        ref[...] = jnp.zeros_like(ref)
    ref[0:1, :] += val


def _tile(n, cap):
    t = min(n, cap)
    assert n % t == 0, (n, t)
    return t


def _resident(shape):
    nd = len(shape)
    return pl.BlockSpec(shape, lambda *_: (0,) * nd, pipeline_mode=pl.Buffered(1))


def _rows(w):
    return math.prod(w.shape[:-1])


def _w2d(ref):
    w = ref[...]
    return w.reshape(-1, w.shape[-1]) if w.ndim == 3 else w


def ffn_fwd(h, gain, wgt, wut, wd):
    T, D = h.shape
    F = _rows(wgt)
    tm = _tile(T, 256)

    def body(h_ref, gain_ref, wg_ref, wu_ref, wd_ref, ho_ref, g_ref, u_ref):
        hh = h_ref[...]
        xn, _, _ = _rms_fwd(hh, gain_ref[...])
        xb = xn.astype(bf16)
        g = _dg(xb, _w2d(wg_ref), _NT2, False)
        u = _dg(xb, _w2d(wu_ref), _NT2, False)
        y = _dg(_silu(g) * u, _w2d(wd_ref), _NN2, False)
        ho_ref[...] = hh + 0.5 * y
        g_ref[...] = g.astype(bf16)
        u_ref[...] = u.astype(bf16)

    row = lambda w: pl.BlockSpec((tm, w), lambda i: (i, 0))
    return _pcall(
        body, name="ffn_fwd", grid=(T // tm,),
        in_specs=[row(D), _resident((1, D)), _resident(wgt.shape), _resident(wut.shape), _resident(wd.shape)],
        out_specs=[row(D), row(F), row(F)],
        out_shape=[_sds((T, D), f32), _sds((T, F), bf16), _sds((T, F), bf16)],
        compiler_params=_params(("parallel",), VMEM_BIG),
    )(h, gain, wgt, wut, wd)


def ffn_bwd(h, gain, g, u, dout, wgt, wut, wd):
    T, D = h.shape
    F = _rows(wgt)
    tm = _tile(T, 256)

    def body(h_ref, gain_ref, g_ref, u_ref, do_ref, wg_ref, wu_ref, wd_ref,
             dh_ref, dgain_ref, xn_ref, act_ref, dg_ref, du_ref, dy_ref):
        hh, dout_ = h_ref[...], do_ref[...]
        gain_ = gain_ref[...]
        xn, xhat, rstd = _rms_fwd(hh, gain_)
        gg, uu = g_ref[...].astype(f32), u_ref[...].astype(f32)
        dy = (0.5 * dout_).astype(bf16)
        dact = _dg(dy, _w2d(wd_ref), _NT2, False)
        sg = jax.nn.sigmoid(gg)
        silu = gg * sg
        dgate = (dact * uu * (sg * (1.0 + gg * (1.0 - sg)))).astype(bf16)
        dup = (dact * silu).astype(bf16)
        dxn = _dg(dgate, _w2d(wg_ref), _NN2, False) + _dg(dup, _w2d(wu_ref), _NN2, False)
        dh, dgain = _rms_bwd(dxn, xhat, rstd, gain_)
        dh_ref[...] = dout_ + dh
        _acc_rows(dgain_ref, dgain, pl.program_id(0) == 0)
        xn_ref[...] = xn.astype(bf16)
        act_ref[...] = (silu * uu).astype(bf16)
        dg_ref[...] = dgate
        du_ref[...] = dup
        dy_ref[...] = dy

    row = lambda w: pl.BlockSpec((tm, w), lambda i: (i, 0))
    return _pcall(
        body, name="ffn_bwd", grid=(T // tm,),
        in_specs=[row(D), _resident((1, D)), row(F), row(F), row(D),
                  _resident(wgt.shape), _resident(wut.shape), _resident(wd.shape)],
        out_specs=[row(D), pl.BlockSpec((8, D), lambda i: (0, 0)), row(D), row(F), row(F), row(F), row(D)],
        out_shape=[_sds((T, D), f32), _sds((8, D), f32), _sds((T, D), bf16), _sds((T, F), bf16),
                   _sds((T, F), bf16), _sds((T, F), bf16), _sds((T, D), bf16)],
        compiler_params=_params(("arbitrary",), VMEM_BIG),
    )(h, gain, g, u, dout, wgt, wut, wd)


def _col_tile(n, cap=1408):
    best = None
    for c in range(128, cap + 1, 128):
        if n % c == 0:
            best = c
    assert best is not None, n
    return best


def wgrad(a, b, after):
    T, N = a.shape
    K = b.shape[1]
    nc, tk = _col_tile(N), _tile(T, 1024)
    nk = T // tk

    def body(a_ref, b_ref, after_ref, o_ref, acc_ref):
        k = pl.program_id(1)

        @pl.when(k == 0)
        def _():
            acc_ref[...] = jnp.zeros_like(acc_ref)

        acc_ref[...] += _dg(a_ref[...], b_ref[...], _TN2, False)

        @pl.when(k == nk - 1)
        def _():
            o_ref[...] = acc_ref[...].astype(bf16)

    return _pcall(
        body, name="wgrad", grid=(N // nc, nk),
        in_specs=[pl.BlockSpec((tk, nc), lambda j, k: (k, j)), pl.BlockSpec((tk, K), lambda j, k: (k, 0)),
                  pl.BlockSpec(memory_space=pl.ANY)],
        out_specs=pl.BlockSpec((nc, K), lambda j, k: (j, 0)),
        out_shape=_sds((N, K), bf16),
        scratch_shapes=[pltpu.VMEM((nc, K), f32)],
        compiler_params=_params(("parallel", "arbitrary"), VMEM_BIG),
    )(a, b, after)


def in_proj_fwd(h, gain, wt, widths):
    T, D = h.shape
    N = _rows(wt)
    assert sum(widths) == N
    tm = _tile(T, 512)
    offs = [sum(widths[:i]) for i in range(len(widths))]

    def body(h_ref, gain_ref, w_ref, *outs):
        xn, _, _ = _rms_fwd(h_ref[...], gain_ref[...])
        p = _dg(xn, _w2d(w_ref), _NT2, False)
        for o_ref, off, wd_ in zip(outs, offs, widths):
            o_ref[...] = p[:, off:off + wd_]

    row = lambda w: pl.BlockSpec((tm, w), lambda i: (i, 0))
    return _pcall(
        body, name="in_proj_fwd", grid=(T // tm,),
        in_specs=[row(D), _resident((1, D)), _resident(wt.shape)],
        out_specs=[row(w) for w in widths],
        out_shape=[_sds((T, w), f32) for w in widths],
        compiler_params=_params(("parallel",), VMEM_BIG),
    )(h, gain, wt)


def in_proj_bwd(h, gain, dpieces, dout, wt):
    T, D = h.shape
    N = _rows(wt)
    widths = [p.shape[1] for p in dpieces]
    assert sum(widths) == N
    tm = _tile(T, 512)
    nt = T // tm
    npc = len(dpieces)

    def body(*refs):
        h_ref, gain_ref = refs[0], refs[1]
        p_refs = refs[2:2 + npc]
        do_ref, w_ref, dh_ref, dgain_ref, dw_ref, acc_ref = refs[2 + npc:]
        i = pl.program_id(0)
        gain_ = gain_ref[...]
        xn, xhat, rstd = _rms_fwd(h_ref[...], gain_)
        dp = jnp.concatenate([r[...].astype(bf16) for r in p_refs], axis=-1)
        dxn = _dg(dp, _w2d(w_ref), _NN2, False)
        dh, dgain = _rms_bwd(dxn, xhat, rstd, gain_)
        dh_ref[...] = do_ref[...] + dh
        _acc_rows(dgain_ref, dgain, i == 0)

        @pl.when(i == 0)
        def _():
            acc_ref[...] = jnp.zeros_like(acc_ref)

        acc_ref[...] += _dg(dp, xn, _TN2, False)

        @pl.when(i == nt - 1)
        def _():
            dw_ref[...] = acc_ref[...].astype(bf16)

    row = lambda w: pl.BlockSpec((tm, w), lambda i: (i, 0))
    return _pcall(
        body, name="in_proj_bwd", grid=(nt,),
        in_specs=[row(D), _resident((1, D))] + [row(w) for w in widths] + [row(D), _resident(wt.shape)],
        out_specs=[row(D), pl.BlockSpec((8, D), lambda i: (0, 0)), pl.BlockSpec((N, D), lambda i: (0, 0))],
        out_shape=[_sds((T, D), f32), _sds((8, D), f32), _sds((N, D), bf16)],
        scratch_shapes=[pltpu.VMEM((N, D), f32)],
        compiler_params=_params(("arbitrary",), VMEM_BIG),
    )(h, gain, *dpieces, dout, wt)


def out_proj_fwd(h, ya, yb, w):
    T, D = h.shape
    half = ya.shape[1]
    tm = _tile(T, 512)

    def body(h_ref, ya_ref, yb_ref, w_ref, o_ref):
        y = jnp.concatenate([ya_ref[...], yb_ref[...]], axis=-1)
        o_ref[...] = h_ref[...] + _dg(y, _w2d(w_ref), _NN2, False)

    row = lambda w_: pl.BlockSpec((tm, w_), lambda i: (i, 0))
    return _pcall(
        body, name="out_proj_fwd", grid=(T // tm,),
        in_specs=[row(D), row(half), row(half), _resident(w.shape)],
        out_specs=row(D), out_shape=_sds((T, D), f32),
        compiler_params=_params(("parallel",), VMEM_MID),
    )(h, ya, yb, w)


def out_proj_bwd(dout, ya, yb, w):
    T, D = dout.shape
    half = ya.shape[1]
    tm = _tile(T, 512)
    nt = T // tm

    def body(do_ref, ya_ref, yb_ref, w_ref, dya_ref, dyb_ref, dw_ref, acc_ref):
        i = pl.program_id(0)
        dob = do_ref[...].astype(bf16)
        dy = _dg(dob, _w2d(w_ref), _NT2, False)
        dya_ref[...] = dy[:, :half]
        dyb_ref[...] = dy[:, half:]

        @pl.when(i == 0)
        def _():
            acc_ref[...] = jnp.zeros_like(acc_ref)

        y = jnp.concatenate([ya_ref[...], yb_ref[...]], axis=-1)
        acc_ref[...] += _dg(y, dob, _TN2, False)

        @pl.when(i == nt - 1)
        def _():
            dw_ref[...] = acc_ref[...].astype(bf16)

    row = lambda w_: pl.BlockSpec((tm, w_), lambda i: (i, 0))
    return _pcall(
        body, name="out_proj_bwd", grid=(nt,),
        in_specs=[row(D), row(half), row(half), _resident(w.shape)],
        out_specs=[row(half), row(half), pl.BlockSpec((2 * half, D), lambda i: (0, 0))],
        out_shape=[_sds((T, half), f32), _sds((T, half), f32), _sds((2 * half, D), bf16)],
        scratch_shapes=[pltpu.VMEM((2 * half, D), f32)],
        compiler_params=_params(("arbitrary",), VMEM_MID),
    )(dout, ya, yb, w)


def _conv_taps(x, w):
    K = w.shape[0]
    acc = x * w[K - 1:K, :]
    for i in range(K - 1):
        acc = acc + shift(x, K - 1 - i) * w[i:i + 1, :]
    return acc


def conv_silu_math(w, x):
    return _silu(_conv_taps(x, w))


def gated_conv_math(w, xd, bg, cg):
    return bg * _conv_taps(cg * xd, w)


def seq_chan_fwd(math, w, xs, seq, out_dtype, name):
    T, C = xs[0].shape
    K = w.shape[0]
    nb, nc = T // seq, C // HEAD

    def body(w_ref, *refs):
        o_ref = refs[-1]
        o_ref[...] = math(w_ref[...], *[r[...] for r in refs[:-1]]).astype(out_dtype)

    blk = pl.BlockSpec((seq, HEAD), lambda j, b: (b, j))
    return _pcall(
        body, name=name, grid=(nc, nb),
        in_specs=[pl.BlockSpec((K, HEAD), lambda j, b: (0, j))] + [blk] * len(xs),
        out_specs=blk, out_shape=_sds((T, C), out_dtype),
        compiler_params=_params(("parallel", "parallel"), VMEM_MID),
    )(w, *xs)


def seq_chan_bwd(math, w, xs, dy, seq, dx_dtype, name):
    T, C = xs[0].shape
    K = w.shape[0]
    nb, nc = T // seq, C // HEAD
    nx = len(xs)

    def body(w_ref, *refs):
        x_refs, dy_ref = refs[:nx], refs[nx]
        dx_refs, dw_ref = refs[nx + 1:2 * nx + 1], refs[2 * nx + 1]
        _, vjp = jax.vjp(math, w_ref[...], *[r[...] for r in x_refs])
        grads = vjp(dy_ref[...].astype(f32))

        @pl.when(pl.program_id(1) == 0)
        def _():
            dw_ref[...] = jnp.zeros_like(dw_ref)

        dw_ref[...] += grads[0]
        for r, gx in zip(dx_refs, grads[1:]):
            r[...] = gx.astype(dx_dtype)

    blk = pl.BlockSpec((seq, HEAD), lambda j, b: (b, j))
    wblk = pl.BlockSpec((K, HEAD), lambda j, b: (0, j))
    return _pcall(
        body, name=name, grid=(nc, nb),
        in_specs=[wblk] + [blk] * (nx + 1),
        out_specs=[blk] * nx + [wblk],
        out_shape=[_sds((T, C), dx_dtype)] * nx + [_sds((K, C), f32)],
        compiler_params=_params(("parallel", "arbitrary"), VMEM_MID),
    )(w, *xs, dy)


def pool_group_math(win, ag, pw, scale):
    t = lax.broadcasted_iota(jnp.int32, (ag.shape[0], 1), 0)
    s, k = ag, 1
    while k < win:
        s = s + shift(s, k)
        k *= 2
    pooled = s / jnp.minimum(t + 1, win).astype(f32) - ag
    return mm(pooled, pw) * scale


def pool_fwd(a, pw, scale, seq):
    T, C = a.shape

    def body(a_ref, pw_ref, sc_ref, o_ref):
        for gi, win in enumerate(POOL_WINDOWS):
            cols = slice(gi * HEAD, (gi + 1) * HEAD)
            o_ref[:, cols] = pool_group_math(win, a_ref[:, cols], pw_ref[gi], sc_ref[:, cols]).astype(bf16)

    blk = pl.BlockSpec((seq, C), lambda b: (b, 0))
    return _pcall(
        body, name="pool_fwd", grid=(T // seq,),
        in_specs=[blk, _resident(pw.shape), _resident((1, C))],
        out_specs=blk, out_shape=_sds((T, C), bf16),
        compiler_params=_params(("parallel",), VMEM_MID),
    )(a, pw, scale)


def pool_bwd(a, pw, scale, dy, seq):
    T, C = a.shape

    def body(a_ref, pw_ref, sc_ref, dy_ref, da_ref, dpw_ref, dsc_ref):
        first = pl.program_id(0) == 0

        @pl.when(first)
        def _():
            dpw_ref[...] = jnp.zeros_like(dpw_ref)
            dsc_ref[...] = jnp.zeros_like(dsc_ref)

        for gi, win in enumerate(POOL_WINDOWS):
            cols = slice(gi * HEAD, (gi + 1) * HEAD)
            _, vjp = jax.vjp(functools.partial(pool_group_math, win), a_ref[:, cols], pw_ref[gi], sc_ref[:, cols])
            da, dpw, dsc = vjp(dy_ref[:, cols])
            dpw_ref[gi] += dpw
            dsc_ref[0:1, cols] += dsc
            da_ref[:, cols] = da.astype(bf16)

    blk = pl.BlockSpec((seq, C), lambda b: (b, 0))
    return _pcall(
        body, name="pool_bwd", grid=(T // seq,),
        in_specs=[blk, _resident(pw.shape), _resident((1, C)), blk],
        out_specs=[blk, pl.BlockSpec(pw.shape, lambda b: (0, 0, 0)), pl.BlockSpec((8, C), lambda b: (0, 0))],
        out_shape=[_sds((T, C), bf16), _sds(pw.shape, f32), _sds((8, C), f32)],
        compiler_params=_params(("arbitrary",), VMEM_MID),
    )(a, pw, scale, dy)


def _neumann_inverse(lmat):
    n = lmat.shape[-1]
    ii = lax.broadcasted_iota(jnp.int32, (n, n), 0)
    jj = lax.broadcasted_iota(jnp.int32, (n, n), 1)
    inv = jnp.where((ii == jj)[None], 1.0, 0.0) - lmat
    pw_ = _dg(lmat, lmat, _NN3, True)
    steps = int(math.log2(n)) - 1
    for i in range(steps):
        inv = inv + _dg(inv, pw_, _NN3, True)
        if i < steps - 1:
            pw_ = _dg(pw_, pw_, _NN3, True)
    return inv


@jax.custom_vjp
def unit_lower_inverse(lmat):
    return _neumann_inverse(lmat)


def _unit_lower_inverse_fwd(lmat):
    inv = _neumann_inverse(lmat)
    return inv, inv


def _unit_lower_inverse_bwd(inv, ct):
    return (-_dg(_dg(inv, ct, _TN3, True), inv, _NT3, True),)


unit_lower_inverse.defvjp(_unit_lower_inverse_fwd, _unit_lower_inverse_bwd)


def dn_prep_math(qkv, bg, alog, dtb):
    tt = qkv.shape[0]
    nt = tt // CHUNK
    nb = nt * N_HEADS
    W = N_HEADS * HEAD
    beta_all = jax.nn.sigmoid(bg)
    g_all = -jnp.exp(alog) * _softplus(bg + dtb)

    def heads(fn):
        return jnp.stack([fn(hd).reshape(nt, CHUNK, HEAD) for hd in range(N_HEADS)], axis=1).reshape(nb, CHUNK, HEAD)

    def l2n(x):
        return x * lax.rsqrt(jnp.sum(x * x, axis=-1, keepdims=True) + EPS)

    q = heads(lambda hd: l2n(qkv[:, hd * HEAD:(hd + 1) * HEAD]) * (HEAD ** -0.5))
    k = heads(lambda hd: l2n(qkv[:, W + hd * HEAD:W + (hd + 1) * HEAD]))
    v = heads(lambda hd: qkv[:, 2 * W + hd * HEAD:2 * W + (hd + 1) * HEAD])
    beta = heads(lambda hd: jnp.broadcast_to(beta_all[:, hd:hd + 1], (tt, HEAD)))
    g = heads(lambda hd: jnp.broadcast_to(g_all[:, N_HEADS + hd:N_HEADS + hd + 1], (tt, HEAD)))

    ii = lax.broadcasted_iota(jnp.int32, (CHUNK, CHUNK), 0)
    jj = lax.broadcasted_iota(jnp.int32, (CHUNK, CHUNK), 1)
    tril, strict = (ii >= jj)[None], (ii > jj)[None]
    ones_b = jnp.ones((nb, CHUNK, CHUNK), f32)
    tril_b = jnp.where(tril, ones_b, 0.0)
    eye_b = jnp.where((ii == jj)[None], ones_b, 0.0)

    gcb = bmm_hi(tril_b, g)
    gcol = gcb[:, :, :CHUNK]
    grow = bmm_hi(ones_b, eye_b * gcol)
    gamma = jnp.where(tril, jnp.exp(jnp.where(tril, gcol - grow, 0.0)), 0.0)
    kb = k * beta
    lmat = jnp.where(strict, bmm_nt(kb, k) * gamma, 0.0)
    inv = unit_lower_inverse(lmat)
    egc = jnp.exp(gcb)
    u = bmm_hi(inv, v * beta)
    w = bmm_hi(inv, kb * egc)
    aqk = bmm_nt(q, k) * gamma
    qd = q * egc
    glast = gcb[:, CHUNK - 1:CHUNK, :]
    kd = k * jnp.exp(glast - gcb)
    last = jnp.exp(glast)
    r4 = lambda x: x.reshape((nt, N_HEADS) + x.shape[1:])
    return r4(u), r4(w), r4(qd), r4(kd), r4(aqk), r4(last)


_PREP_DTYPES = (f32, bf16, bf16, bf16, bf16, f32)


def _prep_specs(nt, T):
    nchunks = T // CHUNK
    shapes = [(HEAD,), (HEAD,), (HEAD,), (HEAD,), (CHUNK,), (HEAD,)]
    rows = [CHUNK, CHUNK, CHUNK, CHUNK, CHUNK, 1]
    specs = [pl.BlockSpec((nt, N_HEADS, r, s[0]), lambda i: (i, 0, 0, 0)) for r, s in zip(rows, shapes)]
    outs = [(nchunks, N_HEADS, r, s[0]) for r, s in zip(rows, shapes)]
    return specs, outs


def dn_prep_fwd(qkv, bg, alog, dtb):
    T = qkv.shape[0]
    tt = _tile(T, 256)
    nt = tt // CHUNK
    specs, shapes = _prep_specs(nt, T)

    def body(qkv_ref, bg_ref, alog_ref, dtb_ref, *outs):
        res = dn_prep_math(qkv_ref[...], bg_ref[...], alog_ref[...], dtb_ref[...])
        for o_ref, r, dt in zip(outs, res, _PREP_DTYPES):
            o_ref[...] = r.astype(dt)

    row = lambda w: pl.BlockSpec((tt, w), lambda i: (i, 0))
    return _pcall(
        body, name="dn_prep_fwd", grid=(T // tt,),
        in_specs=[row(qkv.shape[1]), row(HEAD), _resident((1, HEAD)), _resident((1, HEAD))],
        out_specs=specs, out_shape=[_sds(s, dt) for s, dt in zip(shapes, _PREP_DTYPES)],
        compiler_params=_params(("parallel",), VMEM_BIG),
    )(qkv, bg, alog, dtb)


def dn_prep_bwd(qkv, bg, alog, dtb, cts):
    T = qkv.shape[0]
    tt = _tile(T, 256)
    nt = tt // CHUNK
    specs, _ = _prep_specs(nt, T)

    def body(qkv_ref, bg_ref, alog_ref, dtb_ref, *refs):
        ct_refs, (dqkv_ref, dbg_ref, dalog_ref, ddtb_ref) = refs[:6], refs[6:]
        _, vjp = jax.vjp(dn_prep_math, qkv_ref[...], bg_ref[...], alog_ref[...], dtb_ref[...])
        dqkv, dbg, dalog, ddtb = vjp(tuple(r[...].astype(f32) for r in ct_refs))
        dqkv_ref[...] = dqkv
        dbg_ref[...] = dbg.astype(bf16)
        first = pl.program_id(0) == 0
        _acc_rows(dalog_ref, dalog, first)
        _acc_rows(ddtb_ref, ddtb, first)

    row = lambda w: pl.BlockSpec((tt, w), lambda i: (i, 0))
    small = pl.BlockSpec((8, HEAD), lambda i: (0, 0))
    return _pcall(
        body, name="dn_prep_bwd", grid=(T // tt,),
        in_specs=[row(qkv.shape[1]), row(HEAD), _resident((1, HEAD)), _resident((1, HEAD))] + specs,
        out_specs=[row(qkv.shape[1]), row(HEAD), small, small],
        out_shape=[_sds(qkv.shape, f32), _sds((T, HEAD), bf16), _sds((8, HEAD), f32), _sds((8, HEAD), f32)],
        compiler_params=_params(("arbitrary",), VMEM_BIG),
    )(qkv, bg, alog, dtb, *cts)


def dn_step(state, u, w, qd, kd, aqk, last):
    v_new = u - bmm(w, state)
    o = bmm(qd, state) + bmm(aqk, v_new)
    return state * last + bmm_tn(kd, v_new), o


def dn_gate(o, z, onorm):
    return o * lax.rsqrt(jnp.mean(o * o, axis=-1, keepdims=True) + EPS) * onorm * _silu(z)


DN_HEADS_PER_STEP_FWD, DN_HEADS_PER_STEP_BWD = 4, 2


def _recur_specs(n, hp):
    rows = [CHUNK, CHUNK, CHUNK, CHUNK, CHUNK, 1]
    lanes = [HEAD, HEAD, HEAD, HEAD, CHUNK, HEAD]
    return [pl.BlockSpec((n, hp, r, l), lambda b, p: (b, p, 0, 0)) for r, l in zip(rows, lanes)]


def dn_recur_fwd(prep, z, onorm, seq):
    T, C = z.shape
    n = seq // CHUNK
    hp = DN_HEADS_PER_STEP_FWD

    def body(u_ref, w_ref, qd_ref, kd_ref, aqk_ref, last_ref, z_ref, on_ref, y_ref, o_ref):
        def step(i, state):
            new, o = dn_step(state, u_ref[i], w_ref[i], qd_ref[i], kd_ref[i], aqk_ref[i], last_ref[i])
            o_ref[i] = o
            return new

        lax.fori_loop(0, n, step, jnp.zeros((hp, HEAD, HEAD), f32))
        for j in range(hp):
            cols = slice(j * HEAD, (j + 1) * HEAD)
            y_ref[:, cols] = dn_gate(o_ref[:, j].reshape(seq, HEAD), z_ref[:, cols], on_ref[...]).astype(bf16)

    blk = pl.BlockSpec((seq, hp * HEAD), lambda b, p: (b, p))
    return _pcall(
        body, name="dn_recur_fwd", grid=(T // seq, N_HEADS // hp),
        in_specs=_recur_specs(n, hp) + [blk, pl.BlockSpec((1, HEAD), lambda b, p: (0, 0))],
        out_specs=blk, out_shape=_sds((T, C), bf16),
        scratch_shapes=[pltpu.VMEM((n, hp, CHUNK, HEAD), f32)],
        compiler_params=_params(("parallel", "parallel"), VMEM_BIG),
    )(*prep, z, onorm)


def dn_recur_bwd(prep, z, onorm, dy, seq):
    T, C = z.shape
    n = seq // CHUNK
    nchunks = T // CHUNK
    hp = DN_HEADS_PER_STEP_BWD

    def body(u_ref, w_ref, qd_ref, kd_ref, aqk_ref, last_ref, z_ref, on_ref, dy_ref,
             du_ref, dw_ref, dqd_ref, dkd_ref, daqk_ref, dlast_ref, dz_ref, don_ref, st_ref, o_ref):
        args = lambda i: tuple(r[i].astype(f32) for r in (u_ref, w_ref, qd_ref, kd_ref, aqk_ref, last_ref))

        def fstep(i, state):
            st_ref[i] = state
            new, o = dn_step(state, *args(i))
            o_ref[i] = o
            return new

        lax.fori_loop(0, n, fstep, jnp.zeros((hp, HEAD, HEAD), f32))
        first = jnp.logical_and(pl.program_id(0) == 0, pl.program_id(1) == 0)
        for j in range(hp):
            cols = slice(j * HEAD, (j + 1) * HEAD)
            _, gate_vjp = jax.vjp(dn_gate, o_ref[:, j].reshape(seq, HEAD), z_ref[:, cols], on_ref[...])
            do, dz, don = gate_vjp(dy_ref[:, cols])
            dz_ref[:, cols] = dz.astype(bf16)
            o_ref[:, j] = do.reshape(n, CHUNK, HEAD)
            _acc_rows(don_ref, don, jnp.logical_and(first, j == 0))

        def bstep(j, dstate):
            i = n - 1 - j
            _, vjp = jax.vjp(dn_step, st_ref[i], *args(i))
            ds, du, dw, dqd, dkd, daqk, dlast = vjp((dstate, o_ref[i]))
            du_ref[i], dw_ref[i], dqd_ref[i], dkd_ref[i], daqk_ref[i], dlast_ref[i] = du, dw, dqd, dkd, daqk, dlast
            return ds

        lax.fori_loop(0, n, bstep, jnp.zeros((hp, HEAD, HEAD), f32))

    blk = pl.BlockSpec((seq, hp * HEAD), lambda b, p: (b, p))
    rows = [CHUNK, CHUNK, CHUNK, CHUNK, CHUNK, 1]
    lanes = [HEAD, HEAD, HEAD, HEAD, CHUNK, HEAD]
    return _pcall(
        body, name="dn_recur_bwd", grid=(T // seq, N_HEADS // hp),
        in_specs=_recur_specs(n, hp) + [blk, pl.BlockSpec((1, HEAD), lambda b, p: (0, 0)), blk],
        out_specs=_recur_specs(n, hp) + [blk, pl.BlockSpec((8, HEAD), lambda b, p: (0, 0))],
        out_shape=[_sds((nchunks, N_HEADS, r, l), f32) for r, l in zip(rows, lanes)]
        + [_sds((T, C), bf16), _sds((8, HEAD), f32)],
        scratch_shapes=[pltpu.VMEM((n, hp, HEAD, HEAD), f32), pltpu.VMEM((n, hp, CHUNK, HEAD), f32)],
        compiler_params=_params(("arbitrary", "arbitrary"), VMEM_BIG),
    )(*prep, z, onorm, dy)


def sgu_math(up, vp, ng, nb, sw, sbias):
    S = up.shape[0]
    nblk = S // SGU_BLOCK
    u = jax.nn.gelu(up, approximate=True)
    v = jax.nn.gelu(vp, approximate=True)
    xc = v - jnp.mean(v, axis=-1, keepdims=True)
    vn = xc * lax.rsqrt(jnp.mean(xc * xc, axis=-1, keepdims=True) + EPS) * ng + nb
    ii = lax.broadcasted_iota(jnp.int32, (SGU_BLOCK, SGU_BLOCK), 0)
    jj = lax.broadcasted_iota(jnp.int32, (SGU_BLOCK, SGU_BLOCK), 1)
    outs = []
    for hd in range(N_HEADS):
        vh = vn[:, hd * HEAD:(hd + 1) * HEAD].reshape(nblk, SGU_BLOCK, HEAD)
        ws = jnp.where(ii >= jj, sw[hd], 0.0)
        mixed = bmm(jnp.broadcast_to(ws[None], (nblk, SGU_BLOCK, SGU_BLOCK)), vh) + sbias[hd][None]
        outs.append(mixed.reshape(S, HEAD))
    return u * jnp.concatenate(outs, axis=-1)


def sgu_fwd(up, vp, ng, nb, sw, sbias, seq):
    T, C = up.shape

    def body(up_ref, vp_ref, ng_ref, nb_ref, sw_ref, sb_ref, o_ref):
        o_ref[...] = sgu_math(up_ref[...], vp_ref[...], ng_ref[...], nb_ref[...], sw_ref[...],
                              sb_ref[...]).astype(bf16)

    blk = pl.BlockSpec((seq, C), lambda b: (b, 0))
    return _pcall(
        body, name="sgu_fwd", grid=(T // seq,),
        in_specs=[blk, blk, _resident((1, C)), _resident((1, C)), _resident(sw.shape), _resident(sbias.shape)],
        out_specs=blk, out_shape=_sds((T, C), bf16),
        compiler_params=_params(("parallel",), VMEM_BIG),
    )(up, vp, ng, nb, sw, sbias)


def sgu_bwd(up, vp, ng, nb, sw, sbias, dy, seq):
    T, C = up.shape

    def body(up_ref, vp_ref, ng_ref, nb_ref, sw_ref, sb_ref, dy_ref,
             dup_ref, dvp_ref, dng_ref, dnb_ref, dsw_ref, dsb_ref):
        _, vjp = jax.vjp(sgu_math, up_ref[...], vp_ref[...], ng_ref[...], nb_ref[...], sw_ref[...], sb_ref[...])
        dup, dvp, dng, dnb, dsw, dsb = vjp(dy_ref[...])
        dup_ref[...] = dup.astype(bf16)
        dvp_ref[...] = dvp.astype(bf16)
        first = pl.program_id(0) == 0
        _acc_rows(dng_ref, dng, first)
        _acc_rows(dnb_ref, dnb, first)

        @pl.when(first)
        def _():
            dsw_ref[...] = jnp.zeros_like(dsw_ref)
            dsb_ref[...] = jnp.zeros_like(dsb_ref)

        dsw_ref[...] += dsw
        dsb_ref[...] += dsb

    blk = pl.BlockSpec((seq, C), lambda b: (b, 0))
    small = pl.BlockSpec((8, C), lambda b: (0, 0))
    return _pcall(
        body, name="sgu_bwd", grid=(T // seq,),
        in_specs=[blk, blk, _resident((1, C)), _resident((1, C)), _resident(sw.shape), _resident(sbias.shape), blk],
        out_specs=[blk, blk, small, small, pl.BlockSpec(sw.shape, lambda b: (0, 0, 0)),
                   pl.BlockSpec(sbias.shape, lambda b: (0, 0, 0))],
        out_shape=[_sds((T, C), bf16), _sds((T, C), bf16), _sds((8, C), f32), _sds((8, C), f32),
                   _sds(sw.shape, f32), _sds(sbias.shape, f32)],
        compiler_params=_params(("arbitrary",), VMEM_BIG),
    )(up, vp, ng, nb, sw, sbias, dy)


def loss_fwd_bwd(h, gain, target):
    T, D = h.shape
    tm = _tile(T, 512)

    def body(h_ref, gain_ref, t_ref, loss_ref, dh_ref, dgain_ref):
        gain_ = gain_ref[...]
        y, xhat, rstd = _rms_fwd(h_ref[...], gain_)
        err = y - t_ref[...]
        part = 0.5 * jnp.sum(jnp.mean(err * err, axis=-1, keepdims=True), axis=0, keepdims=True)
        dh, dgain = _rms_bwd(err * (1.0 / D), xhat, rstd, gain_)
        dh_ref[...] = dh
        first = pl.program_id(0) == 0
        _acc_rows(dgain_ref, dgain, first)

        @pl.when(first)
        def _():
            loss_ref[...] = jnp.zeros_like(loss_ref)

        loss_ref[...] += jnp.broadcast_to(part, loss_ref.shape)

    row = pl.BlockSpec((tm, D), lambda i: (i, 0))
    return _pcall(
        body, name="loss_fwd_bwd", grid=(T // tm,),
        in_specs=[row, _resident((1, D)), row],
        out_specs=[pl.BlockSpec((8, 128), lambda i: (0, 0)), row, pl.BlockSpec((8, D), lambda i: (0, 0))],
        out_shape=[_sds((8, 128), f32), _sds((T, D), f32), _sds((8, D), f32)],
        compiler_params=_params(("arbitrary",), VMEM_MID),
    )(h, gain, target)


def adamw(w, g, m, v):
    c1, c2 = 1.0 - ADAM_B1 ** ADAM_STEP, 1.0 - ADAM_B2 ** ADAM_STEP
    if w.ndim == 3:
        grid, blk = (w.shape[0],), pl.BlockSpec((1,) + w.shape[1:], lambda i: (i, 0, 0))
    else:
        R, C = w.shape
        tr = max(c for c in range(8, 513, 8) if R % c == 0)
        grid, blk = (R // tr,), pl.BlockSpec((tr, C), lambda i: (i, 0))

    def body(w_ref, g_ref, m_ref, v_ref, d_ref, nm_ref, nv_ref):
        gg = g_ref[...]
        nm = ADAM_B1 * m_ref[...] + (1.0 - ADAM_B1) * gg
        nv = ADAM_B2 * v_ref[...] + (1.0 - ADAM_B2) * (gg * gg)
        d_ref[...] = -ADAM_LR * ((nm / c1) / (jnp.sqrt(nv / c2) + ADAM_EPS) + ADAM_WD * w_ref[...])
        nm_ref[...] = nm
        nv_ref[...] = nv

    return _pcall(
        body, name="adamw", grid=grid, in_specs=[blk] * 4, out_specs=[blk] * 3,
        out_shape=[_sds(w.shape, f32)] * 3, compiler_params=_params(("parallel",), VMEM_MID),
    )(w, g, m, v)


def sum8(parts):
    _, R, C = parts.shape
    tr = R
    for cand in (512, 352, 336, 320, 256, 128):
        if R % cand == 0:
            tr = cand
            break

    def body(p_ref, o_ref):
        acc = p_ref[0].astype(f32)
        for i in range(1, N_DEV):
            acc = acc + p_ref[i].astype(f32)
        o_ref[...] = acc

    return _pcall(
        body, name="sum8", grid=(R // tr,),
        in_specs=[pl.BlockSpec((N_DEV, tr, C), lambda i: (0, i, 0))],
        out_specs=pl.BlockSpec((tr, C), lambda i: (i, 0)), out_shape=_sds((R, C), f32),
        compiler_params=_params(("parallel",), VMEM_MID),
    )(parts)


_FLIPS = [(fx, fy, fc) for fx in (0, 1) for fy in (0, 1) for fc in (0, 1)][1:]
_HBM = pl.BlockSpec(memory_space=pltpu.HBM)


def _me():
    return lax.axis_index("x"), lax.axis_index("y"), lax.axis_index("c")


def _peer(flip):
    x, y, c = _me()
    fx, fy, fc = flip
    return (1 - x if fx else x, 1 - y if fy else y, 1 - c if fc else c)


def _lin(dev):
    return 4 * dev[0] + 2 * dev[1] + dev[2]


def _src_block(ref, rows, dev, whole):
    return ref if whole else ref.at[pl.ds(pl.multiple_of(dev * rows, 16), rows)]


_SEM = pl.BlockSpec(memory_space=pltpu.SEMAPHORE)
_EFFECT = pltpu.SideEffectType.DATAFLOW_SIDE_EFFECTING


def push_start(srcs, whole, after, name):
    n = len(srcs)
    rows = [s.shape[0] if whole else s.shape[0] // N_DEV for s in srcs]
    land_shapes = [(N_DEV, r, s.shape[1]) for r, s in zip(rows, srcs)]

    def body(*refs):
        src_refs, land_refs = refs[:n], refs[n:2 * n]
        send_sems, recv_sems, own_sems, token = refs[2 * n + 1], refs[2 * n + 2], refs[2 * n + 3], refs[-1]
        me = _lin(_me())
        for i in range(n):
            for k, flip in enumerate(_FLIPS):
                peer = _peer(flip)
                pltpu.make_async_remote_copy(
                    src_ref=_src_block(src_refs[i], rows[i], _lin(peer), whole), dst_ref=land_refs[i].at[me],
                    send_sem=send_sems.at[i * 7 + k], recv_sem=recv_sems.at[i * 7 + k],
                    device_id=peer, device_id_type=MESH).start()
        for i in range(n):
            pltpu.make_async_copy(_src_block(src_refs[i], rows[i], me, whole), land_refs[i].at[me],
                                  own_sems.at[i]).start()
        token[...] = jnp.zeros_like(token)

    hbm = lambda a: pltpu.with_memory_space_constraint(a, pltpu.HBM)
    outs = _pcall(
        body, name=name,
        in_specs=[_HBM] * (2 * n) + [pl.BlockSpec(memory_space=pl.ANY)],
        out_specs=[_SEM, _SEM, _SEM] + [_HBM] * (2 * n) + [pl.BlockSpec(memory_space=pltpu.VMEM)],
        out_shape=[pltpu.SemaphoreType.DMA((7 * n,)), pltpu.SemaphoreType.DMA((7 * n,)),
                   pltpu.SemaphoreType.DMA((n,))]
        + [pltpu.HBM(s.shape, s.dtype) for s in srcs]
        + [pltpu.HBM(shp, s.dtype) for shp, s in zip(land_shapes, srcs)] + [_sds((8, 128), f32)],
        input_output_aliases={i: 3 + i for i in range(2 * n)},
        compiler_params=pltpu.CompilerParams(has_side_effects=_EFFECT),
    )(*[hbm(s) for s in srcs], *[hbm(lax.empty(shp, s.dtype)) for shp, s in zip(land_shapes, srcs)], after)
    return outs[0], outs[1], outs[2], list(outs[3:3 + n]), list(outs[3 + n:3 + 2 * n]), outs[-1]


def push_wait(handle, whole, after, name):
    send_sems, recv_sems, own_sems, srcs, lands, _ = handle
    n = len(srcs)
    rows = [l.shape[1] for l in lands]

    def body(*refs):
        src_refs, land_refs = refs[:n], refs[n:2 * n]
        send_sems_, recv_sems_, own_sems_ = refs[2 * n], refs[2 * n + 1], refs[2 * n + 2]
        me = _lin(_me())
        for i in range(n):
            for k, flip in enumerate(_FLIPS):
                peer = _peer(flip)
                cp = pltpu.make_async_remote_copy(
                    src_ref=_src_block(src_refs[i], rows[i], _lin(peer), whole), dst_ref=land_refs[i].at[_lin(peer)],
                    send_sem=send_sems_.at[i * 7 + k], recv_sem=recv_sems_.at[i * 7 + k],
                    device_id=peer, device_id_type=MESH)
                cp.wait_send()
                cp.wait_recv()
            pltpu.make_async_copy(_src_block(src_refs[i], rows[i], me, whole), land_refs[i].at[me],
                                  own_sems_.at[i]).wait()

    outs = _pcall(
        body, name=name,
        in_specs=[_HBM] * (2 * n) + [_SEM, _SEM, _SEM, pl.BlockSpec(memory_space=pl.ANY)],
        out_specs=[_HBM] * (2 * n),
        out_shape=[pltpu.HBM(a.shape, a.dtype) for a in srcs + lands],
        input_output_aliases={i: i for i in range(2 * n)},
        compiler_params=pltpu.CompilerParams(has_side_effects=_EFFECT),
    )(*srcs, *lands, send_sems, recv_sems, own_sems, after)
    return list(outs[n:])


def all_gather_small(x):
    R, C = x.shape

    def body(x_ref, o_ref, send_sems, recv_sems):
        me = _lin(_me())
        o_ref[me] = x_ref[...]
        sends = []
        for k, flip in enumerate(_FLIPS):
            rc = pltpu.make_async_remote_copy(
                src_ref=x_ref, dst_ref=o_ref.at[me], send_sem=send_sems.at[k], recv_sem=recv_sems.at[k],
                device_id=_peer(flip), device_id_type=MESH)
            rc.start()
            sends.append(rc)
        for k, flip in enumerate(_FLIPS):
            pltpu.make_async_remote_copy(
                src_ref=x_ref, dst_ref=o_ref.at[_lin(_peer(flip))], send_sem=send_sems.at[k],
                recv_sem=recv_sems.at[k], device_id=_peer(flip), device_id_type=MESH).wait_recv()
        for rc in sends:
            rc.wait_send()

    vm = pl.BlockSpec(memory_space=pltpu.VMEM)
    return _pcall(
        body, name="all_gather_small", in_specs=[vm], out_specs=vm, out_shape=_sds((N_DEV, R, C), x.dtype),
        scratch_shapes=[pltpu.SemaphoreType.DMA((7,)), pltpu.SemaphoreType.DMA((7,))],
        compiler_params=_params(None, VMEM_MID),
    )(x)


def sum_slots(parts):
    _, R, C = parts.shape

    def body(p_ref, o_ref):
        acc = p_ref[0]
        for p in range(1, N_DEV):
            acc = acc + p_ref[p]
        o_ref[...] = acc

    vm = pl.BlockSpec(memory_space=pltpu.VMEM)
    return _pcall(body, name="sum_slots", in_specs=[vm], out_specs=vm, out_shape=_sds((R, C), f32),
                  compiler_params=_params(None, VMEM_MID))(parts)


_PACK_ROWS = 8


def _packed_rows(shape):
    return -(-math.prod(shape) // (128 * _PACK_ROWS)) * _PACK_ROWS


def _pack(arrs):
    parts = []
    for a in arrs:
        flat = a.reshape(-1).astype(f32)
        rows = _packed_rows(a.shape)
        parts.append(jnp.pad(flat, (0, rows * 128 - flat.shape[0])).reshape(rows, 128))
    return jnp.concatenate(parts, axis=0)


def _unpack(buf, shapes):
    out, off = [], 0
    for s in shapes:
        rows = _packed_rows(s)
        out.append(buf[off:off + rows].reshape(-1)[:math.prod(s)].reshape(s))
        off += rows
    return out


def _row(v):
    return v.reshape(1, -1)


def _lane_row(vals, offset):
    return jnp.pad(vals.reshape(1, -1), ((0, 0), (offset, HEAD - offset - vals.shape[-1])))


def kernel(x, ffn1_norm, ffn1_w_gate, ffn1_w_up, ffn1_w_down, mix_norm, ffn2_norm, ffn2_w_gate, ffn2_w_up, ffn2_w_down, ab_w_in, pool_w, pool_scale, dn_conv_w, dn_a_log, dn_dt_bias, dn_out_norm, ab_w_out, cd_w_in, sgu_norm_g, sgu_norm_b, sgu_w, sgu_bias, sc_conv_w, cd_w_out, final_norm, loss_target, m_ffn1_norm, m_ffn1_w_gate, m_ffn1_w_up, m_ffn1_w_down, m_mix_norm, m_ffn2_norm, m_ffn2_w_gate, m_ffn2_w_up, m_ffn2_w_down, m_ab_w_in, m_pool_w, m_pool_scale, m_dn_conv_w, m_dn_a_log, m_dn_dt_bias, m_dn_out_norm, m_ab_w_out, m_cd_w_in, m_sgu_norm_g, m_sgu_norm_b, m_sgu_w, m_sgu_bias, m_sc_conv_w, m_cd_w_out, m_final_norm, v_ffn1_norm, v_ffn1_w_gate, v_ffn1_w_up, v_ffn1_w_down, v_mix_norm, v_ffn2_norm, v_ffn2_w_gate, v_ffn2_w_up, v_ffn2_w_down, v_ab_w_in, v_pool_w, v_pool_scale, v_dn_conv_w, v_dn_a_log, v_dn_dt_bias, v_dn_out_norm, v_ab_w_out, v_cd_w_in, v_sgu_norm_g, v_sgu_norm_b, v_sgu_w, v_sgu_bias, v_sc_conv_w, v_cd_w_out, v_final_norm):
    names = ['ffn1_norm', 'ffn1_w_gate', 'ffn1_w_up', 'ffn1_w_down', 'mix_norm', 'ffn2_norm', 'ffn2_w_gate',
             'ffn2_w_up', 'ffn2_w_down', 'ab_w_in', 'pool_w', 'pool_scale', 'dn_conv_w', 'dn_a_log', 'dn_dt_bias',
             'dn_out_norm', 'ab_w_out', 'cd_w_in', 'sgu_norm_g', 'sgu_norm_b', 'sgu_w', 'sgu_bias', 'sc_conv_w',
             'cd_w_out', 'final_norm']
    loc = locals()
    W = {n: loc[n] for n in names}
    M = {n: loc['m_' + n] for n in names}
    V = {n: loc['v_' + n] for n in names}

    B, S, D = x.shape
    T = B * S
    me = _lin(_me())

    def rows_of(w):
        return w.astype(bf16).T

    def layer_shards(layer):
        e = layer // 2
        shards = [rows_of(W['ffn1_w_gate'][layer]), rows_of(W['ffn1_w_up'][layer]), W['ffn1_w_down'][layer].astype(bf16)]
        if layer % 2 == 0:
            win = jnp.pad(rows_of(W['ab_w_in'][e]), ((0, AB_SHARD_PAD - AB_SHARD), (0, 0)))
            wout = W['ab_w_out'][e].astype(bf16)
        else:
            win = rows_of(W['cd_w_in'][e])
            wout = W['cd_w_out'][e].astype(bf16)
        shards += [win, wout]
        shards += [rows_of(W['ffn2_w_gate'][layer]), rows_of(W['ffn2_w_up'][layer]), W['ffn2_w_down'][layer].astype(bf16)]
        return shards

    def full_weight(idx, land, layer):
        if idx == 3 and layer % 2 == 0:
            return jnp.pad(land[:, :AB_SHARD].reshape(AB_IN, D), ((0, AB_IN_PAD - AB_IN), (0, 0)))
        return land

    def tied(gain, token):
        return gain if token is None else gain + token[0:1, 0:1]

    def start_groups(arrays, groups, whole, after, name):
        out = []
        for gi, idx in enumerate(groups):
            suffix = "" if len(groups) == 1 else "abc"[gi]
            handle = push_start([arrays[i] for i in idx], whole, after, name=name + suffix)
            after = handle[5]
            out.append((idx, handle, suffix))
        return out

    def wait_group(inflight, gi, whole, after, name):
        idx, handle, suffix = inflight[gi]
        return dict(zip(idx, push_wait(handle, whole, after, name=name + suffix)))

    one_group = [tuple(range(8))]
    by_block = [(0, 1, 2), (3, 4), (5, 6, 7)]
    zeros_tile = jnp.zeros((8, 128), f32)
    gathered = [None] * DEPTH
    inflight = start_groups(layer_shards(0), by_block, True, zeros_tile, "gather_start_0")

    small_shards = [W['dn_conv_w'], W['sgu_norm_g'], W['sgu_norm_b'], W['sc_conv_w']]
    gs = all_gather_small(tied(_pack(small_shards), inflight[-1][1][5]))
    per_dev = [_unpack(gs[p], [a.shape for a in small_shards]) for p in range(N_DEV)]
    dn_conv_full, sgu_g_full, sgu_b_full, sc_conv_full = [
        jnp.concatenate([per_dev[p][i] for p in range(N_DEV)], axis=-1) for i in range(4)]

    h = x.reshape(T, D)
    saved = []
    for layer in range(DEPTH):
        e = layer // 2
        mine, landed = inflight, {}
        landed.update(wait_group(mine, 0, True, h if layer else gs, f"gather_wait_{layer}"))
        token = None
        if layer + 1 < DEPTH:
            inflight = start_groups(layer_shards(layer + 1), one_group, True, landed[0], f"gather_start_{layer + 1}")
            token = inflight[-1][1][5]
        wg1, wu1, wd1 = [full_weight(i, landed[i], layer) for i in (0, 1, 2)]
        sv = {'h0': h}
        h, sv['g1'], sv['u1'] = ffn_fwd(h, tied(_row(W['ffn1_norm'][layer]), token), wg1, wu1, wd1)
        sv['h1'] = h
        if len(mine) > 1:
            landed.update(wait_group(mine, 1, True, h, f"gather_wait_{layer}"))
        win, wout = [full_weight(i, landed[i], layer) for i in (3, 4)]
        if layer % 2 == 0:
            a_in, qkv_pre, z, bg = in_proj_fwd(h, _row(W['mix_norm'][layer]), win, [512, 1536, 512, 128])
            ya = pool_fwd(a_in, W['pool_w'][e], _row(W['pool_scale'][e]), S)
            qkv = seq_chan_fwd(conv_silu_math, dn_conv_full[e], [qkv_pre], S, f32, "dn_conv_fwd")
            alog, dtb = _lane_row(W['dn_a_log'][e], N_HEADS), _lane_row(W['dn_dt_bias'][e], N_HEADS)
            prep = dn_prep_fwd(qkv, bg, alog, dtb)
            yb = dn_recur_fwd(prep, z, _row(W['dn_out_norm'][e]), S)
            sv.update(a_in=a_in, qkv_pre=qkv_pre, z=z, bg=bg, qkv=qkv, alog=alog, dtb=dtb, prep=prep)
        else:
            up, vp, xd, bgate, cg = in_proj_fwd(h, _row(W['mix_norm'][layer]), win, [512] * 5)
            sbias = W['sgu_bias'][e].reshape(N_HEADS, SGU_BLOCK, 1)
            ya = sgu_fwd(up, vp, _row(sgu_g_full[e]), _row(sgu_b_full[e]), W['sgu_w'][e], sbias,
                         _tile(S, SGU_TILE))
            yb = seq_chan_fwd(gated_conv_math, sc_conv_full[e], [xd, bgate, cg], S, bf16, "sc_conv_fwd")
            sv.update(up=up, vp=vp, xd=xd, bgate=bgate, cg=cg, sbias=sbias)
        sv.update(ya=ya, yb=yb)
        h = out_proj_fwd(h, ya, yb, wout)
        sv['h2'] = h
        if len(mine) > 1:
            landed.update(wait_group(mine, 2, True, h, f"gather_wait_{layer}"))
        wg2, wu2, wd2 = [full_weight(i, landed[i], layer) for i in (5, 6, 7)]
        h, sv['g2'], sv['u2'] = ffn_fwd(h, _row(W['ffn2_norm'][layer]), wg2, wu2, wd2)
        gathered[layer] = [wg1, wu1, wd1, win, wout, wg2, wu2, wd2]
        saved.append(sv)

    loss_part, dh, dfinal = loss_fwd_bwd(h, _row(W['final_norm']), loss_target.reshape(T, D))
    loss = lax.psum(loss_part[0, 0], ("x", "y", "c"))

    G = {}
    G['final_norm'] = dfinal[0]
    for n in ('ffn1_norm', 'mix_norm', 'ffn2_norm'):
        G[n] = [None] * DEPTH
    for n in ('pool_w', 'pool_scale', 'dn_conv_w', 'dn_a_log', 'dn_dt_bias', 'dn_out_norm',
              'sgu_norm_g', 'sgu_norm_b', 'sgu_w', 'sgu_bias', 'sc_conv_w'):
        G[n] = [None] * 2
    small_names = ['ffn1_norm', 'mix_norm', 'ffn2_norm', 'pool_w', 'pool_scale', 'dn_conv_w', 'dn_a_log',
                   'dn_dt_bias', 'dn_out_norm', 'sgu_norm_g', 'sgu_norm_b', 'sgu_w', 'sgu_bias', 'sc_conv_w',
                   'final_norm']
    big = [None] * DEPTH
    inflight = None
    last = []

    for layer in reversed(range(DEPTH)):
        e = layer // 2
        sv = saved[layer]
        wg1, wu1, wd1, win, wout, wg2, wu2, wd2 = gathered[layer]
        token = None if inflight is None else inflight[-1][1][5]
        dh, dgain, xn, act, dg, du, dy = ffn_bwd(sv['h2'], tied(_row(W['ffn2_norm'][layer]), token), sv['g2'],
                                                  sv['u2'], dh, wg2, wu2, wd2)
        G['ffn2_norm'][layer] = dgain[0]
        dwg2, dwu2, dwd2 = wgrad(dg, xn, zeros_tile), wgrad(du, xn, zeros_tile), wgrad(act, dy, zeros_tile)
        mix_token = ffn1_token = None
        if layer == 0:
            big[1] = wait_group(inflight, 0, False, dwd2, "scatter_wait_1")
            last += start_groups({5: dwg2, 6: dwu2, 7: dwd2}, [(5, 6, 7)], False, big[1][0], "scatter_start_0a")
            mix_token = last[-1][1][5]
        dya, dyb, dwout = out_proj_bwd(dh, sv['ya'], sv['yb'], wout)
        if layer % 2 == 0:
            da, dpw, dsc = pool_bwd(sv['a_in'], W['pool_w'][e], _row(W['pool_scale'][e]), dya, S)
            G['pool_w'][e], G['pool_scale'][e] = dpw, dsc[0]
            *cts, dz, don = dn_recur_bwd(sv['prep'], sv['z'], _row(W['dn_out_norm'][e]), dyb, S)
            G['dn_out_norm'][e] = don[0]
            dqkv, dbg, dalog, ddtb = dn_prep_bwd(sv['qkv'], sv['bg'], sv['alog'], sv['dtb'], cts)
            G['dn_a_log'][e], G['dn_dt_bias'][e] = dalog[0, N_HEADS:2 * N_HEADS], ddtb[0, N_HEADS:2 * N_HEADS]
            dqkv_pre, dconv = seq_chan_bwd(conv_silu_math, dn_conv_full[e], [sv['qkv_pre']], dqkv, S, bf16,
                                           "dn_conv_bwd")
            G['dn_conv_w'][e] = dconv
            dpieces = [da, dqkv_pre, dz, dbg]
        else:
            dup, dvp, dng, dnb, dsw, dsb = sgu_bwd(sv['up'], sv['vp'], _row(sgu_g_full[e]), _row(sgu_b_full[e]),
                                                   W['sgu_w'][e], sv['sbias'], dya, _tile(S, SGU_TILE))
            G['sgu_norm_g'][e], G['sgu_norm_b'][e] = dng[0], dnb[0]
            G['sgu_w'][e], G['sgu_bias'][e] = dsw, dsb.reshape(N_HEADS, SGU_BLOCK)
            dxd, dbgate, dcg, dscw = seq_chan_bwd(gated_conv_math, sc_conv_full[e],
                                                  [sv['xd'], sv['bgate'], sv['cg']], dyb, S, bf16, "sc_conv_bwd")
            G['sc_conv_w'][e] = dscw
            dpieces = [dup, dvp, dxd, dbgate, dcg]
        dh, dgain, dwin = in_proj_bwd(sv['h1'], tied(_row(W['mix_norm'][layer]), mix_token), dpieces, dh, win)
        G['mix_norm'][layer] = dgain[0]
        if layer % 2 == 0:
            dwin = jnp.pad(dwin[:AB_IN].reshape(N_DEV, AB_SHARD, D), ((0, 0), (0, AB_SHARD_PAD - AB_SHARD), (0, 0)))
            dwin = dwin.reshape(N_DEV * AB_SHARD_PAD, D)
        if layer == 0:
            last += start_groups({3: dwin, 4: dwout}, [(3, 4)], False, mix_token, "scatter_start_0b")
            ffn1_token = last[-1][1][5]
        dh, dgain, xn, act, dg, du, dy = ffn_bwd(sv['h0'], tied(_row(W['ffn1_norm'][layer]), ffn1_token), sv['g1'],
                                                  sv['u1'], dh, wg1, wu1, wd1)
        G['ffn1_norm'][layer] = dgain[0]
        order = zeros_tile
        if layer == 0:
            small_g = [G[n] if n == 'final_norm' else jnp.stack(G[n]) for n in small_names]
            small_x = start_groups([_pack(small_g)], [(0,)], True, ffn1_token, "small_grads_start")
            order = small_x[-1][1][5]
        dwg1, dwu1, dwd1 = wgrad(dg, xn, order), wgrad(du, xn, order), wgrad(act, dy, order)
        if layer == 0:
            last += start_groups({0: dwg1, 1: dwu1, 2: dwd1}, [(0, 1, 2)], False, order, "scatter_start_0c")
        else:
            after = zeros_tile
            if inflight is not None:
                big[layer + 1] = wait_group(inflight, 0, False, dh, f"scatter_wait_{layer + 1}")
                after = big[layer + 1][0]
            inflight = start_groups([dwg1, dwu1, dwd1, dwin, dwout, dwg2, dwu2, dwd2], one_group, False, after,
                                    f"scatter_start_{layer}")

    grad_x = dh.reshape(B, S, D)

    reduced_buf = sum_slots(wait_group(small_x, 0, True, last[-1][1][5], "small_grads_wait")[0])
    reduced = _unpack(reduced_buf, [a.shape for a in small_g])
    big[0] = wait_group(last, 0, False, reduced_buf, "scatter_wait_0a")
    big[0].update(wait_group(last, 1, False, big[0][5], "scatter_wait_0b"))
    grads = {}
    for n, g in zip(small_names, reduced):
        if n in ('dn_conv_w', 'sgu_norm_g', 'sgu_norm_b', 'sc_conv_w'):
            c = W[n].shape[-1]
            g = lax.dynamic_slice_in_dim(g, me * c, c, axis=g.ndim - 1)
        grads[n] = g

    def stack_layers(idx, sel):
        out = []
        for layer in sel:
            g = sum8(big[layer][idx])
            out.append(g[:AB_SHARD] if idx == 3 and layer % 2 == 0 else g)
        return jnp.stack(out)

    all_layers, even, odd = range(DEPTH), range(0, DEPTH, 2), range(1, DEPTH, 2)
    delta, new_m, new_v = {}, {}, {}

    def update(n, idx, transposed, sel):
        view = (lambda a: jnp.swapaxes(a, 1, 2)) if transposed else (lambda a: a)
        g = stack_layers(idx, sel)
        d_, m_, v_ = adamw(view(W[n]), g, view(M[n]), view(V[n]))
        grads[n], delta[n], new_m[n], new_v[n] = view(g), view(d_), view(m_), view(v_)

    shapes = [W[n].shape for n in small_names]
    d_, m_, v_ = adamw(_pack([W[n] for n in small_names]), _pack([grads[n] for n in small_names]),
                       _pack([M[n] for n in small_names]), _pack([V[n] for n in small_names]))
    for n, a, b_, c_ in zip(small_names, _unpack(d_, shapes), _unpack(m_, shapes), _unpack(v_, shapes)):
        delta[n], new_m[n], new_v[n] = a, b_, c_
    update('ab_w_in', 3, True, even)
    update('cd_w_in', 3, True, odd)
    update('ab_w_out', 4, False, even)
    update('cd_w_out', 4, False, odd)
    update('ffn2_w_gate', 5, True, all_layers)
    update('ffn2_w_up', 6, True, all_layers)
    update('ffn2_w_down', 7, False, all_layers)
    updated = sum(lax.slice(a, (0,) * a.ndim, (1,) * a.ndim).reshape(1) for a in new_v.values())
    big[0].update(wait_group(last, 2, False, updated, "scatter_wait_0c"))
    update('ffn1_w_gate', 0, True, all_layers)
    update('ffn1_w_up', 1, True, all_layers)
    update('ffn1_w_down', 2, False, all_layers)

    return (loss, grad_x, *[grads[n] for n in names], *[delta[n] for n in names],
            *[new_m[n] for n in names], *[new_v[n] for n in names])
```

```python
import functools
import math

import jax
import jax.numpy as jnp
from jax import lax
from jax.experimental import pallas as pl
from jax.experimental.pallas import tpu as pltpu

f32, bf16 = jnp.float32, jnp.bfloat16

D_MODEL = 1024
DEPTH = 4
CHUNK = 64
POOL_WINDOWS = (2, 4, 8, 16)
HEAD = 128
N_HEADS = 4
SGU_BLOCK = 128
SGU_TILE = 512
FFN_DIM = 2816
AB_IN = 2568
AB_IN_PAD = 2688
AB_SHARD = 321
AB_SHARD_PAD = 336
EPS = 1e-6
N_DEV = 8
MESH = pl.DeviceIdType.MESH

ADAM_LR, ADAM_B1, ADAM_B2, ADAM_EPS, ADAM_WD, ADAM_STEP = 0.001, 0.9, 0.999, 1e-08, 0.01, 10

VMEM_BIG = 56 * 1024 * 1024
VMEM_MID = 40 * 1024 * 1024


def _pcall(body, **kw):
    return pl.pallas_call(body, **kw)


def _params(sem=None, vmem=None):
    return pltpu.CompilerParams(dimension_semantics=sem, vmem_limit_bytes=vmem)


def _sds(shape, dtype):
    return jax.ShapeDtypeStruct(shape, dtype)


_NN2, _NT2, _TN2 = (((1,), (0,)), ((), ())), (((1,), (1,)), ((), ())), (((0,), (0,)), ((), ()))
_NN3, _NT3, _TN3 = (((2,), (1,)), ((0,), (0,))), (((2,), (2,)), ((0,), (0,))), (((1,), (1,)), ((0,), (0,)))


def _dg(a, b, dims, hi):
    if hi:
        return lax.dot_general(a.astype(f32), b.astype(f32), dims, preferred_element_type=f32,
                               precision=lax.Precision.HIGH)
    return lax.dot_general(a.astype(bf16), b.astype(bf16), dims, preferred_element_type=f32)


def _make_mm(nn, nt, tn, hi):
    @jax.custom_vjp
    def mm(a, b):
        return _dg(a, b, nn, hi)

    def mm_bwd(res, ct):
        a, b = res
        return _dg(ct, b, nt, hi).astype(a.dtype), _dg(a, ct, tn, hi).astype(b.dtype)

    mm.defvjp(lambda a, b: (_dg(a, b, nn, hi), (a, b)), mm_bwd)

    @jax.custom_vjp
    def mm_nt(a, b):
        return _dg(a, b, nt, hi)

    def mm_nt_bwd(res, ct):
        a, b = res
        return _dg(ct, b, nn, hi).astype(a.dtype), _dg(ct, a, tn, hi).astype(b.dtype)

    mm_nt.defvjp(lambda a, b: (_dg(a, b, nt, hi), (a, b)), mm_nt_bwd)

    @jax.custom_vjp
    def mm_tn(a, b):
        return _dg(a, b, tn, hi)

    def mm_tn_bwd(res, ct):
        a, b = res
        return _dg(b, ct, nt, hi).astype(a.dtype), _dg(a, ct, nn, hi).astype(b.dtype)

    mm_tn.defvjp(lambda a, b: (_dg(a, b, tn, hi), (a, b)), mm_tn_bwd)
    return mm, mm_nt, mm_tn


mm, mm_nt, mm_tn = _make_mm(_NN2, _NT2, _TN2, False)
bmm, bmm_nt, bmm_tn = _make_mm(_NN3, _NT3, _TN3, False)
bmm_hi, _, _ = _make_mm(_NN3, _NT3, _TN3, True)


def _shift_raw(x, k):
    n = x.shape[0]
    t = lax.broadcasted_iota(jnp.int32, x.shape, 0)
    if k > 0:
        return jnp.where(t >= k, pltpu.roll(x, k, axis=0), 0.0)
    k = -k
    return jnp.where(t < n - k, pltpu.roll(x, n - k, axis=0), 0.0)


@functools.partial(jax.custom_vjp, nondiff_argnums=(1,))
def shift(x, k):
    return _shift_raw(x, k)


shift.defvjp(lambda x, k: (_shift_raw(x, k), None), lambda k, _, ct: (_shift_raw(ct, -k),))


def _silu(x):
    return x * jax.nn.sigmoid(x)


def _softplus(x):
    return jnp.maximum(x, 0.0) + jnp.log(1.0 + jnp.exp(-jnp.abs(x)))


def _rms_fwd(h, gain):
    rstd = lax.rsqrt(jnp.mean(h * h, axis=-1, keepdims=True) + EPS)
    xhat = h * rstd
    return xhat * gain, xhat, rstd


def _rms_bwd(dxn, xhat, rstd, gain):
    dxhat = dxn * gain
    dh = rstd * (dxhat - xhat * jnp.mean(dxhat * xhat, axis=-1, keepdims=True))
    return dh, jnp.sum(dxn * xhat, axis=0, keepdims=True)


def _acc_rows(ref, val, first):
    @pl.when(first)
    def _():
        ref[...] = jnp.zeros_like(ref)
    ref[0:1, :] += val


def _tile(n, cap):
    t = min(n, cap)
    assert n % t == 0, (n, t)
    return t


def _resident(shape):
    nd = len(shape)
    return pl.BlockSpec(shape, lambda *_: (0,) * nd, pipeline_mode=pl.Buffered(1))


def _rows(w):
    return math.prod(w.shape[:-1])


def _w2d(ref):
    w = ref[...]
    return w.reshape(-1, w.shape[-1]) if w.ndim == 3 else w


def ffn_fwd(h, gain, wgt, wut, wd):
    T, D = h.shape
    F = _rows(wgt)
    tm = _tile(T, 256)

    def body(h_ref, gain_ref, wg_ref, wu_ref, wd_ref, ho_ref, g_ref, u_ref):
        hh = h_ref[...]
        xn, _, _ = _rms_fwd(hh, gain_ref[...])
        xb = xn.astype(bf16)
        g = _dg(xb, _w2d(wg_ref), _NT2, False)
        u = _dg(xb, _w2d(wu_ref), _NT2, False)
        y = _dg(_silu(g) * u, _w2d(wd_ref), _NN2, False)
        ho_ref[...] = hh + 0.5 * y
        g_ref[...] = g.astype(bf16)
        u_ref[...] = u.astype(bf16)

    row = lambda w: pl.BlockSpec((tm, w), lambda i: (i, 0))
    return _pcall(
        body, name="ffn_fwd", grid=(T // tm,),
        in_specs=[row(D), _resident((1, D)), _resident(wgt.shape), _resident(wut.shape), _resident(wd.shape)],
        out_specs=[row(D), row(F), row(F)],
        out_shape=[_sds((T, D), f32), _sds((T, F), bf16), _sds((T, F), bf16)],
        compiler_params=_params(("parallel",), VMEM_BIG),
    )(h, gain, wgt, wut, wd)


def ffn_bwd(h, gain, g, u, dout, wgt, wut, wd):
    T, D = h.shape
    F = _rows(wgt)
    tm = _tile(T, 256)

    def body(h_ref, gain_ref, g_ref, u_ref, do_ref, wg_ref, wu_ref, wd_ref,
             dh_ref, dgain_ref, xn_ref, act_ref, dg_ref, du_ref, dy_ref):
        hh, dout_ = h_ref[...], do_ref[...]
        gain_ = gain_ref[...]
        xn, xhat, rstd = _rms_fwd(hh, gain_)
        gg, uu = g_ref[...].astype(f32), u_ref[...].astype(f32)
        dy = (0.5 * dout_).astype(bf16)
        dact = _dg(dy, _w2d(wd_ref), _NT2, False)
        sg = jax.nn.sigmoid(gg)
        silu = gg * sg
        dgate = (dact * uu * (sg * (1.0 + gg * (1.0 - sg)))).astype(bf16)
        dup = (dact * silu).astype(bf16)
        dxn = _dg(dgate, _w2d(wg_ref), _NN2, False) + _dg(dup, _w2d(wu_ref), _NN2, False)
        dh, dgain = _rms_bwd(dxn, xhat, rstd, gain_)
        dh_ref[...] = dout_ + dh
        _acc_rows(dgain_ref, dgain, pl.program_id(0) == 0)
        xn_ref[...] = xn.astype(bf16)
        act_ref[...] = (silu * uu).astype(bf16)
        dg_ref[...] = dgate
        du_ref[...] = dup
        dy_ref[...] = dy

    row = lambda w: pl.BlockSpec((tm, w), lambda i: (i, 0))
    return _pcall(
        body, name="ffn_bwd", grid=(T // tm,),
        in_specs=[row(D), _resident((1, D)), row(F), row(F), row(D),
                  _resident(wgt.shape), _resident(wut.shape), _resident(wd.shape)],
        out_specs=[row(D), pl.BlockSpec((8, D), lambda i: (0, 0)), row(D), row(F), row(F), row(F), row(D)],
        out_shape=[_sds((T, D), f32), _sds((8, D), f32), _sds((T, D), bf16), _sds((T, F), bf16),
                   _sds((T, F), bf16), _sds((T, F), bf16), _sds((T, D), bf16)],
        compiler_params=_params(("arbitrary",), VMEM_BIG),
    )(h, gain, g, u, dout, wgt, wut, wd)


def _col_tile(n, cap=1408):
    best = None
    for c in range(128, cap + 1, 128):
        if n % c == 0:
            best = c
    assert best is not None, n
    return best


def wgrad(a, b, after):
    T, N = a.shape
    K = b.shape[1]
    nc, tk = _col_tile(N), _tile(T, 1024)
    nk = T // tk

    def body(a_ref, b_ref, after_ref, o_ref, acc_ref):
        k = pl.program_id(1)

        @pl.when(k == 0)
        def _():
            acc_ref[...] = jnp.zeros_like(acc_ref)

        acc_ref[...] += _dg(a_ref[...], b_ref[...], _TN2, False)

        @pl.when(k == nk - 1)
        def _():
            o_ref[...] = acc_ref[...].astype(bf16)

    return _pcall(
        body, name="wgrad", grid=(N // nc, nk),
        in_specs=[pl.BlockSpec((tk, nc), lambda j, k: (k, j)), pl.BlockSpec((tk, K), lambda j, k: (k, 0)),
                  pl.BlockSpec(memory_space=pl.ANY)],
        out_specs=pl.BlockSpec((nc, K), lambda j, k: (j, 0)),
        out_shape=_sds((N, K), bf16),
        scratch_shapes=[pltpu.VMEM((nc, K), f32)],
        compiler_params=_params(("parallel", "arbitrary"), VMEM_BIG),
    )(a, b, after)


def in_proj_fwd(h, gain, wt, widths):
    T, D = h.shape
    N = _rows(wt)
    assert sum(widths) == N
    tm = _tile(T, 512)
    offs = [sum(widths[:i]) for i in range(len(widths))]

    def body(h_ref, gain_ref, w_ref, *outs):
        xn, _, _ = _rms_fwd(h_ref[...], gain_ref[...])
        p = _dg(xn, _w2d(w_ref), _NT2, False)
        for o_ref, off, wd_ in zip(outs, offs, widths):
            o_ref[...] = p[:, off:off + wd_]

    row = lambda w: pl.BlockSpec((tm, w), lambda i: (i, 0))
    return _pcall(
        body, name="in_proj_fwd", grid=(T // tm,),
        in_specs=[row(D), _resident((1, D)), _resident(wt.shape)],
        out_specs=[row(w) for w in widths],
        out_shape=[_sds((T, w), f32) for w in widths],
        compiler_params=_params(("parallel",), VMEM_BIG),
    )(h, gain, wt)


def in_proj_bwd(h, gain, dpieces, dout, wt):
    T, D = h.shape
    N = _rows(wt)
    widths = [p.shape[1] for p in dpieces]
    assert sum(widths) == N
    tm = _tile(T, 512)
    nt = T // tm
    npc = len(dpieces)

    def body(*refs):
        h_ref, gain_ref = refs[0], refs[1]
        p_refs = refs[2:2 + npc]
        do_ref, w_ref, dh_ref, dgain_ref, dw_ref, acc_ref = refs[2 + npc:]
        i = pl.program_id(0)
        gain_ = gain_ref[...]
        xn, xhat, rstd = _rms_fwd(h_ref[...], gain_)
        dp = jnp.concatenate([r[...].astype(bf16) for r in p_refs], axis=-1)
        dxn = _dg(dp, _w2d(w_ref), _NN2, False)
        dh, dgain = _rms_bwd(dxn, xhat, rstd, gain_)
        dh_ref[...] = do_ref[...] + dh
        _acc_rows(dgain_ref, dgain, i == 0)

        @pl.when(i == 0)
        def _():
            acc_ref[...] = jnp.zeros_like(acc_ref)

        acc_ref[...] += _dg(dp, xn, _TN2, False)

        @pl.when(i == nt - 1)
        def _():
            dw_ref[...] = acc_ref[...].astype(bf16)

    row = lambda w: pl.BlockSpec((tm, w), lambda i: (i, 0))
    return _pcall(
        body, name="in_proj_bwd", grid=(nt,),
        in_specs=[row(D), _resident((1, D))] + [row(w) for w in widths] + [row(D), _resident(wt.shape)],
        out_specs=[row(D), pl.BlockSpec((8, D), lambda i: (0, 0)), pl.BlockSpec((N, D), lambda i: (0, 0))],
        out_shape=[_sds((T, D), f32), _sds((8, D), f32), _sds((N, D), bf16)],
        scratch_shapes=[pltpu.VMEM((N, D), f32)],
        compiler_params=_params(("arbitrary",), VMEM_BIG),
    )(h, gain, *dpieces, dout, wt)


def out_proj_fwd(h, ya, yb, w):
    T, D = h.shape
    half = ya.shape[1]
    tm = _tile(T, 512)

    def body(h_ref, ya_ref, yb_ref, w_ref, o_ref):
        y = jnp.concatenate([ya_ref[...], yb_ref[...]], axis=-1)
        o_ref[...] = h_ref[...] + _dg(y, _w2d(w_ref), _NN2, False)

    row = lambda w_: pl.BlockSpec((tm, w_), lambda i: (i, 0))
    return _pcall(
        body, name="out_proj_fwd", grid=(T // tm,),
        in_specs=[row(D), row(half), row(half), _resident(w.shape)],
        out_specs=row(D), out_shape=_sds((T, D), f32),
        compiler_params=_params(("parallel",), VMEM_MID),
    )(h, ya, yb, w)


def out_proj_bwd(dout, ya, yb, w):
    T, D = dout.shape
    half = ya.shape[1]
    tm = _tile(T, 512)
    nt = T // tm

    def body(do_ref, ya_ref, yb_ref, w_ref, dya_ref, dyb_ref, dw_ref, acc_ref):
        i = pl.program_id(0)
        dob = do_ref[...].astype(bf16)
        dy = _dg(dob, _w2d(w_ref), _NT2, False)
        dya_ref[...] = dy[:, :half]
        dyb_ref[...] = dy[:, half:]

        @pl.when(i == 0)
        def _():
            acc_ref[...] = jnp.zeros_like(acc_ref)

        y = jnp.concatenate([ya_ref[...], yb_ref[...]], axis=-1)
        acc_ref[...] += _dg(y, dob, _TN2, False)

        @pl.when(i == nt - 1)
        def _():
            dw_ref[...] = acc_ref[...].astype(bf16)

    row = lambda w_: pl.BlockSpec((tm, w_), lambda i: (i, 0))
    return _pcall(
        body, name="out_proj_bwd", grid=(nt,),
        in_specs=[row(D), row(half), row(half), _resident(w.shape)],
        out_specs=[row(half), row(half), pl.BlockSpec((2 * half, D), lambda i: (0, 0))],
        out_shape=[_sds((T, half), f32), _sds((T, half), f32), _sds((2 * half, D), bf16)],
        scratch_shapes=[pltpu.VMEM((2 * half, D), f32)],
        compiler_params=_params(("arbitrary",), VMEM_MID),
    )(dout, ya, yb, w)


def _conv_taps(x, w):
    K = w.shape[0]
    acc = x * w[K - 1:K, :]
    for i in range(K - 1):
        acc = acc + shift(x, K - 1 - i) * w[i:i + 1, :]
    return acc


def conv_silu_math(w, x):
    return _silu(_conv_taps(x, w))


def gated_conv_math(w, xd, bg, cg):
    return bg * _conv_taps(cg * xd, w)


def seq_chan_fwd(math, w, xs, seq, out_dtype, name):
    T, C = xs[0].shape
    K = w.shape[0]
    nb, nc = T // seq, C // HEAD

    def body(w_ref, *refs):
        o_ref = refs[-1]
        o_ref[...] = math(w_ref[...], *[r[...] for r in refs[:-1]]).astype(out_dtype)

    blk = pl.BlockSpec((seq, HEAD), lambda j, b: (b, j))
    return _pcall(
        body, name=name, grid=(nc, nb),
        in_specs=[pl.BlockSpec((K, HEAD), lambda j, b: (0, j))] + [blk] * len(xs),
        out_specs=blk, out_shape=_sds((T, C), out_dtype),
        compiler_params=_params(("parallel", "parallel"), VMEM_MID),
    )(w, *xs)


def seq_chan_bwd(math, w, xs, dy, seq, dx_dtype, name):
    T, C = xs[0].shape
    K = w.shape[0]
    nb, nc = T // seq, C // HEAD
    nx = len(xs)

    def body(w_ref, *refs):
        x_refs, dy_ref = refs[:nx], refs[nx]
        dx_refs, dw_ref = refs[nx + 1:2 * nx + 1], refs[2 * nx + 1]
        _, vjp = jax.vjp(math, w_ref[...], *[r[...] for r in x_refs])
        grads = vjp(dy_ref[...].astype(f32))

        @pl.when(pl.program_id(1) == 0)
        def _():
            dw_ref[...] = jnp.zeros_like(dw_ref)

        dw_ref[...] += grads[0]
        for r, gx in zip(dx_refs, grads[1:]):
            r[...] = gx.astype(dx_dtype)

    blk = pl.BlockSpec((seq, HEAD), lambda j, b: (b, j))
    wblk = pl.BlockSpec((K, HEAD), lambda j, b: (0, j))
    return _pcall(
        body, name=name, grid=(nc, nb),
        in_specs=[wblk] + [blk] * (nx + 1),
        out_specs=[blk] * nx + [wblk],
        out_shape=[_sds((T, C), dx_dtype)] * nx + [_sds((K, C), f32)],
        compiler_params=_params(("parallel", "arbitrary"), VMEM_MID),
    )(w, *xs, dy)


def pool_group_math(win, ag, pw, scale):
    t = lax.broadcasted_iota(jnp.int32, (ag.shape[0], 1), 0)
    s, k = ag, 1
    while k < win:
        s = s + shift(s, k)
        k *= 2
    pooled = s / jnp.minimum(t + 1, win).astype(f32) - ag
    return mm(pooled, pw) * scale


def pool_fwd(a, pw, scale, seq):
    T, C = a.shape

    def body(a_ref, pw_ref, sc_ref, o_ref):
        for gi, win in enumerate(POOL_WINDOWS):
            cols = slice(gi * HEAD, (gi + 1) * HEAD)
            o_ref[:, cols] = pool_group_math(win, a_ref[:, cols], pw_ref[gi], sc_ref[:, cols]).astype(bf16)

    blk = pl.BlockSpec((seq, C), lambda b: (b, 0))
    return _pcall(
        body, name="pool_fwd", grid=(T // seq,),
        in_specs=[blk, _resident(pw.shape), _resident((1, C))],
        out_specs=blk, out_shape=_sds((T, C), bf16),
        compiler_params=_params(("parallel",), VMEM_MID),
    )(a, pw, scale)


def pool_bwd(a, pw, scale, dy, seq):
    T, C = a.shape

    def body(a_ref, pw_ref, sc_ref, dy_ref, da_ref, dpw_ref, dsc_ref):
        first = pl.program_id(0) == 0

        @pl.when(first)
        def _():
            dpw_ref[...] = jnp.zeros_like(dpw_ref)
            dsc_ref[...] = jnp.zeros_like(dsc_ref)

        for gi, win in enumerate(POOL_WINDOWS):
            cols = slice(gi * HEAD, (gi + 1) * HEAD)
            _, vjp = jax.vjp(functools.partial(pool_group_math, win), a_ref[:, cols], pw_ref[gi], sc_ref[:, cols])
            da, dpw, dsc = vjp(dy_ref[:, cols])
            dpw_ref[gi] += dpw
            dsc_ref[0:1, cols] += dsc
            da_ref[:, cols] = da.astype(bf16)

    blk = pl.BlockSpec((seq, C), lambda b: (b, 0))
    return _pcall(
        body, name="pool_bwd", grid=(T // seq,),
        in_specs=[blk, _resident(pw.shape), _resident((1, C)), blk],
        out_specs=[blk, pl.BlockSpec(pw.shape, lambda b: (0, 0, 0)), pl.BlockSpec((8, C), lambda b: (0, 0))],
        out_shape=[_sds((T, C), bf16), _sds(pw.shape, f32), _sds((8, C), f32)],
        compiler_params=_params(("arbitrary",), VMEM_MID),
    )(a, pw, scale, dy)


SOLVE_HI = False


def _neumann_inverse(lmat):
    n = lmat.shape[-1]
    ii = lax.broadcasted_iota(jnp.int32, (n, n), 0)
    jj = lax.broadcasted_iota(jnp.int32, (n, n), 1)
    inv = jnp.where((ii == jj)[None], 1.0, 0.0) - lmat
    pw_ = _dg(lmat, lmat, _NN3, SOLVE_HI)
    steps = int(math.log2(n)) - 1
    for i in range(steps):
        inv = inv + _dg(inv, pw_, _NN3, SOLVE_HI)
        if i < steps - 1:
            pw_ = _dg(pw_, pw_, _NN3, SOLVE_HI)
    return inv


@jax.custom_vjp
def unit_lower_inverse(lmat):
    return _neumann_inverse(lmat)


def _unit_lower_inverse_fwd(lmat):
    inv = _neumann_inverse(lmat)
    return inv, inv


def _unit_lower_inverse_bwd(inv, ct):
    return (-_dg(_dg(inv, ct, _TN3, SOLVE_HI), inv, _NT3, SOLVE_HI),)


unit_lower_inverse.defvjp(_unit_lower_inverse_fwd, _unit_lower_inverse_bwd)


def dn_prep_math(qkv, bg, alog, dtb):
    tt = qkv.shape[0]
    nt = tt // CHUNK
    nb = nt * N_HEADS
    W = N_HEADS * HEAD
    beta_all = jax.nn.sigmoid(bg)
    g_all = -jnp.exp(alog) * _softplus(bg + dtb)

    def heads(fn):
        return jnp.stack([fn(hd).reshape(nt, CHUNK, HEAD) for hd in range(N_HEADS)], axis=1).reshape(nb, CHUNK, HEAD)

    def l2n(x):
        return x * lax.rsqrt(jnp.sum(x * x, axis=-1, keepdims=True) + EPS)

    q = heads(lambda hd: l2n(qkv[:, hd * HEAD:(hd + 1) * HEAD]) * (HEAD ** -0.5))
    k = heads(lambda hd: l2n(qkv[:, W + hd * HEAD:W + (hd + 1) * HEAD]))
    v = heads(lambda hd: qkv[:, 2 * W + hd * HEAD:2 * W + (hd + 1) * HEAD])
    beta = heads(lambda hd: jnp.broadcast_to(beta_all[:, hd:hd + 1], (tt, HEAD)))
    g = heads(lambda hd: jnp.broadcast_to(g_all[:, N_HEADS + hd:N_HEADS + hd + 1], (tt, HEAD)))

    ii = lax.broadcasted_iota(jnp.int32, (CHUNK, CHUNK), 0)
    jj = lax.broadcasted_iota(jnp.int32, (CHUNK, CHUNK), 1)
    tril, strict = (ii >= jj)[None], (ii > jj)[None]
    ones_b = jnp.ones((nb, CHUNK, CHUNK), f32)
    tril_b = jnp.where(tril, ones_b, 0.0)
    eye_b = jnp.where((ii == jj)[None], ones_b, 0.0)

    gcb = bmm_hi(tril_b, g)
    gcol = gcb[:, :, :CHUNK]
    grow = bmm_hi(ones_b, eye_b * gcol)
    gamma = jnp.where(tril, jnp.exp(jnp.where(tril, gcol - grow, 0.0)), 0.0)
    kb = k * beta
    lmat = jnp.where(strict, bmm_nt(kb, k) * gamma, 0.0)
    inv = unit_lower_inverse(lmat)
    egc = jnp.exp(gcb)
    u = bmm(inv, v * beta)
    w = bmm(inv, kb * egc)
    aqk = bmm_nt(q, k) * gamma
    qd = q * egc
    glast = gcb[:, CHUNK - 1:CHUNK, :]
    kd = k * jnp.exp(glast - gcb)
    last = jnp.exp(glast)
    r4 = lambda x: x.reshape((nt, N_HEADS) + x.shape[1:])
    return r4(u), r4(w), r4(qd), r4(kd), r4(aqk), r4(last)


_PREP_DTYPES = (f32, bf16, bf16, bf16, bf16, f32)


def _prep_specs(nt, T):
    nchunks = T // CHUNK
    shapes = [(HEAD,), (HEAD,), (HEAD,), (HEAD,), (CHUNK,), (HEAD,)]
    rows = [CHUNK, CHUNK, CHUNK, CHUNK, CHUNK, 1]
    specs = [pl.BlockSpec((nt, N_HEADS, r, s[0]), lambda i: (i, 0, 0, 0)) for r, s in zip(rows, shapes)]
    outs = [(nchunks, N_HEADS, r, s[0]) for r, s in zip(rows, shapes)]
    return specs, outs


def dn_prep_fwd(qkv, bg, alog, dtb):
    T = qkv.shape[0]
    tt = _tile(T, 256)
    nt = tt // CHUNK
    specs, shapes = _prep_specs(nt, T)

    def body(qkv_ref, bg_ref, alog_ref, dtb_ref, *outs):
        res = dn_prep_math(qkv_ref[...], bg_ref[...], alog_ref[...], dtb_ref[...])
        for o_ref, r, dt in zip(outs, res, _PREP_DTYPES):
            o_ref[...] = r.astype(dt)

    row = lambda w: pl.BlockSpec((tt, w), lambda i: (i, 0))
    return _pcall(
        body, name="dn_prep_fwd", grid=(T // tt,),
        in_specs=[row(qkv.shape[1]), row(HEAD), _resident((1, HEAD)), _resident((1, HEAD))],
        out_specs=specs, out_shape=[_sds(s, dt) for s, dt in zip(shapes, _PREP_DTYPES)],
        compiler_params=_params(("parallel",), VMEM_BIG),
    )(qkv, bg, alog, dtb)


def dn_prep_bwd(qkv, bg, alog, dtb, cts):
    T = qkv.shape[0]
    tt = _tile(T, 256)
    nt = tt // CHUNK
    specs, _ = _prep_specs(nt, T)

    def body(qkv_ref, bg_ref, alog_ref, dtb_ref, *refs):
        ct_refs, (dqkv_ref, dbg_ref, dalog_ref, ddtb_ref) = refs[:6], refs[6:]
        _, vjp = jax.vjp(dn_prep_math, qkv_ref[...], bg_ref[...], alog_ref[...], dtb_ref[...])
        dqkv, dbg, dalog, ddtb = vjp(tuple(r[...].astype(f32) for r in ct_refs))
        dqkv_ref[...] = dqkv
        dbg_ref[...] = dbg.astype(bf16)
        first = pl.program_id(0) == 0
        _acc_rows(dalog_ref, dalog, first)
        _acc_rows(ddtb_ref, ddtb, first)

    row = lambda w: pl.BlockSpec((tt, w), lambda i: (i, 0))
    small = pl.BlockSpec((8, HEAD), lambda i: (0, 0))
    return _pcall(
        body, name="dn_prep_bwd", grid=(T // tt,),
        in_specs=[row(qkv.shape[1]), row(HEAD), _resident((1, HEAD)), _resident((1, HEAD))] + specs,
        out_specs=[row(qkv.shape[1]), row(HEAD), small, small],
        out_shape=[_sds(qkv.shape, f32), _sds((T, HEAD), bf16), _sds((8, HEAD), f32), _sds((8, HEAD), f32)],
        compiler_params=_params(("arbitrary",), VMEM_BIG),
    )(qkv, bg, alog, dtb, *cts)


def dn_step(state, u, w, qd, kd, aqk, last):
    v_new = u - bmm(w, state)
    o = bmm(qd, state) + bmm(aqk, v_new)
    return state * last + bmm_tn(kd, v_new), o


def dn_gate(o, z, onorm):
    return o * lax.rsqrt(jnp.mean(o * o, axis=-1, keepdims=True) + EPS) * onorm * _silu(z)


DN_HEADS_PER_STEP_FWD, DN_HEADS_PER_STEP_BWD = 4, 2


def _recur_specs(n, hp):
    rows = [CHUNK, CHUNK, CHUNK, CHUNK, CHUNK, 1]
    lanes = [HEAD, HEAD, HEAD, HEAD, CHUNK, HEAD]
    return [pl.BlockSpec((n, hp, r, l), lambda b, p: (b, p, 0, 0)) for r, l in zip(rows, lanes)]


def dn_recur_fwd(prep, z, onorm, seq):
    T, C = z.shape
    n = seq // CHUNK
    hp = DN_HEADS_PER_STEP_FWD

    def body(u_ref, w_ref, qd_ref, kd_ref, aqk_ref, last_ref, z_ref, on_ref, y_ref, o_ref):
        def step(i, state):
            new, o = dn_step(state, u_ref[i], w_ref[i], qd_ref[i], kd_ref[i], aqk_ref[i], last_ref[i])
            o_ref[i] = o
            return new

        lax.fori_loop(0, n, step, jnp.zeros((hp, HEAD, HEAD), f32))
        for j in range(hp):
            cols = slice(j * HEAD, (j + 1) * HEAD)
            y_ref[:, cols] = dn_gate(o_ref[:, j].reshape(seq, HEAD), z_ref[:, cols], on_ref[...]).astype(bf16)

    blk = pl.BlockSpec((seq, hp * HEAD), lambda b, p: (b, p))
    return _pcall(
        body, name="dn_recur_fwd", grid=(T // seq, N_HEADS // hp),
        in_specs=_recur_specs(n, hp) + [blk, pl.BlockSpec((1, HEAD), lambda b, p: (0, 0))],
        out_specs=blk, out_shape=_sds((T, C), bf16),
        scratch_shapes=[pltpu.VMEM((n, hp, CHUNK, HEAD), f32)],
        compiler_params=_params(("parallel", "parallel"), VMEM_BIG),
    )(*prep, z, onorm)


def dn_recur_bwd(prep, z, onorm, dy, seq):
    T, C = z.shape
    n = seq // CHUNK
    nchunks = T // CHUNK
    hp = DN_HEADS_PER_STEP_BWD

    def body(u_ref, w_ref, qd_ref, kd_ref, aqk_ref, last_ref, z_ref, on_ref, dy_ref,
             du_ref, dw_ref, dqd_ref, dkd_ref, daqk_ref, dlast_ref, dz_ref, don_ref, st_ref, o_ref):
        args = lambda i: tuple(r[i].astype(f32) for r in (u_ref, w_ref, qd_ref, kd_ref, aqk_ref, last_ref))

        def fstep(i, state):
            st_ref[i] = state
            new, o = dn_step(state, *args(i))
            o_ref[i] = o
            return new

        lax.fori_loop(0, n, fstep, jnp.zeros((hp, HEAD, HEAD), f32))
        first = jnp.logical_and(pl.program_id(0) == 0, pl.program_id(1) == 0)
        for j in range(hp):
            cols = slice(j * HEAD, (j + 1) * HEAD)
            _, gate_vjp = jax.vjp(dn_gate, o_ref[:, j].reshape(seq, HEAD), z_ref[:, cols], on_ref[...])
            do, dz, don = gate_vjp(dy_ref[:, cols])
            dz_ref[:, cols] = dz.astype(bf16)
            o_ref[:, j] = do.reshape(n, CHUNK, HEAD)
            _acc_rows(don_ref, don, jnp.logical_and(first, j == 0))

        def bstep(j, dstate):
            i = n - 1 - j
            _, vjp = jax.vjp(dn_step, st_ref[i], *args(i))
            ds, du, dw, dqd, dkd, daqk, dlast = vjp((dstate, o_ref[i]))
            du_ref[i], dw_ref[i], dqd_ref[i], dkd_ref[i], daqk_ref[i], dlast_ref[i] = du, dw, dqd, dkd, daqk, dlast
            return ds

        lax.fori_loop(0, n, bstep, jnp.zeros((hp, HEAD, HEAD), f32))

    blk = pl.BlockSpec((seq, hp * HEAD), lambda b, p: (b, p))
    rows = [CHUNK, CHUNK, CHUNK, CHUNK, CHUNK, 1]
    lanes = [HEAD, HEAD, HEAD, HEAD, CHUNK, HEAD]
    return _pcall(
        body, name="dn_recur_bwd", grid=(T // seq, N_HEADS // hp),
        in_specs=_recur_specs(n, hp) + [blk, pl.BlockSpec((1, HEAD), lambda b, p: (0, 0)), blk],
        out_specs=_recur_specs(n, hp) + [blk, pl.BlockSpec((8, HEAD), lambda b, p: (0, 0))],
        out_shape=[_sds((nchunks, N_HEADS, r, l), f32) for r, l in zip(rows, lanes)]
        + [_sds((T, C), bf16), _sds((8, HEAD), f32)],
        scratch_shapes=[pltpu.VMEM((n, hp, HEAD, HEAD), f32), pltpu.VMEM((n, hp, CHUNK, HEAD), f32)],
        compiler_params=_params(("arbitrary", "arbitrary"), VMEM_BIG),
    )(*prep, z, onorm, dy)


def sgu_math(up, vp, ng, nb, sw, sbias):
    S = up.shape[0]
    nblk = S // SGU_BLOCK
    u = jax.nn.gelu(up, approximate=True)
    v = jax.nn.gelu(vp, approximate=True)
    xc = v - jnp.mean(v, axis=-1, keepdims=True)
    vn = xc * lax.rsqrt(jnp.mean(xc * xc, axis=-1, keepdims=True) + EPS) * ng + nb
    ii = lax.broadcasted_iota(jnp.int32, (SGU_BLOCK, SGU_BLOCK), 0)
    jj = lax.broadcasted_iota(jnp.int32, (SGU_BLOCK, SGU_BLOCK), 1)
    outs = []
    for hd in range(N_HEADS):
        vh = vn[:, hd * HEAD:(hd + 1) * HEAD].reshape(nblk, SGU_BLOCK, HEAD)
        ws = jnp.where(ii >= jj, sw[hd], 0.0)
        mixed = bmm(jnp.broadcast_to(ws[None], (nblk, SGU_BLOCK, SGU_BLOCK)), vh) + sbias[hd][None]
        outs.append(mixed.reshape(S, HEAD))
    return u * jnp.concatenate(outs, axis=-1)


def sgu_fwd(up, vp, ng, nb, sw, sbias, seq):
    T, C = up.shape

    def body(up_ref, vp_ref, ng_ref, nb_ref, sw_ref, sb_ref, o_ref):
        o_ref[...] = sgu_math(up_ref[...], vp_ref[...], ng_ref[...], nb_ref[...], sw_ref[...],
                              sb_ref[...]).astype(bf16)

    blk = pl.BlockSpec((seq, C), lambda b: (b, 0))
    return _pcall(
        body, name="sgu_fwd", grid=(T // seq,),
        in_specs=[blk, blk, _resident((1, C)), _resident((1, C)), _resident(sw.shape), _resident(sbias.shape)],
        out_specs=blk, out_shape=_sds((T, C), bf16),
        compiler_params=_params(("parallel",), VMEM_BIG),
    )(up, vp, ng, nb, sw, sbias)


def sgu_bwd(up, vp, ng, nb, sw, sbias, dy, seq):
    T, C = up.shape

    def body(up_ref, vp_ref, ng_ref, nb_ref, sw_ref, sb_ref, dy_ref,
             dup_ref, dvp_ref, dng_ref, dnb_ref, dsw_ref, dsb_ref):
        _, vjp = jax.vjp(sgu_math, up_ref[...], vp_ref[...], ng_ref[...], nb_ref[...], sw_ref[...], sb_ref[...])
        dup, dvp, dng, dnb, dsw, dsb = vjp(dy_ref[...])
        dup_ref[...] = dup.astype(bf16)
        dvp_ref[...] = dvp.astype(bf16)
        first = pl.program_id(0) == 0
        _acc_rows(dng_ref, dng, first)
        _acc_rows(dnb_ref, dnb, first)

        @pl.when(first)
        def _():
            dsw_ref[...] = jnp.zeros_like(dsw_ref)
            dsb_ref[...] = jnp.zeros_like(dsb_ref)

        dsw_ref[...] += dsw
        dsb_ref[...] += dsb

    blk = pl.BlockSpec((seq, C), lambda b: (b, 0))
    small = pl.BlockSpec((8, C), lambda b: (0, 0))
    return _pcall(
        body, name="sgu_bwd", grid=(T // seq,),
        in_specs=[blk, blk, _resident((1, C)), _resident((1, C)), _resident(sw.shape), _resident(sbias.shape), blk],
        out_specs=[blk, blk, small, small, pl.BlockSpec(sw.shape, lambda b: (0, 0, 0)),
                   pl.BlockSpec(sbias.shape, lambda b: (0, 0, 0))],
        out_shape=[_sds((T, C), bf16), _sds((T, C), bf16), _sds((8, C), f32), _sds((8, C), f32),
                   _sds(sw.shape, f32), _sds(sbias.shape, f32)],
        compiler_params=_params(("arbitrary",), VMEM_BIG),
    )(up, vp, ng, nb, sw, sbias, dy)


def loss_fwd_bwd(h, gain, target):
    T, D = h.shape
    tm = _tile(T, 512)

    def body(h_ref, gain_ref, t_ref, loss_ref, dh_ref, dgain_ref):
        gain_ = gain_ref[...]
        y, xhat, rstd = _rms_fwd(h_ref[...], gain_)
        err = y - t_ref[...]
        part = 0.5 * jnp.sum(jnp.mean(err * err, axis=-1, keepdims=True), axis=0, keepdims=True)
        dh, dgain = _rms_bwd(err * (1.0 / D), xhat, rstd, gain_)
        dh_ref[...] = dh
        first = pl.program_id(0) == 0
        _acc_rows(dgain_ref, dgain, first)

        @pl.when(first)
        def _():
            loss_ref[...] = jnp.zeros_like(loss_ref)

        loss_ref[...] += jnp.broadcast_to(part, loss_ref.shape)

    row = pl.BlockSpec((tm, D), lambda i: (i, 0))
    return _pcall(
        body, name="loss_fwd_bwd", grid=(T // tm,),
        in_specs=[row, _resident((1, D)), row],
        out_specs=[pl.BlockSpec((8, 128), lambda i: (0, 0)), row, pl.BlockSpec((8, D), lambda i: (0, 0))],
        out_shape=[_sds((8, 128), f32), _sds((T, D), f32), _sds((8, D), f32)],
        compiler_params=_params(("arbitrary",), VMEM_MID),
    )(h, gain, target)


def adamw(w, g, m, v):
    c1, c2 = 1.0 - ADAM_B1 ** ADAM_STEP, 1.0 - ADAM_B2 ** ADAM_STEP
    if w.ndim == 3:
        grid, blk = (w.shape[0],), pl.BlockSpec((1,) + w.shape[1:], lambda i: (i, 0, 0))
    else:
        R, C = w.shape
        tr = max(c for c in range(8, 513, 8) if R % c == 0)
        grid, blk = (R // tr,), pl.BlockSpec((tr, C), lambda i: (i, 0))

    def body(w_ref, g_ref, m_ref, v_ref, d_ref, nm_ref, nv_ref):
        gg = g_ref[...]
        nm = ADAM_B1 * m_ref[...] + (1.0 - ADAM_B1) * gg
        nv = ADAM_B2 * v_ref[...] + (1.0 - ADAM_B2) * (gg * gg)
        d_ref[...] = -ADAM_LR * ((nm / c1) / (jnp.sqrt(nv / c2) + ADAM_EPS) + ADAM_WD * w_ref[...])
        nm_ref[...] = nm
        nv_ref[...] = nv

    return _pcall(
        body, name="adamw", grid=grid, in_specs=[blk] * 4, out_specs=[blk] * 3,
        out_shape=[_sds(w.shape, f32)] * 3, compiler_params=_params(("parallel",), VMEM_MID),
    )(w, g, m, v)


def sum8(parts):
    _, R, C = parts.shape
    tr = R
    for cand in (512, 352, 336, 320, 256, 128):
        if R % cand == 0:
            tr = cand
            break

    def body(p_ref, o_ref):
        acc = p_ref[0].astype(f32)
        for i in range(1, N_DEV):
            acc = acc + p_ref[i].astype(f32)
        o_ref[...] = acc

    return _pcall(
        body, name="sum8", grid=(R // tr,),
        in_specs=[pl.BlockSpec((N_DEV, tr, C), lambda i: (0, i, 0))],
        out_specs=pl.BlockSpec((tr, C), lambda i: (i, 0)), out_shape=_sds((R, C), f32),
        compiler_params=_params(("parallel",), VMEM_MID),
    )(parts)


_FLIPS = [(fx, fy, fc) for fx in (0, 1) for fy in (0, 1) for fc in (0, 1)][1:]
_HBM = pl.BlockSpec(memory_space=pltpu.HBM)


def _me():
    return lax.axis_index("x"), lax.axis_index("y"), lax.axis_index("c")


def _peer(flip):
    x, y, c = _me()
    fx, fy, fc = flip
    return (1 - x if fx else x, 1 - y if fy else y, 1 - c if fc else c)


def _lin(dev):
    return 4 * dev[0] + 2 * dev[1] + dev[2]


def _src_block(ref, rows, dev, whole):
    return ref if whole else ref.at[pl.ds(pl.multiple_of(dev * rows, 16), rows)]


_SEM = pl.BlockSpec(memory_space=pltpu.SEMAPHORE)
_EFFECT = pltpu.SideEffectType.DATAFLOW_SIDE_EFFECTING


def push_start(srcs, whole, after, name):
    n = len(srcs)
    rows = [s.shape[0] if whole else s.shape[0] // N_DEV for s in srcs]
    land_shapes = [(N_DEV, r, s.shape[1]) for r, s in zip(rows, srcs)]

    def body(*refs):
        src_refs, land_refs = refs[:n], refs[n:2 * n]
        send_sems, recv_sems, own_sems, token = refs[2 * n + 1], refs[2 * n + 2], refs[2 * n + 3], refs[-1]
        me = _lin(_me())
        for i in range(n):
            for k, flip in enumerate(_FLIPS):
                peer = _peer(flip)
                pltpu.make_async_remote_copy(
                    src_ref=_src_block(src_refs[i], rows[i], _lin(peer), whole), dst_ref=land_refs[i].at[me],
                    send_sem=send_sems.at[i * 7 + k], recv_sem=recv_sems.at[i * 7 + k],
                    device_id=peer, device_id_type=MESH).start()
        for i in range(n):
            pltpu.make_async_copy(_src_block(src_refs[i], rows[i], me, whole), land_refs[i].at[me],
                                  own_sems.at[i]).start()
        token[...] = jnp.zeros_like(token)

    hbm = lambda a: pltpu.with_memory_space_constraint(a, pltpu.HBM)
    outs = _pcall(
        body, name=name,
        in_specs=[_HBM] * (2 * n) + [pl.BlockSpec(memory_space=pl.ANY)],
        out_specs=[_SEM, _SEM, _SEM] + [_HBM] * (2 * n) + [pl.BlockSpec(memory_space=pltpu.VMEM)],
        out_shape=[pltpu.SemaphoreType.DMA((7 * n,)), pltpu.SemaphoreType.DMA((7 * n,)),
                   pltpu.SemaphoreType.DMA((n,))]
        + [pltpu.HBM(s.shape, s.dtype) for s in srcs]
        + [pltpu.HBM(shp, s.dtype) for shp, s in zip(land_shapes, srcs)] + [_sds((8, 128), f32)],
        input_output_aliases={i: 3 + i for i in range(2 * n)},
        compiler_params=pltpu.CompilerParams(has_side_effects=_EFFECT),
    )(*[hbm(s) for s in srcs], *[hbm(lax.empty(shp, s.dtype)) for shp, s in zip(land_shapes, srcs)], after)
    return outs[0], outs[1], outs[2], list(outs[3:3 + n]), list(outs[3 + n:3 + 2 * n]), outs[-1]


def push_wait(handle, whole, after, name):
    send_sems, recv_sems, own_sems, srcs, lands, _ = handle
    n = len(srcs)
    rows = [l.shape[1] for l in lands]

    def body(*refs):
        src_refs, land_refs = refs[:n], refs[n:2 * n]
        send_sems_, recv_sems_, own_sems_ = refs[2 * n], refs[2 * n + 1], refs[2 * n + 2]
        me = _lin(_me())
        for i in range(n):
            for k, flip in enumerate(_FLIPS):
                peer = _peer(flip)
                cp = pltpu.make_async_remote_copy(
                    src_ref=_src_block(src_refs[i], rows[i], _lin(peer), whole), dst_ref=land_refs[i].at[_lin(peer)],
                    send_sem=send_sems_.at[i * 7 + k], recv_sem=recv_sems_.at[i * 7 + k],
                    device_id=peer, device_id_type=MESH)
                cp.wait_send()
                cp.wait_recv()
            pltpu.make_async_copy(_src_block(src_refs[i], rows[i], me, whole), land_refs[i].at[me],
                                  own_sems_.at[i]).wait()

    outs = _pcall(
        body, name=name,
        in_specs=[_HBM] * (2 * n) + [_SEM, _SEM, _SEM, pl.BlockSpec(memory_space=pl.ANY)],
        out_specs=[_HBM] * (2 * n),
        out_shape=[pltpu.HBM(a.shape, a.dtype) for a in srcs + lands],
        input_output_aliases={i: i for i in range(2 * n)},
        compiler_params=pltpu.CompilerParams(has_side_effects=_EFFECT),
    )(*srcs, *lands, send_sems, recv_sems, own_sems, after)
    return list(outs[n:])


def all_gather_small(x):
    R, C = x.shape

    def body(x_ref, o_ref, send_sems, recv_sems):
        me = _lin(_me())
        o_ref[me] = x_ref[...]
        sends = []
        for k, flip in enumerate(_FLIPS):
            rc = pltpu.make_async_remote_copy(
                src_ref=x_ref, dst_ref=o_ref.at[me], send_sem=send_sems.at[k], recv_sem=recv_sems.at[k],
                device_id=_peer(flip), device_id_type=MESH)
            rc.start()
            sends.append(rc)
        for k, flip in enumerate(_FLIPS):
            pltpu.make_async_remote_copy(
                src_ref=x_ref, dst_ref=o_ref.at[_lin(_peer(flip))], send_sem=send_sems.at[k],
                recv_sem=recv_sems.at[k], device_id=_peer(flip), device_id_type=MESH).wait_recv()
        for rc in sends:
            rc.wait_send()

    vm = pl.BlockSpec(memory_space=pltpu.VMEM)
    return _pcall(
        body, name="all_gather_small", in_specs=[vm], out_specs=vm, out_shape=_sds((N_DEV, R, C), x.dtype),
        scratch_shapes=[pltpu.SemaphoreType.DMA((7,)), pltpu.SemaphoreType.DMA((7,))],
        compiler_params=_params(None, VMEM_MID),
    )(x)


def sum_slots(parts):
    _, R, C = parts.shape

    def body(p_ref, o_ref):
        acc = p_ref[0]
        for p in range(1, N_DEV):
            acc = acc + p_ref[p]
        o_ref[...] = acc

    vm = pl.BlockSpec(memory_space=pltpu.VMEM)
    return _pcall(body, name="sum_slots", in_specs=[vm], out_specs=vm, out_shape=_sds((R, C), f32),
                  compiler_params=_params(None, VMEM_MID))(parts)


_PACK_ROWS = 8


def _packed_rows(shape):
    return -(-math.prod(shape) // (128 * _PACK_ROWS)) * _PACK_ROWS


def _pack(arrs):
    parts = []
    for a in arrs:
        flat = a.reshape(-1).astype(f32)
        rows = _packed_rows(a.shape)
        parts.append(jnp.pad(flat, (0, rows * 128 - flat.shape[0])).reshape(rows, 128))
    return jnp.concatenate(parts, axis=0)


def _unpack(buf, shapes):
    out, off = [], 0
    for s in shapes:
        rows = _packed_rows(s)
        out.append(buf[off:off + rows].reshape(-1)[:math.prod(s)].reshape(s))
        off += rows
    return out


def _row(v):
    return v.reshape(1, -1)


def _lane_row(vals, offset):
    return jnp.pad(vals.reshape(1, -1), ((0, 0), (offset, HEAD - offset - vals.shape[-1])))


def kernel(x, ffn1_norm, ffn1_w_gate, ffn1_w_up, ffn1_w_down, mix_norm, ffn2_norm, ffn2_w_gate, ffn2_w_up, ffn2_w_down, ab_w_in, pool_w, pool_scale, dn_conv_w, dn_a_log, dn_dt_bias, dn_out_norm, ab_w_out, cd_w_in, sgu_norm_g, sgu_norm_b, sgu_w, sgu_bias, sc_conv_w, cd_w_out, final_norm, loss_target, m_ffn1_norm, m_ffn1_w_gate, m_ffn1_w_up, m_ffn1_w_down, m_mix_norm, m_ffn2_norm, m_ffn2_w_gate, m_ffn2_w_up, m_ffn2_w_down, m_ab_w_in, m_pool_w, m_pool_scale, m_dn_conv_w, m_dn_a_log, m_dn_dt_bias, m_dn_out_norm, m_ab_w_out, m_cd_w_in, m_sgu_norm_g, m_sgu_norm_b, m_sgu_w, m_sgu_bias, m_sc_conv_w, m_cd_w_out, m_final_norm, v_ffn1_norm, v_ffn1_w_gate, v_ffn1_w_up, v_ffn1_w_down, v_mix_norm, v_ffn2_norm, v_ffn2_w_gate, v_ffn2_w_up, v_ffn2_w_down, v_ab_w_in, v_pool_w, v_pool_scale, v_dn_conv_w, v_dn_a_log, v_dn_dt_bias, v_dn_out_norm, v_ab_w_out, v_cd_w_in, v_sgu_norm_g, v_sgu_norm_b, v_sgu_w, v_sgu_bias, v_sc_conv_w, v_cd_w_out, v_final_norm):
    names = ['ffn1_norm', 'ffn1_w_gate', 'ffn1_w_up', 'ffn1_w_down', 'mix_norm', 'ffn2_norm', 'ffn2_w_gate',
             'ffn2_w_up', 'ffn2_w_down', 'ab_w_in', 'pool_w', 'pool_scale', 'dn_conv_w', 'dn_a_log', 'dn_dt_bias',
             'dn_out_norm', 'ab_w_out', 'cd_w_in', 'sgu_norm_g', 'sgu_norm_b', 'sgu_w', 'sgu_bias', 'sc_conv_w',
             'cd_w_out', 'final_norm']
    loc = locals()
    W = {n: loc[n] for n in names}
    M = {n: loc['m_' + n] for n in names}
    V = {n: loc['v_' + n] for n in names}

    B, S, D = x.shape
    T = B * S
    me = _lin(_me())

    def rows_of(w):
        return w.astype(bf16).T

    def layer_shards(layer):
        e = layer // 2
        shards = [rows_of(W['ffn1_w_gate'][layer]), rows_of(W['ffn1_w_up'][layer]), W['ffn1_w_down'][layer].astype(bf16)]
        if layer % 2 == 0:
            win = jnp.pad(rows_of(W['ab_w_in'][e]), ((0, AB_SHARD_PAD - AB_SHARD), (0, 0)))
            wout = W['ab_w_out'][e].astype(bf16)
        else:
            win = rows_of(W['cd_w_in'][e])
            wout = W['cd_w_out'][e].astype(bf16)
        shards += [win, wout]
        shards += [rows_of(W['ffn2_w_gate'][layer]), rows_of(W['ffn2_w_up'][layer]), W['ffn2_w_down'][layer].astype(bf16)]
        return shards

    def full_weight(idx, land, layer):
        if idx == 3 and layer % 2 == 0:
            return jnp.pad(land[:, :AB_SHARD].reshape(AB_IN, D), ((0, AB_IN_PAD - AB_IN), (0, 0)))
        return land

    def tied(gain, token):
        return gain if token is None else gain + token[0:1, 0:1]

    def start_groups(arrays, groups, whole, after, name):
        out = []
        for gi, idx in enumerate(groups):
            suffix = "" if len(groups) == 1 else "abc"[gi]
            handle = push_start([arrays[i] for i in idx], whole, after, name=name + suffix)
            after = handle[5]
            out.append((idx, handle, suffix))
        return out

    def wait_group(inflight, gi, whole, after, name):
        idx, handle, suffix = inflight[gi]
        return dict(zip(idx, push_wait(handle, whole, after, name=name + suffix)))

    one_group = [tuple(range(8))]
    by_block = [(0, 1, 2), (3, 4), (5, 6, 7)]
    zeros_tile = jnp.zeros((8, 128), f32)
    gathered = [None] * DEPTH
    small_shards = [W['dn_conv_w'], W['sgu_norm_g'], W['sgu_norm_b'], W['sc_conv_w']]
    gs = all_gather_small(_pack(small_shards))
    per_dev = [_unpack(gs[p], [a.shape for a in small_shards]) for p in range(N_DEV)]
    dn_conv_full, sgu_g_full, sgu_b_full, sc_conv_full = [
        jnp.concatenate([per_dev[p][i] for p in range(N_DEV)], axis=-1) for i in range(4)]

    inflight = start_groups(layer_shards(0), by_block, True, gs, "gather_start_0")

    h = x.reshape(T, D)
    saved = []
    for layer in range(DEPTH):
        e = layer // 2
        mine, landed = inflight, {}
        landed.update(wait_group(mine, 0, True, h if layer else mine[-1][1][5], f"gather_wait_{layer}"))
        token = None
        if layer + 1 < DEPTH:
            inflight = start_groups(layer_shards(layer + 1), one_group, True, landed[0], f"gather_start_{layer + 1}")
            token = inflight[-1][1][5]
        wg1, wu1, wd1 = [full_weight(i, landed[i], layer) for i in (0, 1, 2)]
        sv = {'h0': h}
        h, sv['g1'], sv['u1'] = ffn_fwd(h, tied(_row(W['ffn1_norm'][layer]), token), wg1, wu1, wd1)
        sv['h1'] = h
        if len(mine) > 1:
            landed.update(wait_group(mine, 1, True, h, f"gather_wait_{layer}"))
        win, wout = [full_weight(i, landed[i], layer) for i in (3, 4)]
        if layer % 2 == 0:
            a_in, qkv_pre, z, bg = in_proj_fwd(h, _row(W['mix_norm'][layer]), win, [512, 1536, 512, 128])
            ya = pool_fwd(a_in, W['pool_w'][e], _row(W['pool_scale'][e]), S)
            qkv = seq_chan_fwd(conv_silu_math, dn_conv_full[e], [qkv_pre], S, f32, "dn_conv_fwd")
            alog, dtb = _lane_row(W['dn_a_log'][e], N_HEADS), _lane_row(W['dn_dt_bias'][e], N_HEADS)
            prep = dn_prep_fwd(qkv, bg, alog, dtb)
            yb = dn_recur_fwd(prep, z, _row(W['dn_out_norm'][e]), S)
            sv.update(a_in=a_in, qkv_pre=qkv_pre, z=z, bg=bg, qkv=qkv, alog=alog, dtb=dtb, prep=prep)
        else:
            up, vp, xd, bgate, cg = in_proj_fwd(h, _row(W['mix_norm'][layer]), win, [512] * 5)
            sbias = W['sgu_bias'][e].reshape(N_HEADS, SGU_BLOCK, 1)
            ya = sgu_fwd(up, vp, _row(sgu_g_full[e]), _row(sgu_b_full[e]), W['sgu_w'][e], sbias,
                         _tile(S, SGU_TILE))
            yb = seq_chan_fwd(gated_conv_math, sc_conv_full[e], [xd, bgate, cg], S, bf16, "sc_conv_fwd")
            sv.update(up=up, vp=vp, xd=xd, bgate=bgate, cg=cg, sbias=sbias)
        sv.update(ya=ya, yb=yb)
        h = out_proj_fwd(h, ya, yb, wout)
        sv['h2'] = h
        if len(mine) > 1:
            landed.update(wait_group(mine, 2, True, h, f"gather_wait_{layer}"))
        wg2, wu2, wd2 = [full_weight(i, landed[i], layer) for i in (5, 6, 7)]
        h, sv['g2'], sv['u2'] = ffn_fwd(h, _row(W['ffn2_norm'][layer]), wg2, wu2, wd2)
        gathered[layer] = [wg1, wu1, wd1, win, wout, wg2, wu2, wd2]
        saved.append(sv)

    loss_part, dh, dfinal = loss_fwd_bwd(h, _row(W['final_norm']), loss_target.reshape(T, D))
    loss = lax.psum(loss_part[0, 0], ("x", "y", "c"))

    G = {}
    G['final_norm'] = dfinal[0]
    for n in ('ffn1_norm', 'mix_norm', 'ffn2_norm'):
        G[n] = [None] * DEPTH
    for n in ('pool_w', 'pool_scale', 'dn_conv_w', 'dn_a_log', 'dn_dt_bias', 'dn_out_norm',
              'sgu_norm_g', 'sgu_norm_b', 'sgu_w', 'sgu_bias', 'sc_conv_w'):
        G[n] = [None] * 2
    small_names = ['ffn1_norm', 'mix_norm', 'ffn2_norm', 'pool_w', 'pool_scale', 'dn_conv_w', 'dn_a_log',
                   'dn_dt_bias', 'dn_out_norm', 'sgu_norm_g', 'sgu_norm_b', 'sgu_w', 'sgu_bias', 'sc_conv_w',
                   'final_norm']
    big = [None] * DEPTH
    inflight = None
    last = []

    for layer in reversed(range(DEPTH)):
        e = layer // 2
        sv = saved[layer]
        wg1, wu1, wd1, win, wout, wg2, wu2, wd2 = gathered[layer]
        token = None if inflight is None else inflight[-1][1][5]
        dh, dgain, xn, act, dg, du, dy = ffn_bwd(sv['h2'], tied(_row(W['ffn2_norm'][layer]), token), sv['g2'],
                                                  sv['u2'], dh, wg2, wu2, wd2)
        G['ffn2_norm'][layer] = dgain[0]
        dwg2, dwu2, dwd2 = wgrad(dg, xn, zeros_tile), wgrad(du, xn, zeros_tile), wgrad(act, dy, zeros_tile)
        mix_token = ffn1_token = None
        if layer == 0:
            big[1] = wait_group(inflight, 0, False, dwd2, "scatter_wait_1")
            last += start_groups({5: dwg2, 6: dwu2, 7: dwd2}, [(5, 6, 7)], False, big[1][0], "scatter_start_0a")
            mix_token = last[-1][1][5]
        dya, dyb, dwout = out_proj_bwd(dh, sv['ya'], sv['yb'], wout)
        if layer % 2 == 0:
            da, dpw, dsc = pool_bwd(sv['a_in'], W['pool_w'][e], _row(W['pool_scale'][e]), dya, S)
            G['pool_w'][e], G['pool_scale'][e] = dpw, dsc[0]
            *cts, dz, don = dn_recur_bwd(sv['prep'], sv['z'], _row(W['dn_out_norm'][e]), dyb, S)
            G['dn_out_norm'][e] = don[0]
            dqkv, dbg, dalog, ddtb = dn_prep_bwd(sv['qkv'], sv['bg'], sv['alog'], sv['dtb'], cts)
            G['dn_a_log'][e], G['dn_dt_bias'][e] = dalog[0, N_HEADS:2 * N_HEADS], ddtb[0, N_HEADS:2 * N_HEADS]
            dqkv_pre, dconv = seq_chan_bwd(conv_silu_math, dn_conv_full[e], [sv['qkv_pre']], dqkv, S, bf16,
                                           "dn_conv_bwd")
            G['dn_conv_w'][e] = dconv
            dpieces = [da, dqkv_pre, dz, dbg]
        else:
            dup, dvp, dng, dnb, dsw, dsb = sgu_bwd(sv['up'], sv['vp'], _row(sgu_g_full[e]), _row(sgu_b_full[e]),
                                                   W['sgu_w'][e], sv['sbias'], dya, _tile(S, SGU_TILE))
            G['sgu_norm_g'][e], G['sgu_norm_b'][e] = dng[0], dnb[0]
            G['sgu_w'][e], G['sgu_bias'][e] = dsw, dsb.reshape(N_HEADS, SGU_BLOCK)
            dxd, dbgate, dcg, dscw = seq_chan_bwd(gated_conv_math, sc_conv_full[e],
                                                  [sv['xd'], sv['bgate'], sv['cg']], dyb, S, bf16, "sc_conv_bwd")
            G['sc_conv_w'][e] = dscw
            dpieces = [dup, dvp, dxd, dbgate, dcg]
        dh, dgain, dwin = in_proj_bwd(sv['h1'], tied(_row(W['mix_norm'][layer]), mix_token), dpieces, dh, win)
        G['mix_norm'][layer] = dgain[0]
        if layer % 2 == 0:
            dwin = jnp.pad(dwin[:AB_IN].reshape(N_DEV, AB_SHARD, D), ((0, 0), (0, AB_SHARD_PAD - AB_SHARD), (0, 0)))
            dwin = dwin.reshape(N_DEV * AB_SHARD_PAD, D)
        if layer == 0:
            last += start_groups({3: dwin, 4: dwout}, [(3, 4)], False, mix_token, "scatter_start_0b")
            ffn1_token = last[-1][1][5]
        dh, dgain, xn, act, dg, du, dy = ffn_bwd(sv['h0'], tied(_row(W['ffn1_norm'][layer]), ffn1_token), sv['g1'],
                                                  sv['u1'], dh, wg1, wu1, wd1)
        G['ffn1_norm'][layer] = dgain[0]
        order = zeros_tile
        if layer == 0:
            small_g = [G[n] if n == 'final_norm' else jnp.stack(G[n]) for n in small_names]
            small_x = start_groups([_pack(small_g)], [(0,)], True, ffn1_token, "small_grads_start")
            order = small_x[-1][1][5]
        dwg1, dwu1, dwd1 = wgrad(dg, xn, order), wgrad(du, xn, order), wgrad(act, dy, order)
        if layer == 0:
            last += start_groups({0: dwg1, 1: dwu1, 2: dwd1}, [(0, 1, 2)], False, order, "scatter_start_0c")
        else:
            after = zeros_tile
            if inflight is not None:
                big[layer + 1] = wait_group(inflight, 0, False, dh, f"scatter_wait_{layer + 1}")
                after = big[layer + 1][0]
            inflight = start_groups([dwg1, dwu1, dwd1, dwin, dwout, dwg2, dwu2, dwd2], one_group, False, after,
                                    f"scatter_start_{layer}")

    grad_x = dh.reshape(B, S, D)

    reduced_buf = sum_slots(wait_group(small_x, 0, True, last[-1][1][5], "small_grads_wait")[0])
    reduced = _unpack(reduced_buf, [a.shape for a in small_g])
    big[0] = wait_group(last, 0, False, reduced_buf, "scatter_wait_0a")
    big[0].update(wait_group(last, 1, False, big[0][5], "scatter_wait_0b"))
    grads = {}
    for n, g in zip(small_names, reduced):
        if n in ('dn_conv_w', 'sgu_norm_g', 'sgu_norm_b', 'sc_conv_w'):
            c = W[n].shape[-1]
            g = lax.dynamic_slice_in_dim(g, me * c, c, axis=g.ndim - 1)
        grads[n] = g

    def stack_layers(idx, sel):
        out = []
        for layer in sel:
            g = sum8(big[layer][idx])
            out.append(g[:AB_SHARD] if idx == 3 and layer % 2 == 0 else g)
        return jnp.stack(out)

    all_layers, even, odd = range(DEPTH), range(0, DEPTH, 2), range(1, DEPTH, 2)
    delta, new_m, new_v = {}, {}, {}

    def update(n, idx, transposed, sel):
        view = (lambda a: jnp.swapaxes(a, 1, 2)) if transposed else (lambda a: a)
        g = stack_layers(idx, sel)
        d_, m_, v_ = adamw(view(W[n]), g, view(M[n]), view(V[n]))
        grads[n], delta[n], new_m[n], new_v[n] = view(g), view(d_), view(m_), view(v_)

    shapes = [W[n].shape for n in small_names]
    d_, m_, v_ = adamw(_pack([W[n] for n in small_names]), _pack([grads[n] for n in small_names]),
                       _pack([M[n] for n in small_names]), _pack([V[n] for n in small_names]))
    for n, a, b_, c_ in zip(small_names, _unpack(d_, shapes), _unpack(m_, shapes), _unpack(v_, shapes)):
        delta[n], new_m[n], new_v[n] = a, b_, c_
    update('ab_w_in', 3, True, even)
    update('cd_w_in', 3, True, odd)
    update('ab_w_out', 4, False, even)
    update('cd_w_out', 4, False, odd)
    update('ffn2_w_gate', 5, True, all_layers)
    update('ffn2_w_up', 6, True, all_layers)
    update('ffn2_w_down', 7, False, all_layers)
    updated = sum(lax.slice(a, (0,) * a.ndim, (1,) * a.ndim).reshape(1) for a in new_v.values())
    big[0].update(wait_group(last, 2, False, updated, "scatter_wait_0c"))
    update('ffn1_w_gate', 0, True, all_layers)
    update('ffn1_w_up', 1, True, all_layers)
    update('ffn1_w_down', 2, False, all_layers)

    return (loss, grad_x, *[grads[n] for n in names], *[delta[n] for n in names],
            *[new_m[n] for n in names], *[new_v[n] for n in names])
```

```python
import functools
import math

import jax
import jax.numpy as jnp
from jax import lax
from jax.experimental import pallas as pl
from jax.experimental.pallas import tpu as pltpu

f32, bf16 = jnp.float32, jnp.bfloat16

D_MODEL = 1024
DEPTH = 4
CHUNK = 64
POOL_WINDOWS = (2, 4, 8, 16)
HEAD = 128
N_HEADS = 4
SGU_BLOCK = 128
SGU_TILE = 512
FFN_DIM = 2816
AB_IN = 2568
AB_IN_PAD = 2688
AB_SHARD = 321
AB_SHARD_PAD = 336
EPS = 1e-6
N_DEV = 8
MESH = pl.DeviceIdType.MESH

ADAM_LR, ADAM_B1, ADAM_B2, ADAM_EPS, ADAM_WD, ADAM_STEP = 0.001, 0.9, 0.999, 1e-08, 0.01, 10

VMEM_BIG = 56 * 1024 * 1024
VMEM_MID = 40 * 1024 * 1024


def _pcall(body, **kw):
    return pl.pallas_call(body, **kw)


def _params(sem=None, vmem=None):
    return pltpu.CompilerParams(dimension_semantics=sem, vmem_limit_bytes=vmem)


def _sds(shape, dtype):
    return jax.ShapeDtypeStruct(shape, dtype)


_NN2, _NT2, _TN2 = (((1,), (0,)), ((), ())), (((1,), (1,)), ((), ())), (((0,), (0,)), ((), ()))
_NN3, _NT3, _TN3 = (((2,), (1,)), ((0,), (0,))), (((2,), (2,)), ((0,), (0,))), (((1,), (1,)), ((0,), (0,)))


def _dg(a, b, dims, hi):
    if hi:
        return lax.dot_general(a.astype(f32), b.astype(f32), dims, preferred_element_type=f32,
                               precision=lax.Precision.HIGH)
    return lax.dot_general(a.astype(bf16), b.astype(bf16), dims, preferred_element_type=f32)


def _make_mm(nn, nt, tn, hi):
    @jax.custom_vjp
    def mm(a, b):
        return _dg(a, b, nn, hi)

    def mm_bwd(res, ct):
        a, b = res
        return _dg(ct, b, nt, hi).astype(a.dtype), _dg(a, ct, tn, hi).astype(b.dtype)

    mm.defvjp(lambda a, b: (_dg(a, b, nn, hi), (a, b)), mm_bwd)

    @jax.custom_vjp
    def mm_nt(a, b):
        return _dg(a, b, nt, hi)

    def mm_nt_bwd(res, ct):
        a, b = res
        return _dg(ct, b, nn, hi).astype(a.dtype), _dg(ct, a, tn, hi).astype(b.dtype)

    mm_nt.defvjp(lambda a, b: (_dg(a, b, nt, hi), (a, b)), mm_nt_bwd)

    @jax.custom_vjp
    def mm_tn(a, b):
        return _dg(a, b, tn, hi)

    def mm_tn_bwd(res, ct):
        a, b = res
        return _dg(b, ct, nt, hi).astype(a.dtype), _dg(a, ct, nn, hi).astype(b.dtype)

    mm_tn.defvjp(lambda a, b: (_dg(a, b, tn, hi), (a, b)), mm_tn_bwd)
    return mm, mm_nt, mm_tn


mm, mm_nt, mm_tn = _make_mm(_NN2, _NT2, _TN2, False)
bmm, bmm_nt, bmm_tn = _make_mm(_NN3, _NT3, _TN3, False)
bmm_hi, _, _ = _make_mm(_NN3, _NT3, _TN3, True)


def _shift_raw(x, k):
    n = x.shape[0]
    t = lax.broadcasted_iota(jnp.int32, x.shape, 0)
    if k > 0:
        return jnp.where(t >= k, pltpu.roll(x, k, axis=0), 0.0)
    k = -k
    return jnp.where(t < n - k, pltpu.roll(x, n - k, axis=0), 0.0)


@functools.partial(jax.custom_vjp, nondiff_argnums=(1,))
def shift(x, k):
    return _shift_raw(x, k)


shift.defvjp(lambda x, k: (_shift_raw(x, k), None), lambda k, _, ct: (_shift_raw(ct, -k),))


def _silu(x):
    return x * jax.nn.sigmoid(x)


def _softplus(x):
    return jnp.maximum(x, 0.0) + jnp.log(1.0 + jnp.exp(-jnp.abs(x)))


def _rms_fwd(h, gain):
    rstd = lax.rsqrt(jnp.mean(h * h, axis=-1, keepdims=True) + EPS)
    xhat = h * rstd
    return xhat * gain, xhat, rstd


def _rms_bwd(dxn, xhat, rstd, gain):
    dxhat = dxn * gain
    dh = rstd * (dxhat - xhat * jnp.mean(dxhat * xhat, axis=-1, keepdims=True))
    return dh, jnp.sum(dxn * xhat, axis=0, keepdims=True)


def _acc_rows(ref, val, first):
    @pl.when(first)
    def _():
        ref[...] = jnp.zeros_like(ref)
    ref[0:1, :] += val


def _tile(n, cap):
    t = min(n, cap)
    assert n % t == 0, (n, t)
    return t


def _resident(shape):
    nd = len(shape)
    return pl.BlockSpec(shape, lambda *_: (0,) * nd, pipeline_mode=pl.Buffered(1))


def _rows(w):
    return math.prod(w.shape[:-1])


def _w2d(ref):
    w = ref[...]
    return w.reshape(-1, w.shape[-1]) if w.ndim == 3 else w


def ffn_fwd(h, gain, wgt, wut, wd):
    T, D = h.shape
    F = _rows(wgt)
    tm = _tile(T, 256)

    def body(h_ref, gain_ref, wg_ref, wu_ref, wd_ref, ho_ref, g_ref, u_ref):
        hh = h_ref[...]
        xn, _, _ = _rms_fwd(hh, gain_ref[...])
        xb = xn.astype(bf16)
        g = _dg(xb, _w2d(wg_ref), _NT2, False)
        u = _dg(xb, _w2d(wu_ref), _NT2, False)
        y = _dg(_silu(g) * u, _w2d(wd_ref), _NN2, False)
        ho_ref[...] = hh + 0.5 * y
        g_ref[...] = g.astype(bf16)
        u_ref[...] = u.astype(bf16)

    row = lambda w: pl.BlockSpec((tm, w), lambda i: (i, 0))
    return _pcall(
        body, name="ffn_fwd", grid=(T // tm,),
        in_specs=[row(D), _resident((1, D)), _resident(wgt.shape), _resident(wut.shape), _resident(wd.shape)],
        out_specs=[row(D), row(F), row(F)],
        out_shape=[_sds((T, D), f32), _sds((T, F), bf16), _sds((T, F), bf16)],
        compiler_params=_params(("parallel",), VMEM_BIG),
    )(h, gain, wgt, wut, wd)


def ffn_bwd(h, gain, g, u, dout, wgt, wut, wd):
    T, D = h.shape
    F = _rows(wgt)
    tm = _tile(T, 256)

    def body(h_ref, gain_ref, g_ref, u_ref, do_ref, wg_ref, wu_ref, wd_ref,
             dh_ref, dgain_ref, xn_ref, act_ref, dg_ref, du_ref, dy_ref):
        hh, dout_ = h_ref[...], do_ref[...]
        gain_ = gain_ref[...]
        xn, xhat, rstd = _rms_fwd(hh, gain_)
        gg, uu = g_ref[...].astype(f32), u_ref[...].astype(f32)
        dy = (0.5 * dout_).astype(bf16)
        dact = _dg(dy, _w2d(wd_ref), _NT2, False)
        sg = jax.nn.sigmoid(gg)
        silu = gg * sg
        dgate = (dact * uu * (sg * (1.0 + gg * (1.0 - sg)))).astype(bf16)
        dup = (dact * silu).astype(bf16)
        dxn = _dg(dgate, _w2d(wg_ref), _NN2, False) + _dg(dup, _w2d(wu_ref), _NN2, False)
        dh, dgain = _rms_bwd(dxn, xhat, rstd, gain_)
        dh_ref[...] = dout_ + dh
        _acc_rows(dgain_ref, dgain, pl.program_id(0) == 0)
        xn_ref[...] = xn.astype(bf16)
        act_ref[...] = (silu * uu).astype(bf16)
        dg_ref[...] = dgate
        du_ref[...] = dup
        dy_ref[...] = dy

    row = lambda w: pl.BlockSpec((tm, w), lambda i: (i, 0))
    return _pcall(
        body, name="ffn_bwd", grid=(T // tm,),
        in_specs=[row(D), _resident((1, D)), row(F), row(F), row(D),
                  _resident(wgt.shape), _resident(wut.shape), _resident(wd.shape)],
        out_specs=[row(D), pl.BlockSpec((8, D), lambda i: (0, 0)), row(D), row(F), row(F), row(F), row(D)],
        out_shape=[_sds((T, D), f32), _sds((8, D), f32), _sds((T, D), bf16), _sds((T, F), bf16),
                   _sds((T, F), bf16), _sds((T, F), bf16), _sds((T, D), bf16)],
        compiler_params=_params(("arbitrary",), VMEM_BIG),
    )(h, gain, g, u, dout, wgt, wut, wd)


def _col_tile(n, cap=1408):
    best = None
    for c in range(128, cap + 1, 128):
        if n % c == 0:
            best = c
    assert best is not None, n
    return best


def wgrad(a, b, after):
    T, N = a.shape
    K = b.shape[1]
    nc, tk = _col_tile(N), _tile(T, 1024)
    nk = T // tk

    def body(a_ref, b_ref, after_ref, o_ref, acc_ref):
        k = pl.program_id(1)

        @pl.when(k == 0)
        def _():
            acc_ref[...] = jnp.zeros_like(acc_ref)

        acc_ref[...] += _dg(a_ref[...], b_ref[...], _TN2, False)

        @pl.when(k == nk - 1)
        def _():
            o_ref[...] = acc_ref[...].astype(bf16)

    return _pcall(
        body, name="wgrad", grid=(N // nc, nk),
        in_specs=[pl.BlockSpec((tk, nc), lambda j, k: (k, j)), pl.BlockSpec((tk, K), lambda j, k: (k, 0)),
                  pl.BlockSpec(memory_space=pl.ANY)],
        out_specs=pl.BlockSpec((nc, K), lambda j, k: (j, 0)),
        out_shape=_sds((N, K), bf16),
        scratch_shapes=[pltpu.VMEM((nc, K), f32)],
        compiler_params=_params(("parallel", "arbitrary"), VMEM_BIG),
    )(a, b, after)


def in_proj_fwd(h, gain, wt, widths):
    T, D = h.shape
    N = _rows(wt)
    assert sum(widths) == N
    tm = _tile(T, 512)
    offs = [sum(widths[:i]) for i in range(len(widths))]

    def body(h_ref, gain_ref, w_ref, *outs):
        xn, _, _ = _rms_fwd(h_ref[...], gain_ref[...])
        p = _dg(xn, _w2d(w_ref), _NT2, False)
        for o_ref, off, wd_ in zip(outs, offs, widths):
            o_ref[...] = p[:, off:off + wd_]

    row = lambda w: pl.BlockSpec((tm, w), lambda i: (i, 0))
    return _pcall(
        body, name="in_proj_fwd", grid=(T // tm,),
        in_specs=[row(D), _resident((1, D)), _resident(wt.shape)],
        out_specs=[row(w) for w in widths],
        out_shape=[_sds((T, w), f32) for w in widths],
        compiler_params=_params(("parallel",), VMEM_BIG),
    )(h, gain, wt)


def in_proj_bwd(h, gain, dpieces, dout, wt):
    T, D = h.shape
    N = _rows(wt)
    widths = [p.shape[1] for p in dpieces]
    assert sum(widths) == N
    tm = _tile(T, 512)
    nt = T // tm
    npc = len(dpieces)

    def body(*refs):
        h_ref, gain_ref = refs[0], refs[1]
        p_refs = refs[2:2 + npc]
        do_ref, w_ref, dh_ref, dgain_ref, dw_ref, acc_ref = refs[2 + npc:]
        i = pl.program_id(0)
        gain_ = gain_ref[...]
        xn, xhat, rstd = _rms_fwd(h_ref[...], gain_)
        dp = jnp.concatenate([r[...].astype(bf16) for r in p_refs], axis=-1)
        dxn = _dg(dp, _w2d(w_ref), _NN2, False)
        dh, dgain = _rms_bwd(dxn, xhat, rstd, gain_)
        dh_ref[...] = do_ref[...] + dh
        _acc_rows(dgain_ref, dgain, i == 0)

        @pl.when(i == 0)
        def _():
            acc_ref[...] = jnp.zeros_like(acc_ref)

        acc_ref[...] += _dg(dp, xn, _TN2, False)

        @pl.when(i == nt - 1)
        def _():
            dw_ref[...] = acc_ref[...].astype(bf16)

    row = lambda w: pl.BlockSpec((tm, w), lambda i: (i, 0))
    return _pcall(
        body, name="in_proj_bwd", grid=(nt,),
        in_specs=[row(D), _resident((1, D))] + [row(w) for w in widths] + [row(D), _resident(wt.shape)],
        out_specs=[row(D), pl.BlockSpec((8, D), lambda i: (0, 0)), pl.BlockSpec((N, D), lambda i: (0, 0))],
        out_shape=[_sds((T, D), f32), _sds((8, D), f32), _sds((N, D), bf16)],
        scratch_shapes=[pltpu.VMEM((N, D), f32)],
        compiler_params=_params(("arbitrary",), VMEM_BIG),
    )(h, gain, *dpieces, dout, wt)


def out_proj_fwd(h, ya, yb, w):
    T, D = h.shape
    half = ya.shape[1]
    tm = _tile(T, 512)

    def body(h_ref, ya_ref, yb_ref, w_ref, o_ref):
        y = jnp.concatenate([ya_ref[...], yb_ref[...]], axis=-1)
        o_ref[...] = h_ref[...] + _dg(y, _w2d(w_ref), _NN2, False)

    row = lambda w_: pl.BlockSpec((tm, w_), lambda i: (i, 0))
    return _pcall(
        body, name="out_proj_fwd", grid=(T // tm,),
        in_specs=[row(D), row(half), row(half), _resident(w.shape)],
        out_specs=row(D), out_shape=_sds((T, D), f32),
        compiler_params=_params(("parallel",), VMEM_MID),
    )(h, ya, yb, w)


def out_proj_bwd(dout, ya, yb, w):
    T, D = dout.shape
    half = ya.shape[1]
    tm = _tile(T, 512)
    nt = T // tm

    def body(do_ref, ya_ref, yb_ref, w_ref, dya_ref, dyb_ref, dw_ref, acc_ref):
        i = pl.program_id(0)
        dob = do_ref[...].astype(bf16)
        dy = _dg(dob, _w2d(w_ref), _NT2, False)
        dya_ref[...] = dy[:, :half]
        dyb_ref[...] = dy[:, half:]

        @pl.when(i == 0)
        def _():
            acc_ref[...] = jnp.zeros_like(acc_ref)

        y = jnp.concatenate([ya_ref[...], yb_ref[...]], axis=-1)
        acc_ref[...] += _dg(y, dob, _TN2, False)

        @pl.when(i == nt - 1)
        def _():
            dw_ref[...] = acc_ref[...].astype(bf16)

    row = lambda w_: pl.BlockSpec((tm, w_), lambda i: (i, 0))
    return _pcall(
        body, name="out_proj_bwd", grid=(nt,),
        in_specs=[row(D), row(half), row(half), _resident(w.shape)],
        out_specs=[row(half), row(half), pl.BlockSpec((2 * half, D), lambda i: (0, 0))],
        out_shape=[_sds((T, half), f32), _sds((T, half), f32), _sds((2 * half, D), bf16)],
        scratch_shapes=[pltpu.VMEM((2 * half, D), f32)],
        compiler_params=_params(("arbitrary",), VMEM_MID),
    )(dout, ya, yb, w)


def _conv_taps(x, w):
    K = w.shape[0]
    acc = x * w[K - 1:K, :]
    for i in range(K - 1):
        acc = acc + shift(x, K - 1 - i) * w[i:i + 1, :]
    return acc


def conv_silu_math(w, x):
    return _silu(_conv_taps(x, w))


def gated_conv_math(w, xd, bg, cg):
    return bg * _conv_taps(cg * xd, w)


def seq_chan_fwd(math, w, xs, seq, out_dtype, name):
    T, C = xs[0].shape
    K = w.shape[0]
    nb, nc = T // seq, C // HEAD

    def body(w_ref, *refs):
        o_ref = refs[-1]
        o_ref[...] = math(w_ref[...], *[r[...] for r in refs[:-1]]).astype(out_dtype)

    blk = pl.BlockSpec((seq, HEAD), lambda j, b: (b, j))
    return _pcall(
        body, name=name, grid=(nc, nb),
        in_specs=[pl.BlockSpec((K, HEAD), lambda j, b: (0, j))] + [blk] * len(xs),
        out_specs=blk, out_shape=_sds((T, C), out_dtype),
        compiler_params=_params(("parallel", "parallel"), VMEM_MID),
    )(w, *xs)


def seq_chan_bwd(math, w, xs, dy, seq, dx_dtype, name):
    T, C = xs[0].shape
    K = w.shape[0]
    nb, nc = T // seq, C // HEAD
    nx = len(xs)

    def body(w_ref, *refs):
        x_refs, dy_ref = refs[:nx], refs[nx]
        dx_refs, dw_ref = refs[nx + 1:2 * nx + 1], refs[2 * nx + 1]
        _, vjp = jax.vjp(math, w_ref[...], *[r[...] for r in x_refs])
        grads = vjp(dy_ref[...].astype(f32))

        @pl.when(pl.program_id(1) == 0)
        def _():
            dw_ref[...] = jnp.zeros_like(dw_ref)

        dw_ref[...] += grads[0]
        for r, gx in zip(dx_refs, grads[1:]):
            r[...] = gx.astype(dx_dtype)

    blk = pl.BlockSpec((seq, HEAD), lambda j, b: (b, j))
    wblk = pl.BlockSpec((K, HEAD), lambda j, b: (0, j))
    return _pcall(
        body, name=name, grid=(nc, nb),
        in_specs=[wblk] + [blk] * (nx + 1),
        out_specs=[blk] * nx + [wblk],
        out_shape=[_sds((T, C), dx_dtype)] * nx + [_sds((K, C), f32)],
        compiler_params=_params(("parallel", "arbitrary"), VMEM_MID),
    )(w, *xs, dy)


def pool_group_math(win, ag, pw, scale):
    t = lax.broadcasted_iota(jnp.int32, (ag.shape[0], 1), 0)
    s, k = ag, 1
    while k < win:
        s = s + shift(s, k)
        k *= 2
    pooled = s / jnp.minimum(t + 1, win).astype(f32) - ag
    return mm(pooled, pw) * scale


def pool_fwd(a, pw, scale, seq):
    T, C = a.shape

    def body(a_ref, pw_ref, sc_ref, o_ref):
        for gi, win in enumerate(POOL_WINDOWS):
            cols = slice(gi * HEAD, (gi + 1) * HEAD)
            o_ref[:, cols] = pool_group_math(win, a_ref[:, cols], pw_ref[gi], sc_ref[:, cols]).astype(bf16)

    blk = pl.BlockSpec((seq, C), lambda b: (b, 0))
    return _pcall(
        body, name="pool_fwd", grid=(T // seq,),
        in_specs=[blk, _resident(pw.shape), _resident((1, C))],
        out_specs=blk, out_shape=_sds((T, C), bf16),
        compiler_params=_params(("parallel",), VMEM_MID),
    )(a, pw, scale)


def pool_bwd(a, pw, scale, dy, seq):
    T, C = a.shape

    def body(a_ref, pw_ref, sc_ref, dy_ref, da_ref, dpw_ref, dsc_ref):
        first = pl.program_id(0) == 0

        @pl.when(first)
        def _():
            dpw_ref[...] = jnp.zeros_like(dpw_ref)
            dsc_ref[...] = jnp.zeros_like(dsc_ref)

        for gi, win in enumerate(POOL_WINDOWS):
            cols = slice(gi * HEAD, (gi + 1) * HEAD)
            _, vjp = jax.vjp(functools.partial(pool_group_math, win), a_ref[:, cols], pw_ref[gi], sc_ref[:, cols])
            da, dpw, dsc = vjp(dy_ref[:, cols])
            dpw_ref[gi] += dpw
            dsc_ref[0:1, cols] += dsc
            da_ref[:, cols] = da.astype(bf16)

    blk = pl.BlockSpec((seq, C), lambda b: (b, 0))
    return _pcall(
        body, name="pool_bwd", grid=(T // seq,),
        in_specs=[blk, _resident(pw.shape), _resident((1, C)), blk],
        out_specs=[blk, pl.BlockSpec(pw.shape, lambda b: (0, 0, 0)), pl.BlockSpec((8, C), lambda b: (0, 0))],
        out_shape=[_sds((T, C), bf16), _sds(pw.shape, f32), _sds((8, C), f32)],
        compiler_params=_params(("arbitrary",), VMEM_MID),
    )(a, pw, scale, dy)


SOLVE_HI = False


def _neumann_inverse(lmat):
    n = lmat.shape[-1]
    ii = lax.broadcasted_iota(jnp.int32, (n, n), 0)
    jj = lax.broadcasted_iota(jnp.int32, (n, n), 1)
    inv = jnp.where((ii == jj)[None], 1.0, 0.0) - lmat
    pw_ = _dg(lmat, lmat, _NN3, SOLVE_HI)
    steps = int(math.log2(n)) - 1
    for i in range(steps):
        inv = inv + _dg(inv, pw_, _NN3, SOLVE_HI)
        if i < steps - 1:
            pw_ = _dg(pw_, pw_, _NN3, SOLVE_HI)
    return inv


@jax.custom_vjp
def unit_lower_inverse(lmat):
    return _neumann_inverse(lmat)


def _unit_lower_inverse_fwd(lmat):
    inv = _neumann_inverse(lmat)
    return inv, inv


def _unit_lower_inverse_bwd(inv, ct):
    return (-_dg(_dg(inv, ct, _TN3, SOLVE_HI), inv, _NT3, SOLVE_HI),)


unit_lower_inverse.defvjp(_unit_lower_inverse_fwd, _unit_lower_inverse_bwd)


def dn_prep_math(qkv, bg, alog, dtb):
    tt = qkv.shape[0]
    nt = tt // CHUNK
    nb = nt * N_HEADS
    W = N_HEADS * HEAD
    beta_all = jax.nn.sigmoid(bg)
    g_all = -jnp.exp(alog) * _softplus(bg + dtb)

    def heads(fn):
        return jnp.stack([fn(hd).reshape(nt, CHUNK, HEAD) for hd in range(N_HEADS)], axis=1).reshape(nb, CHUNK, HEAD)

    def l2n(x):
        return x * lax.rsqrt(jnp.sum(x * x, axis=-1, keepdims=True) + EPS)

    q = heads(lambda hd: l2n(qkv[:, hd * HEAD:(hd + 1) * HEAD]) * (HEAD ** -0.5))
    k = heads(lambda hd: l2n(qkv[:, W + hd * HEAD:W + (hd + 1) * HEAD]))
    v = heads(lambda hd: qkv[:, 2 * W + hd * HEAD:2 * W + (hd + 1) * HEAD])
    beta = heads(lambda hd: jnp.broadcast_to(beta_all[:, hd:hd + 1], (tt, HEAD)))
    g = heads(lambda hd: jnp.broadcast_to(g_all[:, N_HEADS + hd:N_HEADS + hd + 1], (tt, HEAD)))

    ii = lax.broadcasted_iota(jnp.int32, (CHUNK, CHUNK), 0)
    jj = lax.broadcasted_iota(jnp.int32, (CHUNK, CHUNK), 1)
    tril, strict = (ii >= jj)[None], (ii > jj)[None]
    ones_b = jnp.ones((nb, CHUNK, CHUNK), f32)
    tril_b = jnp.where(tril, ones_b, 0.0)
    eye_b = jnp.where((ii == jj)[None], ones_b, 0.0)

    gcb = bmm_hi(tril_b, g)
    gcol = gcb[:, :, :CHUNK]
    grow = bmm_hi(ones_b, eye_b * gcol)
    gamma = jnp.where(tril, jnp.exp(jnp.where(tril, gcol - grow, 0.0)), 0.0)
    kb = k * beta
    lmat = jnp.where(strict, bmm_nt(kb, k) * gamma, 0.0)
    inv = unit_lower_inverse(lmat)
    egc = jnp.exp(gcb)
    u = bmm(inv, v * beta)
    w = bmm(inv, kb * egc)
    aqk = bmm_nt(q, k) * gamma
    qd = q * egc
    glast = gcb[:, CHUNK - 1:CHUNK, :]
    kd = k * jnp.exp(glast - gcb)
    last = jnp.exp(glast)
    r4 = lambda x: x.reshape((nt, N_HEADS) + x.shape[1:])
    return r4(u), r4(w), r4(qd), r4(kd), r4(aqk), r4(last)


_PREP_DTYPES = (f32, bf16, bf16, bf16, bf16, f32)


def _prep_specs(nt, T):
    nchunks = T // CHUNK
    shapes = [(HEAD,), (HEAD,), (HEAD,), (HEAD,), (CHUNK,), (HEAD,)]
    rows = [CHUNK, CHUNK, CHUNK, CHUNK, CHUNK, 1]
    specs = [pl.BlockSpec((nt, N_HEADS, r, s[0]), lambda i: (i, 0, 0, 0)) for r, s in zip(rows, shapes)]
    outs = [(nchunks, N_HEADS, r, s[0]) for r, s in zip(rows, shapes)]
    return specs, outs


def dn_prep_fwd(qkv, bg, alog, dtb):
    T = qkv.shape[0]
    tt = _tile(T, 256)
    nt = tt // CHUNK
    specs, shapes = _prep_specs(nt, T)

    def body(qkv_ref, bg_ref, alog_ref, dtb_ref, *outs):
        res = dn_prep_math(qkv_ref[...], bg_ref[...], alog_ref[...], dtb_ref[...])
        for o_ref, r, dt in zip(outs, res, _PREP_DTYPES):
            o_ref[...] = r.astype(dt)

    row = lambda w: pl.BlockSpec((tt, w), lambda i: (i, 0))
    return _pcall(
        body, name="dn_prep_fwd", grid=(T // tt,),
        in_specs=[row(qkv.shape[1]), row(HEAD), _resident((1, HEAD)), _resident((1, HEAD))],
        out_specs=specs, out_shape=[_sds(s, dt) for s, dt in zip(shapes, _PREP_DTYPES)],
        compiler_params=_params(("parallel",), VMEM_BIG),
    )(qkv, bg, alog, dtb)


def dn_prep_bwd(qkv, bg, alog, dtb, cts):
    T = qkv.shape[0]
    tt = _tile(T, 256)
    nt = tt // CHUNK
    specs, _ = _prep_specs(nt, T)

    def body(qkv_ref, bg_ref, alog_ref, dtb_ref, *refs):
        ct_refs, (dqkv_ref, dbg_ref, dalog_ref, ddtb_ref) = refs[:6], refs[6:]
        _, vjp = jax.vjp(dn_prep_math, qkv_ref[...], bg_ref[...], alog_ref[...], dtb_ref[...])
        dqkv, dbg, dalog, ddtb = vjp(tuple(r[...].astype(f32) for r in ct_refs))
        dqkv_ref[...] = dqkv
        dbg_ref[...] = dbg.astype(bf16)
        first = pl.program_id(0) == 0
        _acc_rows(dalog_ref, dalog, first)
        _acc_rows(ddtb_ref, ddtb, first)

    row = lambda w: pl.BlockSpec((tt, w), lambda i: (i, 0))
    small = pl.BlockSpec((8, HEAD), lambda i: (0, 0))
    return _pcall(
        body, name="dn_prep_bwd", grid=(T // tt,),
        in_specs=[row(qkv.shape[1]), row(HEAD), _resident((1, HEAD)), _resident((1, HEAD))] + specs,
        out_specs=[row(qkv.shape[1]), row(HEAD), small, small],
        out_shape=[_sds(qkv.shape, f32), _sds((T, HEAD), bf16), _sds((8, HEAD), f32), _sds((8, HEAD), f32)],
        compiler_params=_params(("arbitrary",), VMEM_BIG),
    )(qkv, bg, alog, dtb, *cts)


def dn_step(state, u, w, qd, kd, aqk, last):
    v_new = u - bmm(w, state)
    o = bmm(qd, state) + bmm(aqk, v_new)
    return state * last + bmm_tn(kd, v_new), o


def dn_gate(o, z, onorm):
    return o * lax.rsqrt(jnp.mean(o * o, axis=-1, keepdims=True) + EPS) * onorm * _silu(z)


DN_HEADS_PER_STEP_FWD, DN_HEADS_PER_STEP_BWD = 4, 2


def _recur_specs(n, hp):
    rows = [CHUNK, CHUNK, CHUNK, CHUNK, CHUNK, 1]
    lanes = [HEAD, HEAD, HEAD, HEAD, CHUNK, HEAD]
    return [pl.BlockSpec((n, hp, r, l), lambda b, p: (b, p, 0, 0)) for r, l in zip(rows, lanes)]


def dn_recur_fwd(prep, z, onorm, seq):
    T, C = z.shape
    n = seq // CHUNK
    nchunks = T // CHUNK
    hp = DN_HEADS_PER_STEP_FWD

    def body(u_ref, w_ref, qd_ref, kd_ref, aqk_ref, last_ref, z_ref, on_ref, y_ref, st_ref, o_ref):
        def step(i, state):
            st_ref[i] = state.astype(bf16)
            new, o = dn_step(state, u_ref[i], w_ref[i], qd_ref[i], kd_ref[i], aqk_ref[i], last_ref[i])
            o_ref[i] = o
            return new

        lax.fori_loop(0, n, step, jnp.zeros((hp, HEAD, HEAD), f32))
        for j in range(hp):
            cols = slice(j * HEAD, (j + 1) * HEAD)
            y_ref[:, cols] = dn_gate(o_ref[:, j].reshape(seq, HEAD), z_ref[:, cols], on_ref[...]).astype(bf16)

    blk = pl.BlockSpec((seq, hp * HEAD), lambda b, p: (b, p))
    per_chunk = lambda r: pl.BlockSpec((n, hp, r, HEAD), lambda b, p: (b, p, 0, 0))
    return _pcall(
        body, name="dn_recur_fwd", grid=(T // seq, N_HEADS // hp),
        in_specs=_recur_specs(n, hp) + [blk, pl.BlockSpec((1, HEAD), lambda b, p: (0, 0))],
        out_specs=[blk, per_chunk(HEAD), per_chunk(CHUNK)],
        out_shape=[_sds((T, C), bf16), _sds((nchunks, N_HEADS, HEAD, HEAD), bf16),
                   _sds((nchunks, N_HEADS, CHUNK, HEAD), f32)],
        compiler_params=_params(("parallel", "parallel"), VMEM_BIG),
    )(*prep, z, onorm)


_RECUR_CT_DTYPES = (f32, bf16, bf16, bf16, bf16, f32)


def dn_recur_bwd(prep, states, o, z, onorm, dy, seq):
    T, C = z.shape
    n = seq // CHUNK
    nchunks = T // CHUNK
    hp = DN_HEADS_PER_STEP_BWD

    def body(u_ref, w_ref, qd_ref, kd_ref, aqk_ref, last_ref, st_ref, o_ref, z_ref, on_ref, dy_ref,
             du_ref, dw_ref, dqd_ref, dkd_ref, daqk_ref, dlast_ref, dz_ref, don_ref, do_ref):
        args = lambda i: tuple(r[i].astype(f32) for r in (st_ref, u_ref, w_ref, qd_ref, kd_ref, aqk_ref, last_ref))
        ct_refs = (du_ref, dw_ref, dqd_ref, dkd_ref, daqk_ref, dlast_ref)
        first = jnp.logical_and(pl.program_id(0) == 0, pl.program_id(1) == 0)
        for j in range(hp):
            cols = slice(j * HEAD, (j + 1) * HEAD)
            _, gate_vjp = jax.vjp(dn_gate, o_ref[:, j].reshape(seq, HEAD), z_ref[:, cols], on_ref[...])
            do, dz, don = gate_vjp(dy_ref[:, cols])
            dz_ref[:, cols] = dz.astype(bf16)
            do_ref[:, j] = do.reshape(n, CHUNK, HEAD)
            _acc_rows(don_ref, don, jnp.logical_and(first, j == 0))

        def bstep(j, dstate):
            i = n - 1 - j
            _, vjp = jax.vjp(dn_step, *args(i))
            ds, *cts = vjp((dstate, do_ref[i]))
            for r, ct, dt in zip(ct_refs, cts, _RECUR_CT_DTYPES):
                r[i] = ct.astype(dt)
            return ds

        lax.fori_loop(0, n, bstep, jnp.zeros((hp, HEAD, HEAD), f32))

    blk = pl.BlockSpec((seq, hp * HEAD), lambda b, p: (b, p))
    per_chunk = lambda r: pl.BlockSpec((n, hp, r, HEAD), lambda b, p: (b, p, 0, 0))
    rows = [CHUNK, CHUNK, CHUNK, CHUNK, CHUNK, 1]
    lanes = [HEAD, HEAD, HEAD, HEAD, CHUNK, HEAD]
    return _pcall(
        body, name="dn_recur_bwd", grid=(T // seq, N_HEADS // hp),
        in_specs=_recur_specs(n, hp) + [per_chunk(HEAD), per_chunk(CHUNK), blk,
                                        pl.BlockSpec((1, HEAD), lambda b, p: (0, 0)), blk],
        out_specs=_recur_specs(n, hp) + [blk, pl.BlockSpec((8, HEAD), lambda b, p: (0, 0))],
        out_shape=[_sds((nchunks, N_HEADS, r, l), dt) for r, l, dt in zip(rows, lanes, _RECUR_CT_DTYPES)]
        + [_sds((T, C), bf16), _sds((8, HEAD), f32)],
        scratch_shapes=[pltpu.VMEM((n, hp, CHUNK, HEAD), f32)],
        compiler_params=_params(("arbitrary", "arbitrary"), VMEM_BIG),
    )(*prep, states, o, z, onorm, dy)


def sgu_math(up, vp, ng, nb, sw, sbias):
    S = up.shape[0]
    nblk = S // SGU_BLOCK
    u = jax.nn.gelu(up, approximate=True)
    v = jax.nn.gelu(vp, approximate=True)
    xc = v - jnp.mean(v, axis=-1, keepdims=True)
    vn = xc * lax.rsqrt(jnp.mean(xc * xc, axis=-1, keepdims=True) + EPS) * ng + nb
    ii = lax.broadcasted_iota(jnp.int32, (SGU_BLOCK, SGU_BLOCK), 0)
    jj = lax.broadcasted_iota(jnp.int32, (SGU_BLOCK, SGU_BLOCK), 1)
    outs = []
    for hd in range(N_HEADS):
        vh = vn[:, hd * HEAD:(hd + 1) * HEAD].reshape(nblk, SGU_BLOCK, HEAD)
        ws = jnp.where(ii >= jj, sw[hd], 0.0)
        mixed = bmm(jnp.broadcast_to(ws[None], (nblk, SGU_BLOCK, SGU_BLOCK)), vh) + sbias[hd][None]
        outs.append(mixed.reshape(S, HEAD))
    return u * jnp.concatenate(outs, axis=-1)


def sgu_fwd(up, vp, ng, nb, sw, sbias, seq):
    T, C = up.shape

    def body(up_ref, vp_ref, ng_ref, nb_ref, sw_ref, sb_ref, o_ref):
        o_ref[...] = sgu_math(up_ref[...], vp_ref[...], ng_ref[...], nb_ref[...], sw_ref[...],
                              sb_ref[...]).astype(bf16)

    blk = pl.BlockSpec((seq, C), lambda b: (b, 0))
    return _pcall(
        body, name="sgu_fwd", grid=(T // seq,),
        in_specs=[blk, blk, _resident((1, C)), _resident((1, C)), _resident(sw.shape), _resident(sbias.shape)],
        out_specs=blk, out_shape=_sds((T, C), bf16),
        compiler_params=_params(("parallel",), VMEM_BIG),
    )(up, vp, ng, nb, sw, sbias)


def sgu_bwd(up, vp, ng, nb, sw, sbias, dy, seq):
    T, C = up.shape

    def body(up_ref, vp_ref, ng_ref, nb_ref, sw_ref, sb_ref, dy_ref,
             dup_ref, dvp_ref, dng_ref, dnb_ref, dsw_ref, dsb_ref):
        _, vjp = jax.vjp(sgu_math, up_ref[...], vp_ref[...], ng_ref[...], nb_ref[...], sw_ref[...], sb_ref[...])
        dup, dvp, dng, dnb, dsw, dsb = vjp(dy_ref[...])
        dup_ref[...] = dup.astype(bf16)
        dvp_ref[...] = dvp.astype(bf16)
        first = pl.program_id(0) == 0
        _acc_rows(dng_ref, dng, first)
        _acc_rows(dnb_ref, dnb, first)

        @pl.when(first)
        def _():
            dsw_ref[...] = jnp.zeros_like(dsw_ref)
            dsb_ref[...] = jnp.zeros_like(dsb_ref)

        dsw_ref[...] += dsw
        dsb_ref[...] += dsb

    blk = pl.BlockSpec((seq, C), lambda b: (b, 0))
    small = pl.BlockSpec((8, C), lambda b: (0, 0))
    return _pcall(
        body, name="sgu_bwd", grid=(T // seq,),
        in_specs=[blk, blk, _resident((1, C)), _resident((1, C)), _resident(sw.shape), _resident(sbias.shape), blk],
        out_specs=[blk, blk, small, small, pl.BlockSpec(sw.shape, lambda b: (0, 0, 0)),
                   pl.BlockSpec(sbias.shape, lambda b: (0, 0, 0))],
        out_shape=[_sds((T, C), bf16), _sds((T, C), bf16), _sds((8, C), f32), _sds((8, C), f32),
                   _sds(sw.shape, f32), _sds(sbias.shape, f32)],
        compiler_params=_params(("arbitrary",), VMEM_BIG),
    )(up, vp, ng, nb, sw, sbias, dy)


def loss_fwd_bwd(h, gain, target):
    T, D = h.shape
    tm = _tile(T, 512)

    def body(h_ref, gain_ref, t_ref, loss_ref, dh_ref, dgain_ref):
        gain_ = gain_ref[...]
        y, xhat, rstd = _rms_fwd(h_ref[...], gain_)
        err = y - t_ref[...]
        part = 0.5 * jnp.sum(jnp.mean(err * err, axis=-1, keepdims=True), axis=0, keepdims=True)
        dh, dgain = _rms_bwd(err * (1.0 / D), xhat, rstd, gain_)
        dh_ref[...] = dh
        first = pl.program_id(0) == 0
        _acc_rows(dgain_ref, dgain, first)

        @pl.when(first)
        def _():
            loss_ref[...] = jnp.zeros_like(loss_ref)

        loss_ref[...] += jnp.broadcast_to(part, loss_ref.shape)

    row = pl.BlockSpec((tm, D), lambda i: (i, 0))
    return _pcall(
        body, name="loss_fwd_bwd", grid=(T // tm,),
        in_specs=[row, _resident((1, D)), row],
        out_specs=[pl.BlockSpec((8, 128), lambda i: (0, 0)), row, pl.BlockSpec((8, D), lambda i: (0, 0))],
        out_shape=[_sds((8, 128), f32), _sds((T, D), f32), _sds((8, D), f32)],
        compiler_params=_params(("arbitrary",), VMEM_MID),
    )(h, gain, target)


def adamw(w, g, m, v):
    c1, c2 = 1.0 - ADAM_B1 ** ADAM_STEP, 1.0 - ADAM_B2 ** ADAM_STEP
    if w.ndim == 3:
        grid, blk = (w.shape[0],), pl.BlockSpec((1,) + w.shape[1:], lambda i: (i, 0, 0))
    else:
        R, C = w.shape
        tr = max(c for c in range(8, 513, 8) if R % c == 0)
        grid, blk = (R // tr,), pl.BlockSpec((tr, C), lambda i: (i, 0))

    def body(w_ref, g_ref, m_ref, v_ref, d_ref, nm_ref, nv_ref):
        gg = g_ref[...]
        nm = ADAM_B1 * m_ref[...] + (1.0 - ADAM_B1) * gg
        nv = ADAM_B2 * v_ref[...] + (1.0 - ADAM_B2) * (gg * gg)
        d_ref[...] = -ADAM_LR * ((nm / c1) / (jnp.sqrt(nv / c2) + ADAM_EPS) + ADAM_WD * w_ref[...])
        nm_ref[...] = nm
        nv_ref[...] = nv

    return _pcall(
        body, name="adamw", grid=grid, in_specs=[blk] * 4, out_specs=[blk] * 3,
        out_shape=[_sds(w.shape, f32)] * 3, compiler_params=_params(("parallel",), VMEM_MID),
    )(w, g, m, v)


def sum8(parts):
    _, R, C = parts.shape
    tr = R
    for cand in (512, 352, 336, 320, 256, 128):
        if R % cand == 0:
            tr = cand
            break

    def body(p_ref, o_ref):
        acc = p_ref[0].astype(f32)
        for i in range(1, N_DEV):
            acc = acc + p_ref[i].astype(f32)
        o_ref[...] = acc

    return _pcall(
        body, name="sum8", grid=(R // tr,),
        in_specs=[pl.BlockSpec((N_DEV, tr, C), lambda i: (0, i, 0))],
        out_specs=pl.BlockSpec((tr, C), lambda i: (i, 0)), out_shape=_sds((R, C), f32),
        compiler_params=_params(("parallel",), VMEM_MID),
    )(parts)


_FLIPS = [(fx, fy, fc) for fx in (0, 1) for fy in (0, 1) for fc in (0, 1)][1:]
_HBM = pl.BlockSpec(memory_space=pltpu.HBM)


def _me():
    return lax.axis_index("x"), lax.axis_index("y"), lax.axis_index("c")


def _peer(flip):
    x, y, c = _me()
    fx, fy, fc = flip
    return (1 - x if fx else x, 1 - y if fy else y, 1 - c if fc else c)


def _lin(dev):
    return 4 * dev[0] + 2 * dev[1] + dev[2]


def _src_block(ref, rows, dev, whole):
    return ref if whole else ref.at[pl.ds(pl.multiple_of(dev * rows, 16), rows)]


_SEM = pl.BlockSpec(memory_space=pltpu.SEMAPHORE)
_EFFECT = pltpu.SideEffectType.DATAFLOW_SIDE_EFFECTING


def push_start(srcs, whole, after, name):
    n = len(srcs)
    rows = [s.shape[0] if whole else s.shape[0] // N_DEV for s in srcs]
    land_shapes = [(N_DEV, r, s.shape[1]) for r, s in zip(rows, srcs)]

    def body(*refs):
        src_refs, land_refs = refs[:n], refs[n:2 * n]
        send_sems, recv_sems, own_sems, token = refs[2 * n + 1], refs[2 * n + 2], refs[2 * n + 3], refs[-1]
        me = _lin(_me())
        for i in range(n):
            for k, flip in enumerate(_FLIPS):
                peer = _peer(flip)
                pltpu.make_async_remote_copy(
                    src_ref=_src_block(src_refs[i], rows[i], _lin(peer), whole), dst_ref=land_refs[i].at[me],
                    send_sem=send_sems.at[i * 7 + k], recv_sem=recv_sems.at[i * 7 + k],
                    device_id=peer, device_id_type=MESH).start()
        for i in range(n):
            pltpu.make_async_copy(_src_block(src_refs[i], rows[i], me, whole), land_refs[i].at[me],
                                  own_sems.at[i]).start()
        token[...] = jnp.zeros_like(token)

    hbm = lambda a: pltpu.with_memory_space_constraint(a, pltpu.HBM)
    outs = _pcall(
        body, name=name,
        in_specs=[_HBM] * (2 * n) + [pl.BlockSpec(memory_space=pl.ANY)],
        out_specs=[_SEM, _SEM, _SEM] + [_HBM] * (2 * n) + [pl.BlockSpec(memory_space=pltpu.VMEM)],
        out_shape=[pltpu.SemaphoreType.DMA((7 * n,)), pltpu.SemaphoreType.DMA((7 * n,)),
                   pltpu.SemaphoreType.DMA((n,))]
        + [pltpu.HBM(s.shape, s.dtype) for s in srcs]
        + [pltpu.HBM(shp, s.dtype) for shp, s in zip(land_shapes, srcs)] + [_sds((8, 128), f32)],
        input_output_aliases={i: 3 + i for i in range(2 * n)},
        compiler_params=pltpu.CompilerParams(has_side_effects=_EFFECT),
    )(*[hbm(s) for s in srcs], *[hbm(lax.empty(shp, s.dtype)) for shp, s in zip(land_shapes, srcs)], after)
    return outs[0], outs[1], outs[2], list(outs[3:3 + n]), list(outs[3 + n:3 + 2 * n]), outs[-1]


def push_wait(handle, whole, after, name):
    send_sems, recv_sems, own_sems, srcs, lands, _ = handle
    n = len(srcs)
    rows = [l.shape[1] for l in lands]

    def body(*refs):
        src_refs, land_refs = refs[:n], refs[n:2 * n]
        send_sems_, recv_sems_, own_sems_ = refs[2 * n], refs[2 * n + 1], refs[2 * n + 2]
        me = _lin(_me())
        for i in range(n):
            for k, flip in enumerate(_FLIPS):
                peer = _peer(flip)
                cp = pltpu.make_async_remote_copy(
                    src_ref=_src_block(src_refs[i], rows[i], _lin(peer), whole), dst_ref=land_refs[i].at[_lin(peer)],
                    send_sem=send_sems_.at[i * 7 + k], recv_sem=recv_sems_.at[i * 7 + k],
                    device_id=peer, device_id_type=MESH)
                cp.wait_send()
                cp.wait_recv()
            pltpu.make_async_copy(_src_block(src_refs[i], rows[i], me, whole), land_refs[i].at[me],
                                  own_sems_.at[i]).wait()

    outs = _pcall(
        body, name=name,
        in_specs=[_HBM] * (2 * n) + [_SEM, _SEM, _SEM, pl.BlockSpec(memory_space=pl.ANY)],
        out_specs=[_HBM] * (2 * n),
        out_shape=[pltpu.HBM(a.shape, a.dtype) for a in srcs + lands],
        input_output_aliases={i: i for i in range(2 * n)},
        compiler_params=pltpu.CompilerParams(has_side_effects=_EFFECT),
    )(*srcs, *lands, send_sems, recv_sems, own_sems, after)
    return list(outs[n:])


def all_gather_small(x):
    R, C = x.shape

    def body(x_ref, o_ref, send_sems, recv_sems):
        me = _lin(_me())
        o_ref[me] = x_ref[...]
        sends = []
        for k, flip in enumerate(_FLIPS):
            rc = pltpu.make_async_remote_copy(
                src_ref=x_ref, dst_ref=o_ref.at[me], send_sem=send_sems.at[k], recv_sem=recv_sems.at[k],
                device_id=_peer(flip), device_id_type=MESH)
            rc.start()
            sends.append(rc)
        for k, flip in enumerate(_FLIPS):
            pltpu.make_async_remote_copy(
                src_ref=x_ref, dst_ref=o_ref.at[_lin(_peer(flip))], send_sem=send_sems.at[k],
                recv_sem=recv_sems.at[k], device_id=_peer(flip), device_id_type=MESH).wait_recv()
        for rc in sends:
            rc.wait_send()

    vm = pl.BlockSpec(memory_space=pltpu.VMEM)
    return _pcall(
        body, name="all_gather_small", in_specs=[vm], out_specs=vm, out_shape=_sds((N_DEV, R, C), x.dtype),
        scratch_shapes=[pltpu.SemaphoreType.DMA((7,)), pltpu.SemaphoreType.DMA((7,))],
        compiler_params=_params(None, VMEM_MID),
    )(x)


def sum_slots(parts):
    _, R, C = parts.shape

    def body(p_ref, o_ref):
        acc = p_ref[0]
        for p in range(1, N_DEV):
            acc = acc + p_ref[p]
        o_ref[...] = acc

    vm = pl.BlockSpec(memory_space=pltpu.VMEM)
    return _pcall(body, name="sum_slots", in_specs=[vm], out_specs=vm, out_shape=_sds((R, C), f32),
                  compiler_params=_params(None, VMEM_MID))(parts)


_PACK_ROWS = 8


def _packed_rows(shape):
    return -(-math.prod(shape) // (128 * _PACK_ROWS)) * _PACK_ROWS


def _pack(arrs):
    parts = []
    for a in arrs:
        flat = a.reshape(-1).astype(f32)
        rows = _packed_rows(a.shape)
        parts.append(jnp.pad(flat, (0, rows * 128 - flat.shape[0])).reshape(rows, 128))
    return jnp.concatenate(parts, axis=0)


def _unpack(buf, shapes):
    out, off = [], 0
    for s in shapes:
        rows = _packed_rows(s)
        out.append(buf[off:off + rows].reshape(-1)[:math.prod(s)].reshape(s))
        off += rows
    return out


def _row(v):
    return v.reshape(1, -1)


def _lane_row(vals, offset):
    return jnp.pad(vals.reshape(1, -1), ((0, 0), (offset, HEAD - offset - vals.shape[-1])))


def kernel(x, ffn1_norm, ffn1_w_gate, ffn1_w_up, ffn1_w_down, mix_norm, ffn2_norm, ffn2_w_gate, ffn2_w_up, ffn2_w_down, ab_w_in, pool_w, pool_scale, dn_conv_w, dn_a_log, dn_dt_bias, dn_out_norm, ab_w_out, cd_w_in, sgu_norm_g, sgu_norm_b, sgu_w, sgu_bias, sc_conv_w, cd_w_out, final_norm, loss_target, m_ffn1_norm, m_ffn1_w_gate, m_ffn1_w_up, m_ffn1_w_down, m_mix_norm, m_ffn2_norm, m_ffn2_w_gate, m_ffn2_w_up, m_ffn2_w_down, m_ab_w_in, m_pool_w, m_pool_scale, m_dn_conv_w, m_dn_a_log, m_dn_dt_bias, m_dn_out_norm, m_ab_w_out, m_cd_w_in, m_sgu_norm_g, m_sgu_norm_b, m_sgu_w, m_sgu_bias, m_sc_conv_w, m_cd_w_out, m_final_norm, v_ffn1_norm, v_ffn1_w_gate, v_ffn1_w_up, v_ffn1_w_down, v_mix_norm, v_ffn2_norm, v_ffn2_w_gate, v_ffn2_w_up, v_ffn2_w_down, v_ab_w_in, v_pool_w, v_pool_scale, v_dn_conv_w, v_dn_a_log, v_dn_dt_bias, v_dn_out_norm, v_ab_w_out, v_cd_w_in, v_sgu_norm_g, v_sgu_norm_b, v_sgu_w, v_sgu_bias, v_sc_conv_w, v_cd_w_out, v_final_norm):
    names = ['ffn1_norm', 'ffn1_w_gate', 'ffn1_w_up', 'ffn1_w_down', 'mix_norm', 'ffn2_norm', 'ffn2_w_gate',
             'ffn2_w_up', 'ffn2_w_down', 'ab_w_in', 'pool_w', 'pool_scale', 'dn_conv_w', 'dn_a_log', 'dn_dt_bias',
             'dn_out_norm', 'ab_w_out', 'cd_w_in', 'sgu_norm_g', 'sgu_norm_b', 'sgu_w', 'sgu_bias', 'sc_conv_w',
             'cd_w_out', 'final_norm']
    loc = locals()
    W = {n: loc[n] for n in names}
    M = {n: loc['m_' + n] for n in names}
    V = {n: loc['v_' + n] for n in names}

    B, S, D = x.shape
    T = B * S
    me = _lin(_me())

    def rows_of(w):
        return w.astype(bf16).T

    def layer_shards(layer):
        e = layer // 2
        shards = [rows_of(W['ffn1_w_gate'][layer]), rows_of(W['ffn1_w_up'][layer]), W['ffn1_w_down'][layer].astype(bf16)]
        if layer % 2 == 0:
            win = jnp.pad(rows_of(W['ab_w_in'][e]), ((0, AB_SHARD_PAD - AB_SHARD), (0, 0)))
            wout = W['ab_w_out'][e].astype(bf16)
        else:
            win = rows_of(W['cd_w_in'][e])
            wout = W['cd_w_out'][e].astype(bf16)
        shards += [win, wout]
        shards += [rows_of(W['ffn2_w_gate'][layer]), rows_of(W['ffn2_w_up'][layer]), W['ffn2_w_down'][layer].astype(bf16)]
        return shards

    def full_weight(idx, land, layer):
        if idx == 3 and layer % 2 == 0:
            return jnp.pad(land[:, :AB_SHARD].reshape(AB_IN, D), ((0, AB_IN_PAD - AB_IN), (0, 0)))
        return land

    def tied(gain, token):
        return gain if token is None else gain + token[0:1, 0:1]

    def start_groups(arrays, groups, whole, after, name):
        out = []
        for gi, idx in enumerate(groups):
            suffix = "" if len(groups) == 1 else "abc"[gi]
            handle = push_start([arrays[i] for i in idx], whole, after, name=name + suffix)
            after = handle[5]
            out.append((idx, handle, suffix))
        return out

    def wait_group(inflight, gi, whole, after, name):
        idx, handle, suffix = inflight[gi]
        return dict(zip(idx, push_wait(handle, whole, after, name=name + suffix)))

    one_group = [tuple(range(8))]
    by_block = [(0, 1, 2), (3, 4), (5, 6, 7)]
    zeros_tile = jnp.zeros((8, 128), f32)
    gathered = [None] * DEPTH
    small_shards = [W['dn_conv_w'], W['sgu_norm_g'], W['sgu_norm_b'], W['sc_conv_w']]
    gs = all_gather_small(_pack(small_shards))
    per_dev = [_unpack(gs[p], [a.shape for a in small_shards]) for p in range(N_DEV)]
    dn_conv_full, sgu_g_full, sgu_b_full, sc_conv_full = [
        jnp.concatenate([per_dev[p][i] for p in range(N_DEV)], axis=-1) for i in range(4)]

    inflight = start_groups(layer_shards(0), by_block, True, gs, "gather_start_0")

    h = x.reshape(T, D)
    saved = []
    for layer in range(DEPTH):
        e = layer // 2
        mine, landed = inflight, {}
        landed.update(wait_group(mine, 0, True, h if layer else mine[-1][1][5], f"gather_wait_{layer}"))
        token = None
        if layer + 1 < DEPTH:
            inflight = start_groups(layer_shards(layer + 1), one_group, True, landed[0], f"gather_start_{layer + 1}")
            token = inflight[-1][1][5]
        wg1, wu1, wd1 = [full_weight(i, landed[i], layer) for i in (0, 1, 2)]
        sv = {'h0': h}
        h, sv['g1'], sv['u1'] = ffn_fwd(h, tied(_row(W['ffn1_norm'][layer]), token), wg1, wu1, wd1)
        sv['h1'] = h
        if len(mine) > 1:
            landed.update(wait_group(mine, 1, True, h, f"gather_wait_{layer}"))
        win, wout = [full_weight(i, landed[i], layer) for i in (3, 4)]
        if layer % 2 == 0:
            a_in, qkv_pre, z, bg = in_proj_fwd(h, _row(W['mix_norm'][layer]), win, [512, 1536, 512, 128])
            ya = pool_fwd(a_in, W['pool_w'][e], _row(W['pool_scale'][e]), S)
            qkv = seq_chan_fwd(conv_silu_math, dn_conv_full[e], [qkv_pre], S, f32, "dn_conv_fwd")
            alog, dtb = _lane_row(W['dn_a_log'][e], N_HEADS), _lane_row(W['dn_dt_bias'][e], N_HEADS)
            prep = dn_prep_fwd(qkv, bg, alog, dtb)
            yb, states, o_dn = dn_recur_fwd(prep, z, _row(W['dn_out_norm'][e]), S)
            sv.update(a_in=a_in, qkv_pre=qkv_pre, z=z, bg=bg, qkv=qkv, alog=alog, dtb=dtb, prep=prep,
                      states=states, o_dn=o_dn)
        else:
            up, vp, xd, bgate, cg = in_proj_fwd(h, _row(W['mix_norm'][layer]), win, [512] * 5)
            sbias = W['sgu_bias'][e].reshape(N_HEADS, SGU_BLOCK, 1)
            ya = sgu_fwd(up, vp, _row(sgu_g_full[e]), _row(sgu_b_full[e]), W['sgu_w'][e], sbias,
                         _tile(S, SGU_TILE))
            yb = seq_chan_fwd(gated_conv_math, sc_conv_full[e], [xd, bgate, cg], S, bf16, "sc_conv_fwd")
            sv.update(up=up, vp=vp, xd=xd, bgate=bgate, cg=cg, sbias=sbias)
        sv.update(ya=ya, yb=yb)
        h = out_proj_fwd(h, ya, yb, wout)
        sv['h2'] = h
        if len(mine) > 1:
            landed.update(wait_group(mine, 2, True, h, f"gather_wait_{layer}"))
        wg2, wu2, wd2 = [full_weight(i, landed[i], layer) for i in (5, 6, 7)]
        h, sv['g2'], sv['u2'] = ffn_fwd(h, _row(W['ffn2_norm'][layer]), wg2, wu2, wd2)
        gathered[layer] = [wg1, wu1, wd1, win, wout, wg2, wu2, wd2]
        saved.append(sv)

    loss_part, dh, dfinal = loss_fwd_bwd(h, _row(W['final_norm']), loss_target.reshape(T, D))
    loss = lax.psum(loss_part[0, 0], ("x", "y", "c"))

    G = {}
    G['final_norm'] = dfinal[0]
    for n in ('ffn1_norm', 'mix_norm', 'ffn2_norm'):
        G[n] = [None] * DEPTH
    for n in ('pool_w', 'pool_scale', 'dn_conv_w', 'dn_a_log', 'dn_dt_bias', 'dn_out_norm',
              'sgu_norm_g', 'sgu_norm_b', 'sgu_w', 'sgu_bias', 'sc_conv_w'):
        G[n] = [None] * 2
    small_names = ['ffn1_norm', 'mix_norm', 'ffn2_norm', 'pool_w', 'pool_scale', 'dn_conv_w', 'dn_a_log',
                   'dn_dt_bias', 'dn_out_norm', 'sgu_norm_g', 'sgu_norm_b', 'sgu_w', 'sgu_bias', 'sc_conv_w',
                   'final_norm']
    big = [None] * DEPTH
    inflight = None
    last = []

    for layer in reversed(range(DEPTH)):
        e = layer // 2
        sv = saved[layer]
        wg1, wu1, wd1, win, wout, wg2, wu2, wd2 = gathered[layer]
        token = None if inflight is None else inflight[-1][1][5]
        dh, dgain, xn, act, dg, du, dy = ffn_bwd(sv['h2'], tied(_row(W['ffn2_norm'][layer]), token), sv['g2'],
                                                  sv['u2'], dh, wg2, wu2, wd2)
        G['ffn2_norm'][layer] = dgain[0]
        dwg2, dwu2, dwd2 = wgrad(dg, xn, zeros_tile), wgrad(du, xn, zeros_tile), wgrad(act, dy, zeros_tile)
        mix_token = ffn1_token = None
        if layer == 0:
            big[1] = wait_group(inflight, 0, False, dwd2, "scatter_wait_1")
            last += start_groups({5: dwg2, 6: dwu2, 7: dwd2}, [(5, 6, 7)], False, big[1][0], "scatter_start_0a")
            mix_token = last[-1][1][5]
        dya, dyb, dwout = out_proj_bwd(dh, sv['ya'], sv['yb'], wout)
        if layer % 2 == 0:
            da, dpw, dsc = pool_bwd(sv['a_in'], W['pool_w'][e], _row(W['pool_scale'][e]), dya, S)
            G['pool_w'][e], G['pool_scale'][e] = dpw, dsc[0]
            *cts, dz, don = dn_recur_bwd(sv['prep'], sv['states'], sv['o_dn'], sv['z'],
                                         _row(W['dn_out_norm'][e]), dyb, S)
            G['dn_out_norm'][e] = don[0]
            dqkv, dbg, dalog, ddtb = dn_prep_bwd(sv['qkv'], sv['bg'], sv['alog'], sv['dtb'], cts)
            G['dn_a_log'][e], G['dn_dt_bias'][e] = dalog[0, N_HEADS:2 * N_HEADS], ddtb[0, N_HEADS:2 * N_HEADS]
            dqkv_pre, dconv = seq_chan_bwd(conv_silu_math, dn_conv_full[e], [sv['qkv_pre']], dqkv, S, bf16,
                                           "dn_conv_bwd")
            G['dn_conv_w'][e] = dconv
            dpieces = [da, dqkv_pre, dz, dbg]
        else:
            dup, dvp, dng, dnb, dsw, dsb = sgu_bwd(sv['up'], sv['vp'], _row(sgu_g_full[e]), _row(sgu_b_full[e]),
                                                   W['sgu_w'][e], sv['sbias'], dya, _tile(S, SGU_TILE))
            G['sgu_norm_g'][e], G['sgu_norm_b'][e] = dng[0], dnb[0]
            G['sgu_w'][e], G['sgu_bias'][e] = dsw, dsb.reshape(N_HEADS, SGU_BLOCK)
            dxd, dbgate, dcg, dscw = seq_chan_bwd(gated_conv_math, sc_conv_full[e],
                                                  [sv['xd'], sv['bgate'], sv['cg']], dyb, S, bf16, "sc_conv_bwd")
            G['sc_conv_w'][e] = dscw
            dpieces = [dup, dvp, dxd, dbgate, dcg]
        dh, dgain, dwin = in_proj_bwd(sv['h1'], tied(_row(W['mix_norm'][layer]), mix_token), dpieces, dh, win)
        G['mix_norm'][layer] = dgain[0]
        if layer % 2 == 0:
            dwin = jnp.pad(dwin[:AB_IN].reshape(N_DEV, AB_SHARD, D), ((0, 0), (0, AB_SHARD_PAD - AB_SHARD), (0, 0)))
            dwin = dwin.reshape(N_DEV * AB_SHARD_PAD, D)
        if layer == 0:
            last += start_groups({3: dwin, 4: dwout}, [(3, 4)], False, mix_token, "scatter_start_0b")
            ffn1_token = last[-1][1][5]
        dh, dgain, xn, act, dg, du, dy = ffn_bwd(sv['h0'], tied(_row(W['ffn1_norm'][layer]), ffn1_token), sv['g1'],
                                                  sv['u1'], dh, wg1, wu1, wd1)
        G['ffn1_norm'][layer] = dgain[0]
        order = zeros_tile
        if layer == 0:
            small_g = [G[n] if n == 'final_norm' else jnp.stack(G[n]) for n in small_names]
            small_x = start_groups([_pack(small_g)], [(0,)], True, ffn1_token, "small_grads_start")
            order = small_x[-1][1][5]
        dwg1, dwu1, dwd1 = wgrad(dg, xn, order), wgrad(du, xn, order), wgrad(act, dy, order)
        if layer == 0:
            last += start_groups({0: dwg1, 1: dwu1, 2: dwd1}, [(0, 1, 2)], False, order, "scatter_start_0c")
        else:
            after = zeros_tile
            if inflight is not None:
                big[layer + 1] = wait_group(inflight, 0, False, dh, f"scatter_wait_{layer + 1}")
                after = big[layer + 1][0]
            inflight = start_groups([dwg1, dwu1, dwd1, dwin, dwout, dwg2, dwu2, dwd2], one_group, False, after,
                                    f"scatter_start_{layer}")

    grad_x = dh.reshape(B, S, D)

    reduced_buf = sum_slots(wait_group(small_x, 0, True, last[-1][1][5], "small_grads_wait")[0])
    reduced = _unpack(reduced_buf, [a.shape for a in small_g])
    big[0] = wait_group(last, 0, False, reduced_buf, "scatter_wait_0a")
    big[0].update(wait_group(last, 1, False, big[0][5], "scatter_wait_0b"))
    grads = {}
    for n, g in zip(small_names, reduced):
        if n in ('dn_conv_w', 'sgu_norm_g', 'sgu_norm_b', 'sc_conv_w'):
            c = W[n].shape[-1]
            g = lax.dynamic_slice_in_dim(g, me * c, c, axis=g.ndim - 1)
        grads[n] = g

    def stack_layers(idx, sel):
        out = []
        for layer in sel:
            g = sum8(big[layer][idx])
            out.append(g[:AB_SHARD] if idx == 3 and layer % 2 == 0 else g)
        return jnp.stack(out)

    all_layers, even, odd = range(DEPTH), range(0, DEPTH, 2), range(1, DEPTH, 2)
    delta, new_m, new_v = {}, {}, {}

    def update(n, idx, transposed, sel):
        view = (lambda a: jnp.swapaxes(a, 1, 2)) if transposed else (lambda a: a)
        g = stack_layers(idx, sel)
        d_, m_, v_ = adamw(view(W[n]), g, view(M[n]), view(V[n]))
        grads[n], delta[n], new_m[n], new_v[n] = view(g), view(d_), view(m_), view(v_)

    shapes = [W[n].shape for n in small_names]
    d_, m_, v_ = adamw(_pack([W[n] for n in small_names]), _pack([grads[n] for n in small_names]),
                       _pack([M[n] for n in small_names]), _pack([V[n] for n in small_names]))
    for n, a, b_, c_ in zip(small_names, _unpack(d_, shapes), _unpack(m_, shapes), _unpack(v_, shapes)):
        delta[n], new_m[n], new_v[n] = a, b_, c_
    update('ab_w_in', 3, True, even)
    update('cd_w_in', 3, True, odd)
    update('ab_w_out', 4, False, even)
    update('cd_w_out', 4, False, odd)
    update('ffn2_w_gate', 5, True, all_layers)
    update('ffn2_w_up', 6, True, all_layers)
    update('ffn2_w_down', 7, False, all_layers)
    updated = sum(lax.slice(a, (0,) * a.ndim, (1,) * a.ndim).reshape(1) for a in new_v.values())
    big[0].update(wait_group(last, 2, False, updated, "scatter_wait_0c"))
    update('ffn1_w_gate', 0, True, all_layers)
    update('ffn1_w_up', 1, True, all_layers)
    update('ffn1_w_down', 2, False, all_layers)

    return (loss, grad_x, *[grads[n] for n in names], *[delta[n] for n in names],
            *[new_m[n] for n in names], *[new_v[n] for n in names])
```

```python
import functools
import math

import jax
import jax.numpy as jnp
from jax import lax
from jax.experimental import pallas as pl
from jax.experimental.pallas import tpu as pltpu

f32, bf16 = jnp.float32, jnp.bfloat16

D_MODEL = 1024
DEPTH = 4
CHUNK = 64
POOL_WINDOWS = (2, 4, 8, 16)
HEAD = 128
N_HEADS = 4
SGU_BLOCK = 128
SGU_TILE = 512
FFN_DIM = 2816
AB_IN = 2568
AB_IN_PAD = 2688
AB_SHARD = 321
AB_SHARD_PAD = 336
EPS = 1e-6
N_DEV = 8
MESH = pl.DeviceIdType.MESH

ADAM_LR, ADAM_B1, ADAM_B2, ADAM_EPS, ADAM_WD, ADAM_STEP = 0.001, 0.9, 0.999, 1e-08, 0.01, 10

VMEM_BIG = 56 * 1024 * 1024
VMEM_MID = 40 * 1024 * 1024


def _pcall(body, **kw):
    return pl.pallas_call(body, **kw)


def _params(sem=None, vmem=None):
    return pltpu.CompilerParams(dimension_semantics=sem, vmem_limit_bytes=vmem)


def _sds(shape, dtype):
    return jax.ShapeDtypeStruct(shape, dtype)


_NN2, _NT2, _TN2 = (((1,), (0,)), ((), ())), (((1,), (1,)), ((), ())), (((0,), (0,)), ((), ()))
_NN3, _NT3, _TN3 = (((2,), (1,)), ((0,), (0,))), (((2,), (2,)), ((0,), (0,))), (((1,), (1,)), ((0,), (0,)))


def _dg(a, b, dims, hi):
    if hi:
        return lax.dot_general(a.astype(f32), b.astype(f32), dims, preferred_element_type=f32,
                               precision=lax.Precision.HIGH)
    return lax.dot_general(a.astype(bf16), b.astype(bf16), dims, preferred_element_type=f32)


def _make_mm(nn, nt, tn, hi):
    @jax.custom_vjp
    def mm(a, b):
        return _dg(a, b, nn, hi)

    def mm_bwd(res, ct):
        a, b = res
        return _dg(ct, b, nt, hi).astype(a.dtype), _dg(a, ct, tn, hi).astype(b.dtype)

    mm.defvjp(lambda a, b: (_dg(a, b, nn, hi), (a, b)), mm_bwd)

    @jax.custom_vjp
    def mm_nt(a, b):
        return _dg(a, b, nt, hi)

    def mm_nt_bwd(res, ct):
        a, b = res
        return _dg(ct, b, nn, hi).astype(a.dtype), _dg(ct, a, tn, hi).astype(b.dtype)

    mm_nt.defvjp(lambda a, b: (_dg(a, b, nt, hi), (a, b)), mm_nt_bwd)

    @jax.custom_vjp
    def mm_tn(a, b):
        return _dg(a, b, tn, hi)

    def mm_tn_bwd(res, ct):
        a, b = res
        return _dg(b, ct, nt, hi).astype(a.dtype), _dg(a, ct, nn, hi).astype(b.dtype)

    mm_tn.defvjp(lambda a, b: (_dg(a, b, tn, hi), (a, b)), mm_tn_bwd)
    return mm, mm_nt, mm_tn


mm, mm_nt, mm_tn = _make_mm(_NN2, _NT2, _TN2, False)
bmm, bmm_nt, bmm_tn = _make_mm(_NN3, _NT3, _TN3, False)
bmm_hi, _, _ = _make_mm(_NN3, _NT3, _TN3, True)


def _shift_raw(x, k):
    n = x.shape[0]
    t = lax.broadcasted_iota(jnp.int32, x.shape, 0)
    if k > 0:
        return jnp.where(t >= k, pltpu.roll(x, k, axis=0), 0.0)
    k = -k
    return jnp.where(t < n - k, pltpu.roll(x, n - k, axis=0), 0.0)


@functools.partial(jax.custom_vjp, nondiff_argnums=(1,))
def shift(x, k):
    return _shift_raw(x, k)


shift.defvjp(lambda x, k: (_shift_raw(x, k), None), lambda k, _, ct: (_shift_raw(ct, -k),))


def _silu(x):
    return x * jax.nn.sigmoid(x)


def _softplus(x):
    return jnp.maximum(x, 0.0) + jnp.log(1.0 + jnp.exp(-jnp.abs(x)))


def _rms_fwd(h, gain):
    rstd = lax.rsqrt(jnp.mean(h * h, axis=-1, keepdims=True) + EPS)
    xhat = h * rstd
    return xhat * gain, xhat, rstd


def _rms_bwd(dxn, xhat, rstd, gain):
    dxhat = dxn * gain
    dh = rstd * (dxhat - xhat * jnp.mean(dxhat * xhat, axis=-1, keepdims=True))
    return dh, jnp.sum(dxn * xhat, axis=0, keepdims=True)


def _acc_rows(ref, val, first):
    @pl.when(first)
    def _():
        ref[...] = jnp.zeros_like(ref)
    ref[0:1, :] += val


def _tile(n, cap):
    t = min(n, cap)
    assert n % t == 0, (n, t)
    return t


def _resident(shape):
    nd = len(shape)
    return pl.BlockSpec(shape, lambda *_: (0,) * nd, pipeline_mode=pl.Buffered(1))


def _rows(w):
    return math.prod(w.shape[:-1])


def _w2d(ref):
    w = ref[...]
    return w.reshape(-1, w.shape[-1]) if w.ndim == 3 else w


def ffn_fwd(h, gain, wgt, wut, wd):
    T, D = h.shape
    F = _rows(wgt)
    tm = _tile(T, 256)

    def body(h_ref, gain_ref, wg_ref, wu_ref, wd_ref, ho_ref, g_ref, u_ref):
        hh = h_ref[...]
        xn, _, _ = _rms_fwd(hh, gain_ref[...])
        xb = xn.astype(bf16)
        g = _dg(xb, _w2d(wg_ref), _NT2, False)
        u = _dg(xb, _w2d(wu_ref), _NT2, False)
        y = _dg(_silu(g) * u, _w2d(wd_ref), _NN2, False)
        ho_ref[...] = hh + 0.5 * y
        g_ref[...] = g.astype(bf16)
        u_ref[...] = u.astype(bf16)

    row = lambda w: pl.BlockSpec((tm, w), lambda i: (i, 0))
    return _pcall(
        body, name="ffn_fwd", grid=(T // tm,),
        in_specs=[row(D), _resident((1, D)), _resident(wgt.shape), _resident(wut.shape), _resident(wd.shape)],
        out_specs=[row(D), row(F), row(F)],
        out_shape=[_sds((T, D), f32), _sds((T, F), bf16), _sds((T, F), bf16)],
        compiler_params=_params(("parallel",), VMEM_BIG),
    )(h, gain, wgt, wut, wd)


def ffn_bwd(h, gain, g, u, dout, wgt, wut, wd):
    T, D = h.shape
    F = _rows(wgt)
    tm = _tile(T, 256)

    def body(h_ref, gain_ref, g_ref, u_ref, do_ref, wg_ref, wu_ref, wd_ref,
             dh_ref, dgain_ref, xn_ref, act_ref, dg_ref, du_ref, dy_ref):
        hh, dout_ = h_ref[...], do_ref[...]
        gain_ = gain_ref[...]
        xn, xhat, rstd = _rms_fwd(hh, gain_)
        gg, uu = g_ref[...].astype(f32), u_ref[...].astype(f32)
        dy = (0.5 * dout_).astype(bf16)
        dact = _dg(dy, _w2d(wd_ref), _NT2, False)
        sg = jax.nn.sigmoid(gg)
        silu = gg * sg
        dgate = (dact * uu * (sg * (1.0 + gg * (1.0 - sg)))).astype(bf16)
        dup = (dact * silu).astype(bf16)
        dxn = _dg(dgate, _w2d(wg_ref), _NN2, False) + _dg(dup, _w2d(wu_ref), _NN2, False)
        dh, dgain = _rms_bwd(dxn, xhat, rstd, gain_)
        dh_ref[...] = dout_ + dh
        _acc_rows(dgain_ref, dgain, pl.program_id(0) == 0)
        xn_ref[...] = xn.astype(bf16)
        act_ref[...] = (silu * uu).astype(bf16)
        dg_ref[...] = dgate
        du_ref[...] = dup
        dy_ref[...] = dy

    row = lambda w: pl.BlockSpec((tm, w), lambda i: (i, 0))
    return _pcall(
        body, name="ffn_bwd", grid=(T // tm,),
        in_specs=[row(D), _resident((1, D)), row(F), row(F), row(D),
                  _resident(wgt.shape), _resident(wut.shape), _resident(wd.shape)],
        out_specs=[row(D), pl.BlockSpec((8, D), lambda i: (0, 0)), row(D), row(F), row(F), row(F), row(D)],
        out_shape=[_sds((T, D), f32), _sds((8, D), f32), _sds((T, D), bf16), _sds((T, F), bf16),
                   _sds((T, F), bf16), _sds((T, F), bf16), _sds((T, D), bf16)],
        compiler_params=_params(("arbitrary",), VMEM_BIG),
    )(h, gain, g, u, dout, wgt, wut, wd)


def _col_tile(n, cap=1408):
    best = None
    for c in range(128, cap + 1, 128):
        if n % c == 0:
            best = c
    assert best is not None, n
    return best


def wgrad(a, b, after):
    T, N = a.shape
    K = b.shape[1]
    nc, tk = _col_tile(N), _tile(T, 2048)
    nk = T // tk

    def body(a_ref, b_ref, after_ref, o_ref, acc_ref):
        k = pl.program_id(1)

        @pl.when(k == 0)
        def _():
            acc_ref[...] = jnp.zeros_like(acc_ref)

        acc_ref[...] += _dg(a_ref[...], b_ref[...], _TN2, False)

        @pl.when(k == nk - 1)
        def _():
            o_ref[...] = acc_ref[...].astype(bf16)

    return _pcall(
        body, name="wgrad", grid=(N // nc, nk),
        in_specs=[pl.BlockSpec((tk, nc), lambda j, k: (k, j)), pl.BlockSpec((tk, K), lambda j, k: (k, 0)),
                  pl.BlockSpec(memory_space=pl.ANY)],
        out_specs=pl.BlockSpec((nc, K), lambda j, k: (j, 0)),
        out_shape=_sds((N, K), bf16),
        scratch_shapes=[pltpu.VMEM((nc, K), f32)],
        compiler_params=_params(("parallel", "arbitrary"), VMEM_BIG),
    )(a, b, after)


def in_proj_fwd(h, gain, wt, widths):
    T, D = h.shape
    N = _rows(wt)
    assert sum(widths) == N
    tm = _tile(T, 512)
    offs = [sum(widths[:i]) for i in range(len(widths))]

    def body(h_ref, gain_ref, w_ref, *outs):
        xn, _, _ = _rms_fwd(h_ref[...], gain_ref[...])
        p = _dg(xn, _w2d(w_ref), _NT2, False)
        for o_ref, off, wd_ in zip(outs, offs, widths):
            o_ref[...] = p[:, off:off + wd_]

    row = lambda w: pl.BlockSpec((tm, w), lambda i: (i, 0))
    return _pcall(
        body, name="in_proj_fwd", grid=(T // tm,),
        in_specs=[row(D), _resident((1, D)), _resident(wt.shape)],
        out_specs=[row(w) for w in widths],
        out_shape=[_sds((T, w), f32) for w in widths],
        compiler_params=_params(("parallel",), VMEM_BIG),
    )(h, gain, wt)


def in_proj_bwd(h, gain, dpieces, dout, wt):
    T, D = h.shape
    N = _rows(wt)
    widths = [p.shape[1] for p in dpieces]
    assert sum(widths) == N
    tm = _tile(T, 512)
    nt = T // tm
    npc = len(dpieces)

    def body(*refs):
        h_ref, gain_ref = refs[0], refs[1]
        p_refs = refs[2:2 + npc]
        do_ref, w_ref, dh_ref, dgain_ref, dw_ref, acc_ref = refs[2 + npc:]
        i = pl.program_id(0)
        gain_ = gain_ref[...]
        xn, xhat, rstd = _rms_fwd(h_ref[...], gain_)
        dp = jnp.concatenate([r[...].astype(bf16) for r in p_refs], axis=-1)
        dxn = _dg(dp, _w2d(w_ref), _NN2, False)
        dh, dgain = _rms_bwd(dxn, xhat, rstd, gain_)
        dh_ref[...] = do_ref[...] + dh
        _acc_rows(dgain_ref, dgain, i == 0)

        @pl.when(i == 0)
        def _():
            acc_ref[...] = jnp.zeros_like(acc_ref)

        acc_ref[...] += _dg(dp, xn, _TN2, False)

        @pl.when(i == nt - 1)
        def _():
            dw_ref[...] = acc_ref[...].astype(bf16)

    row = lambda w: pl.BlockSpec((tm, w), lambda i: (i, 0))
    return _pcall(
        body, name="in_proj_bwd", grid=(nt,),
        in_specs=[row(D), _resident((1, D))] + [row(w) for w in widths] + [row(D), _resident(wt.shape)],
        out_specs=[row(D), pl.BlockSpec((8, D), lambda i: (0, 0)), pl.BlockSpec((N, D), lambda i: (0, 0))],
        out_shape=[_sds((T, D), f32), _sds((8, D), f32), _sds((N, D), bf16)],
        scratch_shapes=[pltpu.VMEM((N, D), f32)],
        compiler_params=_params(("arbitrary",), VMEM_BIG),
    )(h, gain, *dpieces, dout, wt)


def out_proj_fwd(h, ya, yb, w):
    T, D = h.shape
    half = ya.shape[1]
    tm = _tile(T, 512)

    def body(h_ref, ya_ref, yb_ref, w_ref, o_ref):
        y = jnp.concatenate([ya_ref[...], yb_ref[...]], axis=-1)
        o_ref[...] = h_ref[...] + _dg(y, _w2d(w_ref), _NN2, False)

    row = lambda w_: pl.BlockSpec((tm, w_), lambda i: (i, 0))
    return _pcall(
        body, name="out_proj_fwd", grid=(T // tm,),
        in_specs=[row(D), row(half), row(half), _resident(w.shape)],
        out_specs=row(D), out_shape=_sds((T, D), f32),
        compiler_params=_params(("parallel",), VMEM_MID),
    )(h, ya, yb, w)


def out_proj_bwd(dout, ya, yb, w):
    T, D = dout.shape
    half = ya.shape[1]
    tm = _tile(T, 512)
    nt = T // tm

    def body(do_ref, ya_ref, yb_ref, w_ref, dya_ref, dyb_ref, dw_ref, acc_ref):
        i = pl.program_id(0)
        dob = do_ref[...].astype(bf16)
        dy = _dg(dob, _w2d(w_ref), _NT2, False)
        dya_ref[...] = dy[:, :half]
        dyb_ref[...] = dy[:, half:]

        @pl.when(i == 0)
        def _():
            acc_ref[...] = jnp.zeros_like(acc_ref)

        y = jnp.concatenate([ya_ref[...], yb_ref[...]], axis=-1)
        acc_ref[...] += _dg(y, dob, _TN2, False)

        @pl.when(i == nt - 1)
        def _():
            dw_ref[...] = acc_ref[...].astype(bf16)

    row = lambda w_: pl.BlockSpec((tm, w_), lambda i: (i, 0))
    return _pcall(
        body, name="out_proj_bwd", grid=(nt,),
        in_specs=[row(D), row(half), row(half), _resident(w.shape)],
        out_specs=[row(half), row(half), pl.BlockSpec((2 * half, D), lambda i: (0, 0))],
        out_shape=[_sds((T, half), f32), _sds((T, half), f32), _sds((2 * half, D), bf16)],
        scratch_shapes=[pltpu.VMEM((2 * half, D), f32)],
        compiler_params=_params(("arbitrary",), VMEM_MID),
    )(dout, ya, yb, w)


def _conv_taps(x, w):
    K = w.shape[0]
    acc = x * w[K - 1:K, :]
    for i in range(K - 1):
        acc = acc + shift(x, K - 1 - i) * w[i:i + 1, :]
    return acc


def conv_silu_math(w, x):
    return _silu(_conv_taps(x, w))


def gated_conv_math(w, xd, bg, cg):
    return bg * _conv_taps(cg * xd, w)


def seq_chan_fwd(math, w, xs, seq, out_dtype, name):
    T, C = xs[0].shape
    K = w.shape[0]
    nb, nc = T // seq, C // HEAD

    def body(w_ref, *refs):
        o_ref = refs[-1]
        o_ref[...] = math(w_ref[...], *[r[...] for r in refs[:-1]]).astype(out_dtype)

    blk = pl.BlockSpec((seq, HEAD), lambda j, b: (b, j))
    return _pcall(
        body, name=name, grid=(nc, nb),
        in_specs=[pl.BlockSpec((K, HEAD), lambda j, b: (0, j))] + [blk] * len(xs),
        out_specs=blk, out_shape=_sds((T, C), out_dtype),
        compiler_params=_params(("parallel", "parallel"), VMEM_MID),
    )(w, *xs)


def seq_chan_bwd(math, w, xs, dy, seq, dx_dtype, name):
    T, C = xs[0].shape
    K = w.shape[0]
    nb, nc = T // seq, C // HEAD
    nx = len(xs)

    def body(w_ref, *refs):
        x_refs, dy_ref = refs[:nx], refs[nx]
        dx_refs, dw_ref = refs[nx + 1:2 * nx + 1], refs[2 * nx + 1]
        _, vjp = jax.vjp(math, w_ref[...], *[r[...] for r in x_refs])
        grads = vjp(dy_ref[...].astype(f32))

        @pl.when(pl.program_id(1) == 0)
        def _():
            dw_ref[...] = jnp.zeros_like(dw_ref)

        dw_ref[...] += grads[0]
        for r, gx in zip(dx_refs, grads[1:]):
            r[...] = gx.astype(dx_dtype)

    blk = pl.BlockSpec((seq, HEAD), lambda j, b: (b, j))
    wblk = pl.BlockSpec((K, HEAD), lambda j, b: (0, j))
    return _pcall(
        body, name=name, grid=(nc, nb),
        in_specs=[wblk] + [blk] * (nx + 1),
        out_specs=[blk] * nx + [wblk],
        out_shape=[_sds((T, C), dx_dtype)] * nx + [_sds((K, C), f32)],
        compiler_params=_params(("parallel", "arbitrary"), VMEM_MID),
    )(w, *xs, dy)


def pool_group_math(win, ag, pw, scale):
    t = lax.broadcasted_iota(jnp.int32, (ag.shape[0], 1), 0)
    s, k = ag, 1
    while k < win:
        s = s + shift(s, k)
        k *= 2
    pooled = s / jnp.minimum(t + 1, win).astype(f32) - ag
    return mm(pooled, pw) * scale


def pool_fwd(a, pw, scale, seq):
    T, C = a.shape

    def body(a_ref, pw_ref, sc_ref, o_ref):
        for gi, win in enumerate(POOL_WINDOWS):
            cols = slice(gi * HEAD, (gi + 1) * HEAD)
            o_ref[:, cols] = pool_group_math(win, a_ref[:, cols], pw_ref[gi], sc_ref[:, cols]).astype(bf16)

    blk = pl.BlockSpec((seq, C), lambda b: (b, 0))
    return _pcall(
        body, name="pool_fwd", grid=(T // seq,),
        in_specs=[blk, _resident(pw.shape), _resident((1, C))],
        out_specs=blk, out_shape=_sds((T, C), bf16),
        compiler_params=_params(("parallel",), VMEM_MID),
    )(a, pw, scale)


def pool_bwd(a, pw, scale, dy, seq):
    T, C = a.shape

    def body(a_ref, pw_ref, sc_ref, dy_ref, da_ref, dpw_ref, dsc_ref):
        first = pl.program_id(0) == 0

        @pl.when(first)
        def _():
            dpw_ref[...] = jnp.zeros_like(dpw_ref)
            dsc_ref[...] = jnp.zeros_like(dsc_ref)

        for gi, win in enumerate(POOL_WINDOWS):
            cols = slice(gi * HEAD, (gi + 1) * HEAD)
            _, vjp = jax.vjp(functools.partial(pool_group_math, win), a_ref[:, cols], pw_ref[gi], sc_ref[:, cols])
            da, dpw, dsc = vjp(dy_ref[:, cols])
            dpw_ref[gi] += dpw
            dsc_ref[0:1, cols] += dsc
            da_ref[:, cols] = da.astype(bf16)

    blk = pl.BlockSpec((seq, C), lambda b: (b, 0))
    return _pcall(
        body, name="pool_bwd", grid=(T // seq,),
        in_specs=[blk, _resident(pw.shape), _resident((1, C)), blk],
        out_specs=[blk, pl.BlockSpec(pw.shape, lambda b: (0, 0, 0)), pl.BlockSpec((8, C), lambda b: (0, 0))],
        out_shape=[_sds((T, C), bf16), _sds(pw.shape, f32), _sds((8, C), f32)],
        compiler_params=_params(("arbitrary",), VMEM_MID),
    )(a, pw, scale, dy)


SOLVE_HI = False


def _neumann_inverse(lmat):
    n = lmat.shape[-1]
    ii = lax.broadcasted_iota(jnp.int32, (n, n), 0)
    jj = lax.broadcasted_iota(jnp.int32, (n, n), 1)
    inv = jnp.where((ii == jj)[None], 1.0, 0.0) - lmat
    pw_ = _dg(lmat, lmat, _NN3, SOLVE_HI)
    steps = int(math.log2(n)) - 1
    for i in range(steps):
        inv = inv + _dg(inv, pw_, _NN3, SOLVE_HI)
        if i < steps - 1:
            pw_ = _dg(pw_, pw_, _NN3, SOLVE_HI)
    return inv


@jax.custom_vjp
def unit_lower_inverse(lmat):
    return _neumann_inverse(lmat)


def _unit_lower_inverse_fwd(lmat):
    inv = _neumann_inverse(lmat)
    return inv, inv


def _unit_lower_inverse_bwd(inv, ct):
    return (-_dg(_dg(inv, ct, _TN3, SOLVE_HI), inv, _NT3, SOLVE_HI),)


unit_lower_inverse.defvjp(_unit_lower_inverse_fwd, _unit_lower_inverse_bwd)


def dn_prep_math(qkv, bg, alog, dtb):
    tt = qkv.shape[0]
    nt = tt // CHUNK
    nb = nt * N_HEADS
    W = N_HEADS * HEAD
    beta_all = jax.nn.sigmoid(bg)
    g_all = -jnp.exp(alog) * _softplus(bg + dtb)

    def heads(fn):
        return jnp.stack([fn(hd).reshape(nt, CHUNK, HEAD) for hd in range(N_HEADS)], axis=1).reshape(nb, CHUNK, HEAD)

    def l2n(x):
        return x * lax.rsqrt(jnp.sum(x * x, axis=-1, keepdims=True) + EPS)

    q = heads(lambda hd: l2n(qkv[:, hd * HEAD:(hd + 1) * HEAD]) * (HEAD ** -0.5))
    k = heads(lambda hd: l2n(qkv[:, W + hd * HEAD:W + (hd + 1) * HEAD]))
    v = heads(lambda hd: qkv[:, 2 * W + hd * HEAD:2 * W + (hd + 1) * HEAD])
    beta = heads(lambda hd: jnp.broadcast_to(beta_all[:, hd:hd + 1], (tt, HEAD)))
    g = heads(lambda hd: jnp.broadcast_to(g_all[:, N_HEADS + hd:N_HEADS + hd + 1], (tt, HEAD)))

    ii = lax.broadcasted_iota(jnp.int32, (CHUNK, CHUNK), 0)
    jj = lax.broadcasted_iota(jnp.int32, (CHUNK, CHUNK), 1)
    tril, strict = (ii >= jj)[None], (ii > jj)[None]
    ones_b = jnp.ones((nb, CHUNK, CHUNK), f32)
    tril_b = jnp.where(tril, ones_b, 0.0)
    eye_b = jnp.where((ii == jj)[None], ones_b, 0.0)

    gcb = bmm_hi(tril_b, g)
    gcol = gcb[:, :, :CHUNK]
    grow = bmm_hi(ones_b, eye_b * gcol)
    gamma = jnp.where(tril, jnp.exp(jnp.where(tril, gcol - grow, 0.0)), 0.0)
    kb = k * beta
    lmat = jnp.where(strict, bmm_nt(kb, k) * gamma, 0.0)
    inv = unit_lower_inverse(lmat)
    egc = jnp.exp(gcb)
    u = bmm(inv, v * beta)
    w = bmm(inv, kb * egc)
    aqk = bmm_nt(q, k) * gamma
    qd = q * egc
    glast = gcb[:, CHUNK - 1:CHUNK, :]
    kd = k * jnp.exp(glast - gcb)
    last = jnp.exp(glast)
    r4 = lambda x: x.reshape((nt, N_HEADS) + x.shape[1:])
    return r4(u), r4(w), r4(qd), r4(kd), r4(aqk), r4(last)


_PREP_DTYPES = (f32, bf16, bf16, bf16, bf16, f32)


def _prep_specs(nt, T):
    nchunks = T // CHUNK
    shapes = [(HEAD,), (HEAD,), (HEAD,), (HEAD,), (CHUNK,), (HEAD,)]
    rows = [CHUNK, CHUNK, CHUNK, CHUNK, CHUNK, 1]
    specs = [pl.BlockSpec((nt, N_HEADS, r, s[0]), lambda i: (i, 0, 0, 0)) for r, s in zip(rows, shapes)]
    outs = [(nchunks, N_HEADS, r, s[0]) for r, s in zip(rows, shapes)]
    return specs, outs


def dn_prep_fwd(qkv, bg, alog, dtb):
    T = qkv.shape[0]
    tt = _tile(T, 256)
    nt = tt // CHUNK
    specs, shapes = _prep_specs(nt, T)

    def body(qkv_ref, bg_ref, alog_ref, dtb_ref, *outs):
        res = dn_prep_math(qkv_ref[...], bg_ref[...], alog_ref[...], dtb_ref[...])
        for o_ref, r, dt in zip(outs, res, _PREP_DTYPES):
            o_ref[...] = r.astype(dt)

    row = lambda w: pl.BlockSpec((tt, w), lambda i: (i, 0))
    return _pcall(
        body, name="dn_prep_fwd", grid=(T // tt,),
        in_specs=[row(qkv.shape[1]), row(HEAD), _resident((1, HEAD)), _resident((1, HEAD))],
        out_specs=specs, out_shape=[_sds(s, dt) for s, dt in zip(shapes, _PREP_DTYPES)],
        compiler_params=_params(("parallel",), VMEM_BIG),
    )(qkv, bg, alog, dtb)


def dn_prep_bwd(qkv, bg, alog, dtb, cts):
    T = qkv.shape[0]
    tt = _tile(T, 256)
    nt = tt // CHUNK
    specs, _ = _prep_specs(nt, T)

    def body(qkv_ref, bg_ref, alog_ref, dtb_ref, *refs):
        ct_refs, (dqkv_ref, dbg_ref, dalog_ref, ddtb_ref) = refs[:6], refs[6:]
        _, vjp = jax.vjp(dn_prep_math, qkv_ref[...], bg_ref[...], alog_ref[...], dtb_ref[...])
        dqkv, dbg, dalog, ddtb = vjp(tuple(r[...].astype(f32) for r in ct_refs))
        dqkv_ref[...] = dqkv
        dbg_ref[...] = dbg.astype(bf16)
        first = pl.program_id(0) == 0
        _acc_rows(dalog_ref, dalog, first)
        _acc_rows(ddtb_ref, ddtb, first)

    row = lambda w: pl.BlockSpec((tt, w), lambda i: (i, 0))
    small = pl.BlockSpec((8, HEAD), lambda i: (0, 0))
    return _pcall(
        body, name="dn_prep_bwd", grid=(T // tt,),
        in_specs=[row(qkv.shape[1]), row(HEAD), _resident((1, HEAD)), _resident((1, HEAD))] + specs,
        out_specs=[row(qkv.shape[1]), row(HEAD), small, small],
        out_shape=[_sds(qkv.shape, f32), _sds((T, HEAD), bf16), _sds((8, HEAD), f32), _sds((8, HEAD), f32)],
        compiler_params=_params(("arbitrary",), VMEM_BIG),
    )(qkv, bg, alog, dtb, *cts)


def dn_step(state, u, w, qd, kd, aqk, last):
    v_new = u - bmm(w, state)
    o = bmm(qd, state) + bmm(aqk, v_new)
    return state * last + bmm_tn(kd, v_new), o


def dn_gate(o, z, onorm):
    return o * lax.rsqrt(jnp.mean(o * o, axis=-1, keepdims=True) + EPS) * onorm * _silu(z)


DN_HEADS_PER_STEP_FWD, DN_HEADS_PER_STEP_BWD = 4, 2


def _recur_specs(n, hp):
    rows = [CHUNK, CHUNK, CHUNK, CHUNK, CHUNK, 1]
    lanes = [HEAD, HEAD, HEAD, HEAD, CHUNK, HEAD]
    return [pl.BlockSpec((n, hp, r, l), lambda b, p: (b, p, 0, 0)) for r, l in zip(rows, lanes)]


def dn_recur_fwd(prep, z, onorm, seq):
    T, C = z.shape
    n = seq // CHUNK
    nchunks = T // CHUNK
    hp = DN_HEADS_PER_STEP_FWD

    def body(u_ref, w_ref, qd_ref, kd_ref, aqk_ref, last_ref, z_ref, on_ref, y_ref, st_ref, o_ref):
        def step(i, state):
            st_ref[i] = state.astype(bf16)
            new, o = dn_step(state, u_ref[i], w_ref[i], qd_ref[i], kd_ref[i], aqk_ref[i], last_ref[i])
            o_ref[i] = o
            return new

        lax.fori_loop(0, n, step, jnp.zeros((hp, HEAD, HEAD), f32))
        for j in range(hp):
            cols = slice(j * HEAD, (j + 1) * HEAD)
            y_ref[:, cols] = dn_gate(o_ref[:, j].reshape(seq, HEAD), z_ref[:, cols], on_ref[...]).astype(bf16)

    blk = pl.BlockSpec((seq, hp * HEAD), lambda b, p: (b, p))
    per_chunk = lambda r: pl.BlockSpec((n, hp, r, HEAD), lambda b, p: (b, p, 0, 0))
    return _pcall(
        body, name="dn_recur_fwd", grid=(T // seq, N_HEADS // hp),
        in_specs=_recur_specs(n, hp) + [blk, pl.BlockSpec((1, HEAD), lambda b, p: (0, 0))],
        out_specs=[blk, per_chunk(HEAD), per_chunk(CHUNK)],
        out_shape=[_sds((T, C), bf16), _sds((nchunks, N_HEADS, HEAD, HEAD), bf16),
                   _sds((nchunks, N_HEADS, CHUNK, HEAD), f32)],
        compiler_params=_params(("parallel", "parallel"), VMEM_BIG),
    )(*prep, z, onorm)


_RECUR_CT_DTYPES = (f32, bf16, bf16, bf16, bf16, f32)


def dn_recur_bwd(prep, states, o, z, onorm, dy, seq):
    T, C = z.shape
    n = seq // CHUNK
    nchunks = T // CHUNK
    hp = DN_HEADS_PER_STEP_BWD

    def body(u_ref, w_ref, qd_ref, kd_ref, aqk_ref, last_ref, st_ref, o_ref, z_ref, on_ref, dy_ref,
             du_ref, dw_ref, dqd_ref, dkd_ref, daqk_ref, dlast_ref, dz_ref, don_ref, do_ref):
        args = lambda i: tuple(r[i].astype(f32) for r in (st_ref, u_ref, w_ref, qd_ref, kd_ref, aqk_ref, last_ref))
        ct_refs = (du_ref, dw_ref, dqd_ref, dkd_ref, daqk_ref, dlast_ref)
        first = jnp.logical_and(pl.program_id(0) == 0, pl.program_id(1) == 0)
        for j in range(hp):
            cols = slice(j * HEAD, (j + 1) * HEAD)
            _, gate_vjp = jax.vjp(dn_gate, o_ref[:, j].reshape(seq, HEAD), z_ref[:, cols], on_ref[...])
            do, dz, don = gate_vjp(dy_ref[:, cols])
            dz_ref[:, cols] = dz.astype(bf16)
            do_ref[:, j] = do.reshape(n, CHUNK, HEAD)
            _acc_rows(don_ref, don, jnp.logical_and(first, j == 0))

        def bstep(j, dstate):
            i = n - 1 - j
            _, vjp = jax.vjp(dn_step, *args(i))
            ds, *cts = vjp((dstate, do_ref[i]))
            for r, ct, dt in zip(ct_refs, cts, _RECUR_CT_DTYPES):
                r[i] = ct.astype(dt)
            return ds

        lax.fori_loop(0, n, bstep, jnp.zeros((hp, HEAD, HEAD), f32))

    blk = pl.BlockSpec((seq, hp * HEAD), lambda b, p: (b, p))
    per_chunk = lambda r: pl.BlockSpec((n, hp, r, HEAD), lambda b, p: (b, p, 0, 0))
    rows = [CHUNK, CHUNK, CHUNK, CHUNK, CHUNK, 1]
    lanes = [HEAD, HEAD, HEAD, HEAD, CHUNK, HEAD]
    return _pcall(
        body, name="dn_recur_bwd", grid=(T // seq, N_HEADS // hp),
        in_specs=_recur_specs(n, hp) + [per_chunk(HEAD), per_chunk(CHUNK), blk,
                                        pl.BlockSpec((1, HEAD), lambda b, p: (0, 0)), blk],
        out_specs=_recur_specs(n, hp) + [blk, pl.BlockSpec((8, HEAD), lambda b, p: (0, 0))],
        out_shape=[_sds((nchunks, N_HEADS, r, l), dt) for r, l, dt in zip(rows, lanes, _RECUR_CT_DTYPES)]
        + [_sds((T, C), bf16), _sds((8, HEAD), f32)],
        scratch_shapes=[pltpu.VMEM((n, hp, CHUNK, HEAD), f32)],
        compiler_params=_params(("arbitrary", "arbitrary"), VMEM_BIG),
    )(*prep, states, o, z, onorm, dy)


def sgu_math(up, vp, ng, nb, sw, sbias):
    S = up.shape[0]
    nblk = S // SGU_BLOCK
    u = jax.nn.gelu(up, approximate=True)
    v = jax.nn.gelu(vp, approximate=True)
    xc = v - jnp.mean(v, axis=-1, keepdims=True)
    vn = xc * lax.rsqrt(jnp.mean(xc * xc, axis=-1, keepdims=True) + EPS) * ng + nb
    ii = lax.broadcasted_iota(jnp.int32, (SGU_BLOCK, SGU_BLOCK), 0)
    jj = lax.broadcasted_iota(jnp.int32, (SGU_BLOCK, SGU_BLOCK), 1)
    outs = []
    for hd in range(N_HEADS):
        vh = vn[:, hd * HEAD:(hd + 1) * HEAD].reshape(nblk, SGU_BLOCK, HEAD)
        ws = jnp.where(ii >= jj, sw[hd], 0.0)
        mixed = bmm(jnp.broadcast_to(ws[None], (nblk, SGU_BLOCK, SGU_BLOCK)), vh) + sbias[hd][None]
        outs.append(mixed.reshape(S, HEAD))
    return u * jnp.concatenate(outs, axis=-1)


def sgu_fwd(up, vp, ng, nb, sw, sbias, seq):
    T, C = up.shape

    def body(up_ref, vp_ref, ng_ref, nb_ref, sw_ref, sb_ref, o_ref):
        o_ref[...] = sgu_math(up_ref[...], vp_ref[...], ng_ref[...], nb_ref[...], sw_ref[...],
                              sb_ref[...]).astype(bf16)

    blk = pl.BlockSpec((seq, C), lambda b: (b, 0))
    return _pcall(
        body, name="sgu_fwd", grid=(T // seq,),
        in_specs=[blk, blk, _resident((1, C)), _resident((1, C)), _resident(sw.shape), _resident(sbias.shape)],
        out_specs=blk, out_shape=_sds((T, C), bf16),
        compiler_params=_params(("parallel",), VMEM_BIG),
    )(up, vp, ng, nb, sw, sbias)


def sgu_bwd(up, vp, ng, nb, sw, sbias, dy, seq):
    T, C = up.shape

    def body(up_ref, vp_ref, ng_ref, nb_ref, sw_ref, sb_ref, dy_ref,
             dup_ref, dvp_ref, dng_ref, dnb_ref, dsw_ref, dsb_ref):
        _, vjp = jax.vjp(sgu_math, up_ref[...], vp_ref[...], ng_ref[...], nb_ref[...], sw_ref[...], sb_ref[...])
        dup, dvp, dng, dnb, dsw, dsb = vjp(dy_ref[...])
        dup_ref[...] = dup.astype(bf16)
        dvp_ref[...] = dvp.astype(bf16)
        first = pl.program_id(0) == 0
        _acc_rows(dng_ref, dng, first)
        _acc_rows(dnb_ref, dnb, first)

        @pl.when(first)
        def _():
            dsw_ref[...] = jnp.zeros_like(dsw_ref)
            dsb_ref[...] = jnp.zeros_like(dsb_ref)

        dsw_ref[...] += dsw
        dsb_ref[...] += dsb

    blk = pl.BlockSpec((seq, C), lambda b: (b, 0))
    small = pl.BlockSpec((8, C), lambda b: (0, 0))
    return _pcall(
        body, name="sgu_bwd", grid=(T // seq,),
        in_specs=[blk, blk, _resident((1, C)), _resident((1, C)), _resident(sw.shape), _resident(sbias.shape), blk],
        out_specs=[blk, blk, small, small, pl.BlockSpec(sw.shape, lambda b: (0, 0, 0)),
                   pl.BlockSpec(sbias.shape, lambda b: (0, 0, 0))],
        out_shape=[_sds((T, C), bf16), _sds((T, C), bf16), _sds((8, C), f32), _sds((8, C), f32),
                   _sds(sw.shape, f32), _sds(sbias.shape, f32)],
        compiler_params=_params(("arbitrary",), VMEM_BIG),
    )(up, vp, ng, nb, sw, sbias, dy)


def loss_fwd_bwd(h, gain, target):
    T, D = h.shape
    tm = _tile(T, 512)

    def body(h_ref, gain_ref, t_ref, loss_ref, dh_ref, dgain_ref):
        gain_ = gain_ref[...]
        y, xhat, rstd = _rms_fwd(h_ref[...], gain_)
        err = y - t_ref[...]
        part = 0.5 * jnp.sum(jnp.mean(err * err, axis=-1, keepdims=True), axis=0, keepdims=True)
        dh, dgain = _rms_bwd(err * (1.0 / D), xhat, rstd, gain_)
        dh_ref[...] = dh
        first = pl.program_id(0) == 0
        _acc_rows(dgain_ref, dgain, first)

        @pl.when(first)
        def _():
            loss_ref[...] = jnp.zeros_like(loss_ref)

        loss_ref[...] += jnp.broadcast_to(part, loss_ref.shape)

    row = pl.BlockSpec((tm, D), lambda i: (i, 0))
    return _pcall(
        body, name="loss_fwd_bwd", grid=(T // tm,),
        in_specs=[row, _resident((1, D)), row],
        out_specs=[pl.BlockSpec((8, 128), lambda i: (0, 0)), row, pl.BlockSpec((8, D), lambda i: (0, 0))],
        out_shape=[_sds((8, 128), f32), _sds((T, D), f32), _sds((8, D), f32)],
        compiler_params=_params(("arbitrary",), VMEM_MID),
    )(h, gain, target)


def adamw(w, g, m, v):
    c1, c2 = 1.0 - ADAM_B1 ** ADAM_STEP, 1.0 - ADAM_B2 ** ADAM_STEP
    if w.ndim == 3:
        grid, blk = (w.shape[0],), pl.BlockSpec((1,) + w.shape[1:], lambda i: (i, 0, 0))
    else:
        R, C = w.shape
        tr = max(c for c in range(8, 513, 8) if R % c == 0)
        grid, blk = (R // tr,), pl.BlockSpec((tr, C), lambda i: (i, 0))

    def body(w_ref, g_ref, m_ref, v_ref, d_ref, nm_ref, nv_ref):
        gg = g_ref[...]
        nm = ADAM_B1 * m_ref[...] + (1.0 - ADAM_B1) * gg
        nv = ADAM_B2 * v_ref[...] + (1.0 - ADAM_B2) * (gg * gg)
        d_ref[...] = -ADAM_LR * ((nm / c1) / (jnp.sqrt(nv / c2) + ADAM_EPS) + ADAM_WD * w_ref[...])
        nm_ref[...] = nm
        nv_ref[...] = nv

    return _pcall(
        body, name="adamw", grid=grid, in_specs=[blk] * 4, out_specs=[blk] * 3,
        out_shape=[_sds(w.shape, f32)] * 3, compiler_params=_params(("parallel",), VMEM_MID),
    )(w, g, m, v)


def sum8(parts):
    _, R, C = parts.shape
    tr = R
    for cand in (512, 352, 336, 320, 256, 128):
        if R % cand == 0:
            tr = cand
            break

    def body(p_ref, o_ref):
        acc = p_ref[0].astype(f32)
        for i in range(1, N_DEV):
            acc = acc + p_ref[i].astype(f32)
        o_ref[...] = acc

    return _pcall(
        body, name="sum8", grid=(R // tr,),
        in_specs=[pl.BlockSpec((N_DEV, tr, C), lambda i: (0, i, 0))],
        out_specs=pl.BlockSpec((tr, C), lambda i: (i, 0)), out_shape=_sds((R, C), f32),
        compiler_params=_params(("parallel",), VMEM_MID),
    )(parts)


_FLIPS = [(fx, fy, fc) for fx in (0, 1) for fy in (0, 1) for fc in (0, 1)][1:]
_HBM = pl.BlockSpec(memory_space=pltpu.HBM)


def _me():
    return lax.axis_index("x"), lax.axis_index("y"), lax.axis_index("c")


def _peer(flip):
    x, y, c = _me()
    fx, fy, fc = flip
    return (1 - x if fx else x, 1 - y if fy else y, 1 - c if fc else c)


def _lin(dev):
    return 4 * dev[0] + 2 * dev[1] + dev[2]


def _src_block(ref, rows, dev, whole):
    return ref if whole else ref.at[pl.ds(pl.multiple_of(dev * rows, 16), rows)]


_SEM = pl.BlockSpec(memory_space=pltpu.SEMAPHORE)
_EFFECT = pltpu.SideEffectType.DATAFLOW_SIDE_EFFECTING


def push_start(srcs, whole, after, name):
    n = len(srcs)
    rows = [s.shape[0] if whole else s.shape[0] // N_DEV for s in srcs]
    land_shapes = [(N_DEV, r, s.shape[1]) for r, s in zip(rows, srcs)]

    def body(*refs):
        src_refs, land_refs = refs[:n], refs[n:2 * n]
        send_sems, recv_sems, own_sems, token = refs[2 * n + 1], refs[2 * n + 2], refs[2 * n + 3], refs[-1]
        me = _lin(_me())
        for i in range(n):
            for k, flip in enumerate(_FLIPS):
                peer = _peer(flip)
                pltpu.make_async_remote_copy(
                    src_ref=_src_block(src_refs[i], rows[i], _lin(peer), whole), dst_ref=land_refs[i].at[me],
                    send_sem=send_sems.at[i * 7 + k], recv_sem=recv_sems.at[i * 7 + k],
                    device_id=peer, device_id_type=MESH).start()
        for i in range(n):
            pltpu.make_async_copy(_src_block(src_refs[i], rows[i], me, whole), land_refs[i].at[me],
                                  own_sems.at[i]).start()
        token[...] = jnp.zeros_like(token)

    hbm = lambda a: pltpu.with_memory_space_constraint(a, pltpu.HBM)
    outs = _pcall(
        body, name=name,
        in_specs=[_HBM] * (2 * n) + [pl.BlockSpec(memory_space=pl.ANY)],
        out_specs=[_SEM, _SEM, _SEM] + [_HBM] * (2 * n) + [pl.BlockSpec(memory_space=pltpu.VMEM)],
        out_shape=[pltpu.SemaphoreType.DMA((7 * n,)), pltpu.SemaphoreType.DMA((7 * n,)),
                   pltpu.SemaphoreType.DMA((n,))]
        + [pltpu.HBM(s.shape, s.dtype) for s in srcs]
        + [pltpu.HBM(shp, s.dtype) for shp, s in zip(land_shapes, srcs)] + [_sds((8, 128), f32)],
        input_output_aliases={i: 3 + i for i in range(2 * n)},
        compiler_params=pltpu.CompilerParams(has_side_effects=_EFFECT),
    )(*[hbm(s) for s in srcs], *[hbm(lax.empty(shp, s.dtype)) for shp, s in zip(land_shapes, srcs)], after)
    return outs[0], outs[1], outs[2], list(outs[3:3 + n]), list(outs[3 + n:3 + 2 * n]), outs[-1]


def push_wait(handle, whole, after, name):
    send_sems, recv_sems, own_sems, srcs, lands, _ = handle
    n = len(srcs)
    rows = [l.shape[1] for l in lands]

    def body(*refs):
        src_refs, land_refs = refs[:n], refs[n:2 * n]
        send_sems_, recv_sems_, own_sems_ = refs[2 * n], refs[2 * n + 1], refs[2 * n + 2]
        me = _lin(_me())
        for i in range(n):
            for k, flip in enumerate(_FLIPS):
                peer = _peer(flip)
                cp = pltpu.make_async_remote_copy(
                    src_ref=_src_block(src_refs[i], rows[i], _lin(peer), whole), dst_ref=land_refs[i].at[_lin(peer)],
                    send_sem=send_sems_.at[i * 7 + k], recv_sem=recv_sems_.at[i * 7 + k],
                    device_id=peer, device_id_type=MESH)
                cp.wait_send()
                cp.wait_recv()
            pltpu.make_async_copy(_src_block(src_refs[i], rows[i], me, whole), land_refs[i].at[me],
                                  own_sems_.at[i]).wait()

    outs = _pcall(
        body, name=name,
        in_specs=[_HBM] * (2 * n) + [_SEM, _SEM, _SEM, pl.BlockSpec(memory_space=pl.ANY)],
        out_specs=[_HBM] * (2 * n),
        out_shape=[pltpu.HBM(a.shape, a.dtype) for a in srcs + lands],
        input_output_aliases={i: i for i in range(2 * n)},
        compiler_params=pltpu.CompilerParams(has_side_effects=_EFFECT),
    )(*srcs, *lands, send_sems, recv_sems, own_sems, after)
    return list(outs[n:])


def all_gather_small(x):
    R, C = x.shape

    def body(x_ref, o_ref, send_sems, recv_sems):
        me = _lin(_me())
        o_ref[me] = x_ref[...]
        sends = []
        for k, flip in enumerate(_FLIPS):
            rc = pltpu.make_async_remote_copy(
                src_ref=x_ref, dst_ref=o_ref.at[me], send_sem=send_sems.at[k], recv_sem=recv_sems.at[k],
                device_id=_peer(flip), device_id_type=MESH)
            rc.start()
            sends.append(rc)
        for k, flip in enumerate(_FLIPS):
            pltpu.make_async_remote_copy(
                src_ref=x_ref, dst_ref=o_ref.at[_lin(_peer(flip))], send_sem=send_sems.at[k],
                recv_sem=recv_sems.at[k], device_id=_peer(flip), device_id_type=MESH).wait_recv()
        for rc in sends:
            rc.wait_send()

    vm = pl.BlockSpec(memory_space=pltpu.VMEM)
    return _pcall(
        body, name="all_gather_small", in_specs=[vm], out_specs=vm, out_shape=_sds((N_DEV, R, C), x.dtype),
        scratch_shapes=[pltpu.SemaphoreType.DMA((7,)), pltpu.SemaphoreType.DMA((7,))],
        compiler_params=_params(None, VMEM_MID),
    )(x)


def sum_slots(parts):
    _, R, C = parts.shape

    def body(p_ref, o_ref):
        acc = p_ref[0]
        for p in range(1, N_DEV):
            acc = acc + p_ref[p]
        o_ref[...] = acc

    vm = pl.BlockSpec(memory_space=pltpu.VMEM)
    return _pcall(body, name="sum_slots", in_specs=[vm], out_specs=vm, out_shape=_sds((R, C), f32),
                  compiler_params=_params(None, VMEM_MID))(parts)


_PACK_ROWS = 8


def _packed_rows(shape):
    return -(-math.prod(shape) // (128 * _PACK_ROWS)) * _PACK_ROWS


def _pack(arrs):
    parts = []
    for a in arrs:
        flat = a.reshape(-1).astype(f32)
        rows = _packed_rows(a.shape)
        parts.append(jnp.pad(flat, (0, rows * 128 - flat.shape[0])).reshape(rows, 128))
    return jnp.concatenate(parts, axis=0)


def _unpack(buf, shapes):
    out, off = [], 0
    for s in shapes:
        rows = _packed_rows(s)
        out.append(buf[off:off + rows].reshape(-1)[:math.prod(s)].reshape(s))
        off += rows
    return out


def _row(v):
    return v.reshape(1, -1)


def _lane_row(vals, offset):
    return jnp.pad(vals.reshape(1, -1), ((0, 0), (offset, HEAD - offset - vals.shape[-1])))


def kernel(x, ffn1_norm, ffn1_w_gate, ffn1_w_up, ffn1_w_down, mix_norm, ffn2_norm, ffn2_w_gate, ffn2_w_up, ffn2_w_down, ab_w_in, pool_w, pool_scale, dn_conv_w, dn_a_log, dn_dt_bias, dn_out_norm, ab_w_out, cd_w_in, sgu_norm_g, sgu_norm_b, sgu_w, sgu_bias, sc_conv_w, cd_w_out, final_norm, loss_target, m_ffn1_norm, m_ffn1_w_gate, m_ffn1_w_up, m_ffn1_w_down, m_mix_norm, m_ffn2_norm, m_ffn2_w_gate, m_ffn2_w_up, m_ffn2_w_down, m_ab_w_in, m_pool_w, m_pool_scale, m_dn_conv_w, m_dn_a_log, m_dn_dt_bias, m_dn_out_norm, m_ab_w_out, m_cd_w_in, m_sgu_norm_g, m_sgu_norm_b, m_sgu_w, m_sgu_bias, m_sc_conv_w, m_cd_w_out, m_final_norm, v_ffn1_norm, v_ffn1_w_gate, v_ffn1_w_up, v_ffn1_w_down, v_mix_norm, v_ffn2_norm, v_ffn2_w_gate, v_ffn2_w_up, v_ffn2_w_down, v_ab_w_in, v_pool_w, v_pool_scale, v_dn_conv_w, v_dn_a_log, v_dn_dt_bias, v_dn_out_norm, v_ab_w_out, v_cd_w_in, v_sgu_norm_g, v_sgu_norm_b, v_sgu_w, v_sgu_bias, v_sc_conv_w, v_cd_w_out, v_final_norm):
    names = ['ffn1_norm', 'ffn1_w_gate', 'ffn1_w_up', 'ffn1_w_down', 'mix_norm', 'ffn2_norm', 'ffn2_w_gate',
             'ffn2_w_up', 'ffn2_w_down', 'ab_w_in', 'pool_w', 'pool_scale', 'dn_conv_w', 'dn_a_log', 'dn_dt_bias',
             'dn_out_norm', 'ab_w_out', 'cd_w_in', 'sgu_norm_g', 'sgu_norm_b', 'sgu_w', 'sgu_bias', 'sc_conv_w',
             'cd_w_out', 'final_norm']
    loc = locals()
    W = {n: loc[n] for n in names}
    M = {n: loc['m_' + n] for n in names}
    V = {n: loc['v_' + n] for n in names}

    B, S, D = x.shape
    T = B * S
    me = _lin(_me())

    def rows_of(w):
        return w.astype(bf16).T

    def layer_shards(layer):
        e = layer // 2
        shards = [rows_of(W['ffn1_w_gate'][layer]), rows_of(W['ffn1_w_up'][layer]), W['ffn1_w_down'][layer].astype(bf16)]
        if layer % 2 == 0:
            win = jnp.pad(rows_of(W['ab_w_in'][e]), ((0, AB_SHARD_PAD - AB_SHARD), (0, 0)))
            wout = W['ab_w_out'][e].astype(bf16)
        else:
            win = rows_of(W['cd_w_in'][e])
            wout = W['cd_w_out'][e].astype(bf16)
        shards += [win, wout]
        shards += [rows_of(W['ffn2_w_gate'][layer]), rows_of(W['ffn2_w_up'][layer]), W['ffn2_w_down'][layer].astype(bf16)]
        return shards

    def full_weight(idx, land, layer):
        if idx == 3 and layer % 2 == 0:
            return jnp.pad(land[:, :AB_SHARD].reshape(AB_IN, D), ((0, AB_IN_PAD - AB_IN), (0, 0)))
        return land

    def tied(gain, token):
        return gain if token is None else gain + token[0:1, 0:1]

    def start_groups(arrays, groups, whole, after, name):
        out = []
        for gi, idx in enumerate(groups):
            suffix = "" if len(groups) == 1 else "abc"[gi]
            handle = push_start([arrays[i] for i in idx], whole, after, name=name + suffix)
            after = handle[5]
            out.append((idx, handle, suffix))
        return out

    def wait_group(inflight, gi, whole, after, name):
        idx, handle, suffix = inflight[gi]
        return dict(zip(idx, push_wait(handle, whole, after, name=name + suffix)))

    one_group = [tuple(range(8))]
    by_block = [(0, 1, 2), (3, 4), (5, 6, 7)]
    zeros_tile = jnp.zeros((8, 128), f32)
    gathered = [None] * DEPTH
    small_shards = [W['dn_conv_w'], W['sgu_norm_g'], W['sgu_norm_b'], W['sc_conv_w']]
    gs = all_gather_small(_pack(small_shards))
    per_dev = [_unpack(gs[p], [a.shape for a in small_shards]) for p in range(N_DEV)]
    dn_conv_full, sgu_g_full, sgu_b_full, sc_conv_full = [
        jnp.concatenate([per_dev[p][i] for p in range(N_DEV)], axis=-1) for i in range(4)]

    inflight = start_groups(layer_shards(0), by_block, True, gs, "gather_start_0")

    h = x.reshape(T, D)
    saved = []
    for layer in range(DEPTH):
        e = layer // 2
        mine, landed = inflight, {}
        landed.update(wait_group(mine, 0, True, h if layer else mine[-1][1][5], f"gather_wait_{layer}"))
        token = None
        if layer + 1 < DEPTH:
            inflight = start_groups(layer_shards(layer + 1), one_group, True, landed[0], f"gather_start_{layer + 1}")
            token = inflight[-1][1][5]
        wg1, wu1, wd1 = [full_weight(i, landed[i], layer) for i in (0, 1, 2)]
        sv = {'h0': h}
        h, sv['g1'], sv['u1'] = ffn_fwd(h, tied(_row(W['ffn1_norm'][layer]), token), wg1, wu1, wd1)
        sv['h1'] = h
        if len(mine) > 1:
            landed.update(wait_group(mine, 1, True, h, f"gather_wait_{layer}"))
        win, wout = [full_weight(i, landed[i], layer) for i in (3, 4)]
        if layer % 2 == 0:
            a_in, qkv_pre, z, bg = in_proj_fwd(h, _row(W['mix_norm'][layer]), win, [512, 1536, 512, 128])
            ya = pool_fwd(a_in, W['pool_w'][e], _row(W['pool_scale'][e]), S)
            qkv = seq_chan_fwd(conv_silu_math, dn_conv_full[e], [qkv_pre], S, f32, "dn_conv_fwd")
            alog, dtb = _lane_row(W['dn_a_log'][e], N_HEADS), _lane_row(W['dn_dt_bias'][e], N_HEADS)
            prep = dn_prep_fwd(qkv, bg, alog, dtb)
            yb, states, o_dn = dn_recur_fwd(prep, z, _row(W['dn_out_norm'][e]), S)
            sv.update(a_in=a_in, qkv_pre=qkv_pre, z=z, bg=bg, qkv=qkv, alog=alog, dtb=dtb, prep=prep,
                      states=states, o_dn=o_dn)
        else:
            up, vp, xd, bgate, cg = in_proj_fwd(h, _row(W['mix_norm'][layer]), win, [512] * 5)
            sbias = W['sgu_bias'][e].reshape(N_HEADS, SGU_BLOCK, 1)
            ya = sgu_fwd(up, vp, _row(sgu_g_full[e]), _row(sgu_b_full[e]), W['sgu_w'][e], sbias,
                         _tile(S, SGU_TILE))
            yb = seq_chan_fwd(gated_conv_math, sc_conv_full[e], [xd, bgate, cg], S, bf16, "sc_conv_fwd")
            sv.update(up=up, vp=vp, xd=xd, bgate=bgate, cg=cg, sbias=sbias)
        sv.update(ya=ya, yb=yb)
        h = out_proj_fwd(h, ya, yb, wout)
        sv['h2'] = h
        if len(mine) > 1:
            landed.update(wait_group(mine, 2, True, h, f"gather_wait_{layer}"))
        wg2, wu2, wd2 = [full_weight(i, landed[i], layer) for i in (5, 6, 7)]
        h, sv['g2'], sv['u2'] = ffn_fwd(h, _row(W['ffn2_norm'][layer]), wg2, wu2, wd2)
        gathered[layer] = [wg1, wu1, wd1, win, wout, wg2, wu2, wd2]
        saved.append(sv)

    loss_part, dh, dfinal = loss_fwd_bwd(h, _row(W['final_norm']), loss_target.reshape(T, D))
    loss = lax.psum(loss_part[0, 0], ("x", "y", "c"))

    G = {}
    G['final_norm'] = dfinal[0]
    for n in ('ffn1_norm', 'mix_norm', 'ffn2_norm'):
        G[n] = [None] * DEPTH
    for n in ('pool_w', 'pool_scale', 'dn_conv_w', 'dn_a_log', 'dn_dt_bias', 'dn_out_norm',
              'sgu_norm_g', 'sgu_norm_b', 'sgu_w', 'sgu_bias', 'sc_conv_w'):
        G[n] = [None] * 2
    small_names = ['ffn1_norm', 'mix_norm', 'ffn2_norm', 'pool_w', 'pool_scale', 'dn_conv_w', 'dn_a_log',
                   'dn_dt_bias', 'dn_out_norm', 'sgu_norm_g', 'sgu_norm_b', 'sgu_w', 'sgu_bias', 'sc_conv_w',
                   'final_norm']
    big = [None] * DEPTH
    inflight = None
    last = []

    for layer in reversed(range(DEPTH)):
        e = layer // 2
        sv = saved[layer]
        wg1, wu1, wd1, win, wout, wg2, wu2, wd2 = gathered[layer]
        token = None if inflight is None else inflight[-1][1][5]
        dh, dgain, xn, act, dg, du, dy = ffn_bwd(sv['h2'], tied(_row(W['ffn2_norm'][layer]), token), sv['g2'],
                                                  sv['u2'], dh, wg2, wu2, wd2)
        G['ffn2_norm'][layer] = dgain[0]
        dwg2, dwu2, dwd2 = wgrad(dg, xn, zeros_tile), wgrad(du, xn, zeros_tile), wgrad(act, dy, zeros_tile)
        mix_token = ffn1_token = None
        if layer == 0:
            last += start_groups({5: dwg2, 6: dwu2, 7: dwd2}, [(5, 6, 7)], False, dwd2, "scatter_start_0a")
            mix_token = last[-1][1][5]
        dya, dyb, dwout = out_proj_bwd(dh, sv['ya'], sv['yb'], wout)
        if layer % 2 == 0:
            da, dpw, dsc = pool_bwd(sv['a_in'], W['pool_w'][e], _row(W['pool_scale'][e]), dya, S)
            G['pool_w'][e], G['pool_scale'][e] = dpw, dsc[0]
            *cts, dz, don = dn_recur_bwd(sv['prep'], sv['states'], sv['o_dn'], sv['z'],
                                         _row(W['dn_out_norm'][e]), dyb, S)
            G['dn_out_norm'][e] = don[0]
            dqkv, dbg, dalog, ddtb = dn_prep_bwd(sv['qkv'], sv['bg'], sv['alog'], sv['dtb'], cts)
            G['dn_a_log'][e], G['dn_dt_bias'][e] = dalog[0, N_HEADS:2 * N_HEADS], ddtb[0, N_HEADS:2 * N_HEADS]
            dqkv_pre, dconv = seq_chan_bwd(conv_silu_math, dn_conv_full[e], [sv['qkv_pre']], dqkv, S, bf16,
                                           "dn_conv_bwd")
            G['dn_conv_w'][e] = dconv
            dpieces = [da, dqkv_pre, dz, dbg]
        else:
            dup, dvp, dng, dnb, dsw, dsb = sgu_bwd(sv['up'], sv['vp'], _row(sgu_g_full[e]), _row(sgu_b_full[e]),
                                                   W['sgu_w'][e], sv['sbias'], dya, _tile(S, SGU_TILE))
            G['sgu_norm_g'][e], G['sgu_norm_b'][e] = dng[0], dnb[0]
            G['sgu_w'][e], G['sgu_bias'][e] = dsw, dsb.reshape(N_HEADS, SGU_BLOCK)
            dxd, dbgate, dcg, dscw = seq_chan_bwd(gated_conv_math, sc_conv_full[e],
                                                  [sv['xd'], sv['bgate'], sv['cg']], dyb, S, bf16, "sc_conv_bwd")
            G['sc_conv_w'][e] = dscw
            dpieces = [dup, dvp, dxd, dbgate, dcg]
        dh, dgain, dwin = in_proj_bwd(sv['h1'], tied(_row(W['mix_norm'][layer]), mix_token), dpieces, dh, win)
        G['mix_norm'][layer] = dgain[0]
        if layer % 2 == 0:
            dwin = jnp.pad(dwin[:AB_IN].reshape(N_DEV, AB_SHARD, D), ((0, 0), (0, AB_SHARD_PAD - AB_SHARD), (0, 0)))
            dwin = dwin.reshape(N_DEV * AB_SHARD_PAD, D)
        if layer == 0:
            last += start_groups({3: dwin, 4: dwout}, [(3, 4)], False, mix_token, "scatter_start_0b")
            ffn1_token = last[-1][1][5]
        dh, dgain, xn, act, dg, du, dy = ffn_bwd(sv['h0'], tied(_row(W['ffn1_norm'][layer]), ffn1_token), sv['g1'],
                                                  sv['u1'], dh, wg1, wu1, wd1)
        G['ffn1_norm'][layer] = dgain[0]
        order = zeros_tile
        if layer == 0:
            small_g = [G[n] if n == 'final_norm' else jnp.stack(G[n]) for n in small_names]
            small_x = start_groups([_pack(small_g)], [(0,)], True, ffn1_token, "small_grads_start")
            order = small_x[-1][1][5]
        dwg1, dwu1, dwd1 = wgrad(dg, xn, order), wgrad(du, xn, order), wgrad(act, dy, order)
        if layer == 0:
            last += start_groups({0: dwg1, 1: dwu1, 2: dwd1}, [(0, 1, 2)], False, order, "scatter_start_0c")
        else:
            after = zeros_tile
            if inflight is not None:
                big[layer + 1] = wait_group(inflight, 0, False, dh, f"scatter_wait_{layer + 1}")
                after = big[layer + 1][0]
            inflight = start_groups([dwg1, dwu1, dwd1, dwin, dwout, dwg2, dwu2, dwd2], one_group, False, after,
                                    f"scatter_start_{layer}")

    grad_x = dh.reshape(B, S, D)

    reduced_buf = sum_slots(wait_group(small_x, 0, True, last[-1][1][5], "small_grads_wait")[0])
    reduced = _unpack(reduced_buf, [a.shape for a in small_g])
    big[1] = wait_group(inflight, 0, False, reduced_buf, "scatter_wait_1")
    big[0] = wait_group(last, 0, False, big[1][0], "scatter_wait_0a")
    big[0].update(wait_group(last, 1, False, big[0][5], "scatter_wait_0b"))
    grads = {}
    for n, g in zip(small_names, reduced):
        if n in ('dn_conv_w', 'sgu_norm_g', 'sgu_norm_b', 'sc_conv_w'):
            c = W[n].shape[-1]
            g = lax.dynamic_slice_in_dim(g, me * c, c, axis=g.ndim - 1)
        grads[n] = g

    def stack_layers(idx, sel):
        out = []
        for layer in sel:
            g = sum8(big[layer][idx])
            out.append(g[:AB_SHARD] if idx == 3 and layer % 2 == 0 else g)
        return jnp.stack(out)

    all_layers, even, odd = range(DEPTH), range(0, DEPTH, 2), range(1, DEPTH, 2)
    delta, new_m, new_v = {}, {}, {}

    def update(n, idx, transposed, sel):
        view = (lambda a: jnp.swapaxes(a, 1, 2)) if transposed else (lambda a: a)
        g = stack_layers(idx, sel)
        d_, m_, v_ = adamw(view(W[n]), g, view(M[n]), view(V[n]))
        grads[n], delta[n], new_m[n], new_v[n] = view(g), view(d_), view(m_), view(v_)

    shapes = [W[n].shape for n in small_names]
    d_, m_, v_ = adamw(_pack([W[n] for n in small_names]), _pack([grads[n] for n in small_names]),
                       _pack([M[n] for n in small_names]), _pack([V[n] for n in small_names]))
    for n, a, b_, c_ in zip(small_names, _unpack(d_, shapes), _unpack(m_, shapes), _unpack(v_, shapes)):
        delta[n], new_m[n], new_v[n] = a, b_, c_
    update('ab_w_in', 3, True, even)
    update('cd_w_in', 3, True, odd)
    update('ab_w_out', 4, False, even)
    update('cd_w_out', 4, False, odd)
    update('ffn2_w_gate', 5, True, all_layers)
    update('ffn2_w_up', 6, True, all_layers)
    update('ffn2_w_down', 7, False, all_layers)
    updated = sum(lax.slice(a, (0,) * a.ndim, (1,) * a.ndim).reshape(1) for a in new_v.values())
    big[0].update(wait_group(last, 2, False, updated, "scatter_wait_0c"))
    update('ffn1_w_gate', 0, True, all_layers)
    update('ffn1_w_up', 1, True, all_layers)
    update('ffn1_w_down', 2, False, all_layers)

    return (loss, grad_x, *[grads[n] for n in names], *[delta[n] for n in names],
            *[new_m[n] for n in names], *[new_v[n] for n in names])
```

```python
import functools
import math

import jax
import jax.numpy as jnp
from jax import lax
from jax.experimental import pallas as pl
from jax.experimental.pallas import tpu as pltpu

f32, bf16 = jnp.float32, jnp.bfloat16

D_MODEL = 1024
DEPTH = 4
CHUNK = 64
POOL_WINDOWS = (2, 4, 8, 16)
HEAD = 128
N_HEADS = 4
SGU_BLOCK = 128
SGU_TILE = 512
FFN_DIM = 2816
AB_IN = 2568
AB_IN_PAD = 2688
AB_SHARD = 321
AB_SHARD_PAD = 336
EPS = 1e-6
N_DEV = 8
MESH = pl.DeviceIdType.MESH

ADAM_LR, ADAM_B1, ADAM_B2, ADAM_EPS, ADAM_WD, ADAM_STEP = 0.001, 0.9, 0.999, 1e-08, 0.01, 10

VMEM_MAX = 62 * 1024 * 1024
VMEM_BIG = 56 * 1024 * 1024
VMEM_MID = 40 * 1024 * 1024


def _pcall(body, **kw):
    return pl.pallas_call(body, **kw)


def _params(sem=None, vmem=None):
    return pltpu.CompilerParams(dimension_semantics=sem, vmem_limit_bytes=vmem)


def _sds(shape, dtype):
    return jax.ShapeDtypeStruct(shape, dtype)


_NN2, _NT2, _TN2 = (((1,), (0,)), ((), ())), (((1,), (1,)), ((), ())), (((0,), (0,)), ((), ()))
_NN3, _NT3, _TN3 = (((2,), (1,)), ((0,), (0,))), (((2,), (2,)), ((0,), (0,))), (((1,), (1,)), ((0,), (0,)))


def _dg(a, b, dims, hi):
    if hi:
        return lax.dot_general(a.astype(f32), b.astype(f32), dims, preferred_element_type=f32,
                               precision=lax.Precision.HIGH)
    return lax.dot_general(a.astype(bf16), b.astype(bf16), dims, preferred_element_type=f32)


def _make_mm(nn, nt, tn, hi):
    @jax.custom_vjp
    def mm(a, b):
        return _dg(a, b, nn, hi)

    def mm_bwd(res, ct):
        a, b = res
        return _dg(ct, b, nt, hi).astype(a.dtype), _dg(a, ct, tn, hi).astype(b.dtype)

    mm.defvjp(lambda a, b: (_dg(a, b, nn, hi), (a, b)), mm_bwd)

    @jax.custom_vjp
    def mm_nt(a, b):
        return _dg(a, b, nt, hi)

    def mm_nt_bwd(res, ct):
        a, b = res
        return _dg(ct, b, nn, hi).astype(a.dtype), _dg(ct, a, tn, hi).astype(b.dtype)

    mm_nt.defvjp(lambda a, b: (_dg(a, b, nt, hi), (a, b)), mm_nt_bwd)

    @jax.custom_vjp
    def mm_tn(a, b):
        return _dg(a, b, tn, hi)

    def mm_tn_bwd(res, ct):
        a, b = res
        return _dg(b, ct, nt, hi).astype(a.dtype), _dg(a, ct, nn, hi).astype(b.dtype)

    mm_tn.defvjp(lambda a, b: (_dg(a, b, tn, hi), (a, b)), mm_tn_bwd)
    return mm, mm_nt, mm_tn


mm, mm_nt, mm_tn = _make_mm(_NN2, _NT2, _TN2, False)
bmm, bmm_nt, bmm_tn = _make_mm(_NN3, _NT3, _TN3, False)
bmm_hi, _, _ = _make_mm(_NN3, _NT3, _TN3, True)


def _shift_raw(x, k):
    n = x.shape[0]
    t = lax.broadcasted_iota(jnp.int32, x.shape, 0)
    if k > 0:
        return jnp.where(t >= k, pltpu.roll(x, k, axis=0), 0.0)
    k = -k
    return jnp.where(t < n - k, pltpu.roll(x, n - k, axis=0), 0.0)


@functools.partial(jax.custom_vjp, nondiff_argnums=(1,))
def shift(x, k):
    return _shift_raw(x, k)


shift.defvjp(lambda x, k: (_shift_raw(x, k), None), lambda k, _, ct: (_shift_raw(ct, -k),))


def _silu(x):
    return x * jax.nn.sigmoid(x)


def _softplus(x):
    return jnp.maximum(x, 0.0) + jnp.log(1.0 + jnp.exp(-jnp.abs(x)))


def _rms_fwd(h, gain):
    rstd = lax.rsqrt(jnp.mean(h * h, axis=-1, keepdims=True) + EPS)
    xhat = h * rstd
    return xhat * gain, xhat, rstd


def _rms_bwd(dxn, xhat, rstd, gain):
    dxhat = dxn * gain
    dh = rstd * (dxhat - xhat * jnp.mean(dxhat * xhat, axis=-1, keepdims=True))
    return dh, jnp.sum(dxn * xhat, axis=0, keepdims=True)


def _acc_rows(ref, val, first):
    @pl.when(first)
    def _():
        ref[...] = jnp.zeros_like(ref)
    ref[0:1, :] += val


def _tile(n, cap):
    t = min(n, cap)
    assert n % t == 0, (n, t)
    return t


def _resident(shape):
    nd = len(shape)
    return pl.BlockSpec(shape, lambda *_: (0,) * nd, pipeline_mode=pl.Buffered(1))


def _rows(w):
    return math.prod(w.shape[:-1])


def _w2d(ref):
    w = ref[...]
    return w.reshape(-1, w.shape[-1]) if w.ndim == 3 else w


def ffn_fwd(h, gain, wgt, wut, wd):
    T, D = h.shape
    F = _rows(wgt)
    tm = _tile(T, 512)

    def body(h_ref, gain_ref, wg_ref, wu_ref, wd_ref, ho_ref, g_ref, u_ref):
        hh = h_ref[...]
        xn, _, _ = _rms_fwd(hh, gain_ref[...])
        xb = xn.astype(bf16)
        g = _dg(xb, _w2d(wg_ref), _NT2, False)
        u = _dg(xb, _w2d(wu_ref), _NT2, False)
        y = _dg(_silu(g) * u, _w2d(wd_ref), _NN2, False)
        ho_ref[...] = hh + 0.5 * y
        g_ref[...] = g.astype(bf16)
        u_ref[...] = u.astype(bf16)

    row = lambda w: pl.BlockSpec((tm, w), lambda i: (i, 0))
    return _pcall(
        body, name="ffn_fwd", grid=(T // tm,),
        in_specs=[row(D), _resident((1, D)), _resident(wgt.shape), _resident(wut.shape), _resident(wd.shape)],
        out_specs=[row(D), row(F), row(F)],
        out_shape=[_sds((T, D), f32), _sds((T, F), bf16), _sds((T, F), bf16)],
        compiler_params=_params(("parallel",), VMEM_MAX),
    )(h, gain, wgt, wut, wd)


def ffn_bwd(h, gain, g, u, dout, wgt, wut, wd):
    T, D = h.shape
    F = _rows(wgt)
    tm = _tile(T, 256)

    def body(h_ref, gain_ref, g_ref, u_ref, do_ref, wg_ref, wu_ref, wd_ref,
             dh_ref, dgain_ref, xn_ref, act_ref, dg_ref, du_ref, dy_ref):
        hh, dout_ = h_ref[...], do_ref[...]
        gain_ = gain_ref[...]
        xn, xhat, rstd = _rms_fwd(hh, gain_)
        gg, uu = g_ref[...].astype(f32), u_ref[...].astype(f32)
        dy = (0.5 * dout_).astype(bf16)
        dact = _dg(dy, _w2d(wd_ref), _NT2, False)
        sg = jax.nn.sigmoid(gg)
        silu = gg * sg
        dgate = (dact * uu * (sg * (1.0 + gg * (1.0 - sg)))).astype(bf16)
        dup = (dact * silu).astype(bf16)
        dxn = _dg(dgate, _w2d(wg_ref), _NN2, False) + _dg(dup, _w2d(wu_ref), _NN2, False)
        dh, dgain = _rms_bwd(dxn, xhat, rstd, gain_)
        dh_ref[...] = dout_ + dh
        _acc_rows(dgain_ref, dgain, pl.program_id(0) == 0)
        xn_ref[...] = xn.astype(bf16)
        act_ref[...] = (silu * uu).astype(bf16)
        dg_ref[...] = dgate
        du_ref[...] = dup
        dy_ref[...] = dy

    row = lambda w: pl.BlockSpec((tm, w), lambda i: (i, 0))
    return _pcall(
        body, name="ffn_bwd", grid=(T // tm,),
        in_specs=[row(D), _resident((1, D)), row(F), row(F), row(D),
                  _resident(wgt.shape), _resident(wut.shape), _resident(wd.shape)],
        out_specs=[row(D), pl.BlockSpec((8, D), lambda i: (0, 0)), row(D), row(F), row(F), row(F), row(D)],
        out_shape=[_sds((T, D), f32), _sds((8, D), f32), _sds((T, D), bf16), _sds((T, F), bf16),
                   _sds((T, F), bf16), _sds((T, F), bf16), _sds((T, D), bf16)],
        compiler_params=_params(("arbitrary",), VMEM_BIG),
    )(h, gain, g, u, dout, wgt, wut, wd)


def _col_tile(n, cap=1408):
    best = None
    for c in range(128, cap + 1, 128):
        if n % c == 0:
            best = c
    assert best is not None, n
    return best


def wgrad(a, b, after):
    T, N = a.shape
    K = b.shape[1]
    nc, tk = _col_tile(N), _tile(T, 2048)
    nk = T // tk

    def body(a_ref, b_ref, after_ref, o_ref, acc_ref):
        k = pl.program_id(1)

        @pl.when(k == 0)
        def _():
            acc_ref[...] = jnp.zeros_like(acc_ref)

        acc_ref[...] += _dg(a_ref[...], b_ref[...], _TN2, False)

        @pl.when(k == nk - 1)
        def _():
            o_ref[...] = acc_ref[...].astype(bf16)

    return _pcall(
        body, name="wgrad", grid=(N // nc, nk),
        in_specs=[pl.BlockSpec((tk, nc), lambda j, k: (k, j)), pl.BlockSpec((tk, K), lambda j, k: (k, 0)),
                  pl.BlockSpec(memory_space=pl.ANY)],
        out_specs=pl.BlockSpec((nc, K), lambda j, k: (j, 0)),
        out_shape=_sds((N, K), bf16),
        scratch_shapes=[pltpu.VMEM((nc, K), f32)],
        compiler_params=_params(("parallel", "arbitrary"), VMEM_BIG),
    )(a, b, after)


def in_proj_fwd(h, gain, wt, widths):
    T, D = h.shape
    N = _rows(wt)
    assert sum(widths) == N
    tm = _tile(T, 512)
    offs = [sum(widths[:i]) for i in range(len(widths))]

    def body(h_ref, gain_ref, w_ref, *outs):
        xn, _, _ = _rms_fwd(h_ref[...], gain_ref[...])
        p = _dg(xn, _w2d(w_ref), _NT2, False)
        for o_ref, off, wd_ in zip(outs, offs, widths):
            o_ref[...] = p[:, off:off + wd_]

    row = lambda w: pl.BlockSpec((tm, w), lambda i: (i, 0))
    return _pcall(
        body, name="in_proj_fwd", grid=(T // tm,),
        in_specs=[row(D), _resident((1, D)), _resident(wt.shape)],
        out_specs=[row(w) for w in widths],
        out_shape=[_sds((T, w), f32) for w in widths],
        compiler_params=_params(("parallel",), VMEM_BIG),
    )(h, gain, wt)


def in_proj_bwd(h, gain, dpieces, dout, wt):
    T, D = h.shape
    N = _rows(wt)
    widths = [p.shape[1] for p in dpieces]
    assert sum(widths) == N
    tm = _tile(T, 512)
    nt = T // tm
    npc = len(dpieces)

    def body(*refs):
        h_ref, gain_ref = refs[0], refs[1]
        p_refs = refs[2:2 + npc]
        do_ref, w_ref, dh_ref, dgain_ref, dw_ref, acc_ref = refs[2 + npc:]
        i = pl.program_id(0)
        gain_ = gain_ref[...]
        xn, xhat, rstd = _rms_fwd(h_ref[...], gain_)
        dp = jnp.concatenate([r[...].astype(bf16) for r in p_refs], axis=-1)
        dxn = _dg(dp, _w2d(w_ref), _NN2, False)
        dh, dgain = _rms_bwd(dxn, xhat, rstd, gain_)
        dh_ref[...] = do_ref[...] + dh
        _acc_rows(dgain_ref, dgain, i == 0)

        @pl.when(i == 0)
        def _():
            acc_ref[...] = jnp.zeros_like(acc_ref)

        acc_ref[...] += _dg(dp, xn, _TN2, False)

        @pl.when(i == nt - 1)
        def _():
            dw_ref[...] = acc_ref[...].astype(bf16)

    row = lambda w: pl.BlockSpec((tm, w), lambda i: (i, 0))
    return _pcall(
        body, name="in_proj_bwd", grid=(nt,),
        in_specs=[row(D), _resident((1, D))] + [row(w) for w in widths] + [row(D), _resident(wt.shape)],
        out_specs=[row(D), pl.BlockSpec((8, D), lambda i: (0, 0)), pl.BlockSpec((N, D), lambda i: (0, 0))],
        out_shape=[_sds((T, D), f32), _sds((8, D), f32), _sds((N, D), bf16)],
        scratch_shapes=[pltpu.VMEM((N, D), f32)],
        compiler_params=_params(("arbitrary",), VMEM_BIG),
    )(h, gain, *dpieces, dout, wt)


def out_proj_fwd(h, ya, yb, w):
    T, D = h.shape
    half = ya.shape[1]
    tm = _tile(T, 512)

    def body(h_ref, ya_ref, yb_ref, w_ref, o_ref):
        y = jnp.concatenate([ya_ref[...], yb_ref[...]], axis=-1)
        o_ref[...] = h_ref[...] + _dg(y, _w2d(w_ref), _NN2, False)

    row = lambda w_: pl.BlockSpec((tm, w_), lambda i: (i, 0))
    return _pcall(
        body, name="out_proj_fwd", grid=(T // tm,),
        in_specs=[row(D), row(half), row(half), _resident(w.shape)],
        out_specs=row(D), out_shape=_sds((T, D), f32),
        compiler_params=_params(("parallel",), VMEM_MID),
    )(h, ya, yb, w)


def out_proj_bwd(dout, ya, yb, w):
    T, D = dout.shape
    half = ya.shape[1]
    tm = _tile(T, 512)
    nt = T // tm

    def body(do_ref, ya_ref, yb_ref, w_ref, dya_ref, dyb_ref, dw_ref, acc_ref):
        i = pl.program_id(0)
        dob = do_ref[...].astype(bf16)
        dy = _dg(dob, _w2d(w_ref), _NT2, False)
        dya_ref[...] = dy[:, :half]
        dyb_ref[...] = dy[:, half:]

        @pl.when(i == 0)
        def _():
            acc_ref[...] = jnp.zeros_like(acc_ref)

        y = jnp.concatenate([ya_ref[...], yb_ref[...]], axis=-1)
        acc_ref[...] += _dg(y, dob, _TN2, False)

        @pl.when(i == nt - 1)
        def _():
            dw_ref[...] = acc_ref[...].astype(bf16)

    row = lambda w_: pl.BlockSpec((tm, w_), lambda i: (i, 0))
    return _pcall(
        body, name="out_proj_bwd", grid=(nt,),
        in_specs=[row(D), row(half), row(half), _resident(w.shape)],
        out_specs=[row(half), row(half), pl.BlockSpec((2 * half, D), lambda i: (0, 0))],
        out_shape=[_sds((T, half), f32), _sds((T, half), f32), _sds((2 * half, D), bf16)],
        scratch_shapes=[pltpu.VMEM((2 * half, D), f32)],
        compiler_params=_params(("arbitrary",), VMEM_MID),
    )(dout, ya, yb, w)


def _conv_taps(x, w):
    K = w.shape[0]
    acc = x * w[K - 1:K, :]
    for i in range(K - 1):
        acc = acc + shift(x, K - 1 - i) * w[i:i + 1, :]
    return acc


def conv_silu_math(w, x):
    return _silu(_conv_taps(x, w))


def gated_conv_math(w, xd, bg, cg):
    return bg * _conv_taps(cg * xd, w)


def seq_chan_fwd(math, w, xs, seq, out_dtype, name):
    T, C = xs[0].shape
    K = w.shape[0]
    nb, nc = T // seq, C // HEAD

    def body(w_ref, *refs):
        o_ref = refs[-1]
        o_ref[...] = math(w_ref[...], *[r[...] for r in refs[:-1]]).astype(out_dtype)

    blk = pl.BlockSpec((seq, HEAD), lambda j, b: (b, j))
    return _pcall(
        body, name=name, grid=(nc, nb),
        in_specs=[pl.BlockSpec((K, HEAD), lambda j, b: (0, j))] + [blk] * len(xs),
        out_specs=blk, out_shape=_sds((T, C), out_dtype),
        compiler_params=_params(("parallel", "parallel"), VMEM_MID),
    )(w, *xs)


def seq_chan_bwd(math, w, xs, dy, seq, dx_dtype, name):
    T, C = xs[0].shape
    K = w.shape[0]
    nb, nc = T // seq, C // HEAD
    nx = len(xs)

    def body(w_ref, *refs):
        x_refs, dy_ref = refs[:nx], refs[nx]
        dx_refs, dw_ref = refs[nx + 1:2 * nx + 1], refs[2 * nx + 1]
        _, vjp = jax.vjp(math, w_ref[...], *[r[...] for r in x_refs])
        grads = vjp(dy_ref[...].astype(f32))

        @pl.when(pl.program_id(1) == 0)
        def _():
            dw_ref[...] = jnp.zeros_like(dw_ref)

        dw_ref[...] += grads[0]
        for r, gx in zip(dx_refs, grads[1:]):
            r[...] = gx.astype(dx_dtype)

    blk = pl.BlockSpec((seq, HEAD), lambda j, b: (b, j))
    wblk = pl.BlockSpec((K, HEAD), lambda j, b: (0, j))
    return _pcall(
        body, name=name, grid=(nc, nb),
        in_specs=[wblk] + [blk] * (nx + 1),
        out_specs=[blk] * nx + [wblk],
        out_shape=[_sds((T, C), dx_dtype)] * nx + [_sds((K, C), f32)],
        compiler_params=_params(("parallel", "arbitrary"), VMEM_MID),
    )(w, *xs, dy)


def pool_group_math(win, ag, pw, scale):
    t = lax.broadcasted_iota(jnp.int32, (ag.shape[0], 1), 0)
    s, k = ag, 1
    while k < win:
        s = s + shift(s, k)
        k *= 2
    pooled = s / jnp.minimum(t + 1, win).astype(f32) - ag
    return mm(pooled, pw) * scale


def pool_fwd(a, pw, scale, seq):
    T, C = a.shape

    def body(a_ref, pw_ref, sc_ref, o_ref):
        for gi, win in enumerate(POOL_WINDOWS):
            cols = slice(gi * HEAD, (gi + 1) * HEAD)
            o_ref[:, cols] = pool_group_math(win, a_ref[:, cols], pw_ref[gi], sc_ref[:, cols]).astype(bf16)

    blk = pl.BlockSpec((seq, C), lambda b: (b, 0))
    return _pcall(
        body, name="pool_fwd", grid=(T // seq,),
        in_specs=[blk, _resident(pw.shape), _resident((1, C))],
        out_specs=blk, out_shape=_sds((T, C), bf16),
        compiler_params=_params(("parallel",), VMEM_MID),
    )(a, pw, scale)


def pool_bwd(a, pw, scale, dy, seq):
    T, C = a.shape

    def body(a_ref, pw_ref, sc_ref, dy_ref, da_ref, dpw_ref, dsc_ref):
        first = pl.program_id(0) == 0

        @pl.when(first)
        def _():
            dpw_ref[...] = jnp.zeros_like(dpw_ref)
            dsc_ref[...] = jnp.zeros_like(dsc_ref)

        for gi, win in enumerate(POOL_WINDOWS):
            cols = slice(gi * HEAD, (gi + 1) * HEAD)
            _, vjp = jax.vjp(functools.partial(pool_group_math, win), a_ref[:, cols], pw_ref[gi], sc_ref[:, cols])
            da, dpw, dsc = vjp(dy_ref[:, cols])
            dpw_ref[gi] += dpw
            dsc_ref[0:1, cols] += dsc
            da_ref[:, cols] = da.astype(bf16)

    blk = pl.BlockSpec((seq, C), lambda b: (b, 0))
    return _pcall(
        body, name="pool_bwd", grid=(T // seq,),
        in_specs=[blk, _resident(pw.shape), _resident((1, C)), blk],
        out_specs=[blk, pl.BlockSpec(pw.shape, lambda b: (0, 0, 0)), pl.BlockSpec((8, C), lambda b: (0, 0))],
        out_shape=[_sds((T, C), bf16), _sds(pw.shape, f32), _sds((8, C), f32)],
        compiler_params=_params(("arbitrary",), VMEM_MID),
    )(a, pw, scale, dy)


SOLVE_HI = False


def _neumann_inverse(lmat):
    n = lmat.shape[-1]
    ii = lax.broadcasted_iota(jnp.int32, (n, n), 0)
    jj = lax.broadcasted_iota(jnp.int32, (n, n), 1)
    inv = jnp.where((ii == jj)[None], 1.0, 0.0) - lmat
    pw_ = _dg(lmat, lmat, _NN3, SOLVE_HI)
    steps = int(math.log2(n)) - 1
    for i in range(steps):
        inv = inv + _dg(inv, pw_, _NN3, SOLVE_HI)
        if i < steps - 1:
            pw_ = _dg(pw_, pw_, _NN3, SOLVE_HI)
    return inv


@jax.custom_vjp
def unit_lower_inverse(lmat):
    return _neumann_inverse(lmat)


def _unit_lower_inverse_fwd(lmat):
    inv = _neumann_inverse(lmat)
    return inv, inv


def _unit_lower_inverse_bwd(inv, ct):
    return (-_dg(_dg(inv, ct, _TN3, SOLVE_HI), inv, _NT3, SOLVE_HI),)


unit_lower_inverse.defvjp(_unit_lower_inverse_fwd, _unit_lower_inverse_bwd)


def dn_prep_math(qkv, bg, alog, dtb):
    tt = qkv.shape[0]
    nt = tt // CHUNK
    nb = nt * N_HEADS
    W = N_HEADS * HEAD
    beta_all = jax.nn.sigmoid(bg)
    g_all = -jnp.exp(alog) * _softplus(bg + dtb)

    def heads(fn):
        return jnp.stack([fn(hd).reshape(nt, CHUNK, HEAD) for hd in range(N_HEADS)], axis=1).reshape(nb, CHUNK, HEAD)

    def l2n(x):
        return x * lax.rsqrt(jnp.sum(x * x, axis=-1, keepdims=True) + EPS)

    q = heads(lambda hd: l2n(qkv[:, hd * HEAD:(hd + 1) * HEAD]) * (HEAD ** -0.5))
    k = heads(lambda hd: l2n(qkv[:, W + hd * HEAD:W + (hd + 1) * HEAD]))
    v = heads(lambda hd: qkv[:, 2 * W + hd * HEAD:2 * W + (hd + 1) * HEAD])
    beta = heads(lambda hd: jnp.broadcast_to(beta_all[:, hd:hd + 1], (tt, HEAD)))
    g = heads(lambda hd: jnp.broadcast_to(g_all[:, N_HEADS + hd:N_HEADS + hd + 1], (tt, HEAD)))

    ii = lax.broadcasted_iota(jnp.int32, (CHUNK, CHUNK), 0)
    jj = lax.broadcasted_iota(jnp.int32, (CHUNK, CHUNK), 1)
    tril, strict = (ii >= jj)[None], (ii > jj)[None]
    ones_b = jnp.ones((nb, CHUNK, CHUNK), f32)
    tril_b = jnp.where(tril, ones_b, 0.0)
    eye_b = jnp.where((ii == jj)[None], ones_b, 0.0)

    gcb = bmm_hi(tril_b, g)
    gcol = gcb[:, :, :CHUNK]
    grow = bmm_hi(ones_b, eye_b * gcol)
    gamma = jnp.where(tril, jnp.exp(jnp.where(tril, gcol - grow, 0.0)), 0.0)
    kb = k * beta
    lmat = jnp.where(strict, bmm_nt(kb, k) * gamma, 0.0)
    inv = unit_lower_inverse(lmat)
    egc = jnp.exp(gcb)
    u = bmm(inv, v * beta)
    w = bmm(inv, kb * egc)
    aqk = bmm_nt(q, k) * gamma
    qd = q * egc
    glast = gcb[:, CHUNK - 1:CHUNK, :]
    kd = k * jnp.exp(glast - gcb)
    last = jnp.exp(glast)
    r4 = lambda x: x.reshape((nt, N_HEADS) + x.shape[1:])
    return r4(u), r4(w), r4(qd), r4(kd), r4(aqk), r4(last)


_PREP_DTYPES = (f32, bf16, bf16, bf16, bf16, f32)


def _prep_specs(nt, T):
    nchunks = T // CHUNK
    shapes = [(HEAD,), (HEAD,), (HEAD,), (HEAD,), (CHUNK,), (HEAD,)]
    rows = [CHUNK, CHUNK, CHUNK, CHUNK, CHUNK, 1]
    specs = [pl.BlockSpec((nt, N_HEADS, r, s[0]), lambda i: (i, 0, 0, 0)) for r, s in zip(rows, shapes)]
    outs = [(nchunks, N_HEADS, r, s[0]) for r, s in zip(rows, shapes)]
    return specs, outs


def dn_prep_fwd(qkv, bg, alog, dtb):
    T = qkv.shape[0]
    tt = _tile(T, 256)
    nt = tt // CHUNK
    specs, shapes = _prep_specs(nt, T)

    def body(qkv_ref, bg_ref, alog_ref, dtb_ref, *outs):
        res = dn_prep_math(qkv_ref[...], bg_ref[...], alog_ref[...], dtb_ref[...])
        for o_ref, r, dt in zip(outs, res, _PREP_DTYPES):
            o_ref[...] = r.astype(dt)

    row = lambda w: pl.BlockSpec((tt, w), lambda i: (i, 0))
    return _pcall(
        body, name="dn_prep_fwd", grid=(T // tt,),
        in_specs=[row(qkv.shape[1]), row(HEAD), _resident((1, HEAD)), _resident((1, HEAD))],
        out_specs=specs, out_shape=[_sds(s, dt) for s, dt in zip(shapes, _PREP_DTYPES)],
        compiler_params=_params(("parallel",), VMEM_BIG),
    )(qkv, bg, alog, dtb)


def dn_prep_bwd(qkv, bg, alog, dtb, cts):
    T = qkv.shape[0]
    tt = _tile(T, 256)
    nt = tt // CHUNK
    specs, _ = _prep_specs(nt, T)

    def body(qkv_ref, bg_ref, alog_ref, dtb_ref, *refs):
        ct_refs, (dqkv_ref, dbg_ref, dalog_ref, ddtb_ref) = refs[:6], refs[6:]
        _, vjp = jax.vjp(dn_prep_math, qkv_ref[...], bg_ref[...], alog_ref[...], dtb_ref[...])
        dqkv, dbg, dalog, ddtb = vjp(tuple(r[...].astype(f32) for r in ct_refs))
        dqkv_ref[...] = dqkv
        dbg_ref[...] = dbg.astype(bf16)
        first = pl.program_id(0) == 0
        _acc_rows(dalog_ref, dalog, first)
        _acc_rows(ddtb_ref, ddtb, first)

    row = lambda w: pl.BlockSpec((tt, w), lambda i: (i, 0))
    small = pl.BlockSpec((8, HEAD), lambda i: (0, 0))
    return _pcall(
        body, name="dn_prep_bwd", grid=(T // tt,),
        in_specs=[row(qkv.shape[1]), row(HEAD), _resident((1, HEAD)), _resident((1, HEAD))] + specs,
        out_specs=[row(qkv.shape[1]), row(HEAD), small, small],
        out_shape=[_sds(qkv.shape, f32), _sds((T, HEAD), bf16), _sds((8, HEAD), f32), _sds((8, HEAD), f32)],
        compiler_params=_params(("arbitrary",), VMEM_BIG),
    )(qkv, bg, alog, dtb, *cts)


def dn_step(state, u, w, qd, kd, aqk, last):
    v_new = u - bmm(w, state)
    o = bmm(qd, state) + bmm(aqk, v_new)
    return state * last + bmm_tn(kd, v_new), o


def dn_gate(o, z, onorm):
    return o * lax.rsqrt(jnp.mean(o * o, axis=-1, keepdims=True) + EPS) * onorm * _silu(z)


DN_HEADS_PER_STEP_FWD, DN_HEADS_PER_STEP_BWD = 4, 2


def _recur_specs(n, hp):
    rows = [CHUNK, CHUNK, CHUNK, CHUNK, CHUNK, 1]
    lanes = [HEAD, HEAD, HEAD, HEAD, CHUNK, HEAD]
    return [pl.BlockSpec((n, hp, r, l), lambda b, p: (b, p, 0, 0)) for r, l in zip(rows, lanes)]


def dn_recur_fwd(prep, z, onorm, seq):
    T, C = z.shape
    n = seq // CHUNK
    nchunks = T // CHUNK
    hp = DN_HEADS_PER_STEP_FWD

    def body(u_ref, w_ref, qd_ref, kd_ref, aqk_ref, last_ref, z_ref, on_ref, y_ref, st_ref, o_ref):
        def step(i, state):
            st_ref[i] = state.astype(bf16)
            new, o = dn_step(state, u_ref[i], w_ref[i], qd_ref[i], kd_ref[i], aqk_ref[i], last_ref[i])
            o_ref[i] = o
            return new

        lax.fori_loop(0, n, step, jnp.zeros((hp, HEAD, HEAD), f32))
        for j in range(hp):
            cols = slice(j * HEAD, (j + 1) * HEAD)
            y_ref[:, cols] = dn_gate(o_ref[:, j].reshape(seq, HEAD), z_ref[:, cols], on_ref[...]).astype(bf16)

    blk = pl.BlockSpec((seq, hp * HEAD), lambda b, p: (b, p))
    per_chunk = lambda r: pl.BlockSpec((n, hp, r, HEAD), lambda b, p: (b, p, 0, 0))
    return _pcall(
        body, name="dn_recur_fwd", grid=(T // seq, N_HEADS // hp),
        in_specs=_recur_specs(n, hp) + [blk, pl.BlockSpec((1, HEAD), lambda b, p: (0, 0))],
        out_specs=[blk, per_chunk(HEAD), per_chunk(CHUNK)],
        out_shape=[_sds((T, C), bf16), _sds((nchunks, N_HEADS, HEAD, HEAD), bf16),
                   _sds((nchunks, N_HEADS, CHUNK, HEAD), f32)],
        compiler_params=_params(("parallel", "parallel"), VMEM_BIG),
    )(*prep, z, onorm)


_RECUR_CT_DTYPES = (f32, bf16, bf16, bf16, bf16, f32)


def dn_recur_bwd(prep, states, o, z, onorm, dy, seq):
    T, C = z.shape
    n = seq // CHUNK
    nchunks = T // CHUNK
    hp = DN_HEADS_PER_STEP_BWD

    def body(u_ref, w_ref, qd_ref, kd_ref, aqk_ref, last_ref, st_ref, o_ref, z_ref, on_ref, dy_ref,
             du_ref, dw_ref, dqd_ref, dkd_ref, daqk_ref, dlast_ref, dz_ref, don_ref, do_ref):
        args = lambda i: tuple(r[i].astype(f32) for r in (st_ref, u_ref, w_ref, qd_ref, kd_ref, aqk_ref, last_ref))
        ct_refs = (du_ref, dw_ref, dqd_ref, dkd_ref, daqk_ref, dlast_ref)
        first = jnp.logical_and(pl.program_id(0) == 0, pl.program_id(1) == 0)
        for j in range(hp):
            cols = slice(j * HEAD, (j + 1) * HEAD)
            _, gate_vjp = jax.vjp(dn_gate, o_ref[:, j].reshape(seq, HEAD), z_ref[:, cols], on_ref[...])
            do, dz, don = gate_vjp(dy_ref[:, cols])
            dz_ref[:, cols] = dz.astype(bf16)
            do_ref[:, j] = do.reshape(n, CHUNK, HEAD)
            _acc_rows(don_ref, don, jnp.logical_and(first, j == 0))

        def bstep(j, dstate):
            i = n - 1 - j
            _, vjp = jax.vjp(dn_step, *args(i))
            ds, *cts = vjp((dstate, do_ref[i]))
            for r, ct, dt in zip(ct_refs, cts, _RECUR_CT_DTYPES):
                r[i] = ct.astype(dt)
            return ds

        lax.fori_loop(0, n, bstep, jnp.zeros((hp, HEAD, HEAD), f32))

    blk = pl.BlockSpec((seq, hp * HEAD), lambda b, p: (b, p))
    per_chunk = lambda r: pl.BlockSpec((n, hp, r, HEAD), lambda b, p: (b, p, 0, 0))
    rows = [CHUNK, CHUNK, CHUNK, CHUNK, CHUNK, 1]
    lanes = [HEAD, HEAD, HEAD, HEAD, CHUNK, HEAD]
    return _pcall(
        body, name="dn_recur_bwd", grid=(T // seq, N_HEADS // hp),
        in_specs=_recur_specs(n, hp) + [per_chunk(HEAD), per_chunk(CHUNK), blk,
                                        pl.BlockSpec((1, HEAD), lambda b, p: (0, 0)), blk],
        out_specs=_recur_specs(n, hp) + [blk, pl.BlockSpec((8, HEAD), lambda b, p: (0, 0))],
        out_shape=[_sds((nchunks, N_HEADS, r, l), dt) for r, l, dt in zip(rows, lanes, _RECUR_CT_DTYPES)]
        + [_sds((T, C), bf16), _sds((8, HEAD), f32)],
        scratch_shapes=[pltpu.VMEM((n, hp, CHUNK, HEAD), f32)],
        compiler_params=_params(("arbitrary", "arbitrary"), VMEM_BIG),
    )(*prep, states, o, z, onorm, dy)


def sgu_math(up, vp, ng, nb, sw, sbias):
    S = up.shape[0]
    nblk = S // SGU_BLOCK
    u = jax.nn.gelu(up, approximate=True)
    v = jax.nn.gelu(vp, approximate=True)
    xc = v - jnp.mean(v, axis=-1, keepdims=True)
    vn = xc * lax.rsqrt(jnp.mean(xc * xc, axis=-1, keepdims=True) + EPS) * ng + nb
    ii = lax.broadcasted_iota(jnp.int32, (SGU_BLOCK, SGU_BLOCK), 0)
    jj = lax.broadcasted_iota(jnp.int32, (SGU_BLOCK, SGU_BLOCK), 1)
    outs = []
    for hd in range(N_HEADS):
        vh = vn[:, hd * HEAD:(hd + 1) * HEAD].reshape(nblk, SGU_BLOCK, HEAD)
        ws = jnp.where(ii >= jj, sw[hd], 0.0)
        mixed = bmm(jnp.broadcast_to(ws[None], (nblk, SGU_BLOCK, SGU_BLOCK)), vh) + sbias[hd][None]
        outs.append(mixed.reshape(S, HEAD))
    return u * jnp.concatenate(outs, axis=-1)


def sgu_fwd(up, vp, ng, nb, sw, sbias, seq):
    T, C = up.shape

    def body(up_ref, vp_ref, ng_ref, nb_ref, sw_ref, sb_ref, o_ref):
        o_ref[...] = sgu_math(up_ref[...], vp_ref[...], ng_ref[...], nb_ref[...], sw_ref[...],
                              sb_ref[...]).astype(bf16)

    blk = pl.BlockSpec((seq, C), lambda b: (b, 0))
    return _pcall(
        body, name="sgu_fwd", grid=(T // seq,),
        in_specs=[blk, blk, _resident((1, C)), _resident((1, C)), _resident(sw.shape), _resident(sbias.shape)],
        out_specs=blk, out_shape=_sds((T, C), bf16),
        compiler_params=_params(("parallel",), VMEM_BIG),
    )(up, vp, ng, nb, sw, sbias)


def sgu_bwd(up, vp, ng, nb, sw, sbias, dy, seq):
    T, C = up.shape

    def body(up_ref, vp_ref, ng_ref, nb_ref, sw_ref, sb_ref, dy_ref,
             dup_ref, dvp_ref, dng_ref, dnb_ref, dsw_ref, dsb_ref):
        _, vjp = jax.vjp(sgu_math, up_ref[...], vp_ref[...], ng_ref[...], nb_ref[...], sw_ref[...], sb_ref[...])
        dup, dvp, dng, dnb, dsw, dsb = vjp(dy_ref[...])
        dup_ref[...] = dup.astype(bf16)
        dvp_ref[...] = dvp.astype(bf16)
        first = pl.program_id(0) == 0
        _acc_rows(dng_ref, dng, first)
        _acc_rows(dnb_ref, dnb, first)

        @pl.when(first)
        def _():
            dsw_ref[...] = jnp.zeros_like(dsw_ref)
            dsb_ref[...] = jnp.zeros_like(dsb_ref)

        dsw_ref[...] += dsw
        dsb_ref[...] += dsb

    blk = pl.BlockSpec((seq, C), lambda b: (b, 0))
    small = pl.BlockSpec((8, C), lambda b: (0, 0))
    return _pcall(
        body, name="sgu_bwd", grid=(T // seq,),
        in_specs=[blk, blk, _resident((1, C)), _resident((1, C)), _resident(sw.shape), _resident(sbias.shape), blk],
        out_specs=[blk, blk, small, small, pl.BlockSpec(sw.shape, lambda b: (0, 0, 0)),
                   pl.BlockSpec(sbias.shape, lambda b: (0, 0, 0))],
        out_shape=[_sds((T, C), bf16), _sds((T, C), bf16), _sds((8, C), f32), _sds((8, C), f32),
                   _sds(sw.shape, f32), _sds(sbias.shape, f32)],
        compiler_params=_params(("arbitrary",), VMEM_BIG),
    )(up, vp, ng, nb, sw, sbias, dy)


def loss_fwd_bwd(h, gain, target):
    T, D = h.shape
    tm = _tile(T, 512)

    def body(h_ref, gain_ref, t_ref, loss_ref, dh_ref, dgain_ref):
        gain_ = gain_ref[...]
        y, xhat, rstd = _rms_fwd(h_ref[...], gain_)
        err = y - t_ref[...]
        part = 0.5 * jnp.sum(jnp.mean(err * err, axis=-1, keepdims=True), axis=0, keepdims=True)
        dh, dgain = _rms_bwd(err * (1.0 / D), xhat, rstd, gain_)
        dh_ref[...] = dh
        first = pl.program_id(0) == 0
        _acc_rows(dgain_ref, dgain, first)

        @pl.when(first)
        def _():
            loss_ref[...] = jnp.zeros_like(loss_ref)

        loss_ref[...] += jnp.broadcast_to(part, loss_ref.shape)

    row = pl.BlockSpec((tm, D), lambda i: (i, 0))
    return _pcall(
        body, name="loss_fwd_bwd", grid=(T // tm,),
        in_specs=[row, _resident((1, D)), row],
        out_specs=[pl.BlockSpec((8, 128), lambda i: (0, 0)), row, pl.BlockSpec((8, D), lambda i: (0, 0))],
        out_shape=[_sds((8, 128), f32), _sds((T, D), f32), _sds((8, D), f32)],
        compiler_params=_params(("arbitrary",), VMEM_MID),
    )(h, gain, target)


def adamw(w, g, m, v):
    c1, c2 = 1.0 - ADAM_B1 ** ADAM_STEP, 1.0 - ADAM_B2 ** ADAM_STEP
    if w.ndim == 3:
        grid, blk = (w.shape[0],), pl.BlockSpec((1,) + w.shape[1:], lambda i: (i, 0, 0))
    else:
        R, C = w.shape
        tr = max(c for c in range(8, 513, 8) if R % c == 0)
        grid, blk = (R // tr,), pl.BlockSpec((tr, C), lambda i: (i, 0))

    def body(w_ref, g_ref, m_ref, v_ref, d_ref, nm_ref, nv_ref):
        gg = g_ref[...]
        nm = ADAM_B1 * m_ref[...] + (1.0 - ADAM_B1) * gg
        nv = ADAM_B2 * v_ref[...] + (1.0 - ADAM_B2) * (gg * gg)
        d_ref[...] = -ADAM_LR * ((nm / c1) / (jnp.sqrt(nv / c2) + ADAM_EPS) + ADAM_WD * w_ref[...])
        nm_ref[...] = nm
        nv_ref[...] = nv

    return _pcall(
        body, name="adamw", grid=grid, in_specs=[blk] * 4, out_specs=[blk] * 3,
        out_shape=[_sds(w.shape, f32)] * 3, compiler_params=_params(("parallel",), VMEM_MID),
    )(w, g, m, v)


def sum8(parts):
    _, R, C = parts.shape
    tr = R
    for cand in (512, 352, 336, 320, 256, 128):
        if R % cand == 0:
            tr = cand
            break

    def body(p_ref, o_ref):
        acc = p_ref[0].astype(f32)
        for i in range(1, N_DEV):
            acc = acc + p_ref[i].astype(f32)
        o_ref[...] = acc

    return _pcall(
        body, name="sum8", grid=(R // tr,),
        in_specs=[pl.BlockSpec((N_DEV, tr, C), lambda i: (0, i, 0))],
        out_specs=pl.BlockSpec((tr, C), lambda i: (i, 0)), out_shape=_sds((R, C), f32),
        compiler_params=_params(("parallel",), VMEM_MID),
    )(parts)


_FLIPS = [(fx, fy, fc) for fx in (0, 1) for fy in (0, 1) for fc in (0, 1)][1:]
_HBM = pl.BlockSpec(memory_space=pltpu.HBM)


def _me():
    return lax.axis_index("x"), lax.axis_index("y"), lax.axis_index("c")


def _peer(flip):
    x, y, c = _me()
    fx, fy, fc = flip
    return (1 - x if fx else x, 1 - y if fy else y, 1 - c if fc else c)


def _lin(dev):
    return 4 * dev[0] + 2 * dev[1] + dev[2]


def _src_block(ref, rows, dev, whole):
    return ref if whole else ref.at[pl.ds(pl.multiple_of(dev * rows, 16), rows)]


_SEM = pl.BlockSpec(memory_space=pltpu.SEMAPHORE)
_EFFECT = pltpu.SideEffectType.DATAFLOW_SIDE_EFFECTING


def push_start(srcs, whole, after, name):
    n = len(srcs)
    rows = [s.shape[0] if whole else s.shape[0] // N_DEV for s in srcs]
    land_shapes = [(N_DEV, r, s.shape[1]) for r, s in zip(rows, srcs)]

    def body(*refs):
        src_refs, land_refs = refs[:n], refs[n:2 * n]
        send_sems, recv_sems, own_sems, token = refs[2 * n + 1], refs[2 * n + 2], refs[2 * n + 3], refs[-1]
        me = _lin(_me())
        for i in range(n):
            for k, flip in enumerate(_FLIPS):
                peer = _peer(flip)
                pltpu.make_async_remote_copy(
                    src_ref=_src_block(src_refs[i], rows[i], _lin(peer), whole), dst_ref=land_refs[i].at[me],
                    send_sem=send_sems.at[i * 7 + k], recv_sem=recv_sems.at[i * 7 + k],
                    device_id=peer, device_id_type=MESH).start()
        for i in range(n):
            pltpu.make_async_copy(_src_block(src_refs[i], rows[i], me, whole), land_refs[i].at[me],
                                  own_sems.at[i]).start()
        token[...] = jnp.zeros_like(token)

    hbm = lambda a: pltpu.with_memory_space_constraint(a, pltpu.HBM)
    outs = _pcall(
        body, name=name,
        in_specs=[_HBM] * (2 * n) + [pl.BlockSpec(memory_space=pl.ANY)],
        out_specs=[_SEM, _SEM, _SEM] + [_HBM] * (2 * n) + [pl.BlockSpec(memory_space=pltpu.VMEM)],
        out_shape=[pltpu.SemaphoreType.DMA((7 * n,)), pltpu.SemaphoreType.DMA((7 * n,)),
                   pltpu.SemaphoreType.DMA((n,))]
        + [pltpu.HBM(s.shape, s.dtype) for s in srcs]
        + [pltpu.HBM(shp, s.dtype) for shp, s in zip(land_shapes, srcs)] + [_sds((8, 128), f32)],
        input_output_aliases={i: 3 + i for i in range(2 * n)},
        compiler_params=pltpu.CompilerParams(has_side_effects=_EFFECT),
    )(*[hbm(s) for s in srcs], *[hbm(lax.empty(shp, s.dtype)) for shp, s in zip(land_shapes, srcs)], after)
    return outs[0], outs[1], outs[2], list(outs[3:3 + n]), list(outs[3 + n:3 + 2 * n]), outs[-1]


def push_wait(handle, whole, after, name):
    send_sems, recv_sems, own_sems, srcs, lands, _ = handle
    n = len(srcs)
    rows = [l.shape[1] for l in lands]

    def body(*refs):
        src_refs, land_refs = refs[:n], refs[n:2 * n]
        send_sems_, recv_sems_, own_sems_ = refs[2 * n], refs[2 * n + 1], refs[2 * n + 2]
        me = _lin(_me())
        for i in range(n):
            for k, flip in enumerate(_FLIPS):
                peer = _peer(flip)
                cp = pltpu.make_async_remote_copy(
                    src_ref=_src_block(src_refs[i], rows[i], _lin(peer), whole), dst_ref=land_refs[i].at[_lin(peer)],
                    send_sem=send_sems_.at[i * 7 + k], recv_sem=recv_sems_.at[i * 7 + k],
                    device_id=peer, device_id_type=MESH)
                cp.wait_send()
                cp.wait_recv()
            pltpu.make_async_copy(_src_block(src_refs[i], rows[i], me, whole), land_refs[i].at[me],
                                  own_sems_.at[i]).wait()

    outs = _pcall(
        body, name=name,
        in_specs=[_HBM] * (2 * n) + [_SEM, _SEM, _SEM, pl.BlockSpec(memory_space=pl.ANY)],
        out_specs=[_HBM] * (2 * n),
        out_shape=[pltpu.HBM(a.shape, a.dtype) for a in srcs + lands],
        input_output_aliases={i: i for i in range(2 * n)},
        compiler_params=pltpu.CompilerParams(has_side_effects=_EFFECT),
    )(*srcs, *lands, send_sems, recv_sems, own_sems, after)
    return list(outs[n:])


def all_gather_small(x):
    R, C = x.shape

    def body(x_ref, o_ref, send_sems, recv_sems):
        me = _lin(_me())
        o_ref[me] = x_ref[...]
        sends = []
        for k, flip in enumerate(_FLIPS):
            rc = pltpu.make_async_remote_copy(
                src_ref=x_ref, dst_ref=o_ref.at[me], send_sem=send_sems.at[k], recv_sem=recv_sems.at[k],
                device_id=_peer(flip), device_id_type=MESH)
            rc.start()
            sends.append(rc)
        for k, flip in enumerate(_FLIPS):
            pltpu.make_async_remote_copy(
                src_ref=x_ref, dst_ref=o_ref.at[_lin(_peer(flip))], send_sem=send_sems.at[k],
                recv_sem=recv_sems.at[k], device_id=_peer(flip), device_id_type=MESH).wait_recv()
        for rc in sends:
            rc.wait_send()

    vm = pl.BlockSpec(memory_space=pltpu.VMEM)
    return _pcall(
        body, name="all_gather_small", in_specs=[vm], out_specs=vm, out_shape=_sds((N_DEV, R, C), x.dtype),
        scratch_shapes=[pltpu.SemaphoreType.DMA((7,)), pltpu.SemaphoreType.DMA((7,))],
        compiler_params=_params(None, VMEM_MID),
    )(x)


def sum_slots(parts):
    _, R, C = parts.shape

    def body(p_ref, o_ref):
        acc = p_ref[0]
        for p in range(1, N_DEV):
            acc = acc + p_ref[p]
        o_ref[...] = acc

    vm = pl.BlockSpec(memory_space=pltpu.VMEM)
    return _pcall(body, name="sum_slots", in_specs=[vm], out_specs=vm, out_shape=_sds((R, C), f32),
                  compiler_params=_params(None, VMEM_MID))(parts)


_PACK_ROWS = 8


def _packed_rows(shape):
    return -(-math.prod(shape) // (128 * _PACK_ROWS)) * _PACK_ROWS


def _pack(arrs):
    parts = []
    for a in arrs:
        flat = a.reshape(-1).astype(f32)
        rows = _packed_rows(a.shape)
        parts.append(jnp.pad(flat, (0, rows * 128 - flat.shape[0])).reshape(rows, 128))
    return jnp.concatenate(parts, axis=0)


def _unpack(buf, shapes):
    out, off = [], 0
    for s in shapes:
        rows = _packed_rows(s)
        out.append(buf[off:off + rows].reshape(-1)[:math.prod(s)].reshape(s))
        off += rows
    return out


def _row(v):
    return v.reshape(1, -1)


def _lane_row(vals, offset):
    return jnp.pad(vals.reshape(1, -1), ((0, 0), (offset, HEAD - offset - vals.shape[-1])))


def kernel(x, ffn1_norm, ffn1_w_gate, ffn1_w_up, ffn1_w_down, mix_norm, ffn2_norm, ffn2_w_gate, ffn2_w_up, ffn2_w_down, ab_w_in, pool_w, pool_scale, dn_conv_w, dn_a_log, dn_dt_bias, dn_out_norm, ab_w_out, cd_w_in, sgu_norm_g, sgu_norm_b, sgu_w, sgu_bias, sc_conv_w, cd_w_out, final_norm, loss_target, m_ffn1_norm, m_ffn1_w_gate, m_ffn1_w_up, m_ffn1_w_down, m_mix_norm, m_ffn2_norm, m_ffn2_w_gate, m_ffn2_w_up, m_ffn2_w_down, m_ab_w_in, m_pool_w, m_pool_scale, m_dn_conv_w, m_dn_a_log, m_dn_dt_bias, m_dn_out_norm, m_ab_w_out, m_cd_w_in, m_sgu_norm_g, m_sgu_norm_b, m_sgu_w, m_sgu_bias, m_sc_conv_w, m_cd_w_out, m_final_norm, v_ffn1_norm, v_ffn1_w_gate, v_ffn1_w_up, v_ffn1_w_down, v_mix_norm, v_ffn2_norm, v_ffn2_w_gate, v_ffn2_w_up, v_ffn2_w_down, v_ab_w_in, v_pool_w, v_pool_scale, v_dn_conv_w, v_dn_a_log, v_dn_dt_bias, v_dn_out_norm, v_ab_w_out, v_cd_w_in, v_sgu_norm_g, v_sgu_norm_b, v_sgu_w, v_sgu_bias, v_sc_conv_w, v_cd_w_out, v_final_norm):
    names = ['ffn1_norm', 'ffn1_w_gate', 'ffn1_w_up', 'ffn1_w_down', 'mix_norm', 'ffn2_norm', 'ffn2_w_gate',
             'ffn2_w_up', 'ffn2_w_down', 'ab_w_in', 'pool_w', 'pool_scale', 'dn_conv_w', 'dn_a_log', 'dn_dt_bias',
             'dn_out_norm', 'ab_w_out', 'cd_w_in', 'sgu_norm_g', 'sgu_norm_b', 'sgu_w', 'sgu_bias', 'sc_conv_w',
             'cd_w_out', 'final_norm']
    loc = locals()
    W = {n: loc[n] for n in names}
    M = {n: loc['m_' + n] for n in names}
    V = {n: loc['v_' + n] for n in names}

    B, S, D = x.shape
    T = B * S
    me = _lin(_me())

    def rows_of(w):
        return w.astype(bf16).T

    def layer_shards(layer):
        e = layer // 2
        shards = [rows_of(W['ffn1_w_gate'][layer]), rows_of(W['ffn1_w_up'][layer]), W['ffn1_w_down'][layer].astype(bf16)]
        if layer % 2 == 0:
            win = jnp.pad(rows_of(W['ab_w_in'][e]), ((0, AB_SHARD_PAD - AB_SHARD), (0, 0)))
            wout = W['ab_w_out'][e].astype(bf16)
        else:
            win = rows_of(W['cd_w_in'][e])
            wout = W['cd_w_out'][e].astype(bf16)
        shards += [win, wout]
        shards += [rows_of(W['ffn2_w_gate'][layer]), rows_of(W['ffn2_w_up'][layer]), W['ffn2_w_down'][layer].astype(bf16)]
        return shards

    def full_weight(idx, land, layer):
        if idx == 3 and layer % 2 == 0:
            return jnp.pad(land[:, :AB_SHARD].reshape(AB_IN, D), ((0, AB_IN_PAD - AB_IN), (0, 0)))
        return land

    def tied(gain, token):
        return gain if token is None else gain + token[0:1, 0:1]

    def start_groups(arrays, groups, whole, after, name):
        out = []
        for gi, idx in enumerate(groups):
            suffix = "" if len(groups) == 1 else "abc"[gi]
            handle = push_start([arrays[i] for i in idx], whole, after, name=name + suffix)
            after = handle[5]
            out.append((idx, handle, suffix))
        return out

    def wait_group(inflight, gi, whole, after, name):
        idx, handle, suffix = inflight[gi]
        return dict(zip(idx, push_wait(handle, whole, after, name=name + suffix)))

    one_group = [tuple(range(8))]
    by_block = [(0, 1, 2), (3, 4), (5, 6, 7)]
    zeros_tile = jnp.zeros((8, 128), f32)
    gathered = [None] * DEPTH
    small_shards = [W['dn_conv_w'], W['sgu_norm_g'], W['sgu_norm_b'], W['sc_conv_w']]
    gs = all_gather_small(_pack(small_shards))
    per_dev = [_unpack(gs[p], [a.shape for a in small_shards]) for p in range(N_DEV)]
    dn_conv_full, sgu_g_full, sgu_b_full, sc_conv_full = [
        jnp.concatenate([per_dev[p][i] for p in range(N_DEV)], axis=-1) for i in range(4)]

    inflight = start_groups(layer_shards(0), by_block, True, gs, "gather_start_0")

    h = x.reshape(T, D)
    saved = []
    for layer in range(DEPTH):
        e = layer // 2
        mine, landed = inflight, {}
        landed.update(wait_group(mine, 0, True, h if layer else mine[-1][1][5], f"gather_wait_{layer}"))
        token = None
        if layer + 1 < DEPTH:
            inflight = start_groups(layer_shards(layer + 1), one_group, True, landed[0], f"gather_start_{layer + 1}")
            token = inflight[-1][1][5]
        wg1, wu1, wd1 = [full_weight(i, landed[i], layer) for i in (0, 1, 2)]
        sv = {'h0': h}
        h, sv['g1'], sv['u1'] = ffn_fwd(h, tied(_row(W['ffn1_norm'][layer]), token), wg1, wu1, wd1)
        sv['h1'] = h
        if len(mine) > 1:
            landed.update(wait_group(mine, 1, True, h, f"gather_wait_{layer}"))
        win, wout = [full_weight(i, landed[i], layer) for i in (3, 4)]
        if layer % 2 == 0:
            a_in, qkv_pre, z, bg = in_proj_fwd(h, _row(W['mix_norm'][layer]), win, [512, 1536, 512, 128])
            ya = pool_fwd(a_in, W['pool_w'][e], _row(W['pool_scale'][e]), S)
            qkv = seq_chan_fwd(conv_silu_math, dn_conv_full[e], [qkv_pre], S, f32, "dn_conv_fwd")
            alog, dtb = _lane_row(W['dn_a_log'][e], N_HEADS), _lane_row(W['dn_dt_bias'][e], N_HEADS)
            prep = dn_prep_fwd(qkv, bg, alog, dtb)
            yb, states, o_dn = dn_recur_fwd(prep, z, _row(W['dn_out_norm'][e]), S)
            sv.update(a_in=a_in, qkv_pre=qkv_pre, z=z, bg=bg, qkv=qkv, alog=alog, dtb=dtb, prep=prep,
                      states=states, o_dn=o_dn)
        else:
            up, vp, xd, bgate, cg = in_proj_fwd(h, _row(W['mix_norm'][layer]), win, [512] * 5)
            sbias = W['sgu_bias'][e].reshape(N_HEADS, SGU_BLOCK, 1)
            ya = sgu_fwd(up, vp, _row(sgu_g_full[e]), _row(sgu_b_full[e]), W['sgu_w'][e], sbias,
                         _tile(S, SGU_TILE))
            yb = seq_chan_fwd(gated_conv_math, sc_conv_full[e], [xd, bgate, cg], S, bf16, "sc_conv_fwd")
            sv.update(up=up, vp=vp, xd=xd, bgate=bgate, cg=cg, sbias=sbias)
        sv.update(ya=ya, yb=yb)
        h = out_proj_fwd(h, ya, yb, wout)
        sv['h2'] = h
        if len(mine) > 1:
            landed.update(wait_group(mine, 2, True, h, f"gather_wait_{layer}"))
        wg2, wu2, wd2 = [full_weight(i, landed[i], layer) for i in (5, 6, 7)]
        h, sv['g2'], sv['u2'] = ffn_fwd(h, _row(W['ffn2_norm'][layer]), wg2, wu2, wd2)
        gathered[layer] = [wg1, wu1, wd1, win, wout, wg2, wu2, wd2]
        saved.append(sv)

    loss_part, dh, dfinal = loss_fwd_bwd(h, _row(W['final_norm']), loss_target.reshape(T, D))
    loss = lax.psum(loss_part[0, 0], ("x", "y", "c"))

    G = {}
    G['final_norm'] = dfinal[0]
    for n in ('ffn1_norm', 'mix_norm', 'ffn2_norm'):
        G[n] = [None] * DEPTH
    for n in ('pool_w', 'pool_scale', 'dn_conv_w', 'dn_a_log', 'dn_dt_bias', 'dn_out_norm',
              'sgu_norm_g', 'sgu_norm_b', 'sgu_w', 'sgu_bias', 'sc_conv_w'):
        G[n] = [None] * 2
    small_names = ['ffn1_norm', 'mix_norm', 'ffn2_norm', 'pool_w', 'pool_scale', 'dn_conv_w', 'dn_a_log',
                   'dn_dt_bias', 'dn_out_norm', 'sgu_norm_g', 'sgu_norm_b', 'sgu_w', 'sgu_bias', 'sc_conv_w',
                   'final_norm']
    big = [None] * DEPTH
    inflight = None
    last = []

    for layer in reversed(range(DEPTH)):
        e = layer // 2
        sv = saved[layer]
        wg1, wu1, wd1, win, wout, wg2, wu2, wd2 = gathered[layer]
        token = None if inflight is None else inflight[-1][1][5]
        dh, dgain, xn, act, dg, du, dy = ffn_bwd(sv['h2'], tied(_row(W['ffn2_norm'][layer]), token), sv['g2'],
                                                  sv['u2'], dh, wg2, wu2, wd2)
        G['ffn2_norm'][layer] = dgain[0]
        dwg2, dwu2, dwd2 = wgrad(dg, xn, zeros_tile), wgrad(du, xn, zeros_tile), wgrad(act, dy, zeros_tile)
        mix_token = ffn1_token = None
        if layer == 0:
            last += start_groups({5: dwg2, 6: dwu2, 7: dwd2}, [(5, 6, 7)], False, dwd2, "scatter_start_0a")
            mix_token = last[-1][1][5]
        dya, dyb, dwout = out_proj_bwd(dh, sv['ya'], sv['yb'], wout)
        if layer % 2 == 0:
            da, dpw, dsc = pool_bwd(sv['a_in'], W['pool_w'][e], _row(W['pool_scale'][e]), dya, S)
            G['pool_w'][e], G['pool_scale'][e] = dpw, dsc[0]
            *cts, dz, don = dn_recur_bwd(sv['prep'], sv['states'], sv['o_dn'], sv['z'],
                                         _row(W['dn_out_norm'][e]), dyb, S)
            G['dn_out_norm'][e] = don[0]
            dqkv, dbg, dalog, ddtb = dn_prep_bwd(sv['qkv'], sv['bg'], sv['alog'], sv['dtb'], cts)
            G['dn_a_log'][e], G['dn_dt_bias'][e] = dalog[0, N_HEADS:2 * N_HEADS], ddtb[0, N_HEADS:2 * N_HEADS]
            dqkv_pre, dconv = seq_chan_bwd(conv_silu_math, dn_conv_full[e], [sv['qkv_pre']], dqkv, S, bf16,
                                           "dn_conv_bwd")
            G['dn_conv_w'][e] = dconv
            dpieces = [da, dqkv_pre, dz, dbg]
        else:
            dup, dvp, dng, dnb, dsw, dsb = sgu_bwd(sv['up'], sv['vp'], _row(sgu_g_full[e]), _row(sgu_b_full[e]),
                                                   W['sgu_w'][e], sv['sbias'], dya, _tile(S, SGU_TILE))
            G['sgu_norm_g'][e], G['sgu_norm_b'][e] = dng[0], dnb[0]
            G['sgu_w'][e], G['sgu_bias'][e] = dsw, dsb.reshape(N_HEADS, SGU_BLOCK)
            dxd, dbgate, dcg, dscw = seq_chan_bwd(gated_conv_math, sc_conv_full[e],
                                                  [sv['xd'], sv['bgate'], sv['cg']], dyb, S, bf16, "sc_conv_bwd")
            G['sc_conv_w'][e] = dscw
            dpieces = [dup, dvp, dxd, dbgate, dcg]
        dh, dgain, dwin = in_proj_bwd(sv['h1'], tied(_row(W['mix_norm'][layer]), mix_token), dpieces, dh, win)
        G['mix_norm'][layer] = dgain[0]
        if layer % 2 == 0:
            dwin = jnp.pad(dwin[:AB_IN].reshape(N_DEV, AB_SHARD, D), ((0, 0), (0, AB_SHARD_PAD - AB_SHARD), (0, 0)))
            dwin = dwin.reshape(N_DEV * AB_SHARD_PAD, D)
        if layer == 0:
            last += start_groups({3: dwin, 4: dwout}, [(3, 4)], False, mix_token, "scatter_start_0b")
            ffn1_token = last[-1][1][5]
        dh, dgain, xn, act, dg, du, dy = ffn_bwd(sv['h0'], tied(_row(W['ffn1_norm'][layer]), ffn1_token), sv['g1'],
                                                  sv['u1'], dh, wg1, wu1, wd1)
        G['ffn1_norm'][layer] = dgain[0]
        order = zeros_tile
        if layer == 0:
            small_g = [G[n] if n == 'final_norm' else jnp.stack(G[n]) for n in small_names]
            small_x = start_groups([_pack(small_g)], [(0,)], True, ffn1_token, "small_grads_start")
            order = small_x[-1][1][5]
        dwg1, dwu1, dwd1 = wgrad(dg, xn, order), wgrad(du, xn, order), wgrad(act, dy, order)
        if layer == 0:
            last += start_groups({0: dwg1, 1: dwu1, 2: dwd1}, [(0, 1, 2)], False, order, "scatter_start_0c")
        else:
            after = zeros_tile
            if inflight is not None:
                big[layer + 1] = wait_group(inflight, 0, False, dh, f"scatter_wait_{layer + 1}")
                after = big[layer + 1][0]
            inflight = start_groups([dwg1, dwu1, dwd1, dwin, dwout, dwg2, dwu2, dwd2], one_group, False, after,
                                    f"scatter_start_{layer}")

    grad_x = dh.reshape(B, S, D)

    reduced_buf = sum_slots(wait_group(small_x, 0, True, last[-1][1][5], "small_grads_wait")[0])
    reduced = _unpack(reduced_buf, [a.shape for a in small_g])
    big[1] = wait_group(inflight, 0, False, reduced_buf, "scatter_wait_1")
    big[0] = wait_group(last, 0, False, big[1][0], "scatter_wait_0a")
    big[0].update(wait_group(last, 1, False, big[0][5], "scatter_wait_0b"))
    grads = {}
    for n, g in zip(small_names, reduced):
        if n in ('dn_conv_w', 'sgu_norm_g', 'sgu_norm_b', 'sc_conv_w'):
            c = W[n].shape[-1]
            g = lax.dynamic_slice_in_dim(g, me * c, c, axis=g.ndim - 1)
        grads[n] = g

    def stack_layers(idx, sel):
        out = []
        for layer in sel:
            g = sum8(big[layer][idx])
            out.append(g[:AB_SHARD] if idx == 3 and layer % 2 == 0 else g)
        return jnp.stack(out)

    all_layers, even, odd = range(DEPTH), range(0, DEPTH, 2), range(1, DEPTH, 2)
    delta, new_m, new_v = {}, {}, {}

    def update(n, idx, transposed, sel):
        view = (lambda a: jnp.swapaxes(a, 1, 2)) if transposed else (lambda a: a)
        g = stack_layers(idx, sel)
        d_, m_, v_ = adamw(view(W[n]), g, view(M[n]), view(V[n]))
        grads[n], delta[n], new_m[n], new_v[n] = view(g), view(d_), view(m_), view(v_)

    shapes = [W[n].shape for n in small_names]
    d_, m_, v_ = adamw(_pack([W[n] for n in small_names]), _pack([grads[n] for n in small_names]),
                       _pack([M[n] for n in small_names]), _pack([V[n] for n in small_names]))
    for n, a, b_, c_ in zip(small_names, _unpack(d_, shapes), _unpack(m_, shapes), _unpack(v_, shapes)):
        delta[n], new_m[n], new_v[n] = a, b_, c_
    update('ab_w_in', 3, True, even)
    update('cd_w_in', 3, True, odd)
    update('ab_w_out', 4, False, even)
    update('cd_w_out', 4, False, odd)
    update('ffn2_w_gate', 5, True, all_layers)
    update('ffn2_w_up', 6, True, all_layers)
    update('ffn2_w_down', 7, False, all_layers)
    updated = sum(lax.slice(a, (0,) * a.ndim, (1,) * a.ndim).reshape(1) for a in new_v.values())
    big[0].update(wait_group(last, 2, False, updated, "scatter_wait_0c"))
    update('ffn1_w_gate', 0, True, all_layers)
    update('ffn1_w_up', 1, True, all_layers)
    update('ffn1_w_down', 2, False, all_layers)

    return (loss, grad_x, *[grads[n] for n in names], *[delta[n] for n in names],
            *[new_m[n] for n in names], *[new_v[n] for n in names])
```

```python
import functools
import math

import jax
import jax.numpy as jnp
from jax import lax
from jax.experimental import pallas as pl
from jax.experimental.pallas import tpu as pltpu

f32, bf16 = jnp.float32, jnp.bfloat16

D_MODEL = 1024
DEPTH = 4
CHUNK = 64
POOL_WINDOWS = (2, 4, 8, 16)
HEAD = 128
N_HEADS = 4
SGU_BLOCK = 128
SGU_TILE = 512
FFN_DIM = 2816
AB_IN = 2568
AB_IN_PAD = 2688
AB_SHARD = 321
AB_SHARD_PAD = 336
EPS = 1e-6
N_DEV = 8
MESH = pl.DeviceIdType.MESH

ADAM_LR, ADAM_B1, ADAM_B2, ADAM_EPS, ADAM_WD, ADAM_STEP = 0.001, 0.9, 0.999, 1e-08, 0.01, 10

VMEM_MAX = 62 * 1024 * 1024
VMEM_BIG = 56 * 1024 * 1024
VMEM_MID = 40 * 1024 * 1024


def _pcall(body, **kw):
    return pl.pallas_call(body, **kw)


def _params(sem=None, vmem=None):
    return pltpu.CompilerParams(dimension_semantics=sem, vmem_limit_bytes=vmem)


def _sds(shape, dtype):
    return jax.ShapeDtypeStruct(shape, dtype)


_NN2, _NT2, _TN2 = (((1,), (0,)), ((), ())), (((1,), (1,)), ((), ())), (((0,), (0,)), ((), ()))
_NN3, _NT3, _TN3 = (((2,), (1,)), ((0,), (0,))), (((2,), (2,)), ((0,), (0,))), (((1,), (1,)), ((0,), (0,)))


def _dg(a, b, dims, hi):
    if hi:
        return lax.dot_general(a.astype(f32), b.astype(f32), dims, preferred_element_type=f32,
                               precision=lax.Precision.HIGH)
    return lax.dot_general(a.astype(bf16), b.astype(bf16), dims, preferred_element_type=f32)


def _make_mm(nn, nt, tn, hi):
    @jax.custom_vjp
    def mm(a, b):
        return _dg(a, b, nn, hi)

    def mm_bwd(res, ct):
        a, b = res
        return _dg(ct, b, nt, hi).astype(a.dtype), _dg(a, ct, tn, hi).astype(b.dtype)

    mm.defvjp(lambda a, b: (_dg(a, b, nn, hi), (a, b)), mm_bwd)

    @jax.custom_vjp
    def mm_nt(a, b):
        return _dg(a, b, nt, hi)

    def mm_nt_bwd(res, ct):
        a, b = res
        return _dg(ct, b, nn, hi).astype(a.dtype), _dg(ct, a, tn, hi).astype(b.dtype)

    mm_nt.defvjp(lambda a, b: (_dg(a, b, nt, hi), (a, b)), mm_nt_bwd)

    @jax.custom_vjp
    def mm_tn(a, b):
        return _dg(a, b, tn, hi)

    def mm_tn_bwd(res, ct):
        a, b = res
        return _dg(b, ct, nt, hi).astype(a.dtype), _dg(a, ct, nn, hi).astype(b.dtype)

    mm_tn.defvjp(lambda a, b: (_dg(a, b, tn, hi), (a, b)), mm_tn_bwd)
    return mm, mm_nt, mm_tn


mm, mm_nt, mm_tn = _make_mm(_NN2, _NT2, _TN2, False)
bmm, bmm_nt, bmm_tn = _make_mm(_NN3, _NT3, _TN3, False)
bmm_hi, _, _ = _make_mm(_NN3, _NT3, _TN3, True)


def _shift_raw(x, k):
    n = x.shape[0]
    t = lax.broadcasted_iota(jnp.int32, x.shape, 0)
    if k > 0:
        return jnp.where(t >= k, pltpu.roll(x, k, axis=0), 0.0)
    k = -k
    return jnp.where(t < n - k, pltpu.roll(x, n - k, axis=0), 0.0)


@functools.partial(jax.custom_vjp, nondiff_argnums=(1,))
def shift(x, k):
    return _shift_raw(x, k)


shift.defvjp(lambda x, k: (_shift_raw(x, k), None), lambda k, _, ct: (_shift_raw(ct, -k),))


def _silu(x):
    return x * jax.nn.sigmoid(x)


def _softplus(x):
    return jnp.maximum(x, 0.0) + jnp.log(1.0 + jnp.exp(-jnp.abs(x)))


def _rms_fwd(h, gain):
    rstd = lax.rsqrt(jnp.mean(h * h, axis=-1, keepdims=True) + EPS)
    xhat = h * rstd
    return xhat * gain, xhat, rstd


def _rms_bwd(dxn, xhat, rstd, gain):
    dxhat = dxn * gain
    dh = rstd * (dxhat - xhat * jnp.mean(dxhat * xhat, axis=-1, keepdims=True))
    return dh, jnp.sum(dxn * xhat, axis=0, keepdims=True)


def _acc_rows(ref, val, first):
    @pl.when(first)
    def _():
        ref[...] = jnp.zeros_like(ref)
    ref[0:1, :] += val


def _tile(n, cap):
    t = min(n, cap)
    assert n % t == 0, (n, t)
    return t


def _resident(shape):
    nd = len(shape)
    return pl.BlockSpec(shape, lambda *_: (0,) * nd, pipeline_mode=pl.Buffered(1))


def _rows(w):
    return math.prod(w.shape[:-1])


def _w2d(ref):
    w = ref[...]
    return w.reshape(-1, w.shape[-1]) if w.ndim == 3 else w


def ffn_fwd(h, gain, wgt, wut, wd):
    T, D = h.shape
    F = _rows(wgt)
    tm = _tile(T, 512)

    def body(h_ref, gain_ref, wg_ref, wu_ref, wd_ref, ho_ref, g_ref, u_ref):
        hh = h_ref[...]
        xn, _, _ = _rms_fwd(hh, gain_ref[...])
        xb = xn.astype(bf16)
        g = _dg(xb, _w2d(wg_ref), _NT2, False)
        u = _dg(xb, _w2d(wu_ref), _NT2, False)
        y = _dg(_silu(g) * u, _w2d(wd_ref), _NN2, False)
        ho_ref[...] = hh + 0.5 * y
        g_ref[...] = g.astype(bf16)
        u_ref[...] = u.astype(bf16)

    row = lambda w: pl.BlockSpec((tm, w), lambda i: (i, 0))
    return _pcall(
        body, name="ffn_fwd", grid=(T // tm,),
        in_specs=[row(D), _resident((1, D)), _resident(wgt.shape), _resident(wut.shape), _resident(wd.shape)],
        out_specs=[row(D), row(F), row(F)],
        out_shape=[_sds((T, D), f32), _sds((T, F), bf16), _sds((T, F), bf16)],
        compiler_params=_params(("parallel",), VMEM_MAX),
    )(h, gain, wgt, wut, wd)


def ffn_bwd(h, gain, g, u, dout, wgt, wut, wd):
    T, D = h.shape
    F = _rows(wgt)
    tm = _tile(T, 256)

    def body(h_ref, gain_ref, g_ref, u_ref, do_ref, wg_ref, wu_ref, wd_ref,
             dh_ref, dgain_ref, xn_ref, act_ref, dg_ref, du_ref, dy_ref):
        hh, dout_ = h_ref[...], do_ref[...]
        gain_ = gain_ref[...]
        xn, xhat, rstd = _rms_fwd(hh, gain_)
        gg, uu = g_ref[...].astype(f32), u_ref[...].astype(f32)
        dy = (0.5 * dout_).astype(bf16)
        dact = _dg(dy, _w2d(wd_ref), _NT2, False)
        sg = jax.nn.sigmoid(gg)
        silu = gg * sg
        dgate = (dact * uu * (sg * (1.0 + gg * (1.0 - sg)))).astype(bf16)
        dup = (dact * silu).astype(bf16)
        dxn = _dg(dgate, _w2d(wg_ref), _NN2, False) + _dg(dup, _w2d(wu_ref), _NN2, False)
        dh, dgain = _rms_bwd(dxn, xhat, rstd, gain_)
        dh_ref[...] = dout_ + dh
        _acc_rows(dgain_ref, dgain, pl.program_id(0) == 0)
        xn_ref[...] = xn.astype(bf16)
        act_ref[...] = (silu * uu).astype(bf16)
        dg_ref[...] = dgate
        du_ref[...] = dup
        dy_ref[...] = dy

    row = lambda w: pl.BlockSpec((tm, w), lambda i: (i, 0))
    return _pcall(
        body, name="ffn_bwd", grid=(T // tm,),
        in_specs=[row(D), _resident((1, D)), row(F), row(F), row(D),
                  _resident(wgt.shape), _resident(wut.shape), _resident(wd.shape)],
        out_specs=[row(D), pl.BlockSpec((8, D), lambda i: (0, 0)), row(D), row(F), row(F), row(F), row(D)],
        out_shape=[_sds((T, D), f32), _sds((8, D), f32), _sds((T, D), bf16), _sds((T, F), bf16),
                   _sds((T, F), bf16), _sds((T, F), bf16), _sds((T, D), bf16)],
        compiler_params=_params(("arbitrary",), VMEM_BIG),
    )(h, gain, g, u, dout, wgt, wut, wd)


def _col_tile(n, cap=1408):
    best = None
    for c in range(128, cap + 1, 128):
        if n % c == 0:
            best = c
    assert best is not None, n
    return best


def wgrad(a, b, after):
    T, N = a.shape
    K = b.shape[1]
    nc, tk = _col_tile(N), _tile(T, 2048)
    nk = T // tk

    def body(a_ref, b_ref, after_ref, o_ref, acc_ref):
        k = pl.program_id(1)

        @pl.when(k == 0)
        def _():
            acc_ref[...] = jnp.zeros_like(acc_ref)

        acc_ref[...] += _dg(a_ref[...], b_ref[...], _TN2, False)

        @pl.when(k == nk - 1)
        def _():
            o_ref[...] = acc_ref[...].astype(bf16)

    return _pcall(
        body, name="wgrad", grid=(N // nc, nk),
        in_specs=[pl.BlockSpec((tk, nc), lambda j, k: (k, j)), pl.BlockSpec((tk, K), lambda j, k: (k, 0)),
                  pl.BlockSpec(memory_space=pl.ANY)],
        out_specs=pl.BlockSpec((nc, K), lambda j, k: (j, 0)),
        out_shape=_sds((N, K), bf16),
        scratch_shapes=[pltpu.VMEM((nc, K), f32)],
        compiler_params=_params(("parallel", "arbitrary"), VMEM_BIG),
    )(a, b, after)


def in_proj_fwd(h, gain, wt, widths):
    T, D = h.shape
    N = _rows(wt)
    assert sum(widths) == N
    tm = _tile(T, 512)
    offs = [sum(widths[:i]) for i in range(len(widths))]

    def body(h_ref, gain_ref, w_ref, *outs):
        xn, _, _ = _rms_fwd(h_ref[...], gain_ref[...])
        p = _dg(xn, _w2d(w_ref), _NT2, False)
        for o_ref, off, wd_ in zip(outs, offs, widths):
            o_ref[...] = p[:, off:off + wd_]

    row = lambda w: pl.BlockSpec((tm, w), lambda i: (i, 0))
    return _pcall(
        body, name="in_proj_fwd", grid=(T // tm,),
        in_specs=[row(D), _resident((1, D)), _resident(wt.shape)],
        out_specs=[row(w) for w in widths],
        out_shape=[_sds((T, w), f32) for w in widths],
        compiler_params=_params(("parallel",), VMEM_BIG),
    )(h, gain, wt)


def in_proj_bwd(h, gain, dpieces, dout, wt):
    T, D = h.shape
    N = _rows(wt)
    widths = [p.shape[1] for p in dpieces]
    assert sum(widths) == N
    tm = _tile(T, 512)
    nt = T // tm
    npc = len(dpieces)

    def body(*refs):
        h_ref, gain_ref = refs[0], refs[1]
        p_refs = refs[2:2 + npc]
        do_ref, w_ref, dh_ref, dgain_ref, dw_ref, acc_ref = refs[2 + npc:]
        i = pl.program_id(0)
        gain_ = gain_ref[...]
        xn, xhat, rstd = _rms_fwd(h_ref[...], gain_)
        dp = jnp.concatenate([r[...].astype(bf16) for r in p_refs], axis=-1)
        dxn = _dg(dp, _w2d(w_ref), _NN2, False)
        dh, dgain = _rms_bwd(dxn, xhat, rstd, gain_)
        dh_ref[...] = do_ref[...] + dh
        _acc_rows(dgain_ref, dgain, i == 0)

        @pl.when(i == 0)
        def _():
            acc_ref[...] = jnp.zeros_like(acc_ref)

        acc_ref[...] += _dg(dp, xn, _TN2, False)

        @pl.when(i == nt - 1)
        def _():
            dw_ref[...] = acc_ref[...].astype(bf16)

    row = lambda w: pl.BlockSpec((tm, w), lambda i: (i, 0))
    return _pcall(
        body, name="in_proj_bwd", grid=(nt,),
        in_specs=[row(D), _resident((1, D))] + [row(w) for w in widths] + [row(D), _resident(wt.shape)],
        out_specs=[row(D), pl.BlockSpec((8, D), lambda i: (0, 0)), pl.BlockSpec((N, D), lambda i: (0, 0))],
        out_shape=[_sds((T, D), f32), _sds((8, D), f32), _sds((N, D), bf16)],
        scratch_shapes=[pltpu.VMEM((N, D), f32)],
        compiler_params=_params(("arbitrary",), VMEM_BIG),
    )(h, gain, *dpieces, dout, wt)


def out_proj_fwd(h, ya, yb, w):
    T, D = h.shape
    half = ya.shape[1]
    tm = _tile(T, 512)

    def body(h_ref, ya_ref, yb_ref, w_ref, o_ref):
        y = jnp.concatenate([ya_ref[...], yb_ref[...]], axis=-1)
        o_ref[...] = h_ref[...] + _dg(y, _w2d(w_ref), _NN2, False)

    row = lambda w_: pl.BlockSpec((tm, w_), lambda i: (i, 0))
    return _pcall(
        body, name="out_proj_fwd", grid=(T // tm,),
        in_specs=[row(D), row(half), row(half), _resident(w.shape)],
        out_specs=row(D), out_shape=_sds((T, D), f32),
        compiler_params=_params(("parallel",), VMEM_MID),
    )(h, ya, yb, w)


def out_proj_bwd(dout, ya, yb, w):
    T, D = dout.shape
    half = ya.shape[1]
    tm = _tile(T, 512)
    nt = T // tm

    def body(do_ref, ya_ref, yb_ref, w_ref, dya_ref, dyb_ref, dw_ref, acc_ref):
        i = pl.program_id(0)
        dob = do_ref[...].astype(bf16)
        dy = _dg(dob, _w2d(w_ref), _NT2, False)
        dya_ref[...] = dy[:, :half]
        dyb_ref[...] = dy[:, half:]

        @pl.when(i == 0)
        def _():
            acc_ref[...] = jnp.zeros_like(acc_ref)

        y = jnp.concatenate([ya_ref[...], yb_ref[...]], axis=-1)
        acc_ref[...] += _dg(y, dob, _TN2, False)

        @pl.when(i == nt - 1)
        def _():
            dw_ref[...] = acc_ref[...].astype(bf16)

    row = lambda w_: pl.BlockSpec((tm, w_), lambda i: (i, 0))
    return _pcall(
        body, name="out_proj_bwd", grid=(nt,),
        in_specs=[row(D), row(half), row(half), _resident(w.shape)],
        out_specs=[row(half), row(half), pl.BlockSpec((2 * half, D), lambda i: (0, 0))],
        out_shape=[_sds((T, half), f32), _sds((T, half), f32), _sds((2 * half, D), bf16)],
        scratch_shapes=[pltpu.VMEM((2 * half, D), f32)],
        compiler_params=_params(("arbitrary",), VMEM_MID),
    )(dout, ya, yb, w)


def _conv_taps(x, w):
    K = w.shape[0]
    acc = x * w[K - 1:K, :]
    for i in range(K - 1):
        acc = acc + shift(x, K - 1 - i) * w[i:i + 1, :]
    return acc


def conv_silu_math(w, x):
    return _silu(_conv_taps(x, w))


def gated_conv_math(w, xd, bg, cg):
    return bg * _conv_taps(cg * xd, w)


def seq_chan_fwd(math, w, xs, seq, out_dtype, name):
    T, C = xs[0].shape
    K = w.shape[0]
    nb, nc = T // seq, C // HEAD

    def body(w_ref, *refs):
        o_ref = refs[-1]
        o_ref[...] = math(w_ref[...], *[r[...] for r in refs[:-1]]).astype(out_dtype)

    blk = pl.BlockSpec((seq, HEAD), lambda j, b: (b, j))
    return _pcall(
        body, name=name, grid=(nc, nb),
        in_specs=[pl.BlockSpec((K, HEAD), lambda j, b: (0, j))] + [blk] * len(xs),
        out_specs=blk, out_shape=_sds((T, C), out_dtype),
        compiler_params=_params(("parallel", "parallel"), VMEM_MID),
    )(w, *xs)


def seq_chan_bwd(math, w, xs, dy, seq, dx_dtype, name):
    T, C = xs[0].shape
    K = w.shape[0]
    nb, nc = T // seq, C // HEAD
    nx = len(xs)

    def body(w_ref, *refs):
        x_refs, dy_ref = refs[:nx], refs[nx]
        dx_refs, dw_ref = refs[nx + 1:2 * nx + 1], refs[2 * nx + 1]
        _, vjp = jax.vjp(math, w_ref[...], *[r[...] for r in x_refs])
        grads = vjp(dy_ref[...].astype(f32))

        @pl.when(pl.program_id(1) == 0)
        def _():
            dw_ref[...] = jnp.zeros_like(dw_ref)

        dw_ref[...] += grads[0]
        for r, gx in zip(dx_refs, grads[1:]):
            r[...] = gx.astype(dx_dtype)

    blk = pl.BlockSpec((seq, HEAD), lambda j, b: (b, j))
    wblk = pl.BlockSpec((K, HEAD), lambda j, b: (0, j))
    return _pcall(
        body, name=name, grid=(nc, nb),
        in_specs=[wblk] + [blk] * (nx + 1),
        out_specs=[blk] * nx + [wblk],
        out_shape=[_sds((T, C), dx_dtype)] * nx + [_sds((K, C), f32)],
        compiler_params=_params(("parallel", "arbitrary"), VMEM_MID),
    )(w, *xs, dy)


def pool_group_math(win, ag, pw, scale):
    t = lax.broadcasted_iota(jnp.int32, (ag.shape[0], 1), 0)
    s, k = ag, 1
    while k < win:
        s = s + shift(s, k)
        k *= 2
    pooled = s / jnp.minimum(t + 1, win).astype(f32) - ag
    return mm(pooled, pw) * scale


def pool_fwd(a, pw, scale, seq):
    T, C = a.shape

    def body(a_ref, pw_ref, sc_ref, o_ref):
        for gi, win in enumerate(POOL_WINDOWS):
            cols = slice(gi * HEAD, (gi + 1) * HEAD)
            o_ref[:, cols] = pool_group_math(win, a_ref[:, cols], pw_ref[gi], sc_ref[:, cols]).astype(bf16)

    blk = pl.BlockSpec((seq, C), lambda b: (b, 0))
    return _pcall(
        body, name="pool_fwd", grid=(T // seq,),
        in_specs=[blk, _resident(pw.shape), _resident((1, C))],
        out_specs=blk, out_shape=_sds((T, C), bf16),
        compiler_params=_params(("parallel",), VMEM_MID),
    )(a, pw, scale)


def pool_bwd(a, pw, scale, dy, seq):
    T, C = a.shape

    def body(a_ref, pw_ref, sc_ref, dy_ref, da_ref, dpw_ref, dsc_ref):
        first = pl.program_id(0) == 0

        @pl.when(first)
        def _():
            dpw_ref[...] = jnp.zeros_like(dpw_ref)
            dsc_ref[...] = jnp.zeros_like(dsc_ref)

        for gi, win in enumerate(POOL_WINDOWS):
            cols = slice(gi * HEAD, (gi + 1) * HEAD)
            _, vjp = jax.vjp(functools.partial(pool_group_math, win), a_ref[:, cols], pw_ref[gi], sc_ref[:, cols])
            da, dpw, dsc = vjp(dy_ref[:, cols])
            dpw_ref[gi] += dpw
            dsc_ref[0:1, cols] += dsc
            da_ref[:, cols] = da.astype(bf16)

    blk = pl.BlockSpec((seq, C), lambda b: (b, 0))
    return _pcall(
        body, name="pool_bwd", grid=(T // seq,),
        in_specs=[blk, _resident(pw.shape), _resident((1, C)), blk],
        out_specs=[blk, pl.BlockSpec(pw.shape, lambda b: (0, 0, 0)), pl.BlockSpec((8, C), lambda b: (0, 0))],
        out_shape=[_sds((T, C), bf16), _sds(pw.shape, f32), _sds((8, C), f32)],
        compiler_params=_params(("arbitrary",), VMEM_MID),
    )(a, pw, scale, dy)


SOLVE_HI = False


def _neumann_inverse(lmat):
    n = lmat.shape[-1]
    ii = lax.broadcasted_iota(jnp.int32, (n, n), 0)
    jj = lax.broadcasted_iota(jnp.int32, (n, n), 1)
    inv = jnp.where((ii == jj)[None], 1.0, 0.0) - lmat
    pw_ = _dg(lmat, lmat, _NN3, SOLVE_HI)
    steps = int(math.log2(n)) - 1
    for i in range(steps):
        inv = inv + _dg(inv, pw_, _NN3, SOLVE_HI)
        if i < steps - 1:
            pw_ = _dg(pw_, pw_, _NN3, SOLVE_HI)
    return inv


@jax.custom_vjp
def unit_lower_inverse(lmat):
    return _neumann_inverse(lmat)


def _unit_lower_inverse_fwd(lmat):
    inv = _neumann_inverse(lmat)
    return inv, inv


def _unit_lower_inverse_bwd(inv, ct):
    return (-_dg(_dg(inv, ct, _TN3, SOLVE_HI), inv, _NT3, SOLVE_HI),)


unit_lower_inverse.defvjp(_unit_lower_inverse_fwd, _unit_lower_inverse_bwd)


def dn_prep_math(qkv, bg, alog, dtb):
    tt = qkv.shape[0]
    nt = tt // CHUNK
    nb = nt * N_HEADS
    W = N_HEADS * HEAD
    beta_all = jax.nn.sigmoid(bg)
    g_all = -jnp.exp(alog) * _softplus(bg + dtb)

    def heads(fn):
        return jnp.stack([fn(hd).reshape(nt, CHUNK, HEAD) for hd in range(N_HEADS)], axis=1).reshape(nb, CHUNK, HEAD)

    def l2n(x):
        return x * lax.rsqrt(jnp.sum(x * x, axis=-1, keepdims=True) + EPS)

    q = heads(lambda hd: l2n(qkv[:, hd * HEAD:(hd + 1) * HEAD]) * (HEAD ** -0.5))
    k = heads(lambda hd: l2n(qkv[:, W + hd * HEAD:W + (hd + 1) * HEAD]))
    v = heads(lambda hd: qkv[:, 2 * W + hd * HEAD:2 * W + (hd + 1) * HEAD])
    beta = heads(lambda hd: jnp.broadcast_to(beta_all[:, hd:hd + 1], (tt, HEAD)))
    g = heads(lambda hd: jnp.broadcast_to(g_all[:, N_HEADS + hd:N_HEADS + hd + 1], (tt, HEAD)))

    ii = lax.broadcasted_iota(jnp.int32, (CHUNK, CHUNK), 0)
    jj = lax.broadcasted_iota(jnp.int32, (CHUNK, CHUNK), 1)
    tril, strict = (ii >= jj)[None], (ii > jj)[None]
    ones_b = jnp.ones((nb, CHUNK, CHUNK), f32)
    tril_b = jnp.where(tril, ones_b, 0.0)
    eye_b = jnp.where((ii == jj)[None], ones_b, 0.0)

    gcb = bmm_hi(tril_b, g)
    gcol = gcb[:, :, :CHUNK]
    grow = bmm_hi(ones_b, eye_b * gcol)
    gamma = jnp.where(tril, jnp.exp(jnp.where(tril, gcol - grow, 0.0)), 0.0)
    kb = k * beta
    lmat = jnp.where(strict, bmm_nt(kb, k) * gamma, 0.0)
    inv = unit_lower_inverse(lmat)
    egc = jnp.exp(gcb)
    u = bmm(inv, v * beta)
    w = bmm(inv, kb * egc)
    aqk = bmm_nt(q, k) * gamma
    qd = q * egc
    glast = gcb[:, CHUNK - 1:CHUNK, :]
    kd = k * jnp.exp(glast - gcb)
    last = jnp.exp(glast)
    r4 = lambda x: x.reshape((nt, N_HEADS) + x.shape[1:])
    return r4(u), r4(w), r4(qd), r4(kd), r4(aqk), r4(last)


_PREP_DTYPES = (f32, bf16, bf16, bf16, bf16, f32)


def _prep_specs(nt, T):
    nchunks = T // CHUNK
    shapes = [(HEAD,), (HEAD,), (HEAD,), (HEAD,), (CHUNK,), (HEAD,)]
    rows = [CHUNK, CHUNK, CHUNK, CHUNK, CHUNK, 1]
    specs = [pl.BlockSpec((nt, N_HEADS, r, s[0]), lambda i: (i, 0, 0, 0)) for r, s in zip(rows, shapes)]
    outs = [(nchunks, N_HEADS, r, s[0]) for r, s in zip(rows, shapes)]
    return specs, outs


def dn_prep_fwd(qkv, bg, alog, dtb):
    T = qkv.shape[0]
    tt = _tile(T, 512)
    nt = tt // CHUNK
    specs, shapes = _prep_specs(nt, T)

    def body(qkv_ref, bg_ref, alog_ref, dtb_ref, *outs):
        res = dn_prep_math(qkv_ref[...], bg_ref[...], alog_ref[...], dtb_ref[...])
        for o_ref, r, dt in zip(outs, res, _PREP_DTYPES):
            o_ref[...] = r.astype(dt)

    row = lambda w: pl.BlockSpec((tt, w), lambda i: (i, 0))
    return _pcall(
        body, name="dn_prep_fwd", grid=(T // tt,),
        in_specs=[row(qkv.shape[1]), row(HEAD), _resident((1, HEAD)), _resident((1, HEAD))],
        out_specs=specs, out_shape=[_sds(s, dt) for s, dt in zip(shapes, _PREP_DTYPES)],
        compiler_params=_params(("parallel",), VMEM_BIG),
    )(qkv, bg, alog, dtb)


def dn_prep_bwd(qkv, bg, alog, dtb, cts):
    T = qkv.shape[0]
    tt = _tile(T, 512)
    nt = tt // CHUNK
    specs, _ = _prep_specs(nt, T)

    def body(qkv_ref, bg_ref, alog_ref, dtb_ref, *refs):
        ct_refs, (dqkv_ref, dbg_ref, dalog_ref, ddtb_ref) = refs[:6], refs[6:]
        _, vjp = jax.vjp(dn_prep_math, qkv_ref[...], bg_ref[...], alog_ref[...], dtb_ref[...])
        dqkv, dbg, dalog, ddtb = vjp(tuple(r[...].astype(f32) for r in ct_refs))
        dqkv_ref[...] = dqkv
        dbg_ref[...] = dbg.astype(bf16)
        first = pl.program_id(0) == 0
        _acc_rows(dalog_ref, dalog, first)
        _acc_rows(ddtb_ref, ddtb, first)

    row = lambda w: pl.BlockSpec((tt, w), lambda i: (i, 0))
    small = pl.BlockSpec((8, HEAD), lambda i: (0, 0))
    return _pcall(
        body, name="dn_prep_bwd", grid=(T // tt,),
        in_specs=[row(qkv.shape[1]), row(HEAD), _resident((1, HEAD)), _resident((1, HEAD))] + specs,
        out_specs=[row(qkv.shape[1]), row(HEAD), small, small],
        out_shape=[_sds(qkv.shape, f32), _sds((T, HEAD), bf16), _sds((8, HEAD), f32), _sds((8, HEAD), f32)],
        compiler_params=_params(("arbitrary",), VMEM_BIG),
    )(qkv, bg, alog, dtb, *cts)


def dn_step(state, u, w, qd, kd, aqk, last):
    v_new = u - bmm(w, state)
    o = bmm(qd, state) + bmm(aqk, v_new)
    return state * last + bmm_tn(kd, v_new), o


def dn_gate(o, z, onorm):
    return o * lax.rsqrt(jnp.mean(o * o, axis=-1, keepdims=True) + EPS) * onorm * _silu(z)


DN_HEADS_PER_STEP_FWD, DN_HEADS_PER_STEP_BWD = 4, 2


def _recur_specs(n, hp):
    rows = [CHUNK, CHUNK, CHUNK, CHUNK, CHUNK, 1]
    lanes = [HEAD, HEAD, HEAD, HEAD, CHUNK, HEAD]
    return [pl.BlockSpec((n, hp, r, l), lambda b, p: (b, p, 0, 0)) for r, l in zip(rows, lanes)]


def dn_recur_fwd(prep, z, onorm, seq):
    T, C = z.shape
    n = seq // CHUNK
    nchunks = T // CHUNK
    hp = DN_HEADS_PER_STEP_FWD

    def body(u_ref, w_ref, qd_ref, kd_ref, aqk_ref, last_ref, z_ref, on_ref, y_ref, st_ref, o_ref):
        def step(i, state):
            st_ref[i] = state.astype(bf16)
            new, o = dn_step(state, u_ref[i], w_ref[i], qd_ref[i], kd_ref[i], aqk_ref[i], last_ref[i])
            o_ref[i] = o
            return new

        lax.fori_loop(0, n, step, jnp.zeros((hp, HEAD, HEAD), f32))
        for j in range(hp):
            cols = slice(j * HEAD, (j + 1) * HEAD)
            y_ref[:, cols] = dn_gate(o_ref[:, j].reshape(seq, HEAD), z_ref[:, cols], on_ref[...]).astype(bf16)

    blk = pl.BlockSpec((seq, hp * HEAD), lambda b, p: (b, p))
    per_chunk = lambda r: pl.BlockSpec((n, hp, r, HEAD), lambda b, p: (b, p, 0, 0))
    return _pcall(
        body, name="dn_recur_fwd", grid=(T // seq, N_HEADS // hp),
        in_specs=_recur_specs(n, hp) + [blk, pl.BlockSpec((1, HEAD), lambda b, p: (0, 0))],
        out_specs=[blk, per_chunk(HEAD), per_chunk(CHUNK)],
        out_shape=[_sds((T, C), bf16), _sds((nchunks, N_HEADS, HEAD, HEAD), bf16),
                   _sds((nchunks, N_HEADS, CHUNK, HEAD), f32)],
        compiler_params=_params(("parallel", "parallel"), VMEM_BIG),
    )(*prep, z, onorm)


_RECUR_CT_DTYPES = (f32, bf16, bf16, bf16, bf16, f32)


def dn_recur_bwd(prep, states, o, z, onorm, dy, seq):
    T, C = z.shape
    n = seq // CHUNK
    nchunks = T // CHUNK
    hp = DN_HEADS_PER_STEP_BWD

    def body(u_ref, w_ref, qd_ref, kd_ref, aqk_ref, last_ref, st_ref, o_ref, z_ref, on_ref, dy_ref,
             du_ref, dw_ref, dqd_ref, dkd_ref, daqk_ref, dlast_ref, dz_ref, don_ref, do_ref):
        args = lambda i: tuple(r[i].astype(f32) for r in (st_ref, u_ref, w_ref, qd_ref, kd_ref, aqk_ref, last_ref))
        ct_refs = (du_ref, dw_ref, dqd_ref, dkd_ref, daqk_ref, dlast_ref)
        first = jnp.logical_and(pl.program_id(0) == 0, pl.program_id(1) == 0)
        for j in range(hp):
            cols = slice(j * HEAD, (j + 1) * HEAD)
            _, gate_vjp = jax.vjp(dn_gate, o_ref[:, j].reshape(seq, HEAD), z_ref[:, cols], on_ref[...])
            do, dz, don = gate_vjp(dy_ref[:, cols])
            dz_ref[:, cols] = dz.astype(bf16)
            do_ref[:, j] = do.reshape(n, CHUNK, HEAD)
            _acc_rows(don_ref, don, jnp.logical_and(first, j == 0))

        def bstep(j, dstate):
            i = n - 1 - j
            _, vjp = jax.vjp(dn_step, *args(i))
            ds, *cts = vjp((dstate, do_ref[i]))
            for r, ct, dt in zip(ct_refs, cts, _RECUR_CT_DTYPES):
                r[i] = ct.astype(dt)
            return ds

        lax.fori_loop(0, n, bstep, jnp.zeros((hp, HEAD, HEAD), f32))

    blk = pl.BlockSpec((seq, hp * HEAD), lambda b, p: (b, p))
    per_chunk = lambda r: pl.BlockSpec((n, hp, r, HEAD), lambda b, p: (b, p, 0, 0))
    rows = [CHUNK, CHUNK, CHUNK, CHUNK, CHUNK, 1]
    lanes = [HEAD, HEAD, HEAD, HEAD, CHUNK, HEAD]
    return _pcall(
        body, name="dn_recur_bwd", grid=(T // seq, N_HEADS // hp),
        in_specs=_recur_specs(n, hp) + [per_chunk(HEAD), per_chunk(CHUNK), blk,
                                        pl.BlockSpec((1, HEAD), lambda b, p: (0, 0)), blk],
        out_specs=_recur_specs(n, hp) + [blk, pl.BlockSpec((8, HEAD), lambda b, p: (0, 0))],
        out_shape=[_sds((nchunks, N_HEADS, r, l), dt) for r, l, dt in zip(rows, lanes, _RECUR_CT_DTYPES)]
        + [_sds((T, C), bf16), _sds((8, HEAD), f32)],
        scratch_shapes=[pltpu.VMEM((n, hp, CHUNK, HEAD), f32)],
        compiler_params=_params(("arbitrary", "arbitrary"), VMEM_BIG),
    )(*prep, states, o, z, onorm, dy)


def sgu_math(up, vp, ng, nb, sw, sbias):
    S = up.shape[0]
    nblk = S // SGU_BLOCK
    u = jax.nn.gelu(up, approximate=True)
    v = jax.nn.gelu(vp, approximate=True)
    xc = v - jnp.mean(v, axis=-1, keepdims=True)
    vn = xc * lax.rsqrt(jnp.mean(xc * xc, axis=-1, keepdims=True) + EPS) * ng + nb
    ii = lax.broadcasted_iota(jnp.int32, (SGU_BLOCK, SGU_BLOCK), 0)
    jj = lax.broadcasted_iota(jnp.int32, (SGU_BLOCK, SGU_BLOCK), 1)
    outs = []
    for hd in range(N_HEADS):
        vh = vn[:, hd * HEAD:(hd + 1) * HEAD].reshape(nblk, SGU_BLOCK, HEAD)
        ws = jnp.where(ii >= jj, sw[hd], 0.0)
        mixed = bmm(jnp.broadcast_to(ws[None], (nblk, SGU_BLOCK, SGU_BLOCK)), vh) + sbias[hd][None]
        outs.append(mixed.reshape(S, HEAD))
    return u * jnp.concatenate(outs, axis=-1)


def sgu_fwd(up, vp, ng, nb, sw, sbias, seq):
    T, C = up.shape

    def body(up_ref, vp_ref, ng_ref, nb_ref, sw_ref, sb_ref, o_ref):
        o_ref[...] = sgu_math(up_ref[...], vp_ref[...], ng_ref[...], nb_ref[...], sw_ref[...],
                              sb_ref[...]).astype(bf16)

    blk = pl.BlockSpec((seq, C), lambda b: (b, 0))
    return _pcall(
        body, name="sgu_fwd", grid=(T // seq,),
        in_specs=[blk, blk, _resident((1, C)), _resident((1, C)), _resident(sw.shape), _resident(sbias.shape)],
        out_specs=blk, out_shape=_sds((T, C), bf16),
        compiler_params=_params(("parallel",), VMEM_BIG),
    )(up, vp, ng, nb, sw, sbias)


def sgu_bwd(up, vp, ng, nb, sw, sbias, dy, seq):
    T, C = up.shape

    def body(up_ref, vp_ref, ng_ref, nb_ref, sw_ref, sb_ref, dy_ref,
             dup_ref, dvp_ref, dng_ref, dnb_ref, dsw_ref, dsb_ref):
        _, vjp = jax.vjp(sgu_math, up_ref[...], vp_ref[...], ng_ref[...], nb_ref[...], sw_ref[...], sb_ref[...])
        dup, dvp, dng, dnb, dsw, dsb = vjp(dy_ref[...])
        dup_ref[...] = dup.astype(bf16)
        dvp_ref[...] = dvp.astype(bf16)
        first = pl.program_id(0) == 0
        _acc_rows(dng_ref, dng, first)
        _acc_rows(dnb_ref, dnb, first)

        @pl.when(first)
        def _():
            dsw_ref[...] = jnp.zeros_like(dsw_ref)
            dsb_ref[...] = jnp.zeros_like(dsb_ref)

        dsw_ref[...] += dsw
        dsb_ref[...] += dsb

    blk = pl.BlockSpec((seq, C), lambda b: (b, 0))
    small = pl.BlockSpec((8, C), lambda b: (0, 0))
    return _pcall(
        body, name="sgu_bwd", grid=(T // seq,),
        in_specs=[blk, blk, _resident((1, C)), _resident((1, C)), _resident(sw.shape), _resident(sbias.shape), blk],
        out_specs=[blk, blk, small, small, pl.BlockSpec(sw.shape, lambda b: (0, 0, 0)),
                   pl.BlockSpec(sbias.shape, lambda b: (0, 0, 0))],
        out_shape=[_sds((T, C), bf16), _sds((T, C), bf16), _sds((8, C), f32), _sds((8, C), f32),
                   _sds(sw.shape, f32), _sds(sbias.shape, f32)],
        compiler_params=_params(("arbitrary",), VMEM_BIG),
    )(up, vp, ng, nb, sw, sbias, dy)


def loss_fwd_bwd(h, gain, target):
    T, D = h.shape
    tm = _tile(T, 512)

    def body(h_ref, gain_ref, t_ref, loss_ref, dh_ref, dgain_ref):
        gain_ = gain_ref[...]
        y, xhat, rstd = _rms_fwd(h_ref[...], gain_)
        err = y - t_ref[...]
        part = 0.5 * jnp.sum(jnp.mean(err * err, axis=-1, keepdims=True), axis=0, keepdims=True)
        dh, dgain = _rms_bwd(err * (1.0 / D), xhat, rstd, gain_)
        dh_ref[...] = dh
        first = pl.program_id(0) == 0
        _acc_rows(dgain_ref, dgain, first)

        @pl.when(first)
        def _():
            loss_ref[...] = jnp.zeros_like(loss_ref)

        loss_ref[...] += jnp.broadcast_to(part, loss_ref.shape)

    row = pl.BlockSpec((tm, D), lambda i: (i, 0))
    return _pcall(
        body, name="loss_fwd_bwd", grid=(T // tm,),
        in_specs=[row, _resident((1, D)), row],
        out_specs=[pl.BlockSpec((8, 128), lambda i: (0, 0)), row, pl.BlockSpec((8, D), lambda i: (0, 0))],
        out_shape=[_sds((8, 128), f32), _sds((T, D), f32), _sds((8, D), f32)],
        compiler_params=_params(("arbitrary",), VMEM_MID),
    )(h, gain, target)


def adamw(w, g, m, v):
    c1, c2 = 1.0 - ADAM_B1 ** ADAM_STEP, 1.0 - ADAM_B2 ** ADAM_STEP
    if w.ndim == 3:
        grid, blk = (w.shape[0],), pl.BlockSpec((1,) + w.shape[1:], lambda i: (i, 0, 0))
    else:
        R, C = w.shape
        tr = max(c for c in range(8, 513, 8) if R % c == 0)
        grid, blk = (R // tr,), pl.BlockSpec((tr, C), lambda i: (i, 0))

    def body(w_ref, g_ref, m_ref, v_ref, d_ref, nm_ref, nv_ref):
        gg = g_ref[...]
        nm = ADAM_B1 * m_ref[...] + (1.0 - ADAM_B1) * gg
        nv = ADAM_B2 * v_ref[...] + (1.0 - ADAM_B2) * (gg * gg)
        d_ref[...] = -ADAM_LR * ((nm / c1) / (jnp.sqrt(nv / c2) + ADAM_EPS) + ADAM_WD * w_ref[...])
        nm_ref[...] = nm
        nv_ref[...] = nv

    return _pcall(
        body, name="adamw", grid=grid, in_specs=[blk] * 4, out_specs=[blk] * 3,
        out_shape=[_sds(w.shape, f32)] * 3, compiler_params=_params(("parallel",), VMEM_MID),
    )(w, g, m, v)


def sum8(parts):
    _, R, C = parts.shape
    tr = R
    for cand in (512, 352, 336, 320, 256, 128):
        if R % cand == 0:
            tr = cand
            break

    def body(p_ref, o_ref):
        acc = p_ref[0].astype(f32)
        for i in range(1, N_DEV):
            acc = acc + p_ref[i].astype(f32)
        o_ref[...] = acc

    return _pcall(
        body, name="sum8", grid=(R // tr,),
        in_specs=[pl.BlockSpec((N_DEV, tr, C), lambda i: (0, i, 0))],
        out_specs=pl.BlockSpec((tr, C), lambda i: (i, 0)), out_shape=_sds((R, C), f32),
        compiler_params=_params(("parallel",), VMEM_MID),
    )(parts)


_FLIPS = [(fx, fy, fc) for fx in (0, 1) for fy in (0, 1) for fc in (0, 1)][1:]
_HBM = pl.BlockSpec(memory_space=pltpu.HBM)


def _me():
    return lax.axis_index("x"), lax.axis_index("y"), lax.axis_index("c")


def _peer(flip):
    x, y, c = _me()
    fx, fy, fc = flip
    return (1 - x if fx else x, 1 - y if fy else y, 1 - c if fc else c)


def _lin(dev):
    return 4 * dev[0] + 2 * dev[1] + dev[2]


def _src_block(ref, rows, dev, whole):
    return ref if whole else ref.at[pl.ds(pl.multiple_of(dev * rows, 16), rows)]


_SEM = pl.BlockSpec(memory_space=pltpu.SEMAPHORE)
_EFFECT = pltpu.SideEffectType.DATAFLOW_SIDE_EFFECTING


def push_start(srcs, whole, after, name):
    n = len(srcs)
    rows = [s.shape[0] if whole else s.shape[0] // N_DEV for s in srcs]
    land_shapes = [(N_DEV, r, s.shape[1]) for r, s in zip(rows, srcs)]

    def body(*refs):
        src_refs, land_refs = refs[:n], refs[n:2 * n]
        send_sems, recv_sems, own_sems, token = refs[2 * n + 1], refs[2 * n + 2], refs[2 * n + 3], refs[-1]
        me = _lin(_me())
        for i in range(n):
            for k, flip in enumerate(_FLIPS):
                peer = _peer(flip)
                pltpu.make_async_remote_copy(
                    src_ref=_src_block(src_refs[i], rows[i], _lin(peer), whole), dst_ref=land_refs[i].at[me],
                    send_sem=send_sems.at[i * 7 + k], recv_sem=recv_sems.at[i * 7 + k],
                    device_id=peer, device_id_type=MESH).start()
        for i in range(n):
            pltpu.make_async_copy(_src_block(src_refs[i], rows[i], me, whole), land_refs[i].at[me],
                                  own_sems.at[i]).start()
        token[...] = jnp.zeros_like(token)

    hbm = lambda a: pltpu.with_memory_space_constraint(a, pltpu.HBM)
    outs = _pcall(
        body, name=name,
        in_specs=[_HBM] * (2 * n) + [pl.BlockSpec(memory_space=pl.ANY)],
        out_specs=[_SEM, _SEM, _SEM] + [_HBM] * (2 * n) + [pl.BlockSpec(memory_space=pltpu.VMEM)],
        out_shape=[pltpu.SemaphoreType.DMA((7 * n,)), pltpu.SemaphoreType.DMA((7 * n,)),
                   pltpu.SemaphoreType.DMA((n,))]
        + [pltpu.HBM(s.shape, s.dtype) for s in srcs]
        + [pltpu.HBM(shp, s.dtype) for shp, s in zip(land_shapes, srcs)] + [_sds((8, 128), f32)],
        input_output_aliases={i: 3 + i for i in range(2 * n)},
        compiler_params=pltpu.CompilerParams(has_side_effects=_EFFECT),
    )(*[hbm(s) for s in srcs], *[hbm(lax.empty(shp, s.dtype)) for shp, s in zip(land_shapes, srcs)], after)
    return outs[0], outs[1], outs[2], list(outs[3:3 + n]), list(outs[3 + n:3 + 2 * n]), outs[-1]


def push_wait(handle, whole, after, name):
    send_sems, recv_sems, own_sems, srcs, lands, _ = handle
    n = len(srcs)
    rows = [l.shape[1] for l in lands]

    def body(*refs):
        src_refs, land_refs = refs[:n], refs[n:2 * n]
        send_sems_, recv_sems_, own_sems_ = refs[2 * n], refs[2 * n + 1], refs[2 * n + 2]
        me = _lin(_me())
        for i in range(n):
            for k, flip in enumerate(_FLIPS):
                peer = _peer(flip)
                cp = pltpu.make_async_remote_copy(
                    src_ref=_src_block(src_refs[i], rows[i], _lin(peer), whole), dst_ref=land_refs[i].at[_lin(peer)],
                    send_sem=send_sems_.at[i * 7 + k], recv_sem=recv_sems_.at[i * 7 + k],
                    device_id=peer, device_id_type=MESH)
                cp.wait_send()
                cp.wait_recv()
            pltpu.make_async_copy(_src_block(src_refs[i], rows[i], me, whole), land_refs[i].at[me],
                                  own_sems_.at[i]).wait()

    outs = _pcall(
        body, name=name,
        in_specs=[_HBM] * (2 * n) + [_SEM, _SEM, _SEM, pl.BlockSpec(memory_space=pl.ANY)],
        out_specs=[_HBM] * (2 * n),
        out_shape=[pltpu.HBM(a.shape, a.dtype) for a in srcs + lands],
        input_output_aliases={i: i for i in range(2 * n)},
        compiler_params=pltpu.CompilerParams(has_side_effects=_EFFECT),
    )(*srcs, *lands, send_sems, recv_sems, own_sems, after)
    return list(outs[n:])


def all_gather_small(x):
    R, C = x.shape

    def body(x_ref, o_ref, send_sems, recv_sems):
        me = _lin(_me())
        o_ref[me] = x_ref[...]
        sends = []
        for k, flip in enumerate(_FLIPS):
            rc = pltpu.make_async_remote_copy(
                src_ref=x_ref, dst_ref=o_ref.at[me], send_sem=send_sems.at[k], recv_sem=recv_sems.at[k],
                device_id=_peer(flip), device_id_type=MESH)
            rc.start()
            sends.append(rc)
        for k, flip in enumerate(_FLIPS):
            pltpu.make_async_remote_copy(
                src_ref=x_ref, dst_ref=o_ref.at[_lin(_peer(flip))], send_sem=send_sems.at[k],
                recv_sem=recv_sems.at[k], device_id=_peer(flip), device_id_type=MESH).wait_recv()
        for rc in sends:
            rc.wait_send()

    vm = pl.BlockSpec(memory_space=pltpu.VMEM)
    return _pcall(
        body, name="all_gather_small", in_specs=[vm], out_specs=vm, out_shape=_sds((N_DEV, R, C), x.dtype),
        scratch_shapes=[pltpu.SemaphoreType.DMA((7,)), pltpu.SemaphoreType.DMA((7,))],
        compiler_params=_params(None, VMEM_MID),
    )(x)


def sum_slots(parts):
    _, R, C = parts.shape

    def body(p_ref, o_ref):
        acc = p_ref[0]
        for p in range(1, N_DEV):
            acc = acc + p_ref[p]
        o_ref[...] = acc

    vm = pl.BlockSpec(memory_space=pltpu.VMEM)
    return _pcall(body, name="sum_slots", in_specs=[vm], out_specs=vm, out_shape=_sds((R, C), f32),
                  compiler_params=_params(None, VMEM_MID))(parts)


_PACK_ROWS = 8


def _packed_rows(shape):
    return -(-math.prod(shape) // (128 * _PACK_ROWS)) * _PACK_ROWS


def _pack(arrs):
    parts = []
    for a in arrs:
        flat = a.reshape(-1).astype(f32)
        rows = _packed_rows(a.shape)
        parts.append(jnp.pad(flat, (0, rows * 128 - flat.shape[0])).reshape(rows, 128))
    return jnp.concatenate(parts, axis=0)


def _unpack(buf, shapes):
    out, off = [], 0
    for s in shapes:
        rows = _packed_rows(s)
        out.append(buf[off:off + rows].reshape(-1)[:math.prod(s)].reshape(s))
        off += rows
    return out


def _row(v):
    return v.reshape(1, -1)


def _lane_row(vals, offset):
    return jnp.pad(vals.reshape(1, -1), ((0, 0), (offset, HEAD - offset - vals.shape[-1])))


def kernel(x, ffn1_norm, ffn1_w_gate, ffn1_w_up, ffn1_w_down, mix_norm, ffn2_norm, ffn2_w_gate, ffn2_w_up, ffn2_w_down, ab_w_in, pool_w, pool_scale, dn_conv_w, dn_a_log, dn_dt_bias, dn_out_norm, ab_w_out, cd_w_in, sgu_norm_g, sgu_norm_b, sgu_w, sgu_bias, sc_conv_w, cd_w_out, final_norm, loss_target, m_ffn1_norm, m_ffn1_w_gate, m_ffn1_w_up, m_ffn1_w_down, m_mix_norm, m_ffn2_norm, m_ffn2_w_gate, m_ffn2_w_up, m_ffn2_w_down, m_ab_w_in, m_pool_w, m_pool_scale, m_dn_conv_w, m_dn_a_log, m_dn_dt_bias, m_dn_out_norm, m_ab_w_out, m_cd_w_in, m_sgu_norm_g, m_sgu_norm_b, m_sgu_w, m_sgu_bias, m_sc_conv_w, m_cd_w_out, m_final_norm, v_ffn1_norm, v_ffn1_w_gate, v_ffn1_w_up, v_ffn1_w_down, v_mix_norm, v_ffn2_norm, v_ffn2_w_gate, v_ffn2_w_up, v_ffn2_w_down, v_ab_w_in, v_pool_w, v_pool_scale, v_dn_conv_w, v_dn_a_log, v_dn_dt_bias, v_dn_out_norm, v_ab_w_out, v_cd_w_in, v_sgu_norm_g, v_sgu_norm_b, v_sgu_w, v_sgu_bias, v_sc_conv_w, v_cd_w_out, v_final_norm):
    names = ['ffn1_norm', 'ffn1_w_gate', 'ffn1_w_up', 'ffn1_w_down', 'mix_norm', 'ffn2_norm', 'ffn2_w_gate',
             'ffn2_w_up', 'ffn2_w_down', 'ab_w_in', 'pool_w', 'pool_scale', 'dn_conv_w', 'dn_a_log', 'dn_dt_bias',
             'dn_out_norm', 'ab_w_out', 'cd_w_in', 'sgu_norm_g', 'sgu_norm_b', 'sgu_w', 'sgu_bias', 'sc_conv_w',
             'cd_w_out', 'final_norm']
    loc = locals()
    W = {n: loc[n] for n in names}
    M = {n: loc['m_' + n] for n in names}
    V = {n: loc['v_' + n] for n in names}

    B, S, D = x.shape
    T = B * S
    me = _lin(_me())

    def rows_of(w):
        return w.astype(bf16).T

    def layer_shards(layer):
        e = layer // 2
        shards = [rows_of(W['ffn1_w_gate'][layer]), rows_of(W['ffn1_w_up'][layer]), W['ffn1_w_down'][layer].astype(bf16)]
        if layer % 2 == 0:
            win = jnp.pad(rows_of(W['ab_w_in'][e]), ((0, AB_SHARD_PAD - AB_SHARD), (0, 0)))
            wout = W['ab_w_out'][e].astype(bf16)
        else:
            win = rows_of(W['cd_w_in'][e])
            wout = W['cd_w_out'][e].astype(bf16)
        shards += [win, wout]
        shards += [rows_of(W['ffn2_w_gate'][layer]), rows_of(W['ffn2_w_up'][layer]), W['ffn2_w_down'][layer].astype(bf16)]
        return shards

    def full_weight(idx, land, layer):
        if idx == 3 and layer % 2 == 0:
            return jnp.pad(land[:, :AB_SHARD].reshape(AB_IN, D), ((0, AB_IN_PAD - AB_IN), (0, 0)))
        return land

    def tied(gain, token):
        return gain if token is None else gain + token[0:1, 0:1]

    def start_groups(arrays, groups, whole, after, name):
        out = []
        for gi, idx in enumerate(groups):
            suffix = "" if len(groups) == 1 else "abc"[gi]
            handle = push_start([arrays[i] for i in idx], whole, after, name=name + suffix)
            after = handle[5]
            out.append((idx, handle, suffix))
        return out

    def wait_group(inflight, gi, whole, after, name):
        idx, handle, suffix = inflight[gi]
        return dict(zip(idx, push_wait(handle, whole, after, name=name + suffix)))

    one_group = [tuple(range(8))]
    by_block = [(0, 1, 2), (3, 4), (5, 6, 7)]
    zeros_tile = jnp.zeros((8, 128), f32)
    gathered = [None] * DEPTH
    small_shards = [W['dn_conv_w'], W['sgu_norm_g'], W['sgu_norm_b'], W['sc_conv_w']]
    gs = all_gather_small(_pack(small_shards))
    per_dev = [_unpack(gs[p], [a.shape for a in small_shards]) for p in range(N_DEV)]
    dn_conv_full, sgu_g_full, sgu_b_full, sc_conv_full = [
        jnp.concatenate([per_dev[p][i] for p in range(N_DEV)], axis=-1) for i in range(4)]

    inflight = start_groups(layer_shards(0), by_block, True, gs, "gather_start_0")

    h = x.reshape(T, D)
    saved = []
    for layer in range(DEPTH):
        e = layer // 2
        mine, landed = inflight, {}
        landed.update(wait_group(mine, 0, True, h if layer else mine[-1][1][5], f"gather_wait_{layer}"))
        token = None
        if layer + 1 < DEPTH:
            inflight = start_groups(layer_shards(layer + 1), one_group, True, landed[0], f"gather_start_{layer + 1}")
            token = inflight[-1][1][5]
        wg1, wu1, wd1 = [full_weight(i, landed[i], layer) for i in (0, 1, 2)]
        sv = {'h0': h}
        h, sv['g1'], sv['u1'] = ffn_fwd(h, tied(_row(W['ffn1_norm'][layer]), token), wg1, wu1, wd1)
        sv['h1'] = h
        if len(mine) > 1:
            landed.update(wait_group(mine, 1, True, h, f"gather_wait_{layer}"))
        win, wout = [full_weight(i, landed[i], layer) for i in (3, 4)]
        if layer % 2 == 0:
            a_in, qkv_pre, z, bg = in_proj_fwd(h, _row(W['mix_norm'][layer]), win, [512, 1536, 512, 128])
            ya = pool_fwd(a_in, W['pool_w'][e], _row(W['pool_scale'][e]), S)
            qkv = seq_chan_fwd(conv_silu_math, dn_conv_full[e], [qkv_pre], S, f32, "dn_conv_fwd")
            alog, dtb = _lane_row(W['dn_a_log'][e], N_HEADS), _lane_row(W['dn_dt_bias'][e], N_HEADS)
            prep = dn_prep_fwd(qkv, bg, alog, dtb)
            yb, states, o_dn = dn_recur_fwd(prep, z, _row(W['dn_out_norm'][e]), S)
            sv.update(a_in=a_in, qkv_pre=qkv_pre, z=z, bg=bg, qkv=qkv, alog=alog, dtb=dtb, prep=prep,
                      states=states, o_dn=o_dn)
        else:
            up, vp, xd, bgate, cg = in_proj_fwd(h, _row(W['mix_norm'][layer]), win, [512] * 5)
            sbias = W['sgu_bias'][e].reshape(N_HEADS, SGU_BLOCK, 1)
            ya = sgu_fwd(up, vp, _row(sgu_g_full[e]), _row(sgu_b_full[e]), W['sgu_w'][e], sbias,
                         _tile(S, SGU_TILE))
            yb = seq_chan_fwd(gated_conv_math, sc_conv_full[e], [xd, bgate, cg], S, bf16, "sc_conv_fwd")
            sv.update(up=up, vp=vp, xd=xd, bgate=bgate, cg=cg, sbias=sbias)
        sv.update(ya=ya, yb=yb)
        h = out_proj_fwd(h, ya, yb, wout)
        sv['h2'] = h
        if len(mine) > 1:
            landed.update(wait_group(mine, 2, True, h, f"gather_wait_{layer}"))
        wg2, wu2, wd2 = [full_weight(i, landed[i], layer) for i in (5, 6, 7)]
        h, sv['g2'], sv['u2'] = ffn_fwd(h, _row(W['ffn2_norm'][layer]), wg2, wu2, wd2)
        gathered[layer] = [wg1, wu1, wd1, win, wout, wg2, wu2, wd2]
        saved.append(sv)

    loss_part, dh, dfinal = loss_fwd_bwd(h, _row(W['final_norm']), loss_target.reshape(T, D))
    loss = lax.psum(loss_part[0, 0], ("x", "y", "c"))

    G = {}
    G['final_norm'] = dfinal[0]
    for n in ('ffn1_norm', 'mix_norm', 'ffn2_norm'):
        G[n] = [None] * DEPTH
    for n in ('pool_w', 'pool_scale', 'dn_conv_w', 'dn_a_log', 'dn_dt_bias', 'dn_out_norm',
              'sgu_norm_g', 'sgu_norm_b', 'sgu_w', 'sgu_bias', 'sc_conv_w'):
        G[n] = [None] * 2
    small_names = ['ffn1_norm', 'mix_norm', 'ffn2_norm', 'pool_w', 'pool_scale', 'dn_conv_w', 'dn_a_log',
                   'dn_dt_bias', 'dn_out_norm', 'sgu_norm_g', 'sgu_norm_b', 'sgu_w', 'sgu_bias', 'sc_conv_w',
                   'final_norm']
    big = [None] * DEPTH
    inflight = None
    last = []

    for layer in reversed(range(DEPTH)):
        e = layer // 2
        sv = saved[layer]
        wg1, wu1, wd1, win, wout, wg2, wu2, wd2 = gathered[layer]
        token = None if inflight is None else inflight[-1][1][5]
        dh, dgain, xn, act, dg, du, dy = ffn_bwd(sv['h2'], tied(_row(W['ffn2_norm'][layer]), token), sv['g2'],
                                                  sv['u2'], dh, wg2, wu2, wd2)
        G['ffn2_norm'][layer] = dgain[0]
        dwg2, dwu2, dwd2 = wgrad(dg, xn, zeros_tile), wgrad(du, xn, zeros_tile), wgrad(act, dy, zeros_tile)
        mix_token = ffn1_token = None
        if layer == 0:
            last += start_groups({5: dwg2, 6: dwu2, 7: dwd2}, [(5, 6, 7)], False, dwd2, "scatter_start_0a")
            mix_token = last[-1][1][5]
        dya, dyb, dwout = out_proj_bwd(dh, sv['ya'], sv['yb'], wout)
        if layer % 2 == 0:
            da, dpw, dsc = pool_bwd(sv['a_in'], W['pool_w'][e], _row(W['pool_scale'][e]), dya, S)
            G['pool_w'][e], G['pool_scale'][e] = dpw, dsc[0]
            *cts, dz, don = dn_recur_bwd(sv['prep'], sv['states'], sv['o_dn'], sv['z'],
                                         _row(W['dn_out_norm'][e]), dyb, S)
            G['dn_out_norm'][e] = don[0]
            dqkv, dbg, dalog, ddtb = dn_prep_bwd(sv['qkv'], sv['bg'], sv['alog'], sv['dtb'], cts)
            G['dn_a_log'][e], G['dn_dt_bias'][e] = dalog[0, N_HEADS:2 * N_HEADS], ddtb[0, N_HEADS:2 * N_HEADS]
            dqkv_pre, dconv = seq_chan_bwd(conv_silu_math, dn_conv_full[e], [sv['qkv_pre']], dqkv, S, bf16,
                                           "dn_conv_bwd")
            G['dn_conv_w'][e] = dconv
            dpieces = [da, dqkv_pre, dz, dbg]
        else:
            dup, dvp, dng, dnb, dsw, dsb = sgu_bwd(sv['up'], sv['vp'], _row(sgu_g_full[e]), _row(sgu_b_full[e]),
                                                   W['sgu_w'][e], sv['sbias'], dya, _tile(S, SGU_TILE))
            G['sgu_norm_g'][e], G['sgu_norm_b'][e] = dng[0], dnb[0]
            G['sgu_w'][e], G['sgu_bias'][e] = dsw, dsb.reshape(N_HEADS, SGU_BLOCK)
            dxd, dbgate, dcg, dscw = seq_chan_bwd(gated_conv_math, sc_conv_full[e],
                                                  [sv['xd'], sv['bgate'], sv['cg']], dyb, S, bf16, "sc_conv_bwd")
            G['sc_conv_w'][e] = dscw
            dpieces = [dup, dvp, dxd, dbgate, dcg]
        dh, dgain, dwin = in_proj_bwd(sv['h1'], tied(_row(W['mix_norm'][layer]), mix_token), dpieces, dh, win)
        G['mix_norm'][layer] = dgain[0]
        if layer % 2 == 0:
            dwin = jnp.pad(dwin[:AB_IN].reshape(N_DEV, AB_SHARD, D), ((0, 0), (0, AB_SHARD_PAD - AB_SHARD), (0, 0)))
            dwin = dwin.reshape(N_DEV * AB_SHARD_PAD, D)
        if layer == 0:
            last += start_groups({3: dwin, 4: dwout}, [(3, 4)], False, mix_token, "scatter_start_0b")
            ffn1_token = last[-1][1][5]
        dh, dgain, xn, act, dg, du, dy = ffn_bwd(sv['h0'], tied(_row(W['ffn1_norm'][layer]), ffn1_token), sv['g1'],
                                                  sv['u1'], dh, wg1, wu1, wd1)
        G['ffn1_norm'][layer] = dgain[0]
        order = zeros_tile
        if layer == 0:
            small_g = [G[n] if n == 'final_norm' else jnp.stack(G[n]) for n in small_names]
            small_x = start_groups([_pack(small_g)], [(0,)], True, ffn1_token, "small_grads_start")
            order = small_x[-1][1][5]
        dwg1, dwu1, dwd1 = wgrad(dg, xn, order), wgrad(du, xn, order), wgrad(act, dy, order)
        if layer == 0:
            last += start_groups({0: dwg1, 1: dwu1, 2: dwd1}, [(0, 1, 2)], False, order, "scatter_start_0c")
        else:
            after = zeros_tile
            if inflight is not None:
                big[layer + 1] = wait_group(inflight, 0, False, dh, f"scatter_wait_{layer + 1}")
                after = big[layer + 1][0]
            inflight = start_groups([dwg1, dwu1, dwd1, dwin, dwout, dwg2, dwu2, dwd2], one_group, False, after,
                                    f"scatter_start_{layer}")

    grad_x = dh.reshape(B, S, D)

    reduced_buf = sum_slots(wait_group(small_x, 0, True, last[-1][1][5], "small_grads_wait")[0])
    reduced = _unpack(reduced_buf, [a.shape for a in small_g])
    big[1] = wait_group(inflight, 0, False, reduced_buf, "scatter_wait_1")
    big[0] = wait_group(last, 0, False, big[1][0], "scatter_wait_0a")
    big[0].update(wait_group(last, 1, False, big[0][5], "scatter_wait_0b"))
    grads = {}
    for n, g in zip(small_names, reduced):
        if n in ('dn_conv_w', 'sgu_norm_g', 'sgu_norm_b', 'sc_conv_w'):
            c = W[n].shape[-1]
            g = lax.dynamic_slice_in_dim(g, me * c, c, axis=g.ndim - 1)
        grads[n] = g

    def stack_layers(idx, sel):
        out = []
        for layer in sel:
            g = sum8(big[layer][idx])
            out.append(g[:AB_SHARD] if idx == 3 and layer % 2 == 0 else g)
        return jnp.stack(out)

    all_layers, even, odd = range(DEPTH), range(0, DEPTH, 2), range(1, DEPTH, 2)
    delta, new_m, new_v = {}, {}, {}

    def update(n, idx, transposed, sel):
        view = (lambda a: jnp.swapaxes(a, 1, 2)) if transposed else (lambda a: a)
        g = stack_layers(idx, sel)
        d_, m_, v_ = adamw(view(W[n]), g, view(M[n]), view(V[n]))
        grads[n], delta[n], new_m[n], new_v[n] = view(g), view(d_), view(m_), view(v_)

    shapes = [W[n].shape for n in small_names]
    d_, m_, v_ = adamw(_pack([W[n] for n in small_names]), _pack([grads[n] for n in small_names]),
                       _pack([M[n] for n in small_names]), _pack([V[n] for n in small_names]))
    for n, a, b_, c_ in zip(small_names, _unpack(d_, shapes), _unpack(m_, shapes), _unpack(v_, shapes)):
        delta[n], new_m[n], new_v[n] = a, b_, c_
    update('ab_w_in', 3, True, even)
    update('cd_w_in', 3, True, odd)
    update('ab_w_out', 4, False, even)
    update('cd_w_out', 4, False, odd)
    update('ffn2_w_gate', 5, True, all_layers)
    update('ffn2_w_up', 6, True, all_layers)
    update('ffn2_w_down', 7, False, all_layers)
    updated = sum(lax.slice(a, (0,) * a.ndim, (1,) * a.ndim).reshape(1) for a in new_v.values())
    big[0].update(wait_group(last, 2, False, updated, "scatter_wait_0c"))
    update('ffn1_w_gate', 0, True, all_layers)
    update('ffn1_w_up', 1, True, all_layers)
    update('ffn1_w_down', 2, False, all_layers)

    return (loss, grad_x, *[grads[n] for n in names], *[delta[n] for n in names],
            *[new_m[n] for n in names], *[new_v[n] for n in names])
```
